```python
import jax
import jax.numpy as jnp
from jax import lax
import numpy as np

D_MODEL = 1024
BATCH = 2
SEQ = 8192
DEPTH = 4
DEC_BATCH = 128
DEC_SEQ = 4
PAST_LEN = 8192
PAGE_SIZE = 128

N_HEADS = 16
HEAD_DIM = D_MODEL // N_HEADS
N_KV_HEADS = 4
GROUP = N_HEADS // N_KV_HEADS
Q_DIM = N_HEADS * HEAD_DIM
KV_DIM = N_KV_HEADS * HEAD_DIM
D_FF = 4 * D_MODEL
ROPE_THETA = 10000.0
NORM_EPS = 1e-6
N_MIXERS = 3
MOBA_BLOCK = 256
MOBA_TOPK = 3
MOBA_Q_BLOCK = 64
IDX_HEADS = 8
IDX_DIM = 64
DSA_TOPK_MAX = 256
DSA_Q_BLOCK = 128
WINDOW = 128
POOL_SPARE_NUM = 5
POOL_SPARE_DEN = 4
IN_ATTN = Q_DIM + 2 * KV_DIM
IN_DSA = IN_ATTN + IDX_HEADS * IDX_DIM + IDX_DIM + IDX_HEADS
ATTN_SCALE = HEAD_DIM ** -0.5
IDX_W_SCALE = (IDX_HEADS * IDX_DIM) ** -0.5

kernel_name = 'hybrid_moba_dsa_swa_decode_step'


def _layer_plan():
    plan = []
    counts = [0] * N_MIXERS
    n_paged = 0
    for i in range(DEPTH):
        m = i % N_MIXERS
        slot = -1
        if m < 2:
            slot = n_paged
            n_paged += 1
        plan.append((m, counts[m], slot))
        counts[m] += 1
    return plan, tuple(counts), n_paged


def _rmsnorm(x, g):
    xf = x.astype(jnp.float32)
    y = xf * lax.rsqrt(jnp.mean(xf * xf, axis=-1, keepdims=True) + NORM_EPS)
    return (y * g.astype(jnp.float32)).astype(x.dtype)


def _adaln(c, w, b):
    return jnp.split(jax.nn.silu(c) @ w + b, 6, axis=-1)


def _modulate(h, shift, scale):
    return h * (1 + scale[:, None, :]) + shift[:, None, :]


def _mlp(h, w1, w2):
    return jnp.square(jax.nn.relu(h @ w1)) @ w2


def _rope(x, pos):
    half = x.shape[-1] // 2
    inv = ROPE_THETA ** (-jnp.arange(half, dtype=jnp.float32) / half)
    ang = pos.astype(jnp.float32)[:, None] * inv[None, :]
    cos = jnp.cos(ang)[:, None, :]
    sin = jnp.sin(ang)[:, None, :]
    x1 = x[..., :half].astype(jnp.float32)
    x2 = x[..., half:].astype(jnp.float32)
    return jnp.concatenate([x1 * cos - x2 * sin, x2 * cos + x1 * sin], axis=-1).astype(x.dtype)


def _project(h, w_in, pos):
    B, T, _ = h.shape
    z = h @ w_in
    q = _rope(z[..., :Q_DIM].reshape(B, T, N_HEADS, HEAD_DIM), pos)
    k = _rope(z[..., Q_DIM:Q_DIM + KV_DIM].reshape(B, T, N_KV_HEADS, HEAD_DIM), pos)
    v = z[..., Q_DIM + KV_DIM:IN_ATTN].reshape(B, T, N_KV_HEADS, HEAD_DIM)
    return q, k, v, z[..., IN_ATTN:]


def _indexer(rest, pos):
    B, T, _ = rest.shape
    nq = IDX_HEADS * IDX_DIM
    qi = _rope(rest[..., :nq].reshape(B, T, IDX_HEADS, IDX_DIM), pos)
    ki = _rope(rest[..., nq:nq + IDX_DIM].reshape(B, T, 1, IDX_DIM), pos)[:, :, 0]
    wi = rest[..., nq + IDX_DIM:] * IDX_W_SCALE
    return qi, ki, wi


def _index_scores(qi, ki, wi):
    dots = jnp.einsum('bthd,bsd->bths', qi.astype(jnp.float32), ki.astype(jnp.float32))
    return jnp.einsum('bth,bths->bts', wi.astype(jnp.float32), jax.nn.relu(dots))


def _gathered_attn(qg, ks, vs, valid):
    lg = jnp.einsum('btkgd,btskd->btkgs', qg, ks).astype(jnp.float32) * ATTN_SCALE
    lg = jnp.where(valid[:, :, None, None, :], lg, -jnp.inf)
    p = jax.nn.softmax(lg, axis=-1).astype(vs.dtype)
    return jnp.einsum('btkgs,btskd->btkgd', p, vs)


def _sink_softmax(lg, sink):
    m = jnp.maximum(jnp.max(lg, axis=-1, keepdims=True), sink)
    e = jnp.exp(lg - m)
    return e / (jnp.sum(e, axis=-1, keepdims=True) + jnp.exp(sink - m))


def _moba_prompt(q, k, v):
    B, S = q.shape[:2]
    nfull = S // MOBA_BLOCK
    nblk = -(-S // MOBA_BLOCK)
    pad = nblk * MOBA_BLOCK - S
    kbt = jnp.pad(k, ((0, 0), (0, pad), (0, 0), (0, 0))).reshape(B, nblk, MOBA_BLOCK, N_KV_HEADS, HEAD_DIM).transpose(0, 3, 1, 2, 4)
    vbt = jnp.pad(v, ((0, 0), (0, pad), (0, 0), (0, 0))).reshape(B, nblk, MOBA_BLOCK, N_KV_HEADS, HEAD_DIM).transpose(0, 3, 1, 2, 4)
    n_sel = min(MOBA_TOPK, nfull)
    kmean = jnp.mean(kbt[:, :, :nfull].astype(jnp.float32), axis=3)
    qg = q.reshape(B, S, N_KV_HEADS, GROUP, HEAD_DIM)
    b_ix = jnp.arange(B)[:, None, None, None, None]
    kv_ix = jnp.arange(N_KV_HEADS)[None, None, :, None, None]

    def chunk(ci):
        t0 = ci * MOBA_Q_BLOCK
        qc = lax.dynamic_slice_in_dim(qg, t0, MOBA_Q_BLOCK, axis=1)
        qpos = t0 + jnp.arange(MOBA_Q_BLOCK)
        blk = t0 // MOBA_BLOCK
        k_own = lax.dynamic_index_in_dim(kbt, blk, axis=2, keepdims=False)
        v_own = lax.dynamic_index_in_dim(vbt, blk, axis=2, keepdims=False)
        kpos = blk * MOBA_BLOCK + jnp.arange(MOBA_BLOCK)
        lg = jnp.einsum('btkgd,bksd->btkgs', qc, k_own).astype(jnp.float32) * ATTN_SCALE
        lg = jnp.where((kpos[None, :] <= qpos[:, None])[None, :, None, None, :], lg, -jnp.inf)
        if n_sel > 0:
            gs = jnp.einsum('btkgd,bknd->btkgn', qc.astype(jnp.float32), kmean)
            gs = jnp.where(jnp.arange(nfull) < blk, gs, -jnp.inf)
            _, sel = lax.top_k(gs, n_sel)
            valid = sel < blk
            kg = kbt[b_ix, kv_ix, sel]
            vg = vbt[b_ix, kv_ix, sel]
            lgs = jnp.einsum('btkgd,btkgjsd->btkgjs', qc, kg).astype(jnp.float32) * ATTN_SCALE
            lgs = jnp.where(valid[..., None], lgs, -jnp.inf).reshape(B, MOBA_Q_BLOCK, N_KV_HEADS, GROUP, n_sel * MOBA_BLOCK)
            lg = jnp.concatenate([lg, lgs], axis=-1)
        p = jax.nn.softmax(lg, axis=-1).astype(v.dtype)
        out = jnp.einsum('btkgs,bksd->btkgd', p[..., :MOBA_BLOCK], v_own)
        if n_sel > 0:
            pg = p[..., MOBA_BLOCK:].reshape(B, MOBA_Q_BLOCK, N_KV_HEADS, GROUP, n_sel, MOBA_BLOCK)
            out = out + jnp.einsum('btkgjs,btkgjsd->btkgd', pg, vg)
        return out

    outs = lax.map(chunk, jnp.arange(S // MOBA_Q_BLOCK))
    return jnp.moveaxis(outs, 0, 1).reshape(B, S, Q_DIM)


def _moba_sample(q, k_new, v_new, cache_k, cache_v, slot, page_table):
    DB, T = q.shape[:2]
    past = PAST_LEN
    ppb = MOBA_BLOCK // PAGE_SIZE
    cur = past // MOBA_BLOCK
    own0 = cur * MOBA_BLOCK
    n_sel = min(MOBA_TOPK, cur)
    k_past = cache_k[slot, page_table].reshape(DB, past, N_KV_HEADS, HEAD_DIM)
    v_own_past = cache_v[slot, page_table[:, own0 // PAGE_SIZE:]].reshape(DB, past - own0, N_KV_HEADS, HEAD_DIM)
    k_own = jnp.concatenate([k_past[:, own0:], k_new], axis=1)
    v_own = jnp.concatenate([v_own_past, v_new], axis=1)
    n_own = k_own.shape[1]
    kpos_own = jnp.concatenate([own0 + jnp.arange(past - own0), past + jnp.arange(T)])
    qg = q.reshape(DB, T, N_KV_HEADS, GROUP, HEAD_DIM)
    if n_sel > 0:
        kmean = jnp.mean(k_past[:, :own0].reshape(DB, cur, MOBA_BLOCK, N_KV_HEADS, HEAD_DIM).astype(jnp.float32), axis=2)
        gs = jnp.einsum('btkgd,bnkd->btkgn', qg.astype(jnp.float32), kmean)
        _, sel = lax.top_k(gs, n_sel)
    else:
        sel = jnp.zeros((DB, T, N_KV_HEADS, GROUP, 0), jnp.int32)
    b_ix = jnp.arange(DB)[:, None, None, None, None]
    kv_ix = jnp.arange(N_KV_HEADS)[None, :, None, None, None, None]
    rows = jnp.arange(PAGE_SIZE)

    def token(args):
        qt, selt, qpos = args
        lg = jnp.einsum('bkgd,bskd->bkgs', qt, k_own).astype(jnp.float32) * ATTN_SCALE
        lg = jnp.where((kpos_own <= qpos)[None, None, None, :], lg, -jnp.inf)
        if n_sel > 0:
            phys = page_table[b_ix, selt[..., None] * ppb + jnp.arange(ppb)]
            kg = cache_k[slot, phys[..., None], rows, kv_ix].reshape(DB, N_KV_HEADS, GROUP, n_sel * MOBA_BLOCK, HEAD_DIM)
            vg = cache_v[slot, phys[..., None], rows, kv_ix].reshape(DB, N_KV_HEADS, GROUP, n_sel * MOBA_BLOCK, HEAD_DIM)
            lgs = jnp.einsum('bkgd,bkgsd->bkgs', qt, kg).astype(jnp.float32) * ATTN_SCALE
            lg = jnp.concatenate([lg, lgs], axis=-1)
        p = jax.nn.softmax(lg, axis=-1).astype(v_new.dtype)
        out = jnp.einsum('bkgs,bskd->bkgd', p[..., :n_own], v_own)
        if n_sel > 0:
            out = out + jnp.einsum('bkgs,bkgsd->bkgd', p[..., n_own:], vg)
        return out

    outs = lax.map(token, (jnp.moveaxis(qg, 1, 0), jnp.moveaxis(sel, 1, 0), past + jnp.arange(T)))
    return jnp.moveaxis(outs, 0, 1).reshape(DB, T, Q_DIM)


def _dsa_prompt(q, k, v, qi, ki, wi):
    B, S = q.shape[:2]
    k_top = min(DSA_TOPK_MAX, S // 4)
    qg = q.reshape(B, S, N_KV_HEADS, GROUP, HEAD_DIM)
    b_ix = jnp.arange(B)[:, None, None]
    kpos = jnp.arange(S)

    def chunk(ci):
        t0 = ci * DSA_Q_BLOCK
        qc = lax.dynamic_slice_in_dim(qg, t0, DSA_Q_BLOCK, axis=1)
        qic = lax.dynamic_slice_in_dim(qi, t0, DSA_Q_BLOCK, axis=1)
        wic = lax.dynamic_slice_in_dim(wi, t0, DSA_Q_BLOCK, axis=1)
        qpos = t0 + jnp.arange(DSA_Q_BLOCK)
        sc = _index_scores(qic, ki, wic)
        sc = jnp.where(kpos[None, None, :] <= qpos[None, :, None], sc, -jnp.inf)
        _, sel = lax.top_k(sc, k_top)
        valid = sel <= qpos[None, :, None]
        return _gathered_attn(qc, k[b_ix, sel], v[b_ix, sel], valid)

    outs = lax.map(chunk, jnp.arange(S // DSA_Q_BLOCK))
    return jnp.moveaxis(outs, 0, 1).reshape(B, S, Q_DIM)


def _dsa_sample(q, k_new, v_new, qi, ki_new, wi, cache_k, cache_v, cache_idx_k, slot, islot, page_table):
    DB, T = q.shape[:2]
    past = PAST_LEN
    L = past + T
    k_top = min(DSA_TOPK_MAX, L // 4)
    ki_past = cache_idx_k[islot, page_table].reshape(DB, past, IDX_DIM)
    ki_all = jnp.concatenate([ki_past, ki_new], axis=1)
    qpos = past + jnp.arange(T)
    kpos = jnp.arange(L)
    sc = _index_scores(qi, ki_all, wi)
    sc = jnp.where(kpos[None, None, :] <= qpos[None, :, None], sc, -jnp.inf)
    _, sel = lax.top_k(sc, k_top)
    valid = sel <= qpos[None, :, None]
    b_ix = jnp.arange(DB)[:, None, None]
    ps = jnp.minimum(sel, past - 1)
    phys = page_table[b_ix, ps // PAGE_SIZE]
    row = ps % PAGE_SIZE
    ns = jnp.clip(sel - past, 0, T - 1)
    in_past = (sel < past)[..., None, None]
    ks = jnp.where(in_past, cache_k[slot, phys, row], k_new[b_ix, ns])
    vs = jnp.where(in_past, cache_v[slot, phys, row], v_new[b_ix, ns])
    out = _gathered_attn(q.reshape(DB, T, N_KV_HEADS, GROUP, HEAD_DIM), ks, vs, valid)
    return out.reshape(DB, T, Q_DIM)


def _swa_prompt(q, k, v, sinks):
    B, S = q.shape[:2]
    W = WINDOW
    nb = S // W
    qb = q.reshape(B, nb, W, N_KV_HEADS, GROUP, HEAD_DIM)
    kb = k.reshape(B, nb, W, N_KV_HEADS, HEAD_DIM)
    vb = v.reshape(B, nb, W, N_KV_HEADS, HEAD_DIM)
    kk = jnp.concatenate([jnp.concatenate([jnp.zeros_like(kb[:, :1]), kb[:, :-1]], axis=1), kb], axis=2)
    vv = jnp.concatenate([jnp.concatenate([jnp.zeros_like(vb[:, :1]), vb[:, :-1]], axis=1), vb], axis=2)
    lg = jnp.einsum('bnqkgd,bnskd->bnkgqs', qb, kk).astype(jnp.float32) * ATTN_SCALE
    blk = jnp.arange(nb)[:, None, None]
    qabs = blk * W + jnp.arange(W)[None, :, None]
    kabs = (blk - 1) * W + jnp.arange(2 * W)[None, None, :]
    ok = (kabs >= 0) & (kabs <= qabs) & (qabs - kabs <= WINDOW)
    lg = jnp.where(ok[None, :, None, None], lg, -jnp.inf)
    sink = sinks.astype(jnp.float32).reshape(N_KV_HEADS, GROUP)[None, None, :, :, None, None]
    p = _sink_softmax(lg, sink).astype(v.dtype)
    out = jnp.einsum('bnkgqs,bnskd->bnqkgd', p, vv)
    return out.reshape(B, S, Q_DIM)


def _swa_sample(q, k_new, v_new, buf_k, buf_v, sinks):
    DB, T = q.shape[:2]
    wk = buf_k.shape[1]
    kk = jnp.concatenate([buf_k, k_new], axis=1)
    vv = jnp.concatenate([buf_v, v_new], axis=1)
    kpos = jnp.concatenate([PAST_LEN - wk + jnp.arange(wk), PAST_LEN + jnp.arange(T)])
    qpos = PAST_LEN + jnp.arange(T)
    ok = (kpos[None, :] <= qpos[:, None]) & (qpos[:, None] - kpos[None, :] <= WINDOW)
    qg = q.reshape(DB, T, N_KV_HEADS, GROUP, HEAD_DIM)
    lg = jnp.einsum('btkgd,bskd->bkgts', qg, kk).astype(jnp.float32) * ATTN_SCALE
    lg = jnp.where(ok[None, None, None], lg, -jnp.inf)
    sink = sinks.astype(jnp.float32).reshape(N_KV_HEADS, GROUP)[None, :, :, None, None]
    p = _sink_softmax(lg, sink).astype(v_new.dtype)
    out = jnp.einsum('bkgts,bskd->btkgd', p, vv).reshape(DB, T, Q_DIM)
    return out, kk[:, -wk:], vv[:, -wk:]


def setup_inputs(seed: int = 0) -> dict:
    key = jax.random.key(seed)
    ks = jax.random.split(key, 24)
    _, (n_moba, n_dsa, n_swa), n_paged = _layer_plan()
    n_pages = PAST_LEN // PAGE_SIZE
    n_pool = DEC_BATCH * n_pages * POOL_SPARE_NUM // POOL_SPARE_DEN
    keep = min(WINDOW, PAST_LEN)

    def nrm(k, shape, s=1.0):
        return jax.random.normal(k, shape, jnp.float32) * s

    page_table = jax.random.permutation(ks[0], n_pool)[:DEC_BATCH * n_pages].reshape(DEC_BATCH, n_pages).astype(jnp.int32)
    return {
        'x_prompt': nrm(ks[1], (BATCH, SEQ, D_MODEL)),
        'x_sample': nrm(ks[2], (DEC_BATCH, DEC_SEQ, D_MODEL)),
        'cache_k': nrm(ks[3], (n_paged, n_pool, PAGE_SIZE, N_KV_HEADS, HEAD_DIM)),
        'cache_v': nrm(ks[4], (n_paged, n_pool, PAGE_SIZE, N_KV_HEADS, HEAD_DIM)),
        'cache_idx_k': nrm(ks[5], (n_dsa, n_pool, PAGE_SIZE, IDX_DIM)),
        'state_swa_k': nrm(ks[6], (n_swa, DEC_BATCH, keep, N_KV_HEADS, HEAD_DIM)),
        'state_swa_v': nrm(ks[7], (n_swa, DEC_BATCH, keep, N_KV_HEADS, HEAD_DIM)),
        'page_table': page_table,
        'c_prompt': nrm(ks[8], (BATCH, D_MODEL)),
        'c_sample': nrm(ks[9], (DEC_BATCH, D_MODEL)),
        'g_attn': 1.0 + nrm(ks[10], (DEPTH, D_MODEL), 0.1),
        'g_mlp': 1.0 + nrm(ks[11], (DEPTH, D_MODEL), 0.1),
        'w_mod': nrm(ks[12], (DEPTH, D_MODEL, 6 * D_MODEL), D_MODEL ** -0.5),
        'b_mod': nrm(ks[13], (DEPTH, 6 * D_MODEL), 0.02),
        'w_in_moba': nrm(ks[14], (n_moba, D_MODEL, IN_ATTN), D_MODEL ** -0.5),
        'w_in_dsa': nrm(ks[15], (n_dsa, D_MODEL, IN_DSA), D_MODEL ** -0.5),
        'w_in_swa': nrm(ks[16], (n_swa, D_MODEL, IN_ATTN), D_MODEL ** -0.5),
        'swa_sinks': nrm(ks[17], (n_swa, N_HEADS), 0.5),
        'w_out': nrm(ks[18], (DEPTH, Q_DIM, D_MODEL), Q_DIM ** -0.5),
        'w_ff1': nrm(ks[19], (DEPTH, D_MODEL, D_FF), D_MODEL ** -0.5),
        'w_ff2': nrm(ks[20], (DEPTH, D_FF, D_MODEL), D_FF ** -0.5),
        'g_final': 1.0 + nrm(ks[21], (D_MODEL,), 0.1),
    }


def reference(x_prompt, x_sample, cache_k, cache_v, cache_idx_k, state_swa_k, state_swa_v, page_table,
              c_prompt, c_sample, g_attn, g_mlp, w_mod, b_mod, w_in_moba, w_in_dsa, w_in_swa, swa_sinks,
              w_out, w_ff1, w_ff2, g_final):
    plan, _, _ = _layer_plan()
    S = x_prompt.shape[1]
    T = x_sample.shape[1]
    pos_p = jnp.arange(S, dtype=jnp.int32)
    pos_s = PAST_LEN + jnp.arange(T, dtype=jnp.int32)
    xp, xs = x_prompt, x_sample
    kp_l, vp_l, ks_l, vs_l = [], [], [], []
    ip_l, is_l = [], []
    skp_l, svp_l, sks_l, svs_l = [], [], [], []
    for i, (mixer, j, slot) in enumerate(plan):
        sh_ap, sc_ap, gt_ap, sh_mp, sc_mp, gt_mp = _adaln(c_prompt, w_mod[i], b_mod[i])
        sh_as, sc_as, gt_as, sh_ms, sc_ms, gt_ms = _adaln(c_sample, w_mod[i], b_mod[i])
        hp = _modulate(_rmsnorm(xp, g_attn[i]), sh_ap, sc_ap)
        hs = _modulate(_rmsnorm(xs, g_attn[i]), sh_as, sc_as)
        if mixer == 0:
            q, k, v, _ = _project(hp, w_in_moba[j], pos_p)
            op = _moba_prompt(q, k, v)
            qn, kn, vn, _ = _project(hs, w_in_moba[j], pos_s)
            on = _moba_sample(qn, kn, vn, cache_k, cache_v, slot, page_table)
            kp_l.append(k)
            vp_l.append(v)
            ks_l.append(kn)
            vs_l.append(vn)
        elif mixer == 1:
            q, k, v, rest = _project(hp, w_in_dsa[j], pos_p)
            qi, ki, wi = _indexer(rest, pos_p)
            op = _dsa_prompt(q, k, v, qi, ki, wi)
            qn, kn, vn, restn = _project(hs, w_in_dsa[j], pos_s)
            qin, kin, win = _indexer(restn, pos_s)
            on = _dsa_sample(qn, kn, vn, qin, kin, win, cache_k, cache_v, cache_idx_k, slot, j, page_table)
            kp_l.append(k)
            vp_l.append(v)
            ks_l.append(kn)
            vs_l.append(vn)
            ip_l.append(ki)
            is_l.append(kin)
        else:
            q, k, v, _ = _project(hp, w_in_swa[j], pos_p)
            op = _swa_prompt(q, k, v, swa_sinks[j])
            keep_p = min(WINDOW, S)
            skp_l.append(k[:, S - keep_p:])
            svp_l.append(v[:, S - keep_p:])
            qn, kn, vn, _ = _project(hs, w_in_swa[j], pos_s)
            on, nbk, nbv = _swa_sample(qn, kn, vn, state_swa_k[j], state_swa_v[j], swa_sinks[j])
            sks_l.append(nbk)
            svs_l.append(nbv)
        xp = xp + gt_ap[:, None, :] * (op @ w_out[i])
        xs = xs + gt_as[:, None, :] * (on @ w_out[i])
        xp = xp + gt_mp[:, None, :] * _mlp(_modulate(_rmsnorm(xp, g_mlp[i]), sh_mp, sc_mp), w_ff1[i], w_ff2[i])
        xs = xs + gt_ms[:, None, :] * _mlp(_modulate(_rmsnorm(xs, g_mlp[i]), sh_ms, sc_ms), w_ff1[i], w_ff2[i])
    y_prompt = _rmsnorm(xp, g_final)
    y_sample = _rmsnorm(xs, g_final)
    return (y_prompt, y_sample, jnp.stack(kp_l), jnp.stack(vp_l), jnp.stack(ks_l), jnp.stack(vs_l),
            jnp.stack(ip_l), jnp.stack(is_l), jnp.stack(skp_l), jnp.stack(svp_l), jnp.stack(sks_l), jnp.stack(svs_l))
```

```python
import functools

import numpy as np
import jax
import jax.numpy as jnp
from jax import lax
from jax.experimental import pallas as pl
from jax.experimental.pallas import tpu as pltpu

D_MODEL = 1024
N_HEADS = 16
HEAD_DIM = D_MODEL // N_HEADS
N_KV_HEADS = 4
GROUP = N_HEADS // N_KV_HEADS
Q_DIM = N_HEADS * HEAD_DIM
KV_DIM = N_KV_HEADS * HEAD_DIM
D_FF = 4 * D_MODEL
ROPE_THETA = 10000.0
NORM_EPS = 1e-6
N_MIXERS = 3
PAGE_SIZE = 128
MOBA_BLOCK = 256
MOBA_TOPK = 3
IDX_HEADS = 8
IDX_DIM = 64
DSA_TOPK_MAX = 256
DSA_Q_BLOCK = 128
WINDOW = 128
IN_ATTN = Q_DIM + 2 * KV_DIM
IDX_Q_DIM = IDX_HEADS * IDX_DIM
IN_DSA = IN_ATTN + IDX_Q_DIM + IDX_DIM + IDX_HEADS
ATTN_SCALE = HEAD_DIM ** -0.5
IDX_W_SCALE = IDX_Q_DIM ** -0.5

LANES = 128
SUBLANES = 8
VMEM_LIMIT = 56 << 20
ROW_TILE = 512
FF_TILE = 1024
DSA_KEY_CHUNK = 512
PAGES_PER_STEP = 8

BF = jnp.bfloat16
F32 = jnp.float32
I32 = jnp.int32
NEG_INF = float("-inf")
INT_MIN = -2 ** 31
KEY_NEG_INF = INT_MIN + 0x7FFFFF
NT = (((1,), (1,)), ((), ()))


def _params(*semantics):
    return pltpu.CompilerParams(dimension_semantics=semantics, vmem_limit_bytes=VMEM_LIMIT)


def _layer_plan(depth):
    plan, counts, n_paged = [], [0] * N_MIXERS, 0
    for i in range(depth):
        m = i % N_MIXERS
        slot = -1
        if m < 2:
            slot = n_paged
            n_paged += 1
        plan.append((m, counts[m], slot))
        counts[m] += 1
    return plan


def _adaln_kernel(c_ref, w_ref, b_ref, o_ref):
    c = c_ref[...]
    a = (c * (1.0 / (1.0 + jnp.exp(-c)))).astype(BF)
    o_ref[...] = jnp.dot(a, w_ref[...].astype(BF), preferred_element_type=F32) + b_ref[...]


def _adaln(c_all, w_mod, b_mod):
    depth, _, n_out = w_mod.shape
    nc = c_all.shape[0]
    tn = 1536
    return pl.pallas_call(
        _adaln_kernel,
        grid=(depth, n_out // tn),
        in_specs=[pl.BlockSpec((nc, D_MODEL), lambda l, j: (0, 0)),
                  pl.BlockSpec((None, D_MODEL, tn), lambda l, j: (l, 0, j)),
                  pl.BlockSpec((None, 1, tn), lambda l, j: (l, 0, j))],
        out_specs=pl.BlockSpec((None, nc, tn), lambda l, j: (l, 0, j)),
        out_shape=jax.ShapeDtypeStruct((depth, nc, n_out), F32),
        compiler_params=_params("parallel", "parallel"),
        name="adaln",
    )(c_all, w_mod, b_mod.reshape(depth, 1, n_out))


def _norm_mod(x, g, scale, shift):
    y = x * lax.rsqrt(jnp.mean(x * x, axis=-1, keepdims=True) + NORM_EPS)
    return (y * g) * (1.0 + scale) + shift


def _rope_chunk(z, cos, sin_signed, lo_half):
    partner = jnp.where(lo_half, pltpu.roll(z, LANES - HEAD_DIM // 2, 1), pltpu.roll(z, HEAD_DIM // 2, 1))
    return z * cos + partner * sin_signed


def _proj_kernel(x_ref, g_ref, sc_ref, sh_ref, cos_ref, sin_ref, w_ref, *out_refs, dsa):
    hb = _norm_mod(x_ref[...], g_ref[...], sc_ref[...], sh_ref[...]).astype(BF)
    cos = cos_ref[...]
    sin_s = sin_ref[...]
    lane = lax.broadcasted_iota(I32, (1, LANES), 1)
    lo_half = (lane % HEAD_DIM) < (HEAD_DIM // 2)
    q_ref, k_ref, v_ref, kb_ref, vb_ref = out_refs[:5]

    zq = jnp.dot(hb, w_ref[:, :Q_DIM], preferred_element_type=F32)
    for c in range(Q_DIM // LANES):
        sl = slice(c * LANES, (c + 1) * LANES)
        q_ref[:, sl] = (_rope_chunk(zq[:, sl], cos, sin_s, lo_half) * ATTN_SCALE).astype(BF)
    zk = jnp.dot(hb, w_ref[:, Q_DIM:Q_DIM + KV_DIM], preferred_element_type=F32)
    for c in range(KV_DIM // LANES):
        sl = slice(c * LANES, (c + 1) * LANES)
        kr = _rope_chunk(zk[:, sl], cos, sin_s, lo_half)
        k_ref[:, sl] = kr
        kb_ref[:, sl] = kr.astype(BF)
    zv = jnp.dot(hb, w_ref[:, Q_DIM + KV_DIM:IN_ATTN], preferred_element_type=F32)
    v_ref[...] = zv
    vb_ref[...] = zv.astype(BF)
    if dsa:
        qi_ref, rest_ref, restb_ref = out_refs[5:]
        zi = jnp.dot(hb, w_ref[:, IN_ATTN:IN_ATTN + IDX_Q_DIM], preferred_element_type=F32)
        for c in range(IDX_Q_DIM // LANES):
            sl = slice(c * LANES, (c + 1) * LANES)
            qi_ref[:, sl] = _rope_chunk(zi[:, sl], cos, sin_s, lo_half).astype(BF)
        zr = jnp.dot(hb, w_ref[:, IN_ATTN + IDX_Q_DIM:], preferred_element_type=F32)
        roped = _rope_chunk(zr, cos, sin_s, lo_half)
        rest = jnp.where(lane < IDX_DIM, roped, zr * IDX_W_SCALE)
        rest_ref[...] = rest
        restb_ref[...] = rest.astype(BF)


def _mod_spec(arr, tm, rows_per_seq):
    if arr.ndim == 3:
        return pl.BlockSpec((None, 1, D_MODEL), lambda i, *_: ((i * tm) // rows_per_seq, 0, 0))
    return pl.BlockSpec((tm, D_MODEL), lambda i, *_: (i, 0))


def _project(x, g, scale, shift, cos_tab, sin_tab, w, rows_per_seq, dsa):
    rows = x.shape[0]
    tm = min(ROW_TILE, rows)
    n_in = w.shape[1]
    tab_tiles = cos_tab.shape[0] // tm
    row_spec = lambda n: pl.BlockSpec((tm, n), lambda i: (i, 0))
    tab_spec = pl.BlockSpec((tm, LANES), lambda i: (i % tab_tiles, 0))
    out_shape = [jax.ShapeDtypeStruct((rows, Q_DIM), BF),
                 jax.ShapeDtypeStruct((rows, KV_DIM), F32), jax.ShapeDtypeStruct((rows, KV_DIM), F32),
                 jax.ShapeDtypeStruct((rows, KV_DIM), BF), jax.ShapeDtypeStruct((rows, KV_DIM), BF)]
    out_specs = [row_spec(Q_DIM), row_spec(KV_DIM), row_spec(KV_DIM), row_spec(KV_DIM), row_spec(KV_DIM)]
    if dsa:
        out_shape += [jax.ShapeDtypeStruct((rows, IDX_Q_DIM), BF),
                      jax.ShapeDtypeStruct((rows, LANES), F32), jax.ShapeDtypeStruct((rows, LANES), BF)]
        out_specs += [row_spec(IDX_Q_DIM), row_spec(LANES), row_spec(LANES)]
    return pl.pallas_call(
        functools.partial(_proj_kernel, dsa=dsa),
        grid=(rows // tm,),
        in_specs=[row_spec(D_MODEL),
                  pl.BlockSpec((1, D_MODEL), lambda i: (0, 0)),
                  _mod_spec(scale, tm, rows_per_seq), _mod_spec(shift, tm, rows_per_seq),
                  tab_spec, tab_spec,
                  pl.BlockSpec((D_MODEL, n_in), lambda i: (0, 0))],
        out_specs=out_specs,
        out_shape=out_shape,
        compiler_params=_params("parallel"),
        name="project_dsa" if dsa else "project",
    )(x, g, scale, shift, cos_tab, sin_tab, w)


def _post_kernel(x_ref, o_ref, wo_ref, ga_ref, g_ref, sc_ref, sh_ref, gm_ref, w1_ref, w2_ref, y_ref,
                 x1_scr, h_scr, acc_scr):
    j = pl.program_id(1)

    @pl.when(j == 0)
    def _():
        x1 = x_ref[...] + ga_ref[...] * jnp.dot(o_ref[...], wo_ref[...], preferred_element_type=F32)
        x1_scr[...] = x1
        h_scr[...] = _norm_mod(x1, g_ref[...], sc_ref[...], sh_ref[...]).astype(BF)
        acc_scr[...] = jnp.zeros_like(acc_scr)

    a = jnp.maximum(jnp.dot(h_scr[...], w1_ref[...], preferred_element_type=F32), 0.0)
    acc_scr[...] += jnp.dot((a * a).astype(BF), w2_ref[...], preferred_element_type=F32)

    @pl.when(j == pl.num_programs(1) - 1)
    def _():
        y_ref[...] = x1_scr[...] + gm_ref[...] * acc_scr[...]


def _post(x, o, w_out, gate_a, g_mlp, scale_m, shift_m, gate_m, w1, w2, rows_per_seq):
    rows = x.shape[0]
    tm = min(ROW_TILE, rows)
    tf = FF_TILE
    ms = lambda a: _mod_spec(a, tm, rows_per_seq)
    return pl.pallas_call(
        _post_kernel,
        grid=(rows // tm, D_FF // tf),
        in_specs=[pl.BlockSpec((tm, D_MODEL), lambda i, j: (i, 0)),
                  pl.BlockSpec((tm, Q_DIM), lambda i, j: (i, 0)),
                  pl.BlockSpec((Q_DIM, D_MODEL), lambda i, j: (0, 0)),
                  ms(gate_a),
                  pl.BlockSpec((1, D_MODEL), lambda i, j: (0, 0)),
                  ms(scale_m), ms(shift_m), ms(gate_m),
                  pl.BlockSpec((D_MODEL, tf), lambda i, j: (0, j)),
                  pl.BlockSpec((tf, D_MODEL), lambda i, j: (j, 0))],
        out_specs=pl.BlockSpec((tm, D_MODEL), lambda i, j: (i, 0)),
        out_shape=jax.ShapeDtypeStruct((rows, D_MODEL), F32),
        scratch_shapes=[pltpu.VMEM((tm, D_MODEL), F32), pltpu.VMEM((tm, D_MODEL), BF),
                        pltpu.VMEM((tm, D_MODEL), F32)],
        compiler_params=_params("parallel", "arbitrary"),
        name="post",
    )(x, o, w_out, gate_a, g_mlp, scale_m, shift_m, gate_m, w1, w2)


def _final_norm_kernel(x_ref, g_ref, y_ref):
    x = x_ref[...]
    y_ref[...] = x * lax.rsqrt(jnp.mean(x * x, axis=-1, keepdims=True) + NORM_EPS) * g_ref[...]


def _final_norm(x, g):
    rows = x.shape[0]
    tm = min(ROW_TILE, rows)
    return pl.pallas_call(
        _final_norm_kernel,
        grid=(rows // tm,),
        in_specs=[pl.BlockSpec((tm, D_MODEL), lambda i: (i, 0)), pl.BlockSpec((1, D_MODEL), lambda i: (0, 0))],
        out_specs=pl.BlockSpec((tm, D_MODEL), lambda i: (i, 0)),
        out_shape=jax.ShapeDtypeStruct((rows, D_MODEL), F32),
        compiler_params=_params("parallel"),
        name="final_norm",
    )(x, g)


def _block_mean_kernel(k_ref, o_ref):
    o_ref[...] = jnp.mean(k_ref[...], axis=0, keepdims=True)


def _block_means(k):
    nb = k.shape[0] // MOBA_BLOCK
    return pl.pallas_call(
        _block_mean_kernel,
        grid=(nb,),
        in_specs=[pl.BlockSpec((MOBA_BLOCK, KV_DIM), lambda i: (i, 0))],
        out_specs=pl.BlockSpec((None, 1, KV_DIM), lambda i: (i, 0, 0)),
        out_shape=jax.ShapeDtypeStruct((nb, 1, KV_DIM), F32),
        compiler_params=_params("parallel"),
        name="moba_block_means",
    )(k)


def _top_rows_bits(gs, ids, n_sel, axis):
    bits = 0
    for _ in range(n_sel):
        mx = jnp.max(gs, axis=axis, keepdims=True)
        am = jnp.min(jnp.where(gs == mx, ids, 2 ** 30), axis=axis, keepdims=True)
        bits = bits | jnp.where(mx > NEG_INF, jnp.left_shift(jnp.ones_like(am), am), 0)
        gs = jnp.where(ids == am, NEG_INF, gs)
    return bits


def _moba_prompt_kernel(qi_tab, kj_tab, last_tab, q_ref, k_ref, v_ref, km_ref, o_ref,
                        m_scr, l_scr, acc_scr, bits_scr, *, nblk, n_sel):
    p = pl.program_id(1)
    i = qi_tab[p]
    j = kj_tab[p]
    diag = (i == j).astype(I32)
    row1 = lax.broadcasted_iota(I32, (MOBA_BLOCK, 1), 0)
    col = lax.broadcasted_iota(I32, (MOBA_BLOCK, MOBA_BLOCK), 1)

    @pl.when(i == j)
    def _():
        m_scr[...] = jnp.full(m_scr.shape, NEG_INF, F32)
        l_scr[...] = jnp.zeros_like(l_scr)
        acc_scr[...] = jnp.zeros_like(acc_scr)
        blk_ids = lax.broadcasted_iota(I32, (MOBA_BLOCK, nblk), 1)
        for h in range(N_HEADS):
            kv = h // GROUP
            qh = q_ref[:, h * HEAD_DIM:(h + 1) * HEAD_DIM].astype(F32)
            km = km_ref[:, kv * HEAD_DIM:(kv + 1) * HEAD_DIM]
            gs = lax.dot_general(qh, km, NT, precision=lax.Precision.HIGHEST, preferred_element_type=F32)
            gs = jnp.where(blk_ids < i, gs, NEG_INF)
            bits_scr[h] = _top_rows_bits(gs, blk_ids, n_sel, 1)

    for h in range(N_HEADS):
        kv = h // GROUP
        hs = slice(h * HEAD_DIM, (h + 1) * HEAD_DIM)
        ks = slice(kv * HEAD_DIM, (kv + 1) * HEAD_DIM)
        s = lax.dot_general(q_ref[:, hs], k_ref[:, ks], NT, preferred_element_type=F32)
        picked = (jnp.right_shift(bits_scr[h], j) & 1) * (2 * MOBA_BLOCK) - MOBA_BLOCK
        limit = diag * row1 + (1 - diag) * picked
        s = jnp.where(col <= limit, s, NEG_INF)
        m_old = m_scr[h]
        m_new = jnp.maximum(m_old, jnp.max(s, axis=1, keepdims=True))
        alpha = jnp.exp(m_old - m_new)
        e = jnp.exp(s - m_new)
        l_scr[h] = alpha * l_scr[h] + jnp.sum(e, axis=1, keepdims=True)
        acc_scr[:, hs] = alpha * acc_scr[:, hs] + jnp.dot(e.astype(BF), v_ref[:, ks], preferred_element_type=F32)
        m_scr[h] = m_new

    @pl.when(last_tab[p] == 1)
    def _():
        for h in range(N_HEADS):
            hs = slice(h * HEAD_DIM, (h + 1) * HEAD_DIM)
            o_ref[:, hs] = (acc_scr[:, hs] * (1.0 / l_scr[h])).astype(BF)


def _moba_prompt(q, kb, vb, k, batch):
    rows = q.shape[0]
    seq = rows // batch
    assert seq % MOBA_BLOCK == 0
    nblk = seq // MOBA_BLOCK
    assert nblk <= 32
    n_sel = min(MOBA_TOPK, nblk)
    kmean = _block_means(k).reshape(batch, nblk, KV_DIM)
    qi, kj, last = [], [], []
    for i in range(nblk):
        order = [i] + list(range(i))
        qi += [i] * len(order)
        kj += order
        last += [0] * (len(order) - 1) + [1]
    tabs = [jnp.asarray(np.array(t, np.int32)) for t in (qi, kj, last)]
    grid_spec = pltpu.PrefetchScalarGridSpec(
        num_scalar_prefetch=3,
        grid=(batch, len(qi)),
        in_specs=[pl.BlockSpec((MOBA_BLOCK, Q_DIM), lambda b, p, qi, kj, la: (b * nblk + qi[p], 0)),
                  pl.BlockSpec((MOBA_BLOCK, KV_DIM), lambda b, p, qi, kj, la: (b * nblk + kj[p], 0)),
                  pl.BlockSpec((MOBA_BLOCK, KV_DIM), lambda b, p, qi, kj, la: (b * nblk + kj[p], 0)),
                  pl.BlockSpec((None, nblk, KV_DIM), lambda b, p, qi, kj, la: (b, 0, 0))],
        out_specs=pl.BlockSpec((MOBA_BLOCK, Q_DIM), lambda b, p, qi, kj, la: (b * nblk + qi[p], 0)),
        scratch_shapes=[pltpu.VMEM((N_HEADS, MOBA_BLOCK, 1), F32), pltpu.VMEM((N_HEADS, MOBA_BLOCK, 1), F32),
                        pltpu.VMEM((MOBA_BLOCK, Q_DIM), F32), pltpu.VMEM((N_HEADS, MOBA_BLOCK, 1), I32)])
    return pl.pallas_call(
        functools.partial(_moba_prompt_kernel, nblk=nblk, n_sel=n_sel),
        grid_spec=grid_spec,
        out_shape=jax.ShapeDtypeStruct((rows, Q_DIM), BF),
        compiler_params=_params("parallel", "arbitrary"),
        name="moba_prompt",
    )(*tabs, q, kb, vb, kmean)


def _float_key(x):
    b = pltpu.bitcast(x, I32)
    return jnp.where(b < 0, b ^ 0x7FFFFFFF, b)


def _kth_largest_key(count_ge, k_top, shape):
    def body(it, cur):
        cand = cur | jnp.left_shift(jnp.int32(1), 31 - it)
        return jnp.where(count_ge(cand ^ INT_MIN) >= k_top, cand, cur)
    return lax.fori_loop(0, 32, body, jnp.zeros(shape, I32)) ^ INT_MIN


def _tie_cutoff(count_eq_below, need, n_bits, shape):
    def body(it, cur):
        cand = cur | jnp.left_shift(jnp.int32(1), n_bits - 1 - it)
        return jnp.where(count_eq_below(cand) < need, cand, cur)
    return lax.fori_loop(0, n_bits, body, jnp.zeros(shape, I32))


def _dsa_prompt_kernel(q_ref, qi_ref, w_ref, ki_ref, k_ref, v_ref, o_ref, key_scr, m_scr, l_scr, acc_scr,
                       *, k_top, n_bits):
    i = pl.program_id(1)
    tq, ch = DSA_Q_BLOCK, DSA_KEY_CHUNK
    n_ch = ((i + 1) * tq + ch - 1) // ch
    qpos = i * tq + lax.broadcasted_iota(I32, (tq, 1), 0)
    col = lax.broadcasted_iota(I32, (tq, ch), 1)
    lane = lax.broadcasted_iota(I32, (tq, LANES), 1)

    def score_body(c, carry):
        kc = ki_ref[pl.ds(pl.multiple_of(c * ch, ch), ch), :][:, :IDX_DIM]
        sc = jnp.zeros((tq, ch), F32)
        for h in range(IDX_HEADS):
            d = lax.dot_general(qi_ref[:, h * IDX_DIM:(h + 1) * IDX_DIM], kc, NT, preferred_element_type=F32)
            sc = sc + w_ref[:, IDX_DIM + h:IDX_DIM + h + 1] * jnp.maximum(d, 0.0)
        sc = jnp.where(c * ch + col <= qpos, sc, NEG_INF)
        key_scr[c] = _float_key(sc)
        return carry
    lax.fori_loop(0, n_ch, score_body, 0)

    def count(pred):
        def body(c, acc):
            key = key_scr[c]
            for s in range(ch // LANES):
                hit = pred(key[:, s * LANES:(s + 1) * LANES], c * ch + s * LANES + lane)
                acc = acc + hit.astype(I32)
            return acc
        return jnp.sum(lax.fori_loop(0, n_ch, body, jnp.zeros((tq, LANES), I32)), axis=1, keepdims=True)

    thr = _kth_largest_key(lambda t: count(lambda key, idx: key >= t), k_top, (tq, 1))
    n_gt = count(lambda key, idx: key > thr)
    n_ge = count(lambda key, idx: key >= thr)
    need = k_top - n_gt
    tie_rows = ((n_ge > k_top) & (thr != KEY_NEG_INF)).astype(I32)
    cutoff = lax.cond(
        jnp.max(tie_rows) > 0,
        lambda: _tie_cutoff(lambda c_: count(lambda key, idx: (key == thr) & (idx < c_)), need, n_bits, (tq, 1)),
        lambda: jnp.full((tq, 1), 2 ** 30, I32))

    m_scr[...] = jnp.full(m_scr.shape, NEG_INF, F32)
    l_scr[...] = jnp.zeros_like(l_scr)
    acc_scr[...] = jnp.zeros_like(acc_scr)

    def attn_body(c, carry):
        key = key_scr[c]
        idx = c * ch + col
        keep = ((key > thr) | ((key == thr) & (idx <= cutoff))) & (idx <= qpos)
        bias = jnp.where(keep, 0.0, NEG_INF)
        rows = pl.ds(pl.multiple_of(c * ch, ch), ch)
        for h in range(N_HEADS):
            kv = h // GROUP
            hs = slice(h * HEAD_DIM, (h + 1) * HEAD_DIM)
            ks = slice(kv * HEAD_DIM, (kv + 1) * HEAD_DIM)
            s = lax.dot_general(q_ref[:, hs], k_ref[rows, :][:, ks], NT, preferred_element_type=F32) + bias
            m_old = m_scr[h]
            m_new = jnp.maximum(m_old, jnp.max(s, axis=1, keepdims=True))
            m_safe = jnp.where(m_new == NEG_INF, 0.0, m_new)
            alpha = jnp.exp(m_old - m_safe)
            e = jnp.exp(s - m_safe)
            l_scr[h] = alpha * l_scr[h] + jnp.sum(e, axis=1, keepdims=True)
            acc_scr[:, hs] = alpha * acc_scr[:, hs] + jnp.dot(e.astype(BF), v_ref[rows, :][:, ks],
                                                             preferred_element_type=F32)
            m_scr[h] = m_new
        return carry
    lax.fori_loop(0, n_ch, attn_body, 0)

    for h in range(N_HEADS):
        hs = slice(h * HEAD_DIM, (h + 1) * HEAD_DIM)
        o_ref[:, hs] = (acc_scr[:, hs] * (1.0 / l_scr[h])).astype(BF)


def _dsa_prompt(q, qi, rest, restb, kb, vb, batch):
    rows = q.shape[0]
    seq = rows // batch
    tq, ch = DSA_Q_BLOCK, DSA_KEY_CHUNK
    k_top = min(DSA_TOPK_MAX, seq // 4)
    assert seq % ch == 0 and ch >= k_top and ch % tq == 0
    nq = seq // tq
    n_bits = max(1, (seq - 1).bit_length())
    return pl.pallas_call(
        functools.partial(_dsa_prompt_kernel, k_top=k_top, n_bits=n_bits),
        grid=(batch, nq),
        in_specs=[pl.BlockSpec((tq, Q_DIM), lambda b, i: (b * nq + i, 0)),
                  pl.BlockSpec((tq, IDX_Q_DIM), lambda b, i: (b * nq + i, 0)),
                  pl.BlockSpec((tq, LANES), lambda b, i: (b * nq + i, 0)),
                  pl.BlockSpec((seq, LANES), lambda b, i: (b, 0)),
                  pl.BlockSpec((seq, KV_DIM), lambda b, i: (b, 0)),
                  pl.BlockSpec((seq, KV_DIM), lambda b, i: (b, 0))],
        out_specs=pl.BlockSpec((tq, Q_DIM), lambda b, i: (b * nq + i, 0)),
        out_shape=jax.ShapeDtypeStruct((rows, Q_DIM), BF),
        scratch_shapes=[pltpu.VMEM((seq // ch, tq, ch), I32),
                        pltpu.VMEM((N_HEADS, tq, 1), F32), pltpu.VMEM((N_HEADS, tq, 1), F32),
                        pltpu.VMEM((tq, Q_DIM), F32)],
        compiler_params=_params("parallel", "arbitrary"),
        name="dsa_prompt",
    )(q, qi, rest, restb, kb, vb)


def _swa_prompt_kernel(q_ref, kp_ref, kc_ref, vp_ref, vc_ref, sink_ref, o_ref):
    n = pl.program_id(1)
    w = WINDOW
    row = lax.broadcasted_iota(I32, (w, 2 * w), 0)
    col = lax.broadcasted_iota(I32, (w, 2 * w), 1)
    keep = (col >= row) & (col <= row + w) & ((col >= w) | (n > 0))
    for kv in range(N_KV_HEADS):
        ks = slice(kv * HEAD_DIM, (kv + 1) * HEAD_DIM)
        kk = jnp.concatenate([kp_ref[:, ks], kc_ref[:, ks]], axis=0)
        vv = jnp.concatenate([vp_ref[:, ks], vc_ref[:, ks]], axis=0)
        for g in range(GROUP):
            h = kv * GROUP + g
            hs = slice(h * HEAD_DIM, (h + 1) * HEAD_DIM)
            s = jnp.where(keep, lax.dot_general(q_ref[:, hs], kk, NT, preferred_element_type=F32), NEG_INF)
            sink = sink_ref[:, h:h + 1]
            m = jnp.maximum(jnp.max(s, axis=1, keepdims=True), sink)
            e = jnp.exp(s - m)
            den = jnp.sum(e, axis=1, keepdims=True) + jnp.exp(sink - m)
            o = jnp.dot(e.astype(BF), vv, preferred_element_type=F32)
            o_ref[:, hs] = (o * (1.0 / den)).astype(BF)


def _swa_prompt(q, kb, vb, sinks, batch):
    rows = q.shape[0]
    seq = rows // batch
    w = WINDOW
    assert seq % w == 0
    nb = seq // w
    cur = lambda b, n: (b * nb + n, 0)
    prev = lambda b, n: (b * nb + jnp.maximum(n - 1, 0), 0)
    return pl.pallas_call(
        _swa_prompt_kernel,
        grid=(batch, nb),
        in_specs=[pl.BlockSpec((w, Q_DIM), cur),
                  pl.BlockSpec((w, KV_DIM), prev), pl.BlockSpec((w, KV_DIM), cur),
                  pl.BlockSpec((w, KV_DIM), prev), pl.BlockSpec((w, KV_DIM), cur),
                  pl.BlockSpec((1, N_HEADS), lambda b, n: (0, 0))],
        out_specs=pl.BlockSpec((w, Q_DIM), cur),
        out_shape=jax.ShapeDtypeStruct((rows, Q_DIM), BF),
        compiler_params=_params("parallel", "parallel"),
        name="swa_prompt",
    )(q, kb, kb, vb, vb, sinks.reshape(1, N_HEADS))


def _row_token(n_rows, n_tok):
    r = lax.broadcasted_iota(I32, (n_rows, 1), 0)
    return (r // GROUP) % n_tok


def _block_diag_queries(q, n_seq, n_tok):
    qr = q.reshape(n_seq, n_tok, N_KV_HEADS, GROUP, HEAD_DIM).transpose(0, 2, 1, 3, 4)
    qr = qr.reshape(n_seq, N_KV_HEADS, n_tok * GROUP, HEAD_DIM)
    eye = jnp.eye(N_KV_HEADS, dtype=q.dtype)
    qbd = qr[:, :, :, None, :] * eye[None, :, None, :, None]
    return qbd.reshape(n_seq, N_KV_HEADS * n_tok * GROUP, KV_DIM)


def _undiag_outputs(out, n_seq, n_tok):
    o = out.reshape(n_seq, N_KV_HEADS, n_tok, GROUP, N_KV_HEADS, HEAD_DIM)
    o = jnp.stack([o[:, kv, :, :, kv] for kv in range(N_KV_HEADS)], axis=1)
    return o.transpose(0, 2, 1, 3, 4).reshape(n_seq * n_tok, Q_DIM)


def _tokens_on_lanes(cache):
    if cache.ndim == 5:
        t = jnp.transpose(cache, (0, 1, 3, 4, 2))
        return t.reshape(t.shape[:2] + (t.shape[2] * t.shape[3], t.shape[4]))
    return jnp.transpose(cache, (0, 1, 3, 2))


def _page_specs(n_pages_per_step, slot, width):
    def spec(r):
        return pl.BlockSpec((None, None, width, PAGE_SIZE),
                            lambda b, s, pt: (slot, pt[b, s * n_pages_per_step + r], 0, 0))
    return [spec(r) for r in range(n_pages_per_step)]


def _pad_rows(a, n_seq, n_tok):
    a = a.reshape(n_seq, n_tok, a.shape[-1])
    return jnp.pad(a, ((0, 0), (0, SUBLANES - n_tok), (0, 0)))


def _moba_sample_kernel(pt_ref, q_ref, kn_ref, vn_ref, *refs, n_tok, n_sel, pps):
    kt_pages = refs[:pps]
    vt_pages = refs[pps:2 * pps]
    o_ref = refs[2 * pps]
    m_scr, l_scr, g_scr, o_scr = refs[2 * pps + 1:]
    s = pl.program_id(1)
    nblk = m_scr.shape[0]
    q = q_ref[...]
    n_row = q.shape[0]
    for r in range(pps // 2):
        kt = jnp.concatenate([kt_pages[2 * r][...], kt_pages[2 * r + 1][...]], axis=1).astype(BF)
        vt = jnp.concatenate([vt_pages[2 * r][...], vt_pages[2 * r + 1][...]], axis=1).astype(BF)
        sc = jnp.dot(q, kt, preferred_element_type=F32)
        m = jnp.max(sc, axis=1, keepdims=True)
        e = jnp.exp(sc - m)
        blk = s * (pps // 2) + r
        m_scr[blk] = m
        l_scr[blk] = jnp.sum(e, axis=1, keepdims=True)
        g_scr[blk] = jnp.sum(sc, axis=1, keepdims=True)
        o_scr[blk] = lax.dot_general(e.astype(BF), vt, NT, preferred_element_type=F32)

    @pl.when(s == pl.num_programs(1) - 1)
    def _():
        blk_ids = lax.broadcasted_iota(I32, (nblk, n_row, 1), 0)
        bits = _top_rows_bits(g_scr[...], blk_ids, n_sel, 0)
        picked = (jnp.right_shift(bits, blk_ids) & 1) == 1
        m_all = m_scr[...]
        own_ok = lax.broadcasted_iota(I32, (n_row, SUBLANES), 1) <= _row_token(n_row, n_tok)
        s_own = jnp.where(own_ok, lax.dot_general(q, kn_ref[...], NT, preferred_element_type=F32), NEG_INF)
        m_tot = jnp.maximum(jnp.max(jnp.where(picked, m_all, NEG_INF), axis=0),
                            jnp.max(s_own, axis=1, keepdims=True))
        wgt = jnp.where(picked, jnp.exp(m_all - m_tot), 0.0)
        e_own = jnp.exp(s_own - m_tot)
        den = jnp.sum(wgt * l_scr[...], axis=0) + jnp.sum(e_own, axis=1, keepdims=True)
        num = jnp.dot(e_own.astype(BF), vn_ref[...], preferred_element_type=F32)
        for n in range(nblk):
            num = num + wgt[n] * o_scr[n]
        o_ref[...] = num * (1.0 / den)


def _paged_attention_call(kernel, name, slot, page_table, cache_kt, cache_vt, extra_inputs, extra_specs,
                          scratch_shapes, n_row):
    n_seq, n_pages = page_table.shape
    pps = PAGES_PER_STEP
    assert n_pages % pps == 0
    grid_spec = pltpu.PrefetchScalarGridSpec(
        num_scalar_prefetch=1,
        grid=(n_seq, n_pages // pps),
        in_specs=extra_specs + _page_specs(pps, slot, KV_DIM) + _page_specs(pps, slot, KV_DIM),
        out_specs=pl.BlockSpec((None, n_row, KV_DIM), lambda b, s, pt: (b, 0, 0)),
        scratch_shapes=scratch_shapes)
    return pl.pallas_call(
        kernel,
        grid_spec=grid_spec,
        out_shape=jax.ShapeDtypeStruct((n_seq, n_row, KV_DIM), F32),
        compiler_params=_params("parallel", "arbitrary"),
        name=name,
    )(page_table, *extra_inputs, *([cache_kt] * pps), *([cache_vt] * pps))


def _moba_sample(qbd, kn_pad, vn_pad, cache_kt, cache_vt, slot, page_table, n_tok):
    n_seq, n_pages = page_table.shape
    past = n_pages * PAGE_SIZE
    assert past % MOBA_BLOCK == 0 and MOBA_BLOCK == 2 * PAGE_SIZE and PAGES_PER_STEP % 2 == 0
    nblk = past // MOBA_BLOCK
    assert nblk <= 32
    n_row = qbd.shape[1]
    per_seq = lambda w: pl.BlockSpec((None, w[0], w[1]), lambda b, s, pt: (b, 0, 0))
    kernel = functools.partial(_moba_sample_kernel, n_tok=n_tok, n_sel=min(MOBA_TOPK, nblk), pps=PAGES_PER_STEP)
    return _paged_attention_call(
        kernel, "moba_sample", slot, page_table, cache_kt, cache_vt,
        [qbd, kn_pad, vn_pad],
        [per_seq((n_row, KV_DIM)), per_seq((SUBLANES, KV_DIM)), per_seq((SUBLANES, KV_DIM))],
        [pltpu.VMEM((nblk, n_row, 1), F32), pltpu.VMEM((nblk, n_row, 1), F32), pltpu.VMEM((nblk, n_row, 1), F32),
         pltpu.VMEM((nblk, n_row, KV_DIM), F32)],
        n_row)


def _dsa_index_kernel(pt_ref, qi_ref, w_ref, kin_ref, *refs, n_tok, k_top, ppi, n_bits):
    ki_pages = refs[:ppi]
    mask_ref = refs[ppi]
    key_scr = refs[ppi + 1]
    s = pl.program_id(1)
    n_steps = pl.num_programs(1)
    n_pages = n_steps * ppi
    qi = qi_ref[...]
    w = w_ref[...]
    fill = jnp.full((SUBLANES - n_tok, LANES), NEG_INF, F32)

    def scores(d):
        wd = (w * jnp.maximum(d, 0.0)).reshape(n_tok, IDX_HEADS, LANES)
        return jnp.sum(wd, axis=1)

    for r in range(ppi):
        sc = scores(jnp.dot(qi, ki_pages[r][...].astype(BF), preferred_element_type=F32))
        key_scr[s * ppi + r] = _float_key(jnp.concatenate([sc, fill], axis=0))

    @pl.when(s == n_steps - 1)
    def _():
        tok = lax.broadcasted_iota(I32, (n_tok, LANES), 0)
        lane_t = lax.broadcasted_iota(I32, (n_tok, LANES), 1)
        sc_new = scores(lax.dot_general(qi, kin_ref[...], NT, preferred_element_type=F32))
        sc_new = jnp.where(lane_t <= tok, sc_new, NEG_INF)
        key_scr[n_pages] = _float_key(jnp.concatenate([sc_new, fill], axis=0))
        lane = lax.broadcasted_iota(I32, (SUBLANES, LANES), 1)

        def count(pred):
            def body(c, acc):
                return acc + pred(key_scr[c], c * PAGE_SIZE + lane).astype(I32)
            return jnp.sum(lax.fori_loop(0, n_pages + 1, body, jnp.zeros((SUBLANES, LANES), I32)),
                           axis=1, keepdims=True)

        thr = _kth_largest_key(lambda t: count(lambda key, idx: key >= t), k_top, (SUBLANES, 1))
        need = k_top - count(lambda key, idx: key > thr)
        cutoff = _tie_cutoff(lambda c_: count(lambda key, idx: (key == thr) & (idx < c_)), need, n_bits,
                             (SUBLANES, 1))

        def emit(c, carry):
            key = key_scr[c]
            keep = (key > thr) | ((key == thr) & (c * PAGE_SIZE + lane <= cutoff))
            mask_ref[c] = jnp.where(keep & (key != KEY_NEG_INF), 1.0, 0.0).astype(BF)
            return carry
        lax.fori_loop(0, n_pages + 1, emit, 0)


def _dsa_index(qi_rows, w_rows, ki_new, cache_idx, islot, page_table, n_tok):
    n_seq, n_pages = page_table.shape
    ppi = PAGES_PER_STEP
    total = n_pages * PAGE_SIZE + n_tok
    k_top = min(DSA_TOPK_MAX, total // 4)
    assert n_pages % ppi == 0 and n_tok <= SUBLANES and n_pages * PAGE_SIZE >= k_top
    n_bits = max(1, ((n_pages + 1) * PAGE_SIZE - 1).bit_length())
    nr = n_tok * IDX_HEADS
    grid_spec = pltpu.PrefetchScalarGridSpec(
        num_scalar_prefetch=1,
        grid=(n_seq, n_pages // ppi),
        in_specs=[pl.BlockSpec((None, nr, IDX_DIM), lambda b, s, pt: (b, 0, 0)),
                  pl.BlockSpec((None, nr, 1), lambda b, s, pt: (b, 0, 0)),
                  pl.BlockSpec((None, LANES, IDX_DIM), lambda b, s, pt: (b, 0, 0))]
        + _page_specs(ppi, islot, IDX_DIM),
        out_specs=pl.BlockSpec((None, n_pages + 1, SUBLANES, LANES), lambda b, s, pt: (b, 0, 0, 0)),
        scratch_shapes=[pltpu.VMEM((n_pages + 1, SUBLANES, LANES), I32)])
    return pl.pallas_call(
        functools.partial(_dsa_index_kernel, n_tok=n_tok, k_top=k_top, ppi=ppi, n_bits=n_bits),
        grid_spec=grid_spec,
        out_shape=jax.ShapeDtypeStruct((n_seq, n_pages + 1, SUBLANES, LANES), BF),
        compiler_params=_params("parallel", "arbitrary"),
        name="dsa_index",
    )(page_table, qi_rows, w_rows, ki_new, *([cache_idx] * ppi))


def _dsa_sample_kernel(pt_ref, q_ref, kn_ref, vn_ref, mask_ref, mnew_ref, *refs, n_tok, pps):
    kt_pages = refs[:pps]
    vt_pages = refs[pps:2 * pps]
    o_ref = refs[2 * pps]
    m_scr, l_scr, acc_scr = refs[2 * pps + 1:]
    s = pl.program_id(1)
    q = q_ref[...]
    n_row = q.shape[0]
    expand = (lax.broadcasted_iota(I32, (n_row, SUBLANES), 1) == _row_token(n_row, n_tok)).astype(BF)

    @pl.when(s == 0)
    def _():
        m_scr[...] = jnp.full(m_scr.shape, NEG_INF, F32)
        l_scr[...] = jnp.zeros_like(l_scr)
        acc_scr[...] = jnp.zeros_like(acc_scr)

    def update(sc, keep, pv):
        sc = jnp.where(keep > 0.5, sc, NEG_INF)
        m_old = m_scr[...]
        m_new = jnp.maximum(m_old, jnp.max(sc, axis=1, keepdims=True))
        m_safe = jnp.where(m_new == NEG_INF, 0.0, m_new)
        alpha = jnp.exp(m_old - m_safe)
        e = jnp.exp(sc - m_safe)
        l_scr[...] = alpha * l_scr[...] + jnp.sum(e, axis=1, keepdims=True)
        acc_scr[...] = alpha * acc_scr[...] + pv(e.astype(BF))
        m_scr[...] = m_new

    for r in range(pps):
        sc = jnp.dot(q, kt_pages[r][...].astype(BF), preferred_element_type=F32)
        keep = jnp.dot(expand, mask_ref[r], preferred_element_type=F32)
        vt = vt_pages[r][...].astype(BF)
        update(sc, keep, lambda e, vt=vt: lax.dot_general(e, vt, NT, preferred_element_type=F32))

    @pl.when(s == pl.num_programs(1) - 1)
    def _():
        sc = lax.dot_general(q, kn_ref[...], NT, preferred_element_type=F32)
        keep = jnp.dot(expand, mnew_ref[...], preferred_element_type=F32)[:, :SUBLANES]
        update(sc, keep, lambda e: jnp.dot(e, vn_ref[...], preferred_element_type=F32))
        o_ref[...] = acc_scr[...] * (1.0 / l_scr[...])


def _dsa_sample(qbd, kn_pad, vn_pad, mask, cache_kt, cache_vt, slot, page_table, n_tok):
    n_seq, n_pages = page_table.shape
    n_row = qbd.shape[1]
    pps = PAGES_PER_STEP
    per_seq = lambda w: pl.BlockSpec((None, w[0], w[1]), lambda b, s, pt: (b, 0, 0))
    return _paged_attention_call(
        functools.partial(_dsa_sample_kernel, n_tok=n_tok, pps=pps), "dsa_sample", slot, page_table,
        cache_kt, cache_vt,
        [qbd, kn_pad, vn_pad, mask, mask],
        [per_seq((n_row, KV_DIM)), per_seq((SUBLANES, KV_DIM)), per_seq((SUBLANES, KV_DIM)),
         pl.BlockSpec((None, pps, SUBLANES, LANES), lambda b, s, pt: (b, s, 0, 0)),
         pl.BlockSpec((None, None, SUBLANES, LANES), lambda b, s, pt: (b, n_pages, 0, 0))],
        [pltpu.VMEM((n_row, 1), F32), pltpu.VMEM((n_row, 1), F32), pltpu.VMEM((n_row, KV_DIM), F32)],
        n_row)


def _swa_sample_kernel(q_ref, kt_ref, vt_ref, kn_ref, vn_ref, sink_ref, o_ref, *, n_tok):
    q = q_ref[...]
    n_row = q.shape[0]
    tok_r = _row_token(n_row, n_tok)
    wk = kt_ref.shape[1]
    s_buf = jnp.dot(q, kt_ref[...].astype(BF), preferred_element_type=F32)
    s_buf = jnp.where(lax.broadcasted_iota(I32, (n_row, wk), 1) >= tok_r, s_buf, NEG_INF)
    s_new = lax.dot_general(q, kn_ref[...], NT, preferred_element_type=F32)
    s_new = jnp.where(lax.broadcasted_iota(I32, (n_row, SUBLANES), 1) <= tok_r, s_new, NEG_INF)
    sink = sink_ref[...]
    m = jnp.maximum(jnp.maximum(jnp.max(s_buf, axis=1, keepdims=True), jnp.max(s_new, axis=1, keepdims=True)), sink)
    e_buf = jnp.exp(s_buf - m)
    e_new = jnp.exp(s_new - m)
    den = jnp.sum(e_buf, axis=1, keepdims=True) + jnp.sum(e_new, axis=1, keepdims=True) + jnp.exp(sink - m)
    num = (lax.dot_general(e_buf.astype(BF), vt_ref[...].astype(BF), NT, preferred_element_type=F32)
           + jnp.dot(e_new.astype(BF), vn_ref[...], preferred_element_type=F32))
    o_ref[...] = num * (1.0 / den)


def _swa_sample(qbd, kn_pad, vn_pad, buf_kt, buf_vt, sinks, n_tok):
    n_seq, _, wk = buf_kt.shape
    assert wk == WINDOW
    n_row = qbd.shape[1]
    sink_rows = jnp.repeat(sinks.reshape(N_KV_HEADS, 1, GROUP), n_tok, axis=1).reshape(n_row, 1)
    per_seq = lambda a, c: pl.BlockSpec((None, a, c), lambda b: (b, 0, 0))
    return pl.pallas_call(
        functools.partial(_swa_sample_kernel, n_tok=n_tok),
        grid=(n_seq,),
        in_specs=[per_seq(n_row, KV_DIM), per_seq(KV_DIM, wk), per_seq(KV_DIM, wk),
                  per_seq(SUBLANES, KV_DIM), per_seq(SUBLANES, KV_DIM),
                  pl.BlockSpec((n_row, 1), lambda b: (0, 0))],
        out_specs=per_seq(n_row, KV_DIM),
        out_shape=jax.ShapeDtypeStruct((n_seq, n_row, KV_DIM), F32),
        compiler_params=_params("parallel"),
        name="swa_sample",
    )(qbd, buf_kt, buf_vt, kn_pad, vn_pad, sink_rows)


def _rope_tables(pos):
    half = HEAD_DIM // 2
    inv = ROPE_THETA ** (-jnp.arange(half, dtype=F32) / half)
    ang = pos.astype(F32)[:, None] * inv[None, :]
    cos, sin = jnp.cos(ang), jnp.sin(ang)
    reps = LANES // HEAD_DIM
    return jnp.tile(jnp.concatenate([cos, cos], axis=1), (1, reps)), jnp.tile(jnp.concatenate([-sin, sin], axis=1), (1, reps))


def kernel(x_prompt, x_sample, cache_k, cache_v, cache_idx_k, state_swa_k, state_swa_v, page_table, c_prompt, c_sample, g_attn, g_mlp, w_mod, b_mod, w_in_moba, w_in_dsa, w_in_swa, swa_sinks, w_out, w_ff1, w_ff2, g_final):
    batch, seq, _ = x_prompt.shape
    n_seq, n_tok, _ = x_sample.shape
    depth = g_attn.shape[0]
    n_pages = page_table.shape[1]
    past = n_pages * PAGE_SIZE
    rows_p, rows_s = batch * seq, n_seq * n_tok

    n_cond = batch + n_seq
    pad_cond = -n_cond % SUBLANES
    c_all = jnp.pad(jnp.concatenate([c_prompt, c_sample], axis=0), ((0, pad_cond), (0, 0)))
    mod = _adaln(c_all, w_mod, b_mod)

    def mods(layer):
        parts = [mod[layer, :, r * D_MODEL:(r + 1) * D_MODEL] for r in range(6)]
        mp = [p[:batch].reshape(batch, 1, D_MODEL) for p in parts]
        ms = [jnp.repeat(p[batch:n_cond], n_tok, axis=0) for p in parts]
        return mp, ms

    cos_p, sin_p = _rope_tables(jnp.arange(seq, dtype=jnp.int32))
    cos_s, sin_s = _rope_tables(jnp.tile(past + jnp.arange(n_tok, dtype=jnp.int32), n_seq))

    cache_kt, cache_vt, cache_it = _tokens_on_lanes(cache_k), _tokens_on_lanes(cache_v), _tokens_on_lanes(cache_idx_k)
    swa_kt, swa_vt = _tokens_on_lanes(state_swa_k), _tokens_on_lanes(state_swa_v)

    xp = x_prompt.reshape(rows_p, D_MODEL)
    xs = x_sample.reshape(rows_s, D_MODEL)
    kp_l, vp_l, ks_l, vs_l, ip_l, is_l = [], [], [], [], [], []
    skp_l, svp_l, sks_l, svs_l = [], [], [], []
    for i, (mixer, j, slot) in enumerate(_layer_plan(depth)):
        (sh_ap, sc_ap, gt_ap, sh_mp, sc_mp, gt_mp), (sh_as, sc_as, gt_as, sh_ms, sc_ms, gt_ms) = mods(i)
        dsa = mixer == 1
        if mixer == 0:
            w_in = w_in_moba[j]
        elif dsa:
            w_in = jnp.pad(w_in_dsa[j], ((0, 0), (0, -IN_DSA % LANES)))
        else:
            w_in = w_in_swa[j]
        w_in = w_in.astype(BF)
        g_a = g_attn[i].reshape(1, D_MODEL)
        outs_p = _project(xp, g_a, sc_ap, sh_ap, cos_p, sin_p, w_in, seq, dsa)
        outs_s = _project(xs, g_a, sc_as, sh_as, cos_s, sin_s, w_in, n_tok, dsa)
        q_p, k_p, v_p, kb_p, vb_p = outs_p[:5]
        q_s, k_s, v_s, kb_s, vb_s = outs_s[:5]
        qbd = _block_diag_queries(q_s, n_seq, n_tok)
        kn_pad = _pad_rows(kb_s, n_seq, n_tok)
        vn_pad = _pad_rows(vb_s, n_seq, n_tok)
        k_p5 = k_p.reshape(batch, seq, N_KV_HEADS, HEAD_DIM)
        v_p5 = v_p.reshape(batch, seq, N_KV_HEADS, HEAD_DIM)
        k_s5 = k_s.reshape(n_seq, n_tok, N_KV_HEADS, HEAD_DIM)
        v_s5 = v_s.reshape(n_seq, n_tok, N_KV_HEADS, HEAD_DIM)
        if mixer == 0:
            o_p = _moba_prompt(q_p, kb_p, vb_p, k_p, batch)
            o_t = _moba_sample(qbd, kn_pad, vn_pad, cache_kt, cache_vt, slot, page_table, n_tok)
        elif dsa:
            qi_p, rest_p, restb_p = outs_p[5:]
            qi_s, rest_s, restb_s = outs_s[5:]
            o_p = _dsa_prompt(q_p, qi_p, rest_p, restb_p, kb_p, vb_p, batch)
            qi_rows = qi_s.reshape(n_seq, n_tok * IDX_HEADS, IDX_DIM)
            w_rows = rest_s[:, IDX_DIM:IDX_DIM + IDX_HEADS].reshape(n_seq, n_tok * IDX_HEADS, 1)
            ki_new = jnp.pad(restb_s[:, :IDX_DIM].reshape(n_seq, n_tok, IDX_DIM),
                             ((0, 0), (0, LANES - n_tok), (0, 0)))
            mask = _dsa_index(qi_rows, w_rows, ki_new, cache_it, j, page_table, n_tok)
            o_t = _dsa_sample(qbd, kn_pad, vn_pad, mask, cache_kt, cache_vt, slot, page_table, n_tok)
            ip_l.append(rest_p[:, :IDX_DIM].reshape(batch, seq, IDX_DIM))
            is_l.append(rest_s[:, :IDX_DIM].reshape(n_seq, n_tok, IDX_DIM))
        else:
            o_p = _swa_prompt(q_p, kb_p, vb_p, swa_sinks[j], batch)
            o_t = _swa_sample(qbd, kn_pad, vn_pad, swa_kt[j], swa_vt[j], swa_sinks[j], n_tok)
            keep_p = min(WINDOW, seq)
            skp_l.append(k_p5[:, seq - keep_p:])
            svp_l.append(v_p5[:, seq - keep_p:])
            wk = state_swa_k.shape[2]
            sks_l.append(jnp.concatenate([state_swa_k[j], k_s5], axis=1)[:, -wk:])
            svs_l.append(jnp.concatenate([state_swa_v[j], v_s5], axis=1)[:, -wk:])
        if mixer < 2:
            kp_l.append(k_p5)
            vp_l.append(v_p5)
            ks_l.append(k_s5)
            vs_l.append(v_s5)
        o_s = _undiag_outputs(o_t, n_seq, n_tok).astype(BF)
        wo, w1, w2 = w_out[i].astype(BF), w_ff1[i].astype(BF), w_ff2[i].astype(BF)
        g_m = g_mlp[i].reshape(1, D_MODEL)
        xp = _post(xp, o_p, wo, gt_ap, g_m, sc_mp, sh_mp, gt_mp, w1, w2, seq)
        xs = _post(xs, o_s, wo, gt_as, g_m, sc_ms, sh_ms, gt_ms, w1, w2, n_tok)
    g_f = g_final.reshape(1, D_MODEL)
    y_prompt = _final_norm(xp, g_f).reshape(batch, seq, D_MODEL)
    y_sample = _final_norm(xs, g_f).reshape(n_seq, n_tok, D_MODEL)
    return (y_prompt, y_sample, jnp.stack(kp_l), jnp.stack(vp_l), jnp.stack(ks_l), jnp.stack(vs_l),
            jnp.stack(ip_l), jnp.stack(is_l), jnp.stack(skp_l), jnp.stack(svp_l), jnp.stack(sks_l), jnp.stack(svs_l))
```

```python
import functools

import numpy as np
import jax
import jax.numpy as jnp
from jax import lax
from jax.experimental import pallas as pl
from jax.experimental.pallas import tpu as pltpu

D_MODEL = 1024
N_HEADS = 16
HEAD_DIM = D_MODEL // N_HEADS
N_KV_HEADS = 4
GROUP = N_HEADS // N_KV_HEADS
Q_DIM = N_HEADS * HEAD_DIM
KV_DIM = N_KV_HEADS * HEAD_DIM
D_FF = 4 * D_MODEL
ROPE_THETA = 10000.0
NORM_EPS = 1e-6
N_MIXERS = 3
PAGE_SIZE = 128
MOBA_BLOCK = 256
MOBA_TOPK = 3
IDX_HEADS = 8
IDX_DIM = 64
DSA_TOPK_MAX = 256
DSA_Q_BLOCK = 128
WINDOW = 128
IN_ATTN = Q_DIM + 2 * KV_DIM
IDX_Q_DIM = IDX_HEADS * IDX_DIM
IN_DSA = IN_ATTN + IDX_Q_DIM + IDX_DIM + IDX_HEADS
ATTN_SCALE = HEAD_DIM ** -0.5
IDX_W_SCALE = IDX_Q_DIM ** -0.5

LANES = 128
SUBLANES = 8
VMEM_LIMIT = 56 << 20
ROW_TILE = 512
FF_TILE = 1024
DSA_KEY_CHUNK = 512
PAGES_PER_STEP = 8

BF = jnp.bfloat16
F32 = jnp.float32
I32 = jnp.int32
NEG_INF = float("-inf")
INT_MIN = -2 ** 31
KEY_NEG_INF = INT_MIN + 0x7FFFFF
NT = (((1,), (1,)), ((), ()))


def _params(*semantics):
    return pltpu.CompilerParams(dimension_semantics=semantics, vmem_limit_bytes=VMEM_LIMIT)


def _layer_plan(depth):
    plan, counts, n_paged = [], [0] * N_MIXERS, 0
    for i in range(depth):
        m = i % N_MIXERS
        slot = -1
        if m < 2:
            slot = n_paged
            n_paged += 1
        plan.append((m, counts[m], slot))
        counts[m] += 1
    return plan


def _adaln_kernel(c_ref, w_ref, b_ref, o_ref):
    c = c_ref[...]
    a = (c * (1.0 / (1.0 + jnp.exp(-c)))).astype(BF)
    o_ref[...] = jnp.dot(a, w_ref[...].astype(BF), preferred_element_type=F32) + b_ref[...]


def _adaln(c_all, w_mod, b_mod):
    depth, _, n_out = w_mod.shape
    nc = c_all.shape[0]
    tn = 1536
    return pl.pallas_call(
        _adaln_kernel,
        grid=(depth, n_out // tn),
        in_specs=[pl.BlockSpec((nc, D_MODEL), lambda l, j: (0, 0)),
                  pl.BlockSpec((None, D_MODEL, tn), lambda l, j: (l, 0, j)),
                  pl.BlockSpec((None, 1, tn), lambda l, j: (l, 0, j))],
        out_specs=pl.BlockSpec((None, nc, tn), lambda l, j: (l, 0, j)),
        out_shape=jax.ShapeDtypeStruct((depth, nc, n_out), F32),
        compiler_params=_params("parallel", "parallel"),
        name="adaln",
    )(c_all, w_mod, b_mod.reshape(depth, 1, n_out))


def _norm_mod(x, g, scale, shift):
    y = x * lax.rsqrt(jnp.mean(x * x, axis=-1, keepdims=True) + NORM_EPS)
    return (y * g) * (1.0 + scale) + shift


def _rope_chunk(z, cos, sin_signed, lo_half):
    partner = jnp.where(lo_half, pltpu.roll(z, LANES - HEAD_DIM // 2, 1), pltpu.roll(z, HEAD_DIM // 2, 1))
    return z * cos + partner * sin_signed


def _proj_kernel(x_ref, g_ref, sc_ref, sh_ref, cos_ref, sin_ref, w_ref, *out_refs, dsa):
    hb = _norm_mod(x_ref[...], g_ref[...], sc_ref[...], sh_ref[...]).astype(BF)
    cos = cos_ref[...]
    sin_s = sin_ref[...]
    lane = lax.broadcasted_iota(I32, (1, LANES), 1)
    lo_half = (lane % HEAD_DIM) < (HEAD_DIM // 2)
    q_ref, k_ref, v_ref, kb_ref, vb_ref = out_refs[:5]

    zq = jnp.dot(hb, w_ref[:, :Q_DIM], preferred_element_type=F32)
    for c in range(Q_DIM // LANES):
        sl = slice(c * LANES, (c + 1) * LANES)
        q_ref[:, sl] = (_rope_chunk(zq[:, sl], cos, sin_s, lo_half) * ATTN_SCALE).astype(BF)
    zk = jnp.dot(hb, w_ref[:, Q_DIM:Q_DIM + KV_DIM], preferred_element_type=F32)
    for c in range(KV_DIM // LANES):
        sl = slice(c * LANES, (c + 1) * LANES)
        kr = _rope_chunk(zk[:, sl], cos, sin_s, lo_half)
        k_ref[:, sl] = kr
        kb_ref[:, sl] = kr.astype(BF)
    zv = jnp.dot(hb, w_ref[:, Q_DIM + KV_DIM:IN_ATTN], preferred_element_type=F32)
    v_ref[...] = zv
    vb_ref[...] = zv.astype(BF)
    if dsa:
        qi_ref, rest_ref, restb_ref = out_refs[5:]
        zi = jnp.dot(hb, w_ref[:, IN_ATTN:IN_ATTN + IDX_Q_DIM], preferred_element_type=F32)
        for c in range(IDX_Q_DIM // LANES):
            sl = slice(c * LANES, (c + 1) * LANES)
            qi_ref[:, sl] = _rope_chunk(zi[:, sl], cos, sin_s, lo_half).astype(BF)
        zr = jnp.dot(hb, w_ref[:, IN_ATTN + IDX_Q_DIM:], preferred_element_type=F32)
        roped = _rope_chunk(zr, cos, sin_s, lo_half)
        rest = jnp.where(lane < IDX_DIM, roped, zr * IDX_W_SCALE)
        rest_ref[...] = rest
        restb_ref[...] = rest.astype(BF)


def _mod_spec(arr, tm, rows_per_seq):
    if arr.ndim == 3:
        return pl.BlockSpec((None, 1, D_MODEL), lambda i, *_: ((i * tm) // rows_per_seq, 0, 0))
    return pl.BlockSpec((tm, D_MODEL), lambda i, *_: (i, 0))


def _project(x, g, scale, shift, cos_tab, sin_tab, w, rows_per_seq, dsa):
    rows = x.shape[0]
    tm = min(ROW_TILE, rows)
    n_in = w.shape[1]
    tab_tiles = cos_tab.shape[0] // tm
    row_spec = lambda n: pl.BlockSpec((tm, n), lambda i: (i, 0))
    tab_spec = pl.BlockSpec((tm, LANES), lambda i: (i % tab_tiles, 0))
    out_shape = [jax.ShapeDtypeStruct((rows, Q_DIM), BF),
                 jax.ShapeDtypeStruct((rows, KV_DIM), F32), jax.ShapeDtypeStruct((rows, KV_DIM), F32),
                 jax.ShapeDtypeStruct((rows, KV_DIM), BF), jax.ShapeDtypeStruct((rows, KV_DIM), BF)]
    out_specs = [row_spec(Q_DIM), row_spec(KV_DIM), row_spec(KV_DIM), row_spec(KV_DIM), row_spec(KV_DIM)]
    if dsa:
        out_shape += [jax.ShapeDtypeStruct((rows, IDX_Q_DIM), BF),
                      jax.ShapeDtypeStruct((rows, LANES), F32), jax.ShapeDtypeStruct((rows, LANES), BF)]
        out_specs += [row_spec(IDX_Q_DIM), row_spec(LANES), row_spec(LANES)]
    return pl.pallas_call(
        functools.partial(_proj_kernel, dsa=dsa),
        grid=(rows // tm,),
        in_specs=[row_spec(D_MODEL),
                  pl.BlockSpec((1, D_MODEL), lambda i: (0, 0)),
                  _mod_spec(scale, tm, rows_per_seq), _mod_spec(shift, tm, rows_per_seq),
                  tab_spec, tab_spec,
                  pl.BlockSpec((D_MODEL, n_in), lambda i: (0, 0))],
        out_specs=out_specs,
        out_shape=out_shape,
        compiler_params=_params("parallel"),
        name="project_dsa" if dsa else "project",
    )(x, g, scale, shift, cos_tab, sin_tab, w)


def _proj_moba_kernel(x_ref, g_ref, sc_ref, sh_ref, cos_ref, sin_ref, cost_ref, sint_ref, w_ref, wqt_ref, wvt_ref,
                      qt_ref, k_ref, v_ref, kz_ref, vt_ref):
    hb = _norm_mod(x_ref[...], g_ref[...], sc_ref[...], sh_ref[...]).astype(BF)
    lane = lax.broadcasted_iota(I32, (1, LANES), 1)
    lo_half = (lane % HEAD_DIM) < (HEAD_DIM // 2)
    half = HEAD_DIM // 2

    zqt = lax.dot_general(wqt_ref[...], hb, NT, preferred_element_type=F32)
    cos_t, sin_t = cost_ref[...], sint_ref[...]
    for h in range(N_HEADS):
        x = zqt[h * HEAD_DIM:(h + 1) * HEAD_DIM]
        partner = jnp.concatenate([x[half:], x[:half]], axis=0)
        qt_ref[h * HEAD_DIM:(h + 1) * HEAD_DIM, :] = ((x * cos_t + partner * sin_t) * ATTN_SCALE).astype(BF)
    zk = jnp.dot(hb, w_ref[:, Q_DIM:Q_DIM + KV_DIM], preferred_element_type=F32)
    for c in range(KV_DIM // LANES):
        sl = slice(c * LANES, (c + 1) * LANES)
        kr = _rope_chunk(zk[:, sl], cos_ref[...], sin_ref[...], lo_half)
        k_ref[:, sl] = kr
        kz_ref[:, 2 * c * LANES:(2 * c + 1) * LANES] = jnp.where(lane < HEAD_DIM, kr, 0.0).astype(BF)
        kz_ref[:, (2 * c + 1) * LANES:(2 * c + 2) * LANES] = jnp.where(
            lane < HEAD_DIM, pltpu.roll(kr, HEAD_DIM, 1), 0.0).astype(BF)
    v_ref[...] = jnp.dot(hb, w_ref[:, Q_DIM + KV_DIM:IN_ATTN], preferred_element_type=F32)
    vt_ref[...] = lax.dot_general(wvt_ref[...], hb, NT, preferred_element_type=F32).astype(BF)


def _project_moba(x, g, scale, shift, cos_tab, sin_tab, cos_t, sin_t, w, rows_per_seq):
    rows = x.shape[0]
    tm = min(ROW_TILE, rows)
    tab_tiles = cos_tab.shape[0] // tm
    row_spec = lambda n: pl.BlockSpec((tm, n), lambda i: (i, 0))
    col_spec = lambda n: pl.BlockSpec((n, tm), lambda i: (0, i))
    tab_spec = pl.BlockSpec((tm, LANES), lambda i: (i % tab_tiles, 0))
    tabt_spec = pl.BlockSpec((HEAD_DIM, tm), lambda i: (0, i % tab_tiles))
    whole = lambda a: pl.BlockSpec(a.shape, lambda i: (0, 0))
    wqt = w[:, :Q_DIM].T
    wvt = w[:, Q_DIM + KV_DIM:IN_ATTN].T
    return pl.pallas_call(
        _proj_moba_kernel,
        grid=(rows // tm,),
        in_specs=[row_spec(D_MODEL),
                  pl.BlockSpec((1, D_MODEL), lambda i: (0, 0)),
                  _mod_spec(scale, tm, rows_per_seq), _mod_spec(shift, tm, rows_per_seq),
                  tab_spec, tab_spec, tabt_spec, tabt_spec,
                  whole(w), whole(wqt), whole(wvt)],
        out_specs=[col_spec(Q_DIM), row_spec(KV_DIM), row_spec(KV_DIM), row_spec(2 * KV_DIM), col_spec(KV_DIM)],
        out_shape=[jax.ShapeDtypeStruct((Q_DIM, rows), BF),
                   jax.ShapeDtypeStruct((rows, KV_DIM), F32), jax.ShapeDtypeStruct((rows, KV_DIM), F32),
                   jax.ShapeDtypeStruct((rows, 2 * KV_DIM), BF), jax.ShapeDtypeStruct((KV_DIM, rows), BF)],
        compiler_params=_params("parallel"),
        name="project_moba",
    )(x, g, scale, shift, cos_tab, sin_tab, cos_t, sin_t, w, wqt, wvt)


def _post_kernel(x_ref, o_ref, wo_ref, ga_ref, g_ref, sc_ref, sh_ref, gm_ref, w1_ref, w2_ref, y_ref,
                 x1_scr, h_scr, acc_scr):
    j = pl.program_id(1)

    @pl.when(j == 0)
    def _():
        x1 = x_ref[...] + ga_ref[...] * jnp.dot(o_ref[...], wo_ref[...], preferred_element_type=F32)
        x1_scr[...] = x1
        h_scr[...] = _norm_mod(x1, g_ref[...], sc_ref[...], sh_ref[...]).astype(BF)
        acc_scr[...] = jnp.zeros_like(acc_scr)

    a = jnp.maximum(jnp.dot(h_scr[...], w1_ref[...], preferred_element_type=F32), 0.0)
    acc_scr[...] += jnp.dot((a * a).astype(BF), w2_ref[...], preferred_element_type=F32)

    @pl.when(j == pl.num_programs(1) - 1)
    def _():
        y_ref[...] = x1_scr[...] + gm_ref[...] * acc_scr[...]


def _post(x, o, w_out, gate_a, g_mlp, scale_m, shift_m, gate_m, w1, w2, rows_per_seq):
    rows = x.shape[0]
    tm = min(ROW_TILE, rows)
    tf = FF_TILE
    ms = lambda a: _mod_spec(a, tm, rows_per_seq)
    return pl.pallas_call(
        _post_kernel,
        grid=(rows // tm, D_FF // tf),
        in_specs=[pl.BlockSpec((tm, D_MODEL), lambda i, j: (i, 0)),
                  pl.BlockSpec((tm, Q_DIM), lambda i, j: (i, 0)),
                  pl.BlockSpec((Q_DIM, D_MODEL), lambda i, j: (0, 0)),
                  ms(gate_a),
                  pl.BlockSpec((1, D_MODEL), lambda i, j: (0, 0)),
                  ms(scale_m), ms(shift_m), ms(gate_m),
                  pl.BlockSpec((D_MODEL, tf), lambda i, j: (0, j)),
                  pl.BlockSpec((tf, D_MODEL), lambda i, j: (j, 0))],
        out_specs=pl.BlockSpec((tm, D_MODEL), lambda i, j: (i, 0)),
        out_shape=jax.ShapeDtypeStruct((rows, D_MODEL), F32),
        scratch_shapes=[pltpu.VMEM((tm, D_MODEL), F32), pltpu.VMEM((tm, D_MODEL), BF),
                        pltpu.VMEM((tm, D_MODEL), F32)],
        compiler_params=_params("parallel", "arbitrary"),
        name="post",
    )(x, o, w_out, gate_a, g_mlp, scale_m, shift_m, gate_m, w1, w2)


def _final_norm_kernel(x_ref, g_ref, y_ref):
    x = x_ref[...]
    y_ref[...] = x * lax.rsqrt(jnp.mean(x * x, axis=-1, keepdims=True) + NORM_EPS) * g_ref[...]


def _final_norm(x, g):
    rows = x.shape[0]
    tm = min(ROW_TILE, rows)
    return pl.pallas_call(
        _final_norm_kernel,
        grid=(rows // tm,),
        in_specs=[pl.BlockSpec((tm, D_MODEL), lambda i: (i, 0)), pl.BlockSpec((1, D_MODEL), lambda i: (0, 0))],
        out_specs=pl.BlockSpec((tm, D_MODEL), lambda i: (i, 0)),
        out_shape=jax.ShapeDtypeStruct((rows, D_MODEL), F32),
        compiler_params=_params("parallel"),
        name="final_norm",
    )(x, g)


def _block_mean_kernel(k_ref, o_ref):
    o_ref[...] = jnp.mean(k_ref[...], axis=0, keepdims=True)


def _block_means(k):
    nb = k.shape[0] // MOBA_BLOCK
    return pl.pallas_call(
        _block_mean_kernel,
        grid=(nb,),
        in_specs=[pl.BlockSpec((MOBA_BLOCK, KV_DIM), lambda i: (i, 0))],
        out_specs=pl.BlockSpec((None, 1, KV_DIM), lambda i: (i, 0, 0)),
        out_shape=jax.ShapeDtypeStruct((nb, 1, KV_DIM), F32),
        compiler_params=_params("parallel"),
        name="moba_block_means",
    )(k)


def _top_mask(gs, ids, n_sel, axis):
    picked = None
    for _ in range(n_sel):
        mx = jnp.max(gs, axis=axis, keepdims=True)
        am = jnp.min(jnp.where(gs == mx, ids, 2 ** 30), axis=axis, keepdims=True)
        hit = (ids == am) & (mx > NEG_INF)
        picked = hit if picked is None else (picked | hit)
        gs = jnp.where(ids == am, NEG_INF, gs)
    return picked


MOBA_ACC_ROWS = HEAD_DIM + 16
MOBA_MASKED = -1e30

def _moba_prompt_kernel(qi_tab, kj_tab, last_tab, qt_ref, kz_ref, vt_ref, kmz_ref, o_ref,
                        qz_scr, m_scr, acc_scr, *, n_sel):
    p = pl.program_id(1)
    i = qi_tab[p]
    j = kj_tab[p]
    blk = MOBA_BLOCK
    lane = lax.broadcasted_iota(I32, (blk, LANES), 1)
    ones_rows = (lax.broadcasted_iota(I32, (MOBA_ACC_ROWS - HEAD_DIM, blk), 0) == 0).astype(BF)

    @pl.when(i == j)
    def _():
        rows = lax.broadcasted_iota(I32, (LANES, blk), 0)
        earlier = (rows >= HEAD_DIM) & (rows < HEAD_DIM + i)
        for h in range(N_HEADS):
            qt = qt_ref[h * HEAD_DIM:(h + 1) * HEAD_DIM, :]
            gs = jnp.dot(kmz_ref[h // GROUP], qt.astype(F32), precision=lax.Precision.HIGHEST,
                         preferred_element_type=F32)
            picked = _top_mask(jnp.where(earlier, gs, NEG_INF), rows, n_sel, 0)
            bias = jnp.where(earlier & jnp.logical_not(picked), MOBA_MASKED, 0.0)
            qz_scr[h // GROUP, :, (h % GROUP) * blk:(h % GROUP + 1) * blk] = jnp.concatenate(
                [qt, bias[HEAD_DIM:].astype(BF)], axis=0)
        m_scr[...] = jnp.full(m_scr.shape, NEG_INF, F32)
        acc_scr[...] = jnp.zeros_like(acc_scr)

    def step(causal):
        for kv in range(N_KV_HEADS):
            kz = jnp.where(lane == HEAD_DIM + j, 1.0, kz_ref[:, kv * LANES:(kv + 1) * LANES]).astype(BF)
            vaug = jnp.concatenate([vt_ref[kv * HEAD_DIM:(kv + 1) * HEAD_DIM, :], ones_rows], axis=0)
            st = jnp.dot(kz, qz_scr[kv], preferred_element_type=F32)
            if causal:
                key_i = lax.broadcasted_iota(I32, (blk, GROUP * blk), 0)
                qry_i = lax.broadcasted_iota(I32, (blk, GROUP * blk), 1) % blk
                st = jnp.where(key_i <= qry_i, st, NEG_INF)
            m_old = m_scr[kv]
            m_new = jnp.maximum(m_old, jnp.max(st, axis=0, keepdims=True))
            alpha = jnp.exp(m_old - m_new)
            e = jnp.exp(st - m_new).astype(BF)
            acc_scr[kv] = alpha * acc_scr[kv] + jnp.dot(vaug, e, preferred_element_type=F32)
            m_scr[kv] = m_new

    @pl.when(i == j)
    def _():
        step(True)

    @pl.when(i != j)
    def _():
        step(False)

    @pl.when(last_tab[p] == 1)
    def _():
        for c in range(N_HEADS // 2):
            halves = []
            for h in (2 * c, 2 * c + 1):
                a = acc_scr[h // GROUP, :, (h % GROUP) * blk:(h % GROUP + 1) * blk]
                halves.append(a[:HEAD_DIM] * (1.0 / a[HEAD_DIM:HEAD_DIM + 1]))
            o_ref[:, c * LANES:(c + 1) * LANES] = jnp.concatenate(halves, axis=0).T.astype(BF)


def _moba_prompt(qt, kz, vt, k, batch):
    rows = qt.shape[1]
    seq = rows // batch
    blk = MOBA_BLOCK
    assert seq % blk == 0
    nblk = seq // blk
    assert HEAD_DIM + nblk <= LANES
    n_sel = min(MOBA_TOPK, nblk)
    kmean = _block_means(k).reshape(batch, nblk, N_KV_HEADS, HEAD_DIM).transpose(0, 2, 1, 3)
    kmz = jnp.pad(kmean, ((0, 0), (0, 0), (HEAD_DIM, LANES - HEAD_DIM - nblk), (0, 0)))
    qi, kj, last = [], [], []
    for i in range(nblk):
        order = [i] + list(range(i))
        qi += [i] * len(order)
        kj += order
        last += [0] * (len(order) - 1) + [1]
    tabs = [jnp.asarray(np.array(t, np.int32)) for t in (qi, kj, last)]
    grid_spec = pltpu.PrefetchScalarGridSpec(
        num_scalar_prefetch=3,
        grid=(batch, len(qi)),
        in_specs=[pl.BlockSpec((Q_DIM, blk), lambda b, p, qi, kj, la: (0, b * nblk + qi[p])),
                  pl.BlockSpec((blk, 2 * KV_DIM), lambda b, p, qi, kj, la: (b * nblk + kj[p], 0)),
                  pl.BlockSpec((KV_DIM, blk), lambda b, p, qi, kj, la: (0, b * nblk + kj[p])),
                  pl.BlockSpec((None, N_KV_HEADS, LANES, HEAD_DIM), lambda b, p, qi, kj, la: (b, 0, 0, 0))],
        out_specs=pl.BlockSpec((blk, Q_DIM), lambda b, p, qi, kj, la: (b * nblk + qi[p], 0)),
        scratch_shapes=[pltpu.VMEM((N_KV_HEADS, LANES, GROUP * blk), BF), pltpu.VMEM((N_KV_HEADS, 1, GROUP * blk), F32),
                        pltpu.VMEM((N_KV_HEADS, MOBA_ACC_ROWS, GROUP * blk), F32)])
    return pl.pallas_call(
        functools.partial(_moba_prompt_kernel, n_sel=n_sel),
        grid_spec=grid_spec,
        out_shape=jax.ShapeDtypeStruct((rows, Q_DIM), BF),
        compiler_params=_params("parallel", "arbitrary"),
        name="moba_prompt",
    )(*tabs, qt, kz, vt, kmz)


def _float_key(x):
    b = pltpu.bitcast(x, I32)
    return jnp.where(b < 0, b ^ 0x7FFFFFFF, b)


def _kth_largest_key(count_ge, k_top, shape):
    def body(it, cur):
        cand = cur | jnp.left_shift(jnp.int32(1), 31 - it)
        return jnp.where(count_ge(cand ^ INT_MIN) >= k_top, cand, cur)
    return lax.fori_loop(0, 32, body, jnp.zeros(shape, I32)) ^ INT_MIN


def _tie_cutoff(count_eq_below, need, n_bits, shape):
    def body(it, cur):
        cand = cur | jnp.left_shift(jnp.int32(1), n_bits - 1 - it)
        return jnp.where(count_eq_below(cand) < need, cand, cur)
    return lax.fori_loop(0, n_bits, body, jnp.zeros(shape, I32))


def _dsa_prompt_kernel(q_ref, qi_ref, w_ref, ki_ref, k_ref, v_ref, o_ref, key_scr, m_scr, l_scr, acc_scr,
                       *, k_top, n_bits):
    i = pl.program_id(1)
    tq, ch = DSA_Q_BLOCK, DSA_KEY_CHUNK
    n_ch = ((i + 1) * tq + ch - 1) // ch
    qpos = i * tq + lax.broadcasted_iota(I32, (tq, 1), 0)
    col = lax.broadcasted_iota(I32, (tq, ch), 1)
    lane = lax.broadcasted_iota(I32, (tq, LANES), 1)

    def score_body(c, carry):
        kc = ki_ref[pl.ds(pl.multiple_of(c * ch, ch), ch), :][:, :IDX_DIM]
        sc = jnp.zeros((tq, ch), F32)
        for h in range(IDX_HEADS):
            d = lax.dot_general(qi_ref[:, h * IDX_DIM:(h + 1) * IDX_DIM], kc, NT, preferred_element_type=F32)
            sc = sc + w_ref[:, IDX_DIM + h:IDX_DIM + h + 1] * jnp.maximum(d, 0.0)
        sc = jnp.where(c * ch + col <= qpos, sc, NEG_INF)
        key_scr[c] = _float_key(sc)
        return carry
    lax.fori_loop(0, n_ch, score_body, 0)

    def count(pred):
        def body(c, acc):
            key = key_scr[c]
            for s in range(ch // LANES):
                hit = pred(key[:, s * LANES:(s + 1) * LANES], c * ch + s * LANES + lane)
                acc = acc + hit.astype(I32)
            return acc
        return jnp.sum(lax.fori_loop(0, n_ch, body, jnp.zeros((tq, LANES), I32)), axis=1, keepdims=True)

    thr = _kth_largest_key(lambda t: count(lambda key, idx: key >= t), k_top, (tq, 1))
    n_gt = count(lambda key, idx: key > thr)
    n_ge = count(lambda key, idx: key >= thr)
    need = k_top - n_gt
    tie_rows = ((n_ge > k_top) & (thr != KEY_NEG_INF)).astype(I32)
    cutoff = lax.cond(
        jnp.max(tie_rows) > 0,
        lambda: _tie_cutoff(lambda c_: count(lambda key, idx: (key == thr) & (idx < c_)), need, n_bits, (tq, 1)),
        lambda: jnp.full((tq, 1), 2 ** 30, I32))

    m_scr[...] = jnp.full(m_scr.shape, NEG_INF, F32)
    l_scr[...] = jnp.zeros_like(l_scr)
    acc_scr[...] = jnp.zeros_like(acc_scr)

    def attn_body(c, carry):
        key = key_scr[c]
        idx = c * ch + col
        keep = ((key > thr) | ((key == thr) & (idx <= cutoff))) & (idx <= qpos)
        bias = jnp.where(keep, 0.0, NEG_INF)
        rows = pl.ds(pl.multiple_of(c * ch, ch), ch)
        for h in range(N_HEADS):
            kv = h // GROUP
            hs = slice(h * HEAD_DIM, (h + 1) * HEAD_DIM)
            ks = slice(kv * HEAD_DIM, (kv + 1) * HEAD_DIM)
            s = lax.dot_general(q_ref[:, hs], k_ref[rows, :][:, ks], NT, preferred_element_type=F32) + bias
            m_old = m_scr[h]
            m_new = jnp.maximum(m_old, jnp.max(s, axis=1, keepdims=True))
            m_safe = jnp.where(m_new == NEG_INF, 0.0, m_new)
            alpha = jnp.exp(m_old - m_safe)
            e = jnp.exp(s - m_safe)
            l_scr[h] = alpha * l_scr[h] + jnp.sum(e, axis=1, keepdims=True)
            acc_scr[:, hs] = alpha * acc_scr[:, hs] + jnp.dot(e.astype(BF), v_ref[rows, :][:, ks],
                                                             preferred_element_type=F32)
            m_scr[h] = m_new
        return carry
    lax.fori_loop(0, n_ch, attn_body, 0)

    for h in range(N_HEADS):
        hs = slice(h * HEAD_DIM, (h + 1) * HEAD_DIM)
        o_ref[:, hs] = (acc_scr[:, hs] * (1.0 / l_scr[h])).astype(BF)


def _dsa_prompt(q, qi, rest, restb, kb, vb, batch):
    rows = q.shape[0]
    seq = rows // batch
    tq, ch = DSA_Q_BLOCK, DSA_KEY_CHUNK
    k_top = min(DSA_TOPK_MAX, seq // 4)
    assert seq % ch == 0 and ch >= k_top and ch % tq == 0
    nq = seq // tq
    n_bits = max(1, (seq - 1).bit_length())
    return pl.pallas_call(
        functools.partial(_dsa_prompt_kernel, k_top=k_top, n_bits=n_bits),
        grid=(batch, nq),
        in_specs=[pl.BlockSpec((tq, Q_DIM), lambda b, i: (b * nq + i, 0)),
                  pl.BlockSpec((tq, IDX_Q_DIM), lambda b, i: (b * nq + i, 0)),
                  pl.BlockSpec((tq, LANES), lambda b, i: (b * nq + i, 0)),
                  pl.BlockSpec((seq, LANES), lambda b, i: (b, 0)),
                  pl.BlockSpec((seq, KV_DIM), lambda b, i: (b, 0)),
                  pl.BlockSpec((seq, KV_DIM), lambda b, i: (b, 0))],
        out_specs=pl.BlockSpec((tq, Q_DIM), lambda b, i: (b * nq + i, 0)),
        out_shape=jax.ShapeDtypeStruct((rows, Q_DIM), BF),
        scratch_shapes=[pltpu.VMEM((seq // ch, tq, ch), I32),
                        pltpu.VMEM((N_HEADS, tq, 1), F32), pltpu.VMEM((N_HEADS, tq, 1), F32),
                        pltpu.VMEM((tq, Q_DIM), F32)],
        compiler_params=_params("parallel", "arbitrary"),
        name="dsa_prompt",
    )(q, qi, rest, restb, kb, vb)


def _swa_prompt_kernel(q_ref, kp_ref, kc_ref, vp_ref, vc_ref, sink_ref, o_ref):
    n = pl.program_id(1)
    w = WINDOW
    row = lax.broadcasted_iota(I32, (w, 2 * w), 0)
    col = lax.broadcasted_iota(I32, (w, 2 * w), 1)
    keep = (col >= row) & (col <= row + w) & ((col >= w) | (n > 0))
    for kv in range(N_KV_HEADS):
        ks = slice(kv * HEAD_DIM, (kv + 1) * HEAD_DIM)
        kk = jnp.concatenate([kp_ref[:, ks], kc_ref[:, ks]], axis=0)
        vv = jnp.concatenate([vp_ref[:, ks], vc_ref[:, ks]], axis=0)
        for g in range(GROUP):
            h = kv * GROUP + g
            hs = slice(h * HEAD_DIM, (h + 1) * HEAD_DIM)
            s = jnp.where(keep, lax.dot_general(q_ref[:, hs], kk, NT, preferred_element_type=F32), NEG_INF)
            sink = sink_ref[:, h:h + 1]
            m = jnp.maximum(jnp.max(s, axis=1, keepdims=True), sink)
            e = jnp.exp(s - m)
            den = jnp.sum(e, axis=1, keepdims=True) + jnp.exp(sink - m)
            o = jnp.dot(e.astype(BF), vv, preferred_element_type=F32)
            o_ref[:, hs] = (o * (1.0 / den)).astype(BF)


def _swa_prompt(q, kb, vb, sinks, batch):
    rows = q.shape[0]
    seq = rows // batch
    w = WINDOW
    assert seq % w == 0
    nb = seq // w
    cur = lambda b, n: (b * nb + n, 0)
    prev = lambda b, n: (b * nb + jnp.maximum(n - 1, 0), 0)
    return pl.pallas_call(
        _swa_prompt_kernel,
        grid=(batch, nb),
        in_specs=[pl.BlockSpec((w, Q_DIM), cur),
                  pl.BlockSpec((w, KV_DIM), prev), pl.BlockSpec((w, KV_DIM), cur),
                  pl.BlockSpec((w, KV_DIM), prev), pl.BlockSpec((w, KV_DIM), cur),
                  pl.BlockSpec((1, N_HEADS), lambda b, n: (0, 0))],
        out_specs=pl.BlockSpec((w, Q_DIM), cur),
        out_shape=jax.ShapeDtypeStruct((rows, Q_DIM), BF),
        compiler_params=_params("parallel", "parallel"),
        name="swa_prompt",
    )(q, kb, kb, vb, vb, sinks.reshape(1, N_HEADS))


def _row_token(n_rows, n_tok):
    r = lax.broadcasted_iota(I32, (n_rows, 1), 0)
    return (r // GROUP) % n_tok


def _block_diag_queries(q, n_seq, n_tok):
    qr = q.reshape(n_seq, n_tok, N_KV_HEADS, GROUP, HEAD_DIM).transpose(0, 2, 1, 3, 4)
    qr = qr.reshape(n_seq, N_KV_HEADS, n_tok * GROUP, HEAD_DIM)
    eye = jnp.eye(N_KV_HEADS, dtype=q.dtype)
    qbd = qr[:, :, :, None, :] * eye[None, :, None, :, None]
    return qbd.reshape(n_seq, N_KV_HEADS * n_tok * GROUP, KV_DIM)


def _undiag_outputs(out, n_seq, n_tok):
    o = out.reshape(n_seq, N_KV_HEADS, n_tok, GROUP, N_KV_HEADS, HEAD_DIM)
    o = jnp.stack([o[:, kv, :, :, kv] for kv in range(N_KV_HEADS)], axis=1)
    return o.transpose(0, 2, 1, 3, 4).reshape(n_seq * n_tok, Q_DIM)


def _tokens_on_lanes(cache):
    if cache.ndim == 5:
        t = jnp.transpose(cache, (0, 1, 3, 4, 2))
        return t.reshape(t.shape[:2] + (t.shape[2] * t.shape[3], t.shape[4]))
    return jnp.transpose(cache, (0, 1, 3, 2))


def _page_specs(n_pages_per_step, slot, width):
    def spec(r):
        return pl.BlockSpec((None, None, width, PAGE_SIZE),
                            lambda b, s, pt: (slot, pt[b, s * n_pages_per_step + r], 0, 0))
    return [spec(r) for r in range(n_pages_per_step)]


def _pad_rows(a, n_seq, n_tok):
    a = a.reshape(n_seq, n_tok, a.shape[-1])
    return jnp.pad(a, ((0, 0), (0, SUBLANES - n_tok), (0, 0)))


def _moba_sample_kernel(pt_ref, q_ref, kn_ref, vn_ref, *refs, n_tok, n_sel, pps):
    kt_pages = refs[:pps]
    vt_pages = refs[pps:2 * pps]
    o_ref = refs[2 * pps]
    m_scr, l_scr, g_scr, o_scr = refs[2 * pps + 1:]
    s = pl.program_id(1)
    nblk = m_scr.shape[0]
    q = q_ref[...]
    n_row = q.shape[0]
    for r in range(pps // 2):
        kt = jnp.concatenate([kt_pages[2 * r][...], kt_pages[2 * r + 1][...]], axis=1).astype(BF)
        vt = jnp.concatenate([vt_pages[2 * r][...], vt_pages[2 * r + 1][...]], axis=1).astype(BF)
        sc = jnp.dot(q, kt, preferred_element_type=F32)
        m = jnp.max(sc, axis=1, keepdims=True)
        e = jnp.exp(sc - m)
        blk = s * (pps // 2) + r
        m_scr[blk] = m
        l_scr[blk] = jnp.sum(e, axis=1, keepdims=True)
        g_scr[blk] = jnp.sum(sc, axis=1, keepdims=True)
        o_scr[blk] = lax.dot_general(e.astype(BF), vt, NT, preferred_element_type=F32)

    @pl.when(s == pl.num_programs(1) - 1)
    def _():
        blk_ids = lax.broadcasted_iota(I32, (nblk, n_row, 1), 0)
        picked = _top_mask(g_scr[...], blk_ids, n_sel, 0)
        m_all = m_scr[...]
        own_ok = lax.broadcasted_iota(I32, (n_row, SUBLANES), 1) <= _row_token(n_row, n_tok)
        s_own = jnp.where(own_ok, lax.dot_general(q, kn_ref[...], NT, preferred_element_type=F32), NEG_INF)
        m_tot = jnp.maximum(jnp.max(jnp.where(picked, m_all, NEG_INF), axis=0),
                            jnp.max(s_own, axis=1, keepdims=True))
        wgt = jnp.where(picked, jnp.exp(m_all - m_tot), 0.0)
        e_own = jnp.exp(s_own - m_tot)
        den = jnp.sum(wgt * l_scr[...], axis=0) + jnp.sum(e_own, axis=1, keepdims=True)
        num = jnp.dot(e_own.astype(BF), vn_ref[...], preferred_element_type=F32)
        for n in range(nblk):
            num = num + wgt[n] * o_scr[n]
        o_ref[...] = num * (1.0 / den)


def _paged_attention_call(kernel, name, slot, page_table, cache_kt, cache_vt, extra_inputs, extra_specs,
                          scratch_shapes, n_row):
    n_seq, n_pages = page_table.shape
    pps = PAGES_PER_STEP
    assert n_pages % pps == 0
    grid_spec = pltpu.PrefetchScalarGridSpec(
        num_scalar_prefetch=1,
        grid=(n_seq, n_pages // pps),
        in_specs=extra_specs + _page_specs(pps, slot, KV_DIM) + _page_specs(pps, slot, KV_DIM),
        out_specs=pl.BlockSpec((None, n_row, KV_DIM), lambda b, s, pt: (b, 0, 0)),
        scratch_shapes=scratch_shapes)
    return pl.pallas_call(
        kernel,
        grid_spec=grid_spec,
        out_shape=jax.ShapeDtypeStruct((n_seq, n_row, KV_DIM), F32),
        compiler_params=_params("parallel", "arbitrary"),
        name=name,
    )(page_table, *extra_inputs, *([cache_kt] * pps), *([cache_vt] * pps))


def _moba_sample(qbd, kn_pad, vn_pad, cache_kt, cache_vt, slot, page_table, n_tok):
    n_seq, n_pages = page_table.shape
    past = n_pages * PAGE_SIZE
    assert past % MOBA_BLOCK == 0 and MOBA_BLOCK == 2 * PAGE_SIZE and PAGES_PER_STEP % 2 == 0
    nblk = past // MOBA_BLOCK
    assert nblk <= 32
    n_row = qbd.shape[1]
    per_seq = lambda w: pl.BlockSpec((None, w[0], w[1]), lambda b, s, pt: (b, 0, 0))
    kernel = functools.partial(_moba_sample_kernel, n_tok=n_tok, n_sel=min(MOBA_TOPK, nblk), pps=PAGES_PER_STEP)
    return _paged_attention_call(
        kernel, "moba_sample", slot, page_table, cache_kt, cache_vt,
        [qbd, kn_pad, vn_pad],
        [per_seq((n_row, KV_DIM)), per_seq((SUBLANES, KV_DIM)), per_seq((SUBLANES, KV_DIM))],
        [pltpu.VMEM((nblk, n_row, 1), F32), pltpu.VMEM((nblk, n_row, 1), F32), pltpu.VMEM((nblk, n_row, 1), F32),
         pltpu.VMEM((nblk, n_row, KV_DIM), F32)],
        n_row)


def _dsa_index_kernel(pt_ref, qi_ref, w_ref, kin_ref, *refs, n_tok, k_top, ppi, n_bits):
    ki_pages = refs[:ppi]
    mask_ref = refs[ppi]
    key_scr = refs[ppi + 1]
    s = pl.program_id(1)
    n_steps = pl.num_programs(1)
    n_pages = n_steps * ppi
    qi = qi_ref[...]
    w = w_ref[...]
    fill = jnp.full((SUBLANES - n_tok, LANES), NEG_INF, F32)

    def scores(d):
        wd = (w * jnp.maximum(d, 0.0)).reshape(n_tok, IDX_HEADS, LANES)
        return jnp.sum(wd, axis=1)

    for r in range(ppi):
        sc = scores(jnp.dot(qi, ki_pages[r][...].astype(BF), preferred_element_type=F32))
        key_scr[s * ppi + r] = _float_key(jnp.concatenate([sc, fill], axis=0))

    @pl.when(s == n_steps - 1)
    def _():
        tok = lax.broadcasted_iota(I32, (n_tok, LANES), 0)
        lane_t = lax.broadcasted_iota(I32, (n_tok, LANES), 1)
        sc_new = scores(lax.dot_general(qi, kin_ref[...], NT, preferred_element_type=F32))
        sc_new = jnp.where(lane_t <= tok, sc_new, NEG_INF)
        key_scr[n_pages] = _float_key(jnp.concatenate([sc_new, fill], axis=0))
        lane = lax.broadcasted_iota(I32, (SUBLANES, LANES), 1)

        def count(pred):
            def body(c, acc):
                return acc + pred(key_scr[c], c * PAGE_SIZE + lane).astype(I32)
            return jnp.sum(lax.fori_loop(0, n_pages + 1, body, jnp.zeros((SUBLANES, LANES), I32)),
                           axis=1, keepdims=True)

        thr = _kth_largest_key(lambda t: count(lambda key, idx: key >= t), k_top, (SUBLANES, 1))
        need = k_top - count(lambda key, idx: key > thr)
        cutoff = _tie_cutoff(lambda c_: count(lambda key, idx: (key == thr) & (idx < c_)), need, n_bits,
                             (SUBLANES, 1))

        def emit(c, carry):
            key = key_scr[c]
            keep = (key > thr) | ((key == thr) & (c * PAGE_SIZE + lane <= cutoff))
            mask_ref[c] = jnp.where(keep & (key != KEY_NEG_INF), 1.0, 0.0).astype(BF)
            return carry
        lax.fori_loop(0, n_pages + 1, emit, 0)


def _dsa_index(qi_rows, w_rows, ki_new, cache_idx, islot, page_table, n_tok):
    n_seq, n_pages = page_table.shape
    ppi = PAGES_PER_STEP
    total = n_pages * PAGE_SIZE + n_tok
    k_top = min(DSA_TOPK_MAX, total // 4)
    assert n_pages % ppi == 0 and n_tok <= SUBLANES and n_pages * PAGE_SIZE >= k_top
    n_bits = max(1, ((n_pages + 1) * PAGE_SIZE - 1).bit_length())
    nr = n_tok * IDX_HEADS
    grid_spec = pltpu.PrefetchScalarGridSpec(
        num_scalar_prefetch=1,
        grid=(n_seq, n_pages // ppi),
        in_specs=[pl.BlockSpec((None, nr, IDX_DIM), lambda b, s, pt: (b, 0, 0)),
                  pl.BlockSpec((None, nr, 1), lambda b, s, pt: (b, 0, 0)),
                  pl.BlockSpec((None, LANES, IDX_DIM), lambda b, s, pt: (b, 0, 0))]
        + _page_specs(ppi, islot, IDX_DIM),
        out_specs=pl.BlockSpec((None, n_pages + 1, SUBLANES, LANES), lambda b, s, pt: (b, 0, 0, 0)),
        scratch_shapes=[pltpu.VMEM((n_pages + 1, SUBLANES, LANES), I32)])
    return pl.pallas_call(
        functools.partial(_dsa_index_kernel, n_tok=n_tok, k_top=k_top, ppi=ppi, n_bits=n_bits),
        grid_spec=grid_spec,
        out_shape=jax.ShapeDtypeStruct((n_seq, n_pages + 1, SUBLANES, LANES), BF),
        compiler_params=_params("parallel", "arbitrary"),
        name="dsa_index",
    )(page_table, qi_rows, w_rows, ki_new, *([cache_idx] * ppi))


def _dsa_sample_kernel(pt_ref, q_ref, kn_ref, vn_ref, mask_ref, mnew_ref, *refs, n_tok, pps):
    kt_pages = refs[:pps]
    vt_pages = refs[pps:2 * pps]
    o_ref = refs[2 * pps]
    m_scr, l_scr, acc_scr = refs[2 * pps + 1:]
    s = pl.program_id(1)
    q = q_ref[...]
    n_row = q.shape[0]
    expand = (lax.broadcasted_iota(I32, (n_row, SUBLANES), 1) == _row_token(n_row, n_tok)).astype(BF)

    @pl.when(s == 0)
    def _():
        m_scr[...] = jnp.full(m_scr.shape, NEG_INF, F32)
        l_scr[...] = jnp.zeros_like(l_scr)
        acc_scr[...] = jnp.zeros_like(acc_scr)

    def update(sc, keep, pv):
        sc = jnp.where(keep > 0.5, sc, NEG_INF)
        m_old = m_scr[...]
        m_new = jnp.maximum(m_old, jnp.max(sc, axis=1, keepdims=True))
        m_safe = jnp.where(m_new == NEG_INF, 0.0, m_new)
        alpha = jnp.exp(m_old - m_safe)
        e = jnp.exp(sc - m_safe)
        l_scr[...] = alpha * l_scr[...] + jnp.sum(e, axis=1, keepdims=True)
        acc_scr[...] = alpha * acc_scr[...] + pv(e.astype(BF))
        m_scr[...] = m_new

    for r in range(pps):
        sc = jnp.dot(q, kt_pages[r][...].astype(BF), preferred_element_type=F32)
        keep = jnp.dot(expand, mask_ref[r], preferred_element_type=F32)
        vt = vt_pages[r][...].astype(BF)
        update(sc, keep, lambda e, vt=vt: lax.dot_general(e, vt, NT, preferred_element_type=F32))

    @pl.when(s == pl.num_programs(1) - 1)
    def _():
        sc = lax.dot_general(q, kn_ref[...], NT, preferred_element_type=F32)
        keep = jnp.dot(expand, mnew_ref[...], preferred_element_type=F32)[:, :SUBLANES]
        update(sc, keep, lambda e: jnp.dot(e, vn_ref[...], preferred_element_type=F32))
        o_ref[...] = acc_scr[...] * (1.0 / l_scr[...])


def _dsa_sample(qbd, kn_pad, vn_pad, mask, cache_kt, cache_vt, slot, page_table, n_tok):
    n_seq, n_pages = page_table.shape
    n_row = qbd.shape[1]
    pps = PAGES_PER_STEP
    per_seq = lambda w: pl.BlockSpec((None, w[0], w[1]), lambda b, s, pt: (b, 0, 0))
    return _paged_attention_call(
        functools.partial(_dsa_sample_kernel, n_tok=n_tok, pps=pps), "dsa_sample", slot, page_table,
        cache_kt, cache_vt,
        [qbd, kn_pad, vn_pad, mask, mask],
        [per_seq((n_row, KV_DIM)), per_seq((SUBLANES, KV_DIM)), per_seq((SUBLANES, KV_DIM)),
         pl.BlockSpec((None, pps, SUBLANES, LANES), lambda b, s, pt: (b, s, 0, 0)),
         pl.BlockSpec((None, None, SUBLANES, LANES), lambda b, s, pt: (b, n_pages, 0, 0))],
        [pltpu.VMEM((n_row, 1), F32), pltpu.VMEM((n_row, 1), F32), pltpu.VMEM((n_row, KV_DIM), F32)],
        n_row)


def _swa_sample_kernel(q_ref, kt_ref, vt_ref, kn_ref, vn_ref, sink_ref, o_ref, *, n_tok):
    q = q_ref[...]
    n_row = q.shape[0]
    tok_r = _row_token(n_row, n_tok)
    wk = kt_ref.shape[1]
    s_buf = jnp.dot(q, kt_ref[...].astype(BF), preferred_element_type=F32)
    s_buf = jnp.where(lax.broadcasted_iota(I32, (n_row, wk), 1) >= tok_r, s_buf, NEG_INF)
    s_new = lax.dot_general(q, kn_ref[...], NT, preferred_element_type=F32)
    s_new = jnp.where(lax.broadcasted_iota(I32, (n_row, SUBLANES), 1) <= tok_r, s_new, NEG_INF)
    sink = sink_ref[...]
    m = jnp.maximum(jnp.maximum(jnp.max(s_buf, axis=1, keepdims=True), jnp.max(s_new, axis=1, keepdims=True)), sink)
    e_buf = jnp.exp(s_buf - m)
    e_new = jnp.exp(s_new - m)
    den = jnp.sum(e_buf, axis=1, keepdims=True) + jnp.sum(e_new, axis=1, keepdims=True) + jnp.exp(sink - m)
    num = (lax.dot_general(e_buf.astype(BF), vt_ref[...].astype(BF), NT, preferred_element_type=F32)
           + jnp.dot(e_new.astype(BF), vn_ref[...], preferred_element_type=F32))
    o_ref[...] = num * (1.0 / den)


def _swa_sample(qbd, kn_pad, vn_pad, buf_kt, buf_vt, sinks, n_tok):
    n_seq, _, wk = buf_kt.shape
    assert wk == WINDOW
    n_row = qbd.shape[1]
    sink_rows = jnp.repeat(sinks.reshape(N_KV_HEADS, 1, GROUP), n_tok, axis=1).reshape(n_row, 1)
    per_seq = lambda a, c: pl.BlockSpec((None, a, c), lambda b: (b, 0, 0))
    return pl.pallas_call(
        functools.partial(_swa_sample_kernel, n_tok=n_tok),
        grid=(n_seq,),
        in_specs=[per_seq(n_row, KV_DIM), per_seq(KV_DIM, wk), per_seq(KV_DIM, wk),
                  per_seq(SUBLANES, KV_DIM), per_seq(SUBLANES, KV_DIM),
                  pl.BlockSpec((n_row, 1), lambda b: (0, 0))],
        out_specs=per_seq(n_row, KV_DIM),
        out_shape=jax.ShapeDtypeStruct((n_seq, n_row, KV_DIM), F32),
        compiler_params=_params("parallel"),
        name="swa_sample",
    )(qbd, buf_kt, buf_vt, kn_pad, vn_pad, sink_rows)


def _rope_tables(pos):
    half = HEAD_DIM // 2
    inv = ROPE_THETA ** (-jnp.arange(half, dtype=F32) / half)
    ang = pos.astype(F32)[:, None] * inv[None, :]
    cos, sin = jnp.cos(ang), jnp.sin(ang)
    reps = LANES // HEAD_DIM
    return jnp.tile(jnp.concatenate([cos, cos], axis=1), (1, reps)), jnp.tile(jnp.concatenate([-sin, sin], axis=1), (1, reps))


def kernel(x_prompt, x_sample, cache_k, cache_v, cache_idx_k, state_swa_k, state_swa_v, page_table, c_prompt, c_sample, g_attn, g_mlp, w_mod, b_mod, w_in_moba, w_in_dsa, w_in_swa, swa_sinks, w_out, w_ff1, w_ff2, g_final):
    batch, seq, _ = x_prompt.shape
    n_seq, n_tok, _ = x_sample.shape
    depth = g_attn.shape[0]
    n_pages = page_table.shape[1]
    past = n_pages * PAGE_SIZE
    rows_p, rows_s = batch * seq, n_seq * n_tok

    n_cond = batch + n_seq
    pad_cond = -n_cond % SUBLANES
    c_all = jnp.pad(jnp.concatenate([c_prompt, c_sample], axis=0), ((0, pad_cond), (0, 0)))
    mod = _adaln(c_all, w_mod, b_mod)

    def mods(layer):
        parts = [mod[layer, :, r * D_MODEL:(r + 1) * D_MODEL] for r in range(6)]
        mp = [p[:batch].reshape(batch, 1, D_MODEL) for p in parts]
        ms = [jnp.repeat(p[batch:n_cond], n_tok, axis=0) for p in parts]
        return mp, ms

    cos_p, sin_p = _rope_tables(jnp.arange(seq, dtype=jnp.int32))
    cos_s, sin_s = _rope_tables(jnp.tile(past + jnp.arange(n_tok, dtype=jnp.int32), n_seq))
    cos_pt, sin_pt = cos_p[:, :HEAD_DIM].T, sin_p[:, :HEAD_DIM].T

    cache_kt, cache_vt, cache_it = _tokens_on_lanes(cache_k), _tokens_on_lanes(cache_v), _tokens_on_lanes(cache_idx_k)
    swa_kt, swa_vt = _tokens_on_lanes(state_swa_k), _tokens_on_lanes(state_swa_v)

    xp = x_prompt.reshape(rows_p, D_MODEL)
    xs = x_sample.reshape(rows_s, D_MODEL)
    kp_l, vp_l, ks_l, vs_l, ip_l, is_l = [], [], [], [], [], []
    skp_l, svp_l, sks_l, svs_l = [], [], [], []
    for i, (mixer, j, slot) in enumerate(_layer_plan(depth)):
        (sh_ap, sc_ap, gt_ap, sh_mp, sc_mp, gt_mp), (sh_as, sc_as, gt_as, sh_ms, sc_ms, gt_ms) = mods(i)
        dsa = mixer == 1
        if mixer == 0:
            w_in = w_in_moba[j]
        elif dsa:
            w_in = jnp.pad(w_in_dsa[j], ((0, 0), (0, -IN_DSA % LANES)))
        else:
            w_in = w_in_swa[j]
        w_in = w_in.astype(BF)
        g_a = g_attn[i].reshape(1, D_MODEL)
        if mixer == 0:
            qt_p, k_p, v_p, kz_p, vt_p = _project_moba(xp, g_a, sc_ap, sh_ap, cos_p, sin_p, cos_pt, sin_pt, w_in, seq)
        else:
            outs_p = _project(xp, g_a, sc_ap, sh_ap, cos_p, sin_p, w_in, seq, dsa)
            q_p, k_p, v_p, kb_p, vb_p = outs_p[:5]
        outs_s = _project(xs, g_a, sc_as, sh_as, cos_s, sin_s, w_in, n_tok, dsa)
        q_s, k_s, v_s, kb_s, vb_s = outs_s[:5]
        qbd = _block_diag_queries(q_s, n_seq, n_tok)
        kn_pad = _pad_rows(kb_s, n_seq, n_tok)
        vn_pad = _pad_rows(vb_s, n_seq, n_tok)
        k_p5 = k_p.reshape(batch, seq, N_KV_HEADS, HEAD_DIM)
        v_p5 = v_p.reshape(batch, seq, N_KV_HEADS, HEAD_DIM)
        k_s5 = k_s.reshape(n_seq, n_tok, N_KV_HEADS, HEAD_DIM)
        v_s5 = v_s.reshape(n_seq, n_tok, N_KV_HEADS, HEAD_DIM)
        if mixer == 0:
            o_p = _moba_prompt(qt_p, kz_p, vt_p, k_p, batch)
            o_t = _moba_sample(qbd, kn_pad, vn_pad, cache_kt, cache_vt, slot, page_table, n_tok)
        elif dsa:
            qi_p, rest_p, restb_p = outs_p[5:]
            qi_s, rest_s, restb_s = outs_s[5:]
            o_p = _dsa_prompt(q_p, qi_p, rest_p, restb_p, kb_p, vb_p, batch)
            qi_rows = qi_s.reshape(n_seq, n_tok * IDX_HEADS, IDX_DIM)
            w_rows = rest_s[:, IDX_DIM:IDX_DIM + IDX_HEADS].reshape(n_seq, n_tok * IDX_HEADS, 1)
            ki_new = jnp.pad(restb_s[:, :IDX_DIM].reshape(n_seq, n_tok, IDX_DIM),
                             ((0, 0), (0, LANES - n_tok), (0, 0)))
            mask = _dsa_index(qi_rows, w_rows, ki_new, cache_it, j, page_table, n_tok)
            o_t = _dsa_sample(qbd, kn_pad, vn_pad, mask, cache_kt, cache_vt, slot, page_table, n_tok)
            ip_l.append(rest_p[:, :IDX_DIM].reshape(batch, seq, IDX_DIM))
            is_l.append(rest_s[:, :IDX_DIM].reshape(n_seq, n_tok, IDX_DIM))
        else:
            o_p = _swa_prompt(q_p, kb_p, vb_p, swa_sinks[j], batch)
            o_t = _swa_sample(qbd, kn_pad, vn_pad, swa_kt[j], swa_vt[j], swa_sinks[j], n_tok)
            keep_p = min(WINDOW, seq)
            skp_l.append(k_p5[:, seq - keep_p:])
            svp_l.append(v_p5[:, seq - keep_p:])
            wk = state_swa_k.shape[2]
            sks_l.append(jnp.concatenate([state_swa_k[j], k_s5], axis=1)[:, -wk:])
            svs_l.append(jnp.concatenate([state_swa_v[j], v_s5], axis=1)[:, -wk:])
        if mixer < 2:
            kp_l.append(k_p5)
            vp_l.append(v_p5)
            ks_l.append(k_s5)
            vs_l.append(v_s5)
        o_s = _undiag_outputs(o_t, n_seq, n_tok).astype(BF)
        wo, w1, w2 = w_out[i].astype(BF), w_ff1[i].astype(BF), w_ff2[i].astype(BF)
        g_m = g_mlp[i].reshape(1, D_MODEL)
        xp = _post(xp, o_p, wo, gt_ap, g_m, sc_mp, sh_mp, gt_mp, w1, w2, seq)
        xs = _post(xs, o_s, wo, gt_as, g_m, sc_ms, sh_ms, gt_ms, w1, w2, n_tok)
    g_f = g_final.reshape(1, D_MODEL)
    y_prompt = _final_norm(xp, g_f).reshape(batch, seq, D_MODEL)
    y_sample = _final_norm(xs, g_f).reshape(n_seq, n_tok, D_MODEL)
    return (y_prompt, y_sample, jnp.stack(kp_l), jnp.stack(vp_l), jnp.stack(ks_l), jnp.stack(vs_l),
            jnp.stack(ip_l), jnp.stack(is_l), jnp.stack(skp_l), jnp.stack(svp_l), jnp.stack(sks_l), jnp.stack(svs_l))
```

```python
import functools

import numpy as np
import jax
import jax.numpy as jnp
from jax import lax
from jax.experimental import pallas as pl
from jax.experimental.pallas import tpu as pltpu

D_MODEL = 1024
N_HEADS = 16
HEAD_DIM = D_MODEL // N_HEADS
N_KV_HEADS = 4
GROUP = N_HEADS // N_KV_HEADS
Q_DIM = N_HEADS * HEAD_DIM
KV_DIM = N_KV_HEADS * HEAD_DIM
D_FF = 4 * D_MODEL
ROPE_THETA = 10000.0
NORM_EPS = 1e-6
N_MIXERS = 3
PAGE_SIZE = 128
MOBA_BLOCK = 256
MOBA_TOPK = 3
IDX_HEADS = 8
IDX_DIM = 64
DSA_TOPK_MAX = 256
WINDOW = 128
IN_ATTN = Q_DIM + 2 * KV_DIM
IDX_Q_DIM = IDX_HEADS * IDX_DIM
IN_DSA = IN_ATTN + IDX_Q_DIM + IDX_DIM + IDX_HEADS
ATTN_SCALE = HEAD_DIM ** -0.5
IDX_W_SCALE = IDX_Q_DIM ** -0.5

LANES = 128
SUBLANES = 8
VMEM_LIMIT = 56 << 20
ROW_TILE = 512
FF_TILE = 1024
DSA_KEY_CHUNK = 512
DSA_QUERY_TILE = 256
PAGES_PER_STEP = 16
INDEX_PAGES_PER_STEP = 64

BF = jnp.bfloat16
F32 = jnp.float32
I32 = jnp.int32
NEG_INF = float("-inf")
INT_MIN = -2 ** 31
KEY_NEG_INF = INT_MIN + 0x7FFFFF
NT = (((1,), (1,)), ((), ()))


def _params(*semantics):
    return pltpu.CompilerParams(dimension_semantics=semantics, vmem_limit_bytes=VMEM_LIMIT)


def _layer_plan(depth):
    plan, counts, n_paged = [], [0] * N_MIXERS, 0
    for i in range(depth):
        m = i % N_MIXERS
        slot = -1
        if m < 2:
            slot = n_paged
            n_paged += 1
        plan.append((m, counts[m], slot))
        counts[m] += 1
    return plan


def _adaln_kernel(c_ref, w_ref, b_ref, o_ref):
    c = c_ref[...]
    a = (c * (1.0 / (1.0 + jnp.exp(-c)))).astype(BF)
    o_ref[...] = jnp.dot(a, w_ref[...].astype(BF), preferred_element_type=F32) + b_ref[...]


def _adaln(c_all, w_mod, b_mod):
    depth, _, n_out = w_mod.shape
    nc = c_all.shape[0]
    tn = 1536
    return pl.pallas_call(
        _adaln_kernel,
        grid=(depth, n_out // tn),
        in_specs=[pl.BlockSpec((nc, D_MODEL), lambda l, j: (0, 0)),
                  pl.BlockSpec((None, D_MODEL, tn), lambda l, j: (l, 0, j)),
                  pl.BlockSpec((None, 1, tn), lambda l, j: (l, 0, j))],
        out_specs=pl.BlockSpec((None, nc, tn), lambda l, j: (l, 0, j)),
        out_shape=jax.ShapeDtypeStruct((depth, nc, n_out), F32),
        compiler_params=_params("parallel", "parallel"),
        name="adaln",
    )(c_all, w_mod, b_mod.reshape(depth, 1, n_out))


def _norm_mod(x, g, scale, shift):
    y = x * lax.rsqrt(jnp.mean(x * x, axis=-1, keepdims=True) + NORM_EPS)
    return (y * g) * (1.0 + scale) + shift


def _rope_chunk(z, cos, sin_signed, lo_half):
    partner = jnp.where(lo_half, pltpu.roll(z, LANES - HEAD_DIM // 2, 1), pltpu.roll(z, HEAD_DIM // 2, 1))
    return z * cos + partner * sin_signed


def _proj_kernel(x_ref, g_ref, sc_ref, sh_ref, cos_ref, sin_ref, w_ref, *out_refs, dsa):
    hb = _norm_mod(x_ref[...], g_ref[...], sc_ref[...], sh_ref[...]).astype(BF)
    cos = cos_ref[...]
    sin_s = sin_ref[...]
    lane = lax.broadcasted_iota(I32, (1, LANES), 1)
    lo_half = (lane % HEAD_DIM) < (HEAD_DIM // 2)
    q_ref, k_ref, v_ref, kb_ref, vb_ref = out_refs[:5]

    zq = jnp.dot(hb, w_ref[:, :Q_DIM], preferred_element_type=F32)
    for c in range(Q_DIM // LANES):
        sl = slice(c * LANES, (c + 1) * LANES)
        q_ref[:, sl] = (_rope_chunk(zq[:, sl], cos, sin_s, lo_half) * ATTN_SCALE).astype(BF)
    zk = jnp.dot(hb, w_ref[:, Q_DIM:Q_DIM + KV_DIM], preferred_element_type=F32)
    for c in range(KV_DIM // LANES):
        sl = slice(c * LANES, (c + 1) * LANES)
        kr = _rope_chunk(zk[:, sl], cos, sin_s, lo_half)
        k_ref[:, sl] = kr
        kb_ref[:, sl] = kr.astype(BF)
    zv = jnp.dot(hb, w_ref[:, Q_DIM + KV_DIM:IN_ATTN], preferred_element_type=F32)
    v_ref[...] = zv
    vb_ref[...] = zv.astype(BF)
    if dsa:
        qi_ref, rest_ref, restb_ref = out_refs[5:]
        zi = jnp.dot(hb, w_ref[:, IN_ATTN:IN_ATTN + IDX_Q_DIM], preferred_element_type=F32)
        for c in range(IDX_Q_DIM // LANES):
            sl = slice(c * LANES, (c + 1) * LANES)
            qi_ref[:, sl] = _rope_chunk(zi[:, sl], cos, sin_s, lo_half).astype(BF)
        zr = jnp.dot(hb, w_ref[:, IN_ATTN + IDX_Q_DIM:], preferred_element_type=F32)
        roped = _rope_chunk(zr, cos, sin_s, lo_half)
        rest = jnp.where(lane < IDX_DIM, roped, zr * IDX_W_SCALE)
        rest_ref[...] = rest
        restb_ref[...] = rest.astype(BF)


def _mod_spec(arr, tm, rows_per_seq):
    if arr.ndim == 3:
        return pl.BlockSpec((None, 1, D_MODEL), lambda i, *_: ((i * tm) // rows_per_seq, 0, 0))
    return pl.BlockSpec((tm, D_MODEL), lambda i, *_: (i, 0))


def _project(x, g, scale, shift, cos_tab, sin_tab, w, rows_per_seq, dsa):
    rows = x.shape[0]
    tm = min(ROW_TILE, rows)
    n_in = w.shape[1]
    tab_tiles = cos_tab.shape[0] // tm
    row_spec = lambda n: pl.BlockSpec((tm, n), lambda i: (i, 0))
    tab_spec = pl.BlockSpec((tm, LANES), lambda i: (i % tab_tiles, 0))
    out_shape = [jax.ShapeDtypeStruct((rows, Q_DIM), BF),
                 jax.ShapeDtypeStruct((rows, KV_DIM), F32), jax.ShapeDtypeStruct((rows, KV_DIM), F32),
                 jax.ShapeDtypeStruct((rows, KV_DIM), BF), jax.ShapeDtypeStruct((rows, KV_DIM), BF)]
    out_specs = [row_spec(Q_DIM), row_spec(KV_DIM), row_spec(KV_DIM), row_spec(KV_DIM), row_spec(KV_DIM)]
    if dsa:
        out_shape += [jax.ShapeDtypeStruct((rows, IDX_Q_DIM), BF),
                      jax.ShapeDtypeStruct((rows, LANES), F32), jax.ShapeDtypeStruct((rows, LANES), BF)]
        out_specs += [row_spec(IDX_Q_DIM), row_spec(LANES), row_spec(LANES)]
    return pl.pallas_call(
        functools.partial(_proj_kernel, dsa=dsa),
        grid=(rows // tm,),
        in_specs=[row_spec(D_MODEL),
                  pl.BlockSpec((1, D_MODEL), lambda i: (0, 0)),
                  _mod_spec(scale, tm, rows_per_seq), _mod_spec(shift, tm, rows_per_seq),
                  tab_spec, tab_spec,
                  pl.BlockSpec((D_MODEL, n_in), lambda i: (0, 0))],
        out_specs=out_specs,
        out_shape=out_shape,
        compiler_params=_params("parallel"),
        name="project_dsa" if dsa else "project",
    )(x, g, scale, shift, cos_tab, sin_tab, w)


def _rope_rows(zt, cos_t, sin_t, n_heads, scale):
    half = HEAD_DIM // 2
    out = []
    for h in range(n_heads):
        x = zt[h * HEAD_DIM:(h + 1) * HEAD_DIM]
        partner = jnp.concatenate([x[half:], x[:half]], axis=0)
        out.append(((x * cos_t + partner * sin_t) * scale).astype(BF))
    return out


def _proj_t_kernel(x_ref, g_ref, sc_ref, sh_ref, cos_ref, sin_ref, cost_ref, sint_ref, w_ref, wqt_ref, wvt_ref,
                   *refs, dsa):
    if dsa:
        wit_ref, wrt_ref = refs[:2]
        refs = refs[2:]
    qt_ref, k_ref, v_ref, kz_ref, vt_ref = refs[:5]
    hb = _norm_mod(x_ref[...], g_ref[...], sc_ref[...], sh_ref[...]).astype(BF)
    lane = lax.broadcasted_iota(I32, (1, LANES), 1)
    lo_half = (lane % HEAD_DIM) < (HEAD_DIM // 2)

    zqt = lax.dot_general(wqt_ref[...], hb, NT, preferred_element_type=F32)
    cos_t, sin_t = cost_ref[...], sint_ref[...]
    for h, qh in enumerate(_rope_rows(zqt, cos_t, sin_t, N_HEADS, ATTN_SCALE)):
        qt_ref[h * HEAD_DIM:(h + 1) * HEAD_DIM, :] = qh
    if dsa:
        qit_ref, wt_ref, rest_ref, kiz_ref = refs[5:]
        zit = lax.dot_general(wit_ref[...], hb, NT, preferred_element_type=F32)
        for h, qh in enumerate(_rope_rows(zit, cos_t, sin_t, IDX_HEADS, 1.0)):
            qit_ref[h * IDX_DIM:(h + 1) * IDX_DIM, :] = qh
        zr = jnp.dot(hb, w_ref[:, IN_ATTN + IDX_Q_DIM:], preferred_element_type=F32)
        roped = _rope_chunk(zr, cos_ref[...], sin_ref[...], lo_half)
        rest_ref[...] = jnp.where(lane < IDX_DIM, roped, zr * IDX_W_SCALE)
        kiz_ref[...] = jnp.where(lane < IDX_DIM, roped, 0.0).astype(BF)
        zrt = lax.dot_general(wrt_ref[...], hb, NT, preferred_element_type=F32)
        wt_ref[...] = zrt[IDX_DIM:IDX_DIM + IDX_HEADS] * IDX_W_SCALE
    zk = jnp.dot(hb, w_ref[:, Q_DIM:Q_DIM + KV_DIM], preferred_element_type=F32)
    for c in range(KV_DIM // LANES):
        sl = slice(c * LANES, (c + 1) * LANES)
        kr = _rope_chunk(zk[:, sl], cos_ref[...], sin_ref[...], lo_half)
        k_ref[:, sl] = kr
        kz_ref[:, 2 * c * LANES:(2 * c + 1) * LANES] = jnp.where(lane < HEAD_DIM, kr, 0.0).astype(BF)
        kz_ref[:, (2 * c + 1) * LANES:(2 * c + 2) * LANES] = jnp.where(
            lane < HEAD_DIM, pltpu.roll(kr, HEAD_DIM, 1), 0.0).astype(BF)
    v_ref[...] = jnp.dot(hb, w_ref[:, Q_DIM + KV_DIM:IN_ATTN], preferred_element_type=F32)
    vt_ref[...] = lax.dot_general(wvt_ref[...], hb, NT, preferred_element_type=F32).astype(BF)


def _project_t(x, g, scale, shift, cos_tab, sin_tab, cos_t, sin_t, w, rows_per_seq, dsa):
    rows = x.shape[0]
    tm = min(ROW_TILE, rows)
    tab_tiles = cos_tab.shape[0] // tm
    row_spec = lambda n: pl.BlockSpec((tm, n), lambda i: (i, 0))
    col_spec = lambda n: pl.BlockSpec((n, tm), lambda i: (0, i))
    tab_spec = pl.BlockSpec((tm, LANES), lambda i: (i % tab_tiles, 0))
    tabt_spec = pl.BlockSpec((HEAD_DIM, tm), lambda i: (0, i % tab_tiles))
    whole = lambda a: pl.BlockSpec(a.shape, lambda i: (0, 0))
    weights = [w, w[:, :Q_DIM].T, w[:, Q_DIM + KV_DIM:IN_ATTN].T]
    out_specs = [col_spec(Q_DIM), row_spec(KV_DIM), row_spec(KV_DIM), row_spec(2 * KV_DIM), col_spec(KV_DIM)]
    out_shape = [jax.ShapeDtypeStruct((Q_DIM, rows), BF),
                 jax.ShapeDtypeStruct((rows, KV_DIM), F32), jax.ShapeDtypeStruct((rows, KV_DIM), F32),
                 jax.ShapeDtypeStruct((rows, 2 * KV_DIM), BF), jax.ShapeDtypeStruct((KV_DIM, rows), BF)]
    if dsa:
        weights += [w[:, IN_ATTN:IN_ATTN + IDX_Q_DIM].T, w[:, IN_ATTN + IDX_Q_DIM:].T]
        out_specs[4] = pl.BlockSpec((None, KV_DIM, tm), lambda i: (i, 0, 0))
        out_shape[4] = jax.ShapeDtypeStruct((rows // tm, KV_DIM, tm), BF)
        out_specs += [col_spec(IDX_Q_DIM), col_spec(IDX_HEADS), row_spec(LANES), row_spec(LANES)]
        out_shape += [jax.ShapeDtypeStruct((IDX_Q_DIM, rows), BF), jax.ShapeDtypeStruct((IDX_HEADS, rows), F32),
                      jax.ShapeDtypeStruct((rows, LANES), F32), jax.ShapeDtypeStruct((rows, LANES), BF)]
    return pl.pallas_call(
        functools.partial(_proj_t_kernel, dsa=dsa),
        grid=(rows // tm,),
        in_specs=[row_spec(D_MODEL),
                  pl.BlockSpec((1, D_MODEL), lambda i: (0, 0)),
                  _mod_spec(scale, tm, rows_per_seq), _mod_spec(shift, tm, rows_per_seq),
                  tab_spec, tab_spec, tabt_spec, tabt_spec] + [whole(a) for a in weights],
        out_specs=out_specs,
        out_shape=out_shape,
        compiler_params=_params("parallel"),
        name="project_t_dsa" if dsa else "project_t",
    )(x, g, scale, shift, cos_tab, sin_tab, cos_t, sin_t, *weights)


def _post_kernel(x_ref, o_ref, wo_ref, ga_ref, g_ref, sc_ref, sh_ref, gm_ref, w1_ref, w2_ref, y_ref,
                 x1_scr, h_scr, acc_scr):
    j = pl.program_id(1)

    @pl.when(j == 0)
    def _():
        x1 = x_ref[...] + ga_ref[...] * jnp.dot(o_ref[...], wo_ref[...], preferred_element_type=F32)
        x1_scr[...] = x1
        h_scr[...] = _norm_mod(x1, g_ref[...], sc_ref[...], sh_ref[...]).astype(BF)
        acc_scr[...] = jnp.zeros_like(acc_scr)

    a = jnp.maximum(jnp.dot(h_scr[...], w1_ref[...], preferred_element_type=F32), 0.0)
    acc_scr[...] += jnp.dot((a * a).astype(BF), w2_ref[...], preferred_element_type=F32)

    @pl.when(j == pl.num_programs(1) - 1)
    def _():
        y_ref[...] = x1_scr[...] + gm_ref[...] * acc_scr[...]


def _post(x, o, w_out, gate_a, g_mlp, scale_m, shift_m, gate_m, w1, w2, rows_per_seq):
    rows = x.shape[0]
    tm = min(ROW_TILE, rows)
    tf = FF_TILE
    ms = lambda a: _mod_spec(a, tm, rows_per_seq)
    return pl.pallas_call(
        _post_kernel,
        grid=(rows // tm, D_FF // tf),
        in_specs=[pl.BlockSpec((tm, D_MODEL), lambda i, j: (i, 0)),
                  pl.BlockSpec((tm, Q_DIM), lambda i, j: (i, 0)),
                  pl.BlockSpec((Q_DIM, D_MODEL), lambda i, j: (0, 0)),
                  ms(gate_a),
                  pl.BlockSpec((1, D_MODEL), lambda i, j: (0, 0)),
                  ms(scale_m), ms(shift_m), ms(gate_m),
                  pl.BlockSpec((D_MODEL, tf), lambda i, j: (0, j)),
                  pl.BlockSpec((tf, D_MODEL), lambda i, j: (j, 0))],
        out_specs=pl.BlockSpec((tm, D_MODEL), lambda i, j: (i, 0)),
        out_shape=jax.ShapeDtypeStruct((rows, D_MODEL), F32),
        scratch_shapes=[pltpu.VMEM((tm, D_MODEL), F32), pltpu.VMEM((tm, D_MODEL), BF),
                        pltpu.VMEM((tm, D_MODEL), F32)],
        compiler_params=_params("parallel", "arbitrary"),
        name="post",
    )(x, o, w_out, gate_a, g_mlp, scale_m, shift_m, gate_m, w1, w2)


def _final_norm_kernel(x_ref, g_ref, y_ref):
    x = x_ref[...]
    y_ref[...] = x * lax.rsqrt(jnp.mean(x * x, axis=-1, keepdims=True) + NORM_EPS) * g_ref[...]


def _final_norm(x, g):
    rows = x.shape[0]
    tm = min(ROW_TILE, rows)
    return pl.pallas_call(
        _final_norm_kernel,
        grid=(rows // tm,),
        in_specs=[pl.BlockSpec((tm, D_MODEL), lambda i: (i, 0)), pl.BlockSpec((1, D_MODEL), lambda i: (0, 0))],
        out_specs=pl.BlockSpec((tm, D_MODEL), lambda i: (i, 0)),
        out_shape=jax.ShapeDtypeStruct((rows, D_MODEL), F32),
        compiler_params=_params("parallel"),
        name="final_norm",
    )(x, g)


def _block_mean_kernel(k_ref, o_ref):
    o_ref[...] = jnp.mean(k_ref[...], axis=0, keepdims=True)


def _block_means(k):
    nb = k.shape[0] // MOBA_BLOCK
    return pl.pallas_call(
        _block_mean_kernel,
        grid=(nb,),
        in_specs=[pl.BlockSpec((MOBA_BLOCK, KV_DIM), lambda i: (i, 0))],
        out_specs=pl.BlockSpec((None, 1, KV_DIM), lambda i: (i, 0, 0)),
        out_shape=jax.ShapeDtypeStruct((nb, 1, KV_DIM), F32),
        compiler_params=_params("parallel"),
        name="moba_block_means",
    )(k)


def _top_mask(gs, ids, n_sel, axis):
    picked = None
    for _ in range(n_sel):
        mx = jnp.max(gs, axis=axis, keepdims=True)
        am = jnp.min(jnp.where(gs == mx, ids, 2 ** 30), axis=axis, keepdims=True)
        hit = (ids == am) & (mx > NEG_INF)
        picked = hit if picked is None else (picked | hit)
        gs = jnp.where(ids == am, NEG_INF, gs)
    return picked


MOBA_ACC_ROWS = HEAD_DIM + 16
MOBA_MASKED = -1e30

def _moba_prompt_kernel(qi_tab, kj_tab, last_tab, qt_ref, kz_ref, vt_ref, kmz_ref, o_ref,
                        qz_scr, m_scr, acc_scr, *, n_sel):
    p = pl.program_id(1)
    i = qi_tab[p]
    j = kj_tab[p]
    blk = MOBA_BLOCK
    lane = lax.broadcasted_iota(I32, (blk, LANES), 1)
    ones_rows = (lax.broadcasted_iota(I32, (MOBA_ACC_ROWS - HEAD_DIM, blk), 0) == 0).astype(BF)

    @pl.when(i == j)
    def _():
        rows = lax.broadcasted_iota(I32, (LANES, blk), 0)
        earlier = (rows >= HEAD_DIM) & (rows < HEAD_DIM + i)
        for h in range(N_HEADS):
            qt = qt_ref[h * HEAD_DIM:(h + 1) * HEAD_DIM, :]
            gs = jnp.dot(kmz_ref[h // GROUP], qt.astype(F32), precision=lax.Precision.HIGHEST,
                         preferred_element_type=F32)
            picked = _top_mask(jnp.where(earlier, gs, NEG_INF), rows, n_sel, 0)
            bias = jnp.where(earlier & jnp.logical_not(picked), MOBA_MASKED, 0.0)
            qz_scr[h // GROUP, :, (h % GROUP) * blk:(h % GROUP + 1) * blk] = jnp.concatenate(
                [qt, bias[HEAD_DIM:].astype(BF)], axis=0)
        m_scr[...] = jnp.full(m_scr.shape, NEG_INF, F32)
        acc_scr[...] = jnp.zeros_like(acc_scr)

    def step(causal):
        for kv in range(N_KV_HEADS):
            kz = jnp.where(lane == HEAD_DIM + j, 1.0, kz_ref[:, kv * LANES:(kv + 1) * LANES]).astype(BF)
            vaug = jnp.concatenate([vt_ref[kv * HEAD_DIM:(kv + 1) * HEAD_DIM, :], ones_rows], axis=0)
            st = jnp.dot(kz, qz_scr[kv], preferred_element_type=F32)
            if causal:
                key_i = lax.broadcasted_iota(I32, (blk, GROUP * blk), 0)
                qry_i = lax.broadcasted_iota(I32, (blk, GROUP * blk), 1) % blk
                st = jnp.where(key_i <= qry_i, st, NEG_INF)
            m_old = m_scr[kv]
            m_new = jnp.maximum(m_old, jnp.max(st, axis=0, keepdims=True))
            alpha = jnp.exp(m_old - m_new)
            e = jnp.exp(st - m_new).astype(BF)
            acc_scr[kv] = alpha * acc_scr[kv] + jnp.dot(vaug, e, preferred_element_type=F32)
            m_scr[kv] = m_new

    @pl.when(i == j)
    def _():
        step(True)

    @pl.when(i != j)
    def _():
        step(False)

    @pl.when(last_tab[p] == 1)
    def _():
        for c in range(N_HEADS // 2):
            halves = []
            for h in (2 * c, 2 * c + 1):
                a = acc_scr[h // GROUP, :, (h % GROUP) * blk:(h % GROUP + 1) * blk]
                halves.append(a[:HEAD_DIM] * (1.0 / a[HEAD_DIM:HEAD_DIM + 1]))
            o_ref[:, c * LANES:(c + 1) * LANES] = jnp.concatenate(halves, axis=0).T.astype(BF)


def _moba_prompt(qt, kz, vt, k, batch):
    rows = qt.shape[1]
    seq = rows // batch
    blk = MOBA_BLOCK
    assert seq % blk == 0
    nblk = seq // blk
    assert HEAD_DIM + nblk <= LANES
    n_sel = min(MOBA_TOPK, nblk)
    kmean = _block_means(k).reshape(batch, nblk, N_KV_HEADS, HEAD_DIM).transpose(0, 2, 1, 3)
    kmz = jnp.pad(kmean, ((0, 0), (0, 0), (HEAD_DIM, LANES - HEAD_DIM - nblk), (0, 0)))
    qi, kj, last = [], [], []
    for i in range(nblk):
        order = [i] + list(range(i))
        qi += [i] * len(order)
        kj += order
        last += [0] * (len(order) - 1) + [1]
    tabs = [jnp.asarray(np.array(t, np.int32)) for t in (qi, kj, last)]
    grid_spec = pltpu.PrefetchScalarGridSpec(
        num_scalar_prefetch=3,
        grid=(batch, len(qi)),
        in_specs=[pl.BlockSpec((Q_DIM, blk), lambda b, p, qi, kj, la: (0, b * nblk + qi[p])),
                  pl.BlockSpec((blk, 2 * KV_DIM), lambda b, p, qi, kj, la: (b * nblk + kj[p], 0)),
                  pl.BlockSpec((KV_DIM, blk), lambda b, p, qi, kj, la: (0, b * nblk + kj[p])),
                  pl.BlockSpec((None, N_KV_HEADS, LANES, HEAD_DIM), lambda b, p, qi, kj, la: (b, 0, 0, 0))],
        out_specs=pl.BlockSpec((blk, Q_DIM), lambda b, p, qi, kj, la: (b * nblk + qi[p], 0)),
        scratch_shapes=[pltpu.VMEM((N_KV_HEADS, LANES, GROUP * blk), BF), pltpu.VMEM((N_KV_HEADS, 1, GROUP * blk), F32),
                        pltpu.VMEM((N_KV_HEADS, MOBA_ACC_ROWS, GROUP * blk), F32)])
    return pl.pallas_call(
        functools.partial(_moba_prompt_kernel, n_sel=n_sel),
        grid_spec=grid_spec,
        out_shape=jax.ShapeDtypeStruct((rows, Q_DIM), BF),
        compiler_params=_params("parallel", "arbitrary"),
        name="moba_prompt",
    )(*tabs, qt, kz, vt, kmz)


def _float_key(x):
    b = pltpu.bitcast(x, I32)
    return jnp.where(b < 0, b ^ 0x7FFFFFFF, b)


def _kth_largest_key(count_ge, k_top, shape):
    def body(it, cur):
        cand = cur | jnp.left_shift(jnp.int32(1), 31 - it)
        return jnp.where(count_ge(cand ^ INT_MIN) >= k_top, cand, cur)
    return lax.fori_loop(0, 32, body, jnp.zeros(shape, I32)) ^ INT_MIN


def _tie_cutoff(count_eq_below, need, n_bits, shape):
    def body(it, cur):
        cand = cur | jnp.left_shift(jnp.int32(1), n_bits - 1 - it)
        return jnp.where(count_eq_below(cand) < need, cand, cur)
    return lax.fori_loop(0, n_bits, body, jnp.zeros(shape, I32))


def _swap_halves(x):
    return jnp.concatenate([x[HEAD_DIM:], x[:HEAD_DIM]], axis=0)


def _dsa_prompt_kernel(qt_ref, qit_ref, wt_ref, kiz_ref, kz_ref, vt_ref, o_ref,
                       qp_scr, qip_scr, key_scr, m_scr, acc_scr, *, k_top, n_bits):
    i = pl.program_id(1)
    tq, ch = DSA_QUERY_TILE, DSA_KEY_CHUNK
    n_ch = ((i + 1) * tq + ch - 1) // ch
    qpos = i * tq + lax.broadcasted_iota(I32, (1, tq), 1)
    key_row = lax.broadcasted_iota(I32, (ch, tq), 0)
    ones_rows = (lax.broadcasted_iota(I32, (MOBA_ACC_ROWS - HEAD_DIM, ch), 0) == 0).astype(BF)

    for c in range(IDX_HEADS // 2):
        pair = qit_ref[c * LANES:(c + 1) * LANES, :]
        qip_scr[0, :, c * tq:(c + 1) * tq] = pair
        qip_scr[1, :, c * tq:(c + 1) * tq] = _swap_halves(pair)
    for c in range(N_HEADS // 2):
        pair = qt_ref[c * LANES:(c + 1) * LANES, :]
        cols = slice((c % 2) * tq, (c % 2 + 1) * tq)
        qp_scr[2 * (c // 2), :, cols] = pair
        qp_scr[2 * (c // 2) + 1, :, cols] = _swap_halves(pair)

    def score_body(c, carry):
        kiz = kiz_ref[pl.ds(pl.multiple_of(c * ch, ch), ch), :]
        sc = jnp.zeros((ch, tq), F32)
        for second in range(2):
            d = jnp.dot(kiz, qip_scr[second], preferred_element_type=F32)
            for u in range(IDX_HEADS // 2):
                h = 2 * u + second
                sc = sc + wt_ref[h:h + 1, :] * jnp.maximum(d[:, u * tq:(u + 1) * tq], 0.0)
        sc = jnp.where(c * ch + key_row <= qpos, sc, NEG_INF)
        key_scr[c] = _float_key(sc)
        return carry
    lax.fori_loop(0, n_ch, score_body, 0)

    def count(pred):
        def body(c, acc):
            return acc + jnp.sum(pred(key_scr[c], c * ch + key_row).astype(I32), axis=0, keepdims=True)
        return lax.fori_loop(0, n_ch, body, jnp.zeros((1, tq), I32))

    thr = _kth_largest_key(lambda t: count(lambda key, idx: key >= t), k_top, (1, tq))
    n_gt = count(lambda key, idx: key > thr)
    n_ge = count(lambda key, idx: key >= thr)
    need = k_top - n_gt
    tie_cols = ((n_ge > k_top) & (thr != KEY_NEG_INF)).astype(I32)
    cutoff = lax.cond(
        jnp.max(tie_cols) > 0,
        lambda: _tie_cutoff(lambda c_: count(lambda key, idx: (key == thr) & (idx < c_)), need, n_bits, (1, tq)),
        lambda: jnp.full((1, tq), 2 ** 30, I32))

    m_scr[...] = jnp.full(m_scr.shape, NEG_INF, F32)
    acc_scr[...] = jnp.zeros_like(acc_scr)

    def attn_body(c, carry):
        key = key_scr[c]
        idx = c * ch + key_row
        keep = ((key > thr) | ((key == thr) & (idx <= cutoff))) & (idx <= qpos)
        bias = jnp.where(keep, 0.0, NEG_INF)
        bias = jnp.concatenate([bias, bias], axis=1)
        rows = pl.ds(pl.multiple_of(c * ch, ch), ch)
        for kv in range(N_KV_HEADS):
            kz = kz_ref[rows, kv * LANES:(kv + 1) * LANES]
            vaug = jnp.concatenate([vt_ref[c, kv * HEAD_DIM:(kv + 1) * HEAD_DIM, :], ones_rows], axis=0)
            for second in range(2):
                g = 2 * kv + second
                st = jnp.dot(kz, qp_scr[g], preferred_element_type=F32) + bias
                m_old = m_scr[g]
                m_new = jnp.maximum(m_old, jnp.max(st, axis=0, keepdims=True))
                m_safe = jnp.where(m_new == NEG_INF, 0.0, m_new)
                alpha = jnp.exp(m_old - m_safe)
                e = jnp.exp(st - m_safe).astype(BF)
                acc_scr[g] = alpha * acc_scr[g] + jnp.dot(vaug, e, preferred_element_type=F32)
                m_scr[g] = m_new
        return carry
    lax.fori_loop(0, n_ch, attn_body, 0)

    for kv in range(N_KV_HEADS):
        outs = []
        for second in range(2):
            a = acc_scr[2 * kv + second]
            outs.append(a[:HEAD_DIM] * (1.0 / a[HEAD_DIM:HEAD_DIM + 1]))
        for u in range(2):
            both = jnp.concatenate([outs[0][:, u * tq:(u + 1) * tq], outs[1][:, u * tq:(u + 1) * tq]], axis=0)
            o_ref[:, (2 * kv + u) * LANES:(2 * kv + u + 1) * LANES] = both.T.astype(BF)


def _dsa_prompt(qt, qit, wt, kiz, kz, vt3, batch):
    rows = qt.shape[1]
    seq = rows // batch
    tq, ch = DSA_QUERY_TILE, DSA_KEY_CHUNK
    k_top = min(DSA_TOPK_MAX, seq // 4)
    assert seq % ch == 0 and ch >= k_top and ch % tq == 0 and tq % LANES == 0 and vt3.shape[2] == ch
    nq, n_chunks = seq // tq, seq // ch
    n_bits = max(1, (seq - 1).bit_length())
    tile = lambda n: pl.BlockSpec((n, tq), lambda b, i: (0, b * nq + i))
    return pl.pallas_call(
        functools.partial(_dsa_prompt_kernel, k_top=k_top, n_bits=n_bits),
        grid=(batch, nq),
        in_specs=[tile(Q_DIM), tile(IDX_Q_DIM), tile(IDX_HEADS),
                  pl.BlockSpec((seq, LANES), lambda b, i: (b, 0)),
                  pl.BlockSpec((seq, 2 * KV_DIM), lambda b, i: (b, 0)),
                  pl.BlockSpec((n_chunks, KV_DIM, ch), lambda b, i: (b, 0, 0))],
        out_specs=pl.BlockSpec((tq, Q_DIM), lambda b, i: (b * nq + i, 0)),
        out_shape=jax.ShapeDtypeStruct((rows, Q_DIM), BF),
        scratch_shapes=[pltpu.VMEM((2 * N_KV_HEADS, LANES, 2 * tq), BF),
                        pltpu.VMEM((2, LANES, (IDX_HEADS // 2) * tq), BF),
                        pltpu.VMEM((n_chunks, ch, tq), I32),
                        pltpu.VMEM((2 * N_KV_HEADS, 1, 2 * tq), F32),
                        pltpu.VMEM((2 * N_KV_HEADS, MOBA_ACC_ROWS, 2 * tq), F32)],
        compiler_params=_params("parallel", "arbitrary"),
        name="dsa_prompt",
    )(qt, qit, wt, kiz, kz, vt3)


def _swa_prompt_kernel(q_ref, kp_ref, kc_ref, vp_ref, vc_ref, sink_ref, o_ref):
    n = pl.program_id(1)
    w = WINDOW
    row = lax.broadcasted_iota(I32, (w, 2 * w), 0)
    col = lax.broadcasted_iota(I32, (w, 2 * w), 1)
    keep = (col >= row) & (col <= row + w) & ((col >= w) | (n > 0))
    for kv in range(N_KV_HEADS):
        ks = slice(kv * HEAD_DIM, (kv + 1) * HEAD_DIM)
        kk = jnp.concatenate([kp_ref[:, ks], kc_ref[:, ks]], axis=0)
        vv = jnp.concatenate([vp_ref[:, ks], vc_ref[:, ks]], axis=0)
        for g in range(GROUP):
            h = kv * GROUP + g
            hs = slice(h * HEAD_DIM, (h + 1) * HEAD_DIM)
            s = jnp.where(keep, lax.dot_general(q_ref[:, hs], kk, NT, preferred_element_type=F32), NEG_INF)
            sink = sink_ref[:, h:h + 1]
            m = jnp.maximum(jnp.max(s, axis=1, keepdims=True), sink)
            e = jnp.exp(s - m)
            den = jnp.sum(e, axis=1, keepdims=True) + jnp.exp(sink - m)
            o = jnp.dot(e.astype(BF), vv, preferred_element_type=F32)
            o_ref[:, hs] = (o * (1.0 / den)).astype(BF)


def _swa_prompt(q, kb, vb, sinks, batch):
    rows = q.shape[0]
    seq = rows // batch
    w = WINDOW
    assert seq % w == 0
    nb = seq // w
    cur = lambda b, n: (b * nb + n, 0)
    prev = lambda b, n: (b * nb + jnp.maximum(n - 1, 0), 0)
    return pl.pallas_call(
        _swa_prompt_kernel,
        grid=(batch, nb),
        in_specs=[pl.BlockSpec((w, Q_DIM), cur),
                  pl.BlockSpec((w, KV_DIM), prev), pl.BlockSpec((w, KV_DIM), cur),
                  pl.BlockSpec((w, KV_DIM), prev), pl.BlockSpec((w, KV_DIM), cur),
                  pl.BlockSpec((1, N_HEADS), lambda b, n: (0, 0))],
        out_specs=pl.BlockSpec((w, Q_DIM), cur),
        out_shape=jax.ShapeDtypeStruct((rows, Q_DIM), BF),
        compiler_params=_params("parallel", "parallel"),
        name="swa_prompt",
    )(q, kb, kb, vb, vb, sinks.reshape(1, N_HEADS))


def _row_token(n_rows, n_tok):
    r = lax.broadcasted_iota(I32, (n_rows, 1), 0)
    return (r // GROUP) % n_tok


def _block_diag_queries(q, n_seq, n_tok):
    qr = q.reshape(n_seq, n_tok, N_KV_HEADS, GROUP, HEAD_DIM).transpose(0, 2, 1, 3, 4)
    qr = qr.reshape(n_seq, N_KV_HEADS, n_tok * GROUP, HEAD_DIM)
    eye = jnp.eye(N_KV_HEADS, dtype=q.dtype)
    qbd = qr[:, :, :, None, :] * eye[None, :, None, :, None]
    return qbd.reshape(n_seq, N_KV_HEADS * n_tok * GROUP, KV_DIM)


def _undiag_outputs(out, n_seq, n_tok):
    o = out.reshape(n_seq, N_KV_HEADS, n_tok, GROUP, N_KV_HEADS, HEAD_DIM)
    o = jnp.stack([o[:, kv, :, :, kv] for kv in range(N_KV_HEADS)], axis=1)
    return o.transpose(0, 2, 1, 3, 4).reshape(n_seq * n_tok, Q_DIM)


def _tokens_on_lanes(cache):
    if cache.ndim == 5:
        t = jnp.transpose(cache, (0, 1, 3, 4, 2))
        return t.reshape(t.shape[:2] + (t.shape[2] * t.shape[3], t.shape[4]))
    return jnp.transpose(cache, (0, 1, 3, 2))


def _page_specs(n_pages_per_step, slot, width):
    def spec(r):
        return pl.BlockSpec((None, None, width, PAGE_SIZE),
                            lambda b, s, pt: (slot, pt[b, s * n_pages_per_step + r], 0, 0))
    return [spec(r) for r in range(n_pages_per_step)]


def _pad_rows(a, n_seq, n_tok):
    a = a.reshape(n_seq, n_tok, a.shape[-1])
    return jnp.pad(a, ((0, 0), (0, SUBLANES - n_tok), (0, 0)))


def _moba_sample_kernel(pt_ref, q_ref, kn_ref, vn_ref, *refs, n_tok, n_sel, pps):
    kt_pages = refs[:pps]
    vt_pages = refs[pps:2 * pps]
    o_ref = refs[2 * pps]
    m_scr, l_scr, g_scr, o_scr = refs[2 * pps + 1:]
    s = pl.program_id(1)
    nblk = m_scr.shape[0]
    q = q_ref[...]
    n_row = q.shape[0]
    for r in range(pps // 2):
        kt = jnp.concatenate([kt_pages[2 * r][...], kt_pages[2 * r + 1][...]], axis=1).astype(BF)
        vt = jnp.concatenate([vt_pages[2 * r][...], vt_pages[2 * r + 1][...]], axis=1).astype(BF)
        sc = jnp.dot(q, kt, preferred_element_type=F32)
        m = jnp.max(sc, axis=1, keepdims=True)
        e = jnp.exp(sc - m)
        blk = s * (pps // 2) + r
        m_scr[blk] = m
        l_scr[blk] = jnp.sum(e, axis=1, keepdims=True)
        g_scr[blk] = jnp.sum(sc, axis=1, keepdims=True)
        o_scr[blk] = lax.dot_general(e.astype(BF), vt, NT, preferred_element_type=F32)

    @pl.when(s == pl.num_programs(1) - 1)
    def _():
        blk_ids = lax.broadcasted_iota(I32, (nblk, n_row, 1), 0)
        picked = _top_mask(g_scr[...], blk_ids, n_sel, 0)
        m_all = m_scr[...]
        own_ok = lax.broadcasted_iota(I32, (n_row, SUBLANES), 1) <= _row_token(n_row, n_tok)
        s_own = jnp.where(own_ok, lax.dot_general(q, kn_ref[...], NT, preferred_element_type=F32), NEG_INF)
        m_tot = jnp.maximum(jnp.max(jnp.where(picked, m_all, NEG_INF), axis=0),
                            jnp.max(s_own, axis=1, keepdims=True))
        wgt = jnp.where(picked, jnp.exp(m_all - m_tot), 0.0)
        e_own = jnp.exp(s_own - m_tot)
        den = jnp.sum(wgt * l_scr[...], axis=0) + jnp.sum(e_own, axis=1, keepdims=True)
        num = jnp.dot(e_own.astype(BF), vn_ref[...], preferred_element_type=F32)
        for n in range(nblk):
            num = num + wgt[n] * o_scr[n]
        o_ref[...] = num * (1.0 / den)


def _paged_attention_call(kernel, name, slot, page_table, cache_kt, cache_vt, extra_inputs, extra_specs,
                          scratch_shapes, n_row):
    n_seq, n_pages = page_table.shape
    pps = PAGES_PER_STEP
    assert n_pages % pps == 0
    grid_spec = pltpu.PrefetchScalarGridSpec(
        num_scalar_prefetch=1,
        grid=(n_seq, n_pages // pps),
        in_specs=extra_specs + _page_specs(pps, slot, KV_DIM) + _page_specs(pps, slot, KV_DIM),
        out_specs=pl.BlockSpec((None, n_row, KV_DIM), lambda b, s, pt: (b, 0, 0)),
        scratch_shapes=scratch_shapes)
    return pl.pallas_call(
        kernel,
        grid_spec=grid_spec,
        out_shape=jax.ShapeDtypeStruct((n_seq, n_row, KV_DIM), F32),
        compiler_params=_params("parallel", "arbitrary"),
        name=name,
    )(page_table, *extra_inputs, *([cache_kt] * pps), *([cache_vt] * pps))


def _moba_sample(qbd, kn_pad, vn_pad, cache_kt, cache_vt, slot, page_table, n_tok):
    n_seq, n_pages = page_table.shape
    past = n_pages * PAGE_SIZE
    assert past % MOBA_BLOCK == 0 and MOBA_BLOCK == 2 * PAGE_SIZE and PAGES_PER_STEP % 2 == 0
    nblk = past // MOBA_BLOCK
    assert nblk <= 32
    n_row = qbd.shape[1]
    per_seq = lambda w: pl.BlockSpec((None, w[0], w[1]), lambda b, s, pt: (b, 0, 0))
    kernel = functools.partial(_moba_sample_kernel, n_tok=n_tok, n_sel=min(MOBA_TOPK, nblk), pps=PAGES_PER_STEP)
    return _paged_attention_call(
        kernel, "moba_sample", slot, page_table, cache_kt, cache_vt,
        [qbd, kn_pad, vn_pad],
        [per_seq((n_row, KV_DIM)), per_seq((SUBLANES, KV_DIM)), per_seq((SUBLANES, KV_DIM))],
        [pltpu.VMEM((nblk, n_row, 1), F32), pltpu.VMEM((nblk, n_row, 1), F32), pltpu.VMEM((nblk, n_row, 1), F32),
         pltpu.VMEM((nblk, n_row, KV_DIM), F32)],
        n_row)


def _dsa_index_kernel(pt_ref, qi_ref, w_ref, kin_ref, *refs, n_tok, k_top, ppi, n_bits):
    ki_pages = refs[:ppi]
    mask_ref = refs[ppi]
    key_scr = refs[ppi + 1]
    s = pl.program_id(1)
    n_steps = pl.num_programs(1)
    n_pages = n_steps * ppi
    qi = qi_ref[...]
    w = w_ref[...]
    fill = jnp.full((SUBLANES - n_tok, LANES), NEG_INF, F32)

    def scores(d):
        wd = (w * jnp.maximum(d, 0.0)).reshape(n_tok, IDX_HEADS, LANES)
        return jnp.sum(wd, axis=1)

    for r in range(ppi):
        sc = scores(jnp.dot(qi, ki_pages[r][...].astype(BF), preferred_element_type=F32))
        key_scr[s * ppi + r] = _float_key(jnp.concatenate([sc, fill], axis=0))

    @pl.when(s == n_steps - 1)
    def _():
        tok = lax.broadcasted_iota(I32, (n_tok, LANES), 0)
        lane_t = lax.broadcasted_iota(I32, (n_tok, LANES), 1)
        sc_new = scores(lax.dot_general(qi, kin_ref[...], NT, preferred_element_type=F32))
        sc_new = jnp.where(lane_t <= tok, sc_new, NEG_INF)
        key_scr[n_pages] = _float_key(jnp.concatenate([sc_new, fill], axis=0))
        lane = lax.broadcasted_iota(I32, (SUBLANES, LANES), 1)

        def count(pred):
            def body(c, acc):
                return acc + pred(key_scr[c], c * PAGE_SIZE + lane).astype(I32)
            return jnp.sum(lax.fori_loop(0, n_pages + 1, body, jnp.zeros((SUBLANES, LANES), I32)),
                           axis=1, keepdims=True)

        thr = _kth_largest_key(lambda t: count(lambda key, idx: key >= t), k_top, (SUBLANES, 1))
        need = k_top - count(lambda key, idx: key > thr)
        cutoff = _tie_cutoff(lambda c_: count(lambda key, idx: (key == thr) & (idx < c_)), need, n_bits,
                             (SUBLANES, 1))

        def emit(c, carry):
            key = key_scr[c]
            keep = (key > thr) | ((key == thr) & (c * PAGE_SIZE + lane <= cutoff))
            mask_ref[c] = jnp.where(keep & (key != KEY_NEG_INF), 1.0, 0.0).astype(BF)
            return carry
        lax.fori_loop(0, n_pages + 1, emit, 0)


def _dsa_index(qi_rows, w_rows, ki_new, cache_idx, islot, page_table, n_tok):
    n_seq, n_pages = page_table.shape
    ppi = min(INDEX_PAGES_PER_STEP, n_pages)
    total = n_pages * PAGE_SIZE + n_tok
    k_top = min(DSA_TOPK_MAX, total // 4)
    assert n_pages % ppi == 0 and n_tok <= SUBLANES and n_pages * PAGE_SIZE >= k_top
    n_bits = max(1, ((n_pages + 1) * PAGE_SIZE - 1).bit_length())
    nr = n_tok * IDX_HEADS
    grid_spec = pltpu.PrefetchScalarGridSpec(
        num_scalar_prefetch=1,
        grid=(n_seq, n_pages // ppi),
        in_specs=[pl.BlockSpec((None, nr, IDX_DIM), lambda b, s, pt: (b, 0, 0)),
                  pl.BlockSpec((None, nr, 1), lambda b, s, pt: (b, 0, 0)),
                  pl.BlockSpec((None, LANES, IDX_DIM), lambda b, s, pt: (b, 0, 0))]
        + _page_specs(ppi, islot, IDX_DIM),
        out_specs=pl.BlockSpec((None, n_pages + 1, SUBLANES, LANES), lambda b, s, pt: (b, 0, 0, 0)),
        scratch_shapes=[pltpu.VMEM((n_pages + 1, SUBLANES, LANES), I32)])
    return pl.pallas_call(
        functools.partial(_dsa_index_kernel, n_tok=n_tok, k_top=k_top, ppi=ppi, n_bits=n_bits),
        grid_spec=grid_spec,
        out_shape=jax.ShapeDtypeStruct((n_seq, n_pages + 1, SUBLANES, LANES), BF),
        compiler_params=_params("parallel", "arbitrary"),
        name="dsa_index",
    )(page_table, qi_rows, w_rows, ki_new, *([cache_idx] * ppi))


def _dsa_sample_kernel(pt_ref, q_ref, kn_ref, vn_ref, mask_ref, mnew_ref, *refs, n_tok, pps):
    kt_pages = refs[:pps]
    vt_pages = refs[pps:2 * pps]
    o_ref = refs[2 * pps]
    m_scr, l_scr, acc_scr = refs[2 * pps + 1:]
    s = pl.program_id(1)
    q = q_ref[...]
    n_row = q.shape[0]
    expand = (lax.broadcasted_iota(I32, (n_row, SUBLANES), 1) == _row_token(n_row, n_tok)).astype(BF)

    @pl.when(s == 0)
    def _():
        m_scr[...] = jnp.full(m_scr.shape, NEG_INF, F32)
        l_scr[...] = jnp.zeros_like(l_scr)
        acc_scr[...] = jnp.zeros_like(acc_scr)

    def update(sc, keep, pv):
        sc = jnp.where(keep > 0.5, sc, NEG_INF)
        m_old = m_scr[...]
        m_new = jnp.maximum(m_old, jnp.max(sc, axis=1, keepdims=True))
        m_safe = jnp.where(m_new == NEG_INF, 0.0, m_new)
        alpha = jnp.exp(m_old - m_safe)
        e = jnp.exp(sc - m_safe)
        l_scr[...] = alpha * l_scr[...] + jnp.sum(e, axis=1, keepdims=True)
        acc_scr[...] = alpha * acc_scr[...] + pv(e.astype(BF))
        m_scr[...] = m_new

    for r in range(pps):
        sc = jnp.dot(q, kt_pages[r][...].astype(BF), preferred_element_type=F32)
        keep = jnp.dot(expand, mask_ref[r], preferred_element_type=F32)
        vt = vt_pages[r][...].astype(BF)
        update(sc, keep, lambda e, vt=vt: lax.dot_general(e, vt, NT, preferred_element_type=F32))

    @pl.when(s == pl.num_programs(1) - 1)
    def _():
        sc = lax.dot_general(q, kn_ref[...], NT, preferred_element_type=F32)
        keep = jnp.dot(expand, mnew_ref[...], preferred_element_type=F32)[:, :SUBLANES]
        update(sc, keep, lambda e: jnp.dot(e, vn_ref[...], preferred_element_type=F32))
        o_ref[...] = acc_scr[...] * (1.0 / l_scr[...])


def _dsa_sample(qbd, kn_pad, vn_pad, mask, cache_kt, cache_vt, slot, page_table, n_tok):
    n_seq, n_pages = page_table.shape
    n_row = qbd.shape[1]
    pps = PAGES_PER_STEP
    per_seq = lambda w: pl.BlockSpec((None, w[0], w[1]), lambda b, s, pt: (b, 0, 0))
    return _paged_attention_call(
        functools.partial(_dsa_sample_kernel, n_tok=n_tok, pps=pps), "dsa_sample", slot, page_table,
        cache_kt, cache_vt,
        [qbd, kn_pad, vn_pad, mask, mask],
        [per_seq((n_row, KV_DIM)), per_seq((SUBLANES, KV_DIM)), per_seq((SUBLANES, KV_DIM)),
         pl.BlockSpec((None, pps, SUBLANES, LANES), lambda b, s, pt: (b, s, 0, 0)),
         pl.BlockSpec((None, None, SUBLANES, LANES), lambda b, s, pt: (b, n_pages, 0, 0))],
        [pltpu.VMEM((n_row, 1), F32), pltpu.VMEM((n_row, 1), F32), pltpu.VMEM((n_row, KV_DIM), F32)],
        n_row)


def _swa_sample_kernel(q_ref, kt_ref, vt_ref, kn_ref, vn_ref, sink_ref, o_ref, *, n_tok):
    q = q_ref[...]
    n_row = q.shape[0]
    tok_r = _row_token(n_row, n_tok)
    wk = kt_ref.shape[1]
    s_buf = jnp.dot(q, kt_ref[...].astype(BF), preferred_element_type=F32)
    s_buf = jnp.where(lax.broadcasted_iota(I32, (n_row, wk), 1) >= tok_r, s_buf, NEG_INF)
    s_new = lax.dot_general(q, kn_ref[...], NT, preferred_element_type=F32)
    s_new = jnp.where(lax.broadcasted_iota(I32, (n_row, SUBLANES), 1) <= tok_r, s_new, NEG_INF)
    sink = sink_ref[...]
    m = jnp.maximum(jnp.maximum(jnp.max(s_buf, axis=1, keepdims=True), jnp.max(s_new, axis=1, keepdims=True)), sink)
    e_buf = jnp.exp(s_buf - m)
    e_new = jnp.exp(s_new - m)
    den = jnp.sum(e_buf, axis=1, keepdims=True) + jnp.sum(e_new, axis=1, keepdims=True) + jnp.exp(sink - m)
    num = (lax.dot_general(e_buf.astype(BF), vt_ref[...].astype(BF), NT, preferred_element_type=F32)
           + jnp.dot(e_new.astype(BF), vn_ref[...], preferred_element_type=F32))
    o_ref[...] = num * (1.0 / den)


def _swa_sample(qbd, kn_pad, vn_pad, buf_kt, buf_vt, sinks, n_tok):
    n_seq, _, wk = buf_kt.shape
    assert wk == WINDOW
    n_row = qbd.shape[1]
    sink_rows = jnp.repeat(sinks.reshape(N_KV_HEADS, 1, GROUP), n_tok, axis=1).reshape(n_row, 1)
    per_seq = lambda a, c: pl.BlockSpec((None, a, c), lambda b: (b, 0, 0))
    return pl.pallas_call(
        functools.partial(_swa_sample_kernel, n_tok=n_tok),
        grid=(n_seq,),
        in_specs=[per_seq(n_row, KV_DIM), per_seq(KV_DIM, wk), per_seq(KV_DIM, wk),
                  per_seq(SUBLANES, KV_DIM), per_seq(SUBLANES, KV_DIM),
                  pl.BlockSpec((n_row, 1), lambda b: (0, 0))],
        out_specs=per_seq(n_row, KV_DIM),
        out_shape=jax.ShapeDtypeStruct((n_seq, n_row, KV_DIM), F32),
        compiler_params=_params("parallel"),
        name="swa_sample",
    )(qbd, buf_kt, buf_vt, kn_pad, vn_pad, sink_rows)


def _rope_tables(pos):
    half = HEAD_DIM // 2
    inv = ROPE_THETA ** (-jnp.arange(half, dtype=F32) / half)
    ang = pos.astype(F32)[:, None] * inv[None, :]
    cos, sin = jnp.cos(ang), jnp.sin(ang)
    reps = LANES // HEAD_DIM
    return jnp.tile(jnp.concatenate([cos, cos], axis=1), (1, reps)), jnp.tile(jnp.concatenate([-sin, sin], axis=1), (1, reps))


def kernel(x_prompt, x_sample, cache_k, cache_v, cache_idx_k, state_swa_k, state_swa_v, page_table, c_prompt, c_sample, g_attn, g_mlp, w_mod, b_mod, w_in_moba, w_in_dsa, w_in_swa, swa_sinks, w_out, w_ff1, w_ff2, g_final):
    batch, seq, _ = x_prompt.shape
    n_seq, n_tok, _ = x_sample.shape
    depth = g_attn.shape[0]
    n_pages = page_table.shape[1]
    past = n_pages * PAGE_SIZE
    rows_p, rows_s = batch * seq, n_seq * n_tok

    n_cond = batch + n_seq
    pad_cond = -n_cond % SUBLANES
    c_all = jnp.pad(jnp.concatenate([c_prompt, c_sample], axis=0), ((0, pad_cond), (0, 0)))
    mod = _adaln(c_all, w_mod, b_mod)

    def mods(layer):
        parts = [mod[layer, :, r * D_MODEL:(r + 1) * D_MODEL] for r in range(6)]
        mp = [p[:batch].reshape(batch, 1, D_MODEL) for p in parts]
        ms = [jnp.repeat(p[batch:n_cond], n_tok, axis=0) for p in parts]
        return mp, ms

    cos_p, sin_p = _rope_tables(jnp.arange(seq, dtype=jnp.int32))
    cos_s, sin_s = _rope_tables(jnp.tile(past + jnp.arange(n_tok, dtype=jnp.int32), n_seq))
    cos_pt, sin_pt = cos_p[:, :HEAD_DIM].T, sin_p[:, :HEAD_DIM].T

    cache_kt, cache_vt, cache_it = _tokens_on_lanes(cache_k), _tokens_on_lanes(cache_v), _tokens_on_lanes(cache_idx_k)
    swa_kt, swa_vt = _tokens_on_lanes(state_swa_k), _tokens_on_lanes(state_swa_v)

    xp = x_prompt.reshape(rows_p, D_MODEL)
    xs = x_sample.reshape(rows_s, D_MODEL)
    kp_l, vp_l, ks_l, vs_l, ip_l, is_l = [], [], [], [], [], []
    skp_l, svp_l, sks_l, svs_l = [], [], [], []
    for i, (mixer, j, slot) in enumerate(_layer_plan(depth)):
        (sh_ap, sc_ap, gt_ap, sh_mp, sc_mp, gt_mp), (sh_as, sc_as, gt_as, sh_ms, sc_ms, gt_ms) = mods(i)
        dsa = mixer == 1
        if mixer == 0:
            w_in = w_in_moba[j]
        elif dsa:
            w_in = jnp.pad(w_in_dsa[j], ((0, 0), (0, -IN_DSA % LANES)))
        else:
            w_in = w_in_swa[j]
        w_in = w_in.astype(BF)
        g_a = g_attn[i].reshape(1, D_MODEL)
        if mixer < 2:
            outs_p = _project_t(xp, g_a, sc_ap, sh_ap, cos_p, sin_p, cos_pt, sin_pt, w_in, seq, dsa)
            qt_p, k_p, v_p, kz_p, vt_p = outs_p[:5]
        else:
            q_p, k_p, v_p, kb_p, vb_p = _project(xp, g_a, sc_ap, sh_ap, cos_p, sin_p, w_in, seq, False)
        outs_s = _project(xs, g_a, sc_as, sh_as, cos_s, sin_s, w_in, n_tok, dsa)
        q_s, k_s, v_s, kb_s, vb_s = outs_s[:5]
        qbd = _block_diag_queries(q_s, n_seq, n_tok)
        kn_pad = _pad_rows(kb_s, n_seq, n_tok)
        vn_pad = _pad_rows(vb_s, n_seq, n_tok)
        k_p5 = k_p.reshape(batch, seq, N_KV_HEADS, HEAD_DIM)
        v_p5 = v_p.reshape(batch, seq, N_KV_HEADS, HEAD_DIM)
        k_s5 = k_s.reshape(n_seq, n_tok, N_KV_HEADS, HEAD_DIM)
        v_s5 = v_s.reshape(n_seq, n_tok, N_KV_HEADS, HEAD_DIM)
        if mixer == 0:
            o_p = _moba_prompt(qt_p, kz_p, vt_p, k_p, batch)
            o_t = _moba_sample(qbd, kn_pad, vn_pad, cache_kt, cache_vt, slot, page_table, n_tok)
        elif dsa:
            qit_p, wt_p, rest_p, kiz_p = outs_p[5:]
            qi_s, rest_s, restb_s = outs_s[5:]
            o_p = _dsa_prompt(qt_p, qit_p, wt_p, kiz_p, kz_p, vt_p, batch)
            qi_rows = qi_s.reshape(n_seq, n_tok * IDX_HEADS, IDX_DIM)
            w_rows = rest_s[:, IDX_DIM:IDX_DIM + IDX_HEADS].reshape(n_seq, n_tok * IDX_HEADS, 1)
            ki_new = jnp.pad(restb_s[:, :IDX_DIM].reshape(n_seq, n_tok, IDX_DIM),
                             ((0, 0), (0, LANES - n_tok), (0, 0)))
            mask = _dsa_index(qi_rows, w_rows, ki_new, cache_it, j, page_table, n_tok)
            o_t = _dsa_sample(qbd, kn_pad, vn_pad, mask, cache_kt, cache_vt, slot, page_table, n_tok)
            ip_l.append(rest_p[:, :IDX_DIM].reshape(batch, seq, IDX_DIM))
            is_l.append(rest_s[:, :IDX_DIM].reshape(n_seq, n_tok, IDX_DIM))
        else:
            o_p = _swa_prompt(q_p, kb_p, vb_p, swa_sinks[j], batch)
            o_t = _swa_sample(qbd, kn_pad, vn_pad, swa_kt[j], swa_vt[j], swa_sinks[j], n_tok)
            keep_p = min(WINDOW, seq)
            skp_l.append(k_p5[:, seq - keep_p:])
            svp_l.append(v_p5[:, seq - keep_p:])
            wk = state_swa_k.shape[2]
            sks_l.append(jnp.concatenate([state_swa_k[j], k_s5], axis=1)[:, -wk:])
            svs_l.append(jnp.concatenate([state_swa_v[j], v_s5], axis=1)[:, -wk:])
        if mixer < 2:
            kp_l.append(k_p5)
            vp_l.append(v_p5)
            ks_l.append(k_s5)
            vs_l.append(v_s5)
        o_s = _undiag_outputs(o_t, n_seq, n_tok).astype(BF)
        wo, w1, w2 = w_out[i].astype(BF), w_ff1[i].astype(BF), w_ff2[i].astype(BF)
        g_m = g_mlp[i].reshape(1, D_MODEL)
        xp = _post(xp, o_p, wo, gt_ap, g_m, sc_mp, sh_mp, gt_mp, w1, w2, seq)
        xs = _post(xs, o_s, wo, gt_as, g_m, sc_ms, sh_ms, gt_ms, w1, w2, n_tok)
    g_f = g_final.reshape(1, D_MODEL)
    y_prompt = _final_norm(xp, g_f).reshape(batch, seq, D_MODEL)
    y_sample = _final_norm(xs, g_f).reshape(n_seq, n_tok, D_MODEL)
    return (y_prompt, y_sample, jnp.stack(kp_l), jnp.stack(vp_l), jnp.stack(ks_l), jnp.stack(vs_l),
            jnp.stack(ip_l), jnp.stack(is_l), jnp.stack(skp_l), jnp.stack(svp_l), jnp.stack(sks_l), jnp.stack(svs_l))
```

```python
import functools

import numpy as np
import jax
import jax.numpy as jnp
from jax import lax
from jax.experimental import pallas as pl
from jax.experimental.pallas import tpu as pltpu

D_MODEL = 1024
N_HEADS = 16
HEAD_DIM = D_MODEL // N_HEADS
N_KV_HEADS = 4
GROUP = N_HEADS // N_KV_HEADS
Q_DIM = N_HEADS * HEAD_DIM
KV_DIM = N_KV_HEADS * HEAD_DIM
D_FF = 4 * D_MODEL
ROPE_THETA = 10000.0
NORM_EPS = 1e-6
N_MIXERS = 3
PAGE_SIZE = 128
MOBA_BLOCK = 256
MOBA_TOPK = 3
IDX_HEADS = 8
IDX_DIM = 64
DSA_TOPK_MAX = 256
WINDOW = 128
IN_ATTN = Q_DIM + 2 * KV_DIM
IDX_Q_DIM = IDX_HEADS * IDX_DIM
IN_DSA = IN_ATTN + IDX_Q_DIM + IDX_DIM + IDX_HEADS
ATTN_SCALE = HEAD_DIM ** -0.5
IDX_W_SCALE = IDX_Q_DIM ** -0.5

LANES = 128
SUBLANES = 8
VMEM_LIMIT = 56 << 20
ROW_TILE = 512
FF_TILE = 1024
DSA_KEY_CHUNK = 512
DSA_QUERY_TILE = 256
PAGES_PER_STEP = 16
INDEX_PAGES_PER_STEP = 64
DSA_PAGES_PER_UPDATE = 4

BF = jnp.bfloat16
F32 = jnp.float32
I32 = jnp.int32
NEG_INF = float("-inf")
INT_MIN = -2 ** 31
KEY_NEG_INF = INT_MIN + 0x7FFFFF
NT = (((1,), (1,)), ((), ()))


def _params(*semantics):
    return pltpu.CompilerParams(dimension_semantics=semantics, vmem_limit_bytes=VMEM_LIMIT)


def _layer_plan(depth):
    plan, counts, n_paged = [], [0] * N_MIXERS, 0
    for i in range(depth):
        m = i % N_MIXERS
        slot = -1
        if m < 2:
            slot = n_paged
            n_paged += 1
        plan.append((m, counts[m], slot))
        counts[m] += 1
    return plan


def _adaln_kernel(c_ref, w_ref, b_ref, o_ref):
    c = c_ref[...]
    a = (c * (1.0 / (1.0 + jnp.exp(-c)))).astype(BF)
    o_ref[...] = jnp.dot(a, w_ref[...].astype(BF), preferred_element_type=F32) + b_ref[...]


def _adaln(c_all, w_mod, b_mod):
    depth, _, n_out = w_mod.shape
    nc = c_all.shape[0]
    tn = 1536
    return pl.pallas_call(
        _adaln_kernel,
        grid=(depth, n_out // tn),
        in_specs=[pl.BlockSpec((nc, D_MODEL), lambda l, j: (0, 0)),
                  pl.BlockSpec((None, D_MODEL, tn), lambda l, j: (l, 0, j)),
                  pl.BlockSpec((None, 1, tn), lambda l, j: (l, 0, j))],
        out_specs=pl.BlockSpec((None, nc, tn), lambda l, j: (l, 0, j)),
        out_shape=jax.ShapeDtypeStruct((depth, nc, n_out), F32),
        compiler_params=_params("parallel", "parallel"),
        name="adaln",
    )(c_all, w_mod, b_mod.reshape(depth, 1, n_out))


def _norm_mod(x, g, scale, shift):
    y = x * lax.rsqrt(jnp.mean(x * x, axis=-1, keepdims=True) + NORM_EPS)
    return (y * g) * (1.0 + scale) + shift


def _rope_chunk(z, cos, sin_signed, lo_half):
    partner = jnp.where(lo_half, pltpu.roll(z, LANES - HEAD_DIM // 2, 1), pltpu.roll(z, HEAD_DIM // 2, 1))
    return z * cos + partner * sin_signed


def _proj_kernel(x_ref, g_ref, sc_ref, sh_ref, cos_ref, sin_ref, w_ref, *out_refs, dsa):
    hb = _norm_mod(x_ref[...], g_ref[...], sc_ref[...], sh_ref[...]).astype(BF)
    cos = cos_ref[...]
    sin_s = sin_ref[...]
    lane = lax.broadcasted_iota(I32, (1, LANES), 1)
    lo_half = (lane % HEAD_DIM) < (HEAD_DIM // 2)
    q_ref, k_ref, v_ref, kb_ref, vb_ref = out_refs[:5]

    zq = jnp.dot(hb, w_ref[:, :Q_DIM], preferred_element_type=F32)
    for c in range(Q_DIM // LANES):
        sl = slice(c * LANES, (c + 1) * LANES)
        q_ref[:, sl] = (_rope_chunk(zq[:, sl], cos, sin_s, lo_half) * ATTN_SCALE).astype(BF)
    zk = jnp.dot(hb, w_ref[:, Q_DIM:Q_DIM + KV_DIM], preferred_element_type=F32)
    for c in range(KV_DIM // LANES):
        sl = slice(c * LANES, (c + 1) * LANES)
        kr = _rope_chunk(zk[:, sl], cos, sin_s, lo_half)
        k_ref[:, sl] = kr
        kb_ref[:, sl] = kr.astype(BF)
    zv = jnp.dot(hb, w_ref[:, Q_DIM + KV_DIM:IN_ATTN], preferred_element_type=F32)
    v_ref[...] = zv
    vb_ref[...] = zv.astype(BF)
    if dsa:
        qi_ref, rest_ref, restb_ref = out_refs[5:]
        zi = jnp.dot(hb, w_ref[:, IN_ATTN:IN_ATTN + IDX_Q_DIM], preferred_element_type=F32)
        for c in range(IDX_Q_DIM // LANES):
            sl = slice(c * LANES, (c + 1) * LANES)
            qi_ref[:, sl] = _rope_chunk(zi[:, sl], cos, sin_s, lo_half).astype(BF)
        zr = jnp.dot(hb, w_ref[:, IN_ATTN + IDX_Q_DIM:], preferred_element_type=F32)
        roped = _rope_chunk(zr, cos, sin_s, lo_half)
        rest = jnp.where(lane < IDX_DIM, roped, zr * IDX_W_SCALE)
        rest_ref[...] = rest
        restb_ref[...] = rest.astype(BF)


def _mod_spec(arr, tm, rows_per_seq):
    if arr.ndim == 3:
        return pl.BlockSpec((None, 1, D_MODEL), lambda i, *_: ((i * tm) // rows_per_seq, 0, 0))
    return pl.BlockSpec((tm, D_MODEL), lambda i, *_: (i, 0))


def _project(x, g, scale, shift, cos_tab, sin_tab, w, rows_per_seq, dsa):
    rows = x.shape[0]
    tm = min(ROW_TILE, rows)
    n_in = w.shape[1]
    tab_tiles = cos_tab.shape[0] // tm
    row_spec = lambda n: pl.BlockSpec((tm, n), lambda i: (i, 0))
    tab_spec = pl.BlockSpec((tm, LANES), lambda i: (i % tab_tiles, 0))
    out_shape = [jax.ShapeDtypeStruct((rows, Q_DIM), BF),
                 jax.ShapeDtypeStruct((rows, KV_DIM), F32), jax.ShapeDtypeStruct((rows, KV_DIM), F32),
                 jax.ShapeDtypeStruct((rows, KV_DIM), BF), jax.ShapeDtypeStruct((rows, KV_DIM), BF)]
    out_specs = [row_spec(Q_DIM), row_spec(KV_DIM), row_spec(KV_DIM), row_spec(KV_DIM), row_spec(KV_DIM)]
    if dsa:
        out_shape += [jax.ShapeDtypeStruct((rows, IDX_Q_DIM), BF),
                      jax.ShapeDtypeStruct((rows, LANES), F32), jax.ShapeDtypeStruct((rows, LANES), BF)]
        out_specs += [row_spec(IDX_Q_DIM), row_spec(LANES), row_spec(LANES)]
    return pl.pallas_call(
        functools.partial(_proj_kernel, dsa=dsa),
        grid=(rows // tm,),
        in_specs=[row_spec(D_MODEL),
                  pl.BlockSpec((1, D_MODEL), lambda i: (0, 0)),
                  _mod_spec(scale, tm, rows_per_seq), _mod_spec(shift, tm, rows_per_seq),
                  tab_spec, tab_spec,
                  pl.BlockSpec((D_MODEL, n_in), lambda i: (0, 0))],
        out_specs=out_specs,
        out_shape=out_shape,
        compiler_params=_params("parallel"),
        name="project_dsa" if dsa else "project",
    )(x, g, scale, shift, cos_tab, sin_tab, w)


def _rope_rows(zt, cos_t, sin_t, n_heads, scale):
    half = HEAD_DIM // 2
    out = []
    for h in range(n_heads):
        x = zt[h * HEAD_DIM:(h + 1) * HEAD_DIM]
        partner = jnp.concatenate([x[half:], x[:half]], axis=0)
        out.append(((x * cos_t + partner * sin_t) * scale).astype(BF))
    return out


def _proj_t_kernel(x_ref, g_ref, sc_ref, sh_ref, cos_ref, sin_ref, cost_ref, sint_ref, w_ref, wqt_ref, wvt_ref,
                   *refs, dsa):
    if dsa:
        wit_ref, wrt_ref = refs[:2]
        refs = refs[2:]
    qt_ref, k_ref, v_ref, kz_ref, vt_ref = refs[:5]
    hb = _norm_mod(x_ref[...], g_ref[...], sc_ref[...], sh_ref[...]).astype(BF)
    lane = lax.broadcasted_iota(I32, (1, LANES), 1)
    lo_half = (lane % HEAD_DIM) < (HEAD_DIM // 2)

    zqt = lax.dot_general(wqt_ref[...], hb, NT, preferred_element_type=F32)
    cos_t, sin_t = cost_ref[...], sint_ref[...]
    for h, qh in enumerate(_rope_rows(zqt, cos_t, sin_t, N_HEADS, ATTN_SCALE)):
        qt_ref[h * HEAD_DIM:(h + 1) * HEAD_DIM, :] = qh
    if dsa:
        qit_ref, wt_ref, rest_ref, kiz_ref = refs[5:]
        zit = lax.dot_general(wit_ref[...], hb, NT, preferred_element_type=F32)
        for h, qh in enumerate(_rope_rows(zit, cos_t, sin_t, IDX_HEADS, 1.0)):
            qit_ref[h * IDX_DIM:(h + 1) * IDX_DIM, :] = qh
        zr = jnp.dot(hb, w_ref[:, IN_ATTN + IDX_Q_DIM:], preferred_element_type=F32)
        roped = _rope_chunk(zr, cos_ref[...], sin_ref[...], lo_half)
        rest_ref[...] = jnp.where(lane < IDX_DIM, roped, zr * IDX_W_SCALE)
        kiz_ref[...] = jnp.where(lane < IDX_DIM, roped, 0.0).astype(BF)
        zrt = lax.dot_general(wrt_ref[...], hb, NT, preferred_element_type=F32)
        wt_ref[...] = zrt[IDX_DIM:IDX_DIM + IDX_HEADS] * IDX_W_SCALE
    zk = jnp.dot(hb, w_ref[:, Q_DIM:Q_DIM + KV_DIM], preferred_element_type=F32)
    for c in range(KV_DIM // LANES):
        sl = slice(c * LANES, (c + 1) * LANES)
        kr = _rope_chunk(zk[:, sl], cos_ref[...], sin_ref[...], lo_half)
        k_ref[:, sl] = kr
        kz_ref[:, 2 * c * LANES:(2 * c + 1) * LANES] = jnp.where(lane < HEAD_DIM, kr, 0.0).astype(BF)
        kz_ref[:, (2 * c + 1) * LANES:(2 * c + 2) * LANES] = jnp.where(
            lane < HEAD_DIM, pltpu.roll(kr, HEAD_DIM, 1), 0.0).astype(BF)
    v_ref[...] = jnp.dot(hb, w_ref[:, Q_DIM + KV_DIM:IN_ATTN], preferred_element_type=F32)
    vt_ref[...] = lax.dot_general(wvt_ref[...], hb, NT, preferred_element_type=F32).astype(BF)


def _project_t(x, g, scale, shift, cos_tab, sin_tab, cos_t, sin_t, w, rows_per_seq, dsa):
    rows = x.shape[0]
    tm = min(ROW_TILE, rows)
    tab_tiles = cos_tab.shape[0] // tm
    row_spec = lambda n: pl.BlockSpec((tm, n), lambda i: (i, 0))
    col_spec = lambda n: pl.BlockSpec((n, tm), lambda i: (0, i))
    tab_spec = pl.BlockSpec((tm, LANES), lambda i: (i % tab_tiles, 0))
    tabt_spec = pl.BlockSpec((HEAD_DIM, tm), lambda i: (0, i % tab_tiles))
    whole = lambda a: pl.BlockSpec(a.shape, lambda i: (0, 0))
    weights = [w, w[:, :Q_DIM].T, w[:, Q_DIM + KV_DIM:IN_ATTN].T]
    out_specs = [col_spec(Q_DIM), row_spec(KV_DIM), row_spec(KV_DIM), row_spec(2 * KV_DIM), col_spec(KV_DIM)]
    out_shape = [jax.ShapeDtypeStruct((Q_DIM, rows), BF),
                 jax.ShapeDtypeStruct((rows, KV_DIM), F32), jax.ShapeDtypeStruct((rows, KV_DIM), F32),
                 jax.ShapeDtypeStruct((rows, 2 * KV_DIM), BF), jax.ShapeDtypeStruct((KV_DIM, rows), BF)]
    if dsa:
        weights += [w[:, IN_ATTN:IN_ATTN + IDX_Q_DIM].T, w[:, IN_ATTN + IDX_Q_DIM:].T]
        out_specs[4] = pl.BlockSpec((None, KV_DIM, tm), lambda i: (i, 0, 0))
        out_shape[4] = jax.ShapeDtypeStruct((rows // tm, KV_DIM, tm), BF)
        out_specs += [col_spec(IDX_Q_DIM), col_spec(IDX_HEADS), row_spec(LANES), row_spec(LANES)]
        out_shape += [jax.ShapeDtypeStruct((IDX_Q_DIM, rows), BF), jax.ShapeDtypeStruct((IDX_HEADS, rows), F32),
                      jax.ShapeDtypeStruct((rows, LANES), F32), jax.ShapeDtypeStruct((rows, LANES), BF)]
    return pl.pallas_call(
        functools.partial(_proj_t_kernel, dsa=dsa),
        grid=(rows // tm,),
        in_specs=[row_spec(D_MODEL),
                  pl.BlockSpec((1, D_MODEL), lambda i: (0, 0)),
                  _mod_spec(scale, tm, rows_per_seq), _mod_spec(shift, tm, rows_per_seq),
                  tab_spec, tab_spec, tabt_spec, tabt_spec] + [whole(a) for a in weights],
        out_specs=out_specs,
        out_shape=out_shape,
        compiler_params=_params("parallel"),
        name="project_t_dsa" if dsa else "project_t",
    )(x, g, scale, shift, cos_tab, sin_tab, cos_t, sin_t, *weights)


def _post_kernel(x_ref, o_ref, wo_ref, ga_ref, g_ref, sc_ref, sh_ref, gm_ref, w1_ref, w2_ref, y_ref,
                 x1_scr, h_scr, acc_scr):
    j = pl.program_id(1)

    @pl.when(j == 0)
    def _():
        x1 = x_ref[...] + ga_ref[...] * jnp.dot(o_ref[...], wo_ref[...], preferred_element_type=F32)
        x1_scr[...] = x1
        h_scr[...] = _norm_mod(x1, g_ref[...], sc_ref[...], sh_ref[...]).astype(BF)
        acc_scr[...] = jnp.zeros_like(acc_scr)

    a = jnp.maximum(jnp.dot(h_scr[...], w1_ref[...], preferred_element_type=F32), 0.0)
    acc_scr[...] += jnp.dot((a * a).astype(BF), w2_ref[...], preferred_element_type=F32)

    @pl.when(j == pl.num_programs(1) - 1)
    def _():
        y_ref[...] = x1_scr[...] + gm_ref[...] * acc_scr[...]


def _post(x, o, w_out, gate_a, g_mlp, scale_m, shift_m, gate_m, w1, w2, rows_per_seq):
    rows = x.shape[0]
    tm = min(ROW_TILE, rows)
    tf = FF_TILE
    ms = lambda a: _mod_spec(a, tm, rows_per_seq)
    return pl.pallas_call(
        _post_kernel,
        grid=(rows // tm, D_FF // tf),
        in_specs=[pl.BlockSpec((tm, D_MODEL), lambda i, j: (i, 0)),
                  pl.BlockSpec((tm, Q_DIM), lambda i, j: (i, 0)),
                  pl.BlockSpec((Q_DIM, D_MODEL), lambda i, j: (0, 0)),
                  ms(gate_a),
                  pl.BlockSpec((1, D_MODEL), lambda i, j: (0, 0)),
                  ms(scale_m), ms(shift_m), ms(gate_m),
                  pl.BlockSpec((D_MODEL, tf), lambda i, j: (0, j)),
                  pl.BlockSpec((tf, D_MODEL), lambda i, j: (j, 0))],
        out_specs=pl.BlockSpec((tm, D_MODEL), lambda i, j: (i, 0)),
        out_shape=jax.ShapeDtypeStruct((rows, D_MODEL), F32),
        scratch_shapes=[pltpu.VMEM((tm, D_MODEL), F32), pltpu.VMEM((tm, D_MODEL), BF),
                        pltpu.VMEM((tm, D_MODEL), F32)],
        compiler_params=_params("parallel", "arbitrary"),
        name="post",
    )(x, o, w_out, gate_a, g_mlp, scale_m, shift_m, gate_m, w1, w2)


def _final_norm_kernel(x_ref, g_ref, y_ref):
    x = x_ref[...]
    y_ref[...] = x * lax.rsqrt(jnp.mean(x * x, axis=-1, keepdims=True) + NORM_EPS) * g_ref[...]


def _final_norm(x, g):
    rows = x.shape[0]
    tm = min(ROW_TILE, rows)
    return pl.pallas_call(
        _final_norm_kernel,
        grid=(rows // tm,),
        in_specs=[pl.BlockSpec((tm, D_MODEL), lambda i: (i, 0)), pl.BlockSpec((1, D_MODEL), lambda i: (0, 0))],
        out_specs=pl.BlockSpec((tm, D_MODEL), lambda i: (i, 0)),
        out_shape=jax.ShapeDtypeStruct((rows, D_MODEL), F32),
        compiler_params=_params("parallel"),
        name="final_norm",
    )(x, g)


def _block_mean_kernel(k_ref, o_ref):
    o_ref[...] = jnp.mean(k_ref[...], axis=0, keepdims=True)


def _block_means(k):
    nb = k.shape[0] // MOBA_BLOCK
    return pl.pallas_call(
        _block_mean_kernel,
        grid=(nb,),
        in_specs=[pl.BlockSpec((MOBA_BLOCK, KV_DIM), lambda i: (i, 0))],
        out_specs=pl.BlockSpec((None, 1, KV_DIM), lambda i: (i, 0, 0)),
        out_shape=jax.ShapeDtypeStruct((nb, 1, KV_DIM), F32),
        compiler_params=_params("parallel"),
        name="moba_block_means",
    )(k)


def _top_mask(gs, ids, n_sel, axis):
    picked = None
    for _ in range(n_sel):
        mx = jnp.max(gs, axis=axis, keepdims=True)
        am = jnp.min(jnp.where(gs == mx, ids, 2 ** 30), axis=axis, keepdims=True)
        hit = (ids == am) & (mx > NEG_INF)
        picked = hit if picked is None else (picked | hit)
        gs = jnp.where(ids == am, NEG_INF, gs)
    return picked


MOBA_ACC_ROWS = HEAD_DIM + 16
MOBA_MASKED = -1e30

def _moba_prompt_kernel(qi_tab, kj_tab, last_tab, qt_ref, kz_ref, vt_ref, kmz_ref, o_ref,
                        qz_scr, m_scr, acc_scr, *, n_sel):
    p = pl.program_id(1)
    i = qi_tab[p]
    j = kj_tab[p]
    blk = MOBA_BLOCK
    lane = lax.broadcasted_iota(I32, (blk, LANES), 1)
    ones_rows = (lax.broadcasted_iota(I32, (MOBA_ACC_ROWS - HEAD_DIM, blk), 0) == 0).astype(BF)

    @pl.when(i == j)
    def _():
        rows = lax.broadcasted_iota(I32, (LANES, blk), 0)
        earlier = (rows >= HEAD_DIM) & (rows < HEAD_DIM + i)
        for h in range(N_HEADS):
            qt = qt_ref[h * HEAD_DIM:(h + 1) * HEAD_DIM, :]
            gs = jnp.dot(kmz_ref[h // GROUP], qt.astype(F32), precision=lax.Precision.HIGHEST,
                         preferred_element_type=F32)
            picked = _top_mask(jnp.where(earlier, gs, NEG_INF), rows, n_sel, 0)
            bias = jnp.where(earlier & jnp.logical_not(picked), MOBA_MASKED, 0.0)
            qz_scr[h // GROUP, :, (h % GROUP) * blk:(h % GROUP + 1) * blk] = jnp.concatenate(
                [qt, bias[HEAD_DIM:].astype(BF)], axis=0)
        m_scr[...] = jnp.full(m_scr.shape, NEG_INF, F32)
        acc_scr[...] = jnp.zeros_like(acc_scr)

    def step(causal):
        sts = []
        for kv in range(N_KV_HEADS):
            kz = jnp.where(lane == HEAD_DIM + j, 1.0, kz_ref[:, kv * LANES:(kv + 1) * LANES]).astype(BF)
            sts.append(jnp.dot(kz, qz_scr[kv], preferred_element_type=F32))
        es, alphas = [], []
        for kv in range(N_KV_HEADS):
            st = sts[kv]
            if causal:
                key_i = lax.broadcasted_iota(I32, (blk, GROUP * blk), 0)
                qry_i = lax.broadcasted_iota(I32, (blk, GROUP * blk), 1) % blk
                st = jnp.where(key_i <= qry_i, st, NEG_INF)
            m_old = m_scr[kv]
            m_new = jnp.maximum(m_old, jnp.max(st, axis=0, keepdims=True))
            alphas.append(jnp.exp(m_old - m_new))
            es.append(jnp.exp(st - m_new).astype(BF))
            m_scr[kv] = m_new
        for kv in range(N_KV_HEADS):
            vaug = jnp.concatenate([vt_ref[kv * HEAD_DIM:(kv + 1) * HEAD_DIM, :], ones_rows], axis=0)
            acc_scr[kv] = alphas[kv] * acc_scr[kv] + jnp.dot(vaug, es[kv], preferred_element_type=F32)

    @pl.when(i == j)
    def _():
        step(True)

    @pl.when(i != j)
    def _():
        step(False)

    @pl.when(last_tab[p] == 1)
    def _():
        for c in range(N_HEADS // 2):
            halves = []
            for h in (2 * c, 2 * c + 1):
                a = acc_scr[h // GROUP, :, (h % GROUP) * blk:(h % GROUP + 1) * blk]
                halves.append(a[:HEAD_DIM] * (1.0 / a[HEAD_DIM:HEAD_DIM + 1]))
            o_ref[:, c * LANES:(c + 1) * LANES] = jnp.concatenate(halves, axis=0).T.astype(BF)


def _moba_prompt(qt, kz, vt, k, batch):
    rows = qt.shape[1]
    seq = rows // batch
    blk = MOBA_BLOCK
    assert seq % blk == 0
    nblk = seq // blk
    assert HEAD_DIM + nblk <= LANES
    n_sel = min(MOBA_TOPK, nblk)
    kmean = _block_means(k).reshape(batch, nblk, N_KV_HEADS, HEAD_DIM).transpose(0, 2, 1, 3)
    kmz = jnp.pad(kmean, ((0, 0), (0, 0), (HEAD_DIM, LANES - HEAD_DIM - nblk), (0, 0)))
    qi, kj, last = [], [], []
    for i in range(nblk):
        order = [i] + list(range(i))
        qi += [i] * len(order)
        kj += order
        last += [0] * (len(order) - 1) + [1]
    tabs = [jnp.asarray(np.array(t, np.int32)) for t in (qi, kj, last)]
    grid_spec = pltpu.PrefetchScalarGridSpec(
        num_scalar_prefetch=3,
        grid=(batch, len(qi)),
        in_specs=[pl.BlockSpec((Q_DIM, blk), lambda b, p, qi, kj, la: (0, b * nblk + qi[p])),
                  pl.BlockSpec((blk, 2 * KV_DIM), lambda b, p, qi, kj, la: (b * nblk + kj[p], 0)),
                  pl.BlockSpec((KV_DIM, blk), lambda b, p, qi, kj, la: (0, b * nblk + kj[p])),
                  pl.BlockSpec((None, N_KV_HEADS, LANES, HEAD_DIM), lambda b, p, qi, kj, la: (b, 0, 0, 0))],
        out_specs=pl.BlockSpec((blk, Q_DIM), lambda b, p, qi, kj, la: (b * nblk + qi[p], 0)),
        scratch_shapes=[pltpu.VMEM((N_KV_HEADS, LANES, GROUP * blk), BF), pltpu.VMEM((N_KV_HEADS, 1, GROUP * blk), F32),
                        pltpu.VMEM((N_KV_HEADS, MOBA_ACC_ROWS, GROUP * blk), F32)])
    return pl.pallas_call(
        functools.partial(_moba_prompt_kernel, n_sel=n_sel),
        grid_spec=grid_spec,
        out_shape=jax.ShapeDtypeStruct((rows, Q_DIM), BF),
        compiler_params=_params("parallel", "arbitrary"),
        name="moba_prompt",
    )(*tabs, qt, kz, vt, kmz)


def _float_key(x):
    b = pltpu.bitcast(x, I32)
    return jnp.where(b < 0, b ^ 0x7FFFFFFF, b)


def _kth_largest_key(count_ge, k_top, shape):
    def body(it, cur):
        cand = cur | jnp.left_shift(jnp.int32(1), 31 - it)
        return jnp.where(count_ge(cand ^ INT_MIN) >= k_top, cand, cur)
    return lax.fori_loop(0, 32, body, jnp.zeros(shape, I32)) ^ INT_MIN


def _tie_cutoff(count_eq_below, need, n_bits, shape):
    def body(it, cur):
        cand = cur | jnp.left_shift(jnp.int32(1), n_bits - 1 - it)
        return jnp.where(count_eq_below(cand) < need, cand, cur)
    return lax.fori_loop(0, n_bits, body, jnp.zeros(shape, I32))


def _swap_halves(x):
    return jnp.concatenate([x[HEAD_DIM:], x[:HEAD_DIM]], axis=0)


def _dsa_prompt_kernel(qt_ref, qit_ref, wt_ref, kiz_ref, kz_ref, vt_ref, o_ref,
                       qp_scr, qip_scr, key_scr, m_scr, acc_scr, *, k_top, n_bits):
    i = pl.program_id(1)
    tq, ch = DSA_QUERY_TILE, DSA_KEY_CHUNK
    n_ch = ((i + 1) * tq + ch - 1) // ch
    qpos = i * tq + lax.broadcasted_iota(I32, (1, tq), 1)
    key_row = lax.broadcasted_iota(I32, (ch, tq), 0)
    ones_rows = (lax.broadcasted_iota(I32, (MOBA_ACC_ROWS - HEAD_DIM, ch), 0) == 0).astype(BF)

    for c in range(IDX_HEADS // 2):
        pair = qit_ref[c * LANES:(c + 1) * LANES, :]
        qip_scr[0, :, c * tq:(c + 1) * tq] = pair
        qip_scr[1, :, c * tq:(c + 1) * tq] = _swap_halves(pair)
    for c in range(N_HEADS // 2):
        pair = qt_ref[c * LANES:(c + 1) * LANES, :]
        cols = slice((c % 2) * tq, (c % 2 + 1) * tq)
        qp_scr[2 * (c // 2), :, cols] = pair
        qp_scr[2 * (c // 2) + 1, :, cols] = _swap_halves(pair)

    def score_body(c, carry):
        kiz = kiz_ref[pl.ds(pl.multiple_of(c * ch, ch), ch), :]
        sc = jnp.zeros((ch, tq), F32)
        for second in range(2):
            d = jnp.dot(kiz, qip_scr[second], preferred_element_type=F32)
            for u in range(IDX_HEADS // 2):
                h = 2 * u + second
                sc = sc + wt_ref[h:h + 1, :] * jnp.maximum(d[:, u * tq:(u + 1) * tq], 0.0)
        sc = jnp.where(c * ch + key_row <= qpos, sc, NEG_INF)
        key_scr[c] = _float_key(sc)
        return carry
    lax.fori_loop(0, n_ch, score_body, 0)

    def count(pred):
        def body(c, acc):
            return acc + jnp.sum(pred(key_scr[c], c * ch + key_row).astype(I32), axis=0, keepdims=True)
        return lax.fori_loop(0, n_ch, body, jnp.zeros((1, tq), I32))

    thr = _kth_largest_key(lambda t: count(lambda key, idx: key >= t), k_top, (1, tq))
    n_gt = count(lambda key, idx: key > thr)
    n_ge = count(lambda key, idx: key >= thr)
    need = k_top - n_gt
    tie_cols = ((n_ge > k_top) & (thr != KEY_NEG_INF)).astype(I32)
    cutoff = lax.cond(
        jnp.max(tie_cols) > 0,
        lambda: _tie_cutoff(lambda c_: count(lambda key, idx: (key == thr) & (idx < c_)), need, n_bits, (1, tq)),
        lambda: jnp.full((1, tq), 2 ** 30, I32))

    m_scr[...] = jnp.full(m_scr.shape, NEG_INF, F32)
    acc_scr[...] = jnp.zeros_like(acc_scr)

    def attn_body(c, carry):
        key = key_scr[c]
        idx = c * ch + key_row
        keep = ((key > thr) | ((key == thr) & (idx <= cutoff))) & (idx <= qpos)
        bias = jnp.where(keep, 0.0, NEG_INF)
        bias = jnp.concatenate([bias, bias], axis=1)
        rows = pl.ds(pl.multiple_of(c * ch, ch), ch)
        sts = []
        for kv in range(N_KV_HEADS):
            kz = kz_ref[rows, kv * LANES:(kv + 1) * LANES]
            for second in range(2):
                sts.append(jnp.dot(kz, qp_scr[2 * kv + second], preferred_element_type=F32))
        for kv in range(N_KV_HEADS):
            vaug = jnp.concatenate([vt_ref[c, kv * HEAD_DIM:(kv + 1) * HEAD_DIM, :], ones_rows], axis=0)
            for second in range(2):
                g = 2 * kv + second
                st = sts[g] + bias
                m_old = m_scr[g]
                m_new = jnp.maximum(m_old, jnp.max(st, axis=0, keepdims=True))
                m_safe = jnp.where(m_new == NEG_INF, 0.0, m_new)
                alpha = jnp.exp(m_old - m_safe)
                e = jnp.exp(st - m_safe).astype(BF)
                acc_scr[g] = alpha * acc_scr[g] + jnp.dot(vaug, e, preferred_element_type=F32)
                m_scr[g] = m_new
        return carry
    lax.fori_loop(0, n_ch, attn_body, 0)

    for kv in range(N_KV_HEADS):
        outs = []
        for second in range(2):
            a = acc_scr[2 * kv + second]
            outs.append(a[:HEAD_DIM] * (1.0 / a[HEAD_DIM:HEAD_DIM + 1]))
        for u in range(2):
            both = jnp.concatenate([outs[0][:, u * tq:(u + 1) * tq], outs[1][:, u * tq:(u + 1) * tq]], axis=0)
            o_ref[:, (2 * kv + u) * LANES:(2 * kv + u + 1) * LANES] = both.T.astype(BF)


def _dsa_prompt(qt, qit, wt, kiz, kz, vt3, batch):
    rows = qt.shape[1]
    seq = rows // batch
    tq, ch = DSA_QUERY_TILE, DSA_KEY_CHUNK
    k_top = min(DSA_TOPK_MAX, seq // 4)
    assert seq % ch == 0 and ch >= k_top and ch % tq == 0 and tq % LANES == 0 and vt3.shape[2] == ch
    nq, n_chunks = seq // tq, seq // ch
    n_bits = max(1, (seq - 1).bit_length())
    tile = lambda n: pl.BlockSpec((n, tq), lambda b, i: (0, b * nq + i))
    return pl.pallas_call(
        functools.partial(_dsa_prompt_kernel, k_top=k_top, n_bits=n_bits),
        grid=(batch, nq),
        in_specs=[tile(Q_DIM), tile(IDX_Q_DIM), tile(IDX_HEADS),
                  pl.BlockSpec((seq, LANES), lambda b, i: (b, 0)),
                  pl.BlockSpec((seq, 2 * KV_DIM), lambda b, i: (b, 0)),
                  pl.BlockSpec((n_chunks, KV_DIM, ch), lambda b, i: (b, 0, 0))],
        out_specs=pl.BlockSpec((tq, Q_DIM), lambda b, i: (b * nq + i, 0)),
        out_shape=jax.ShapeDtypeStruct((rows, Q_DIM), BF),
        scratch_shapes=[pltpu.VMEM((2 * N_KV_HEADS, LANES, 2 * tq), BF),
                        pltpu.VMEM((2, LANES, (IDX_HEADS // 2) * tq), BF),
                        pltpu.VMEM((n_chunks, ch, tq), I32),
                        pltpu.VMEM((2 * N_KV_HEADS, 1, 2 * tq), F32),
                        pltpu.VMEM((2 * N_KV_HEADS, MOBA_ACC_ROWS, 2 * tq), F32)],
        compiler_params=_params("parallel", "arbitrary"),
        name="dsa_prompt",
    )(qt, qit, wt, kiz, kz, vt3)


def _swa_prompt_kernel(qt_ref, kzp_ref, kzc_ref, vtp_ref, vtc_ref, sink_ref, o_ref):
    n = pl.program_id(1)
    w = WINDOW
    key_i = lax.broadcasted_iota(I32, (2 * w, w), 0)
    qry_i = lax.broadcasted_iota(I32, (2 * w, w), 1)
    keep = (key_i >= qry_i) & (key_i <= qry_i + w) & ((key_i >= w) | (n > 0))
    bias = jnp.where(keep, 0.0, NEG_INF)
    bias = jnp.concatenate([bias, bias], axis=1)
    ones_rows = (lax.broadcasted_iota(I32, (MOBA_ACC_ROWS - HEAD_DIM, 2 * w), 0) == 0).astype(BF)
    for kv in range(N_KV_HEADS):
        lanes = slice(kv * LANES, (kv + 1) * LANES)
        feats = slice(kv * HEAD_DIM, (kv + 1) * HEAD_DIM)
        kz = jnp.concatenate([kzp_ref[:, lanes], kzc_ref[:, lanes]], axis=0)
        vaug = jnp.concatenate([jnp.concatenate([vtp_ref[feats, :], vtc_ref[feats, :]], axis=1), ones_rows], axis=0)
        pairs = [qt_ref[(2 * kv + u) * LANES:(2 * kv + u + 1) * LANES, :] for u in range(2)]
        outs = []
        for second in range(2):
            ops = pairs if second == 0 else [_swap_halves(p) for p in pairs]
            st = jnp.dot(kz, jnp.concatenate(ops, axis=1), preferred_element_type=F32) + bias
            sink = sink_ref[2 * kv + second]
            m = jnp.maximum(jnp.max(st, axis=0, keepdims=True), sink)
            acc = jnp.dot(vaug, jnp.exp(st - m).astype(BF), preferred_element_type=F32)
            outs.append(acc[:HEAD_DIM] * (1.0 / (acc[HEAD_DIM:HEAD_DIM + 1] + jnp.exp(sink - m))))
        for u in range(2):
            both = jnp.concatenate([outs[0][:, u * w:(u + 1) * w], outs[1][:, u * w:(u + 1) * w]], axis=0)
            o_ref[:, (2 * kv + u) * LANES:(2 * kv + u + 1) * LANES] = both.T.astype(BF)


def _swa_prompt(qt, kz, vt, sinks, batch):
    rows = qt.shape[1]
    seq = rows // batch
    w = WINDOW
    assert seq % w == 0 and w == LANES
    nb = seq // w
    cur = lambda b, n: b * nb + n
    prev = lambda b, n: b * nb + jnp.maximum(n - 1, 0)
    sk = sinks.astype(F32).reshape(N_KV_HEADS, 2, 2)
    sink_cols = jnp.repeat(sk.transpose(0, 2, 1), w, axis=2).reshape(2 * N_KV_HEADS, 1, 2 * w)
    return pl.pallas_call(
        _swa_prompt_kernel,
        grid=(batch, nb),
        in_specs=[pl.BlockSpec((Q_DIM, w), lambda b, n: (0, cur(b, n))),
                  pl.BlockSpec((w, 2 * KV_DIM), lambda b, n: (prev(b, n), 0)),
                  pl.BlockSpec((w, 2 * KV_DIM), lambda b, n: (cur(b, n), 0)),
                  pl.BlockSpec((KV_DIM, w), lambda b, n: (0, prev(b, n))),
                  pl.BlockSpec((KV_DIM, w), lambda b, n: (0, cur(b, n))),
                  pl.BlockSpec((2 * N_KV_HEADS, 1, 2 * w), lambda b, n: (0, 0, 0))],
        out_specs=pl.BlockSpec((w, Q_DIM), lambda b, n: (cur(b, n), 0)),
        out_shape=jax.ShapeDtypeStruct((rows, Q_DIM), BF),
        compiler_params=_params("parallel", "parallel"),
        name="swa_prompt",
    )(qt, kz, kz, vt, vt, sink_cols)


def _row_token(n_rows, n_tok):
    r = lax.broadcasted_iota(I32, (n_rows, 1), 0)
    return (r // GROUP) % n_tok


def _block_diag_queries(q, n_seq, n_tok):
    qr = q.reshape(n_seq, n_tok, N_KV_HEADS, GROUP, HEAD_DIM).transpose(0, 2, 1, 3, 4)
    qr = qr.reshape(n_seq, N_KV_HEADS, n_tok * GROUP, HEAD_DIM)
    eye = jnp.eye(N_KV_HEADS, dtype=q.dtype)
    qbd = qr[:, :, :, None, :] * eye[None, :, None, :, None]
    return qbd.reshape(n_seq, N_KV_HEADS * n_tok * GROUP, KV_DIM)


def _undiag_outputs(out, n_seq, n_tok):
    o = out.reshape(n_seq, N_KV_HEADS, n_tok, GROUP, N_KV_HEADS, HEAD_DIM)
    o = jnp.stack([o[:, kv, :, :, kv] for kv in range(N_KV_HEADS)], axis=1)
    return o.transpose(0, 2, 1, 3, 4).reshape(n_seq * n_tok, Q_DIM)


def _tokens_on_lanes(cache):
    if cache.ndim == 5:
        t = jnp.transpose(cache, (0, 1, 3, 4, 2))
        return t.reshape(t.shape[:2] + (t.shape[2] * t.shape[3], t.shape[4]))
    return jnp.transpose(cache, (0, 1, 3, 2))


def _page_specs(n_pages_per_step, slot, width):
    def spec(r):
        return pl.BlockSpec((None, None, width, PAGE_SIZE),
                            lambda b, s, pt: (slot, pt[b, s * n_pages_per_step + r], 0, 0))
    return [spec(r) for r in range(n_pages_per_step)]


def _pad_rows(a, n_seq, n_tok):
    a = a.reshape(n_seq, n_tok, a.shape[-1])
    return jnp.pad(a, ((0, 0), (0, SUBLANES - n_tok), (0, 0)))


def _moba_sample_kernel(pt_ref, q_ref, kn_ref, vn_ref, *refs, n_tok, n_sel, pps):
    kt_pages = refs[:pps]
    vt_pages = refs[pps:2 * pps]
    o_ref = refs[2 * pps]
    m_scr, l_scr, g_scr, o_scr = refs[2 * pps + 1:]
    s = pl.program_id(1)
    n_steps = m_scr.shape[0]
    bps = pps // 2
    q = q_ref[...]
    n_row = q.shape[0]
    m_scr[s] = jnp.full(m_scr.shape[1:], NEG_INF, F32)
    l_scr[s] = jnp.zeros(l_scr.shape[1:], F32)
    g_scr[s] = jnp.full(g_scr.shape[1:], NEG_INF, F32)
    for r in range(bps):
        kt = jnp.concatenate([kt_pages[2 * r][...], kt_pages[2 * r + 1][...]], axis=1).astype(BF)
        vt = jnp.concatenate([vt_pages[2 * r][...], vt_pages[2 * r + 1][...]], axis=1).astype(BF)
        sc = jnp.dot(q, kt, preferred_element_type=F32)
        m = jnp.max(sc, axis=1, keepdims=True)
        e = jnp.exp(sc - m)
        m_scr[s, :, r:r + 1] = m
        l_scr[s, :, r:r + 1] = jnp.sum(e, axis=1, keepdims=True)
        g_scr[s, :, r:r + 1] = jnp.sum(sc, axis=1, keepdims=True)
        o_scr[s * bps + r] = lax.dot_general(e.astype(BF), vt, NT, preferred_element_type=F32)

    @pl.when(s == pl.num_programs(1) - 1)
    def _():
        lane = lax.broadcasted_iota(I32, (n_row, LANES), 1)
        ids = [t * bps + lane for t in range(n_steps)]
        gs = [g_scr[t] for t in range(n_steps)]
        picked = [None] * n_steps
        for _ in range(n_sel):
            mx = functools.reduce(jnp.maximum, [jnp.max(g, axis=1, keepdims=True) for g in gs])
            am = functools.reduce(jnp.minimum, [jnp.min(jnp.where(g == mx, i_, 2 ** 30), axis=1, keepdims=True)
                                                for g, i_ in zip(gs, ids)])
            for t in range(n_steps):
                hit = (ids[t] == am) & (mx > NEG_INF)
                picked[t] = hit if picked[t] is None else (picked[t] | hit)
                gs[t] = jnp.where(ids[t] == am, NEG_INF, gs[t])
        own_ok = lax.broadcasted_iota(I32, (n_row, SUBLANES), 1) <= _row_token(n_row, n_tok)
        s_own = jnp.where(own_ok, lax.dot_general(q, kn_ref[...], NT, preferred_element_type=F32), NEG_INF)
        m_tot = jnp.max(s_own, axis=1, keepdims=True)
        for t in range(n_steps):
            m_tot = jnp.maximum(m_tot, jnp.max(jnp.where(picked[t], m_scr[t], NEG_INF), axis=1, keepdims=True))
        e_own = jnp.exp(s_own - m_tot)
        den = jnp.sum(e_own, axis=1, keepdims=True)
        num = jnp.dot(e_own.astype(BF), vn_ref[...], preferred_element_type=F32)
        for t in range(n_steps):
            wgt = jnp.where(picked[t], jnp.exp(m_scr[t] - m_tot), 0.0)
            den = den + jnp.sum(wgt * l_scr[t], axis=1, keepdims=True)
            for r in range(bps):
                num = num + wgt[:, r:r + 1] * o_scr[t * bps + r]
        o_ref[...] = num * (1.0 / den)


def _paged_attention_call(kernel, name, slot, page_table, cache_kt, cache_vt, extra_inputs, extra_specs,
                          scratch_shapes, n_row):
    n_seq, n_pages = page_table.shape
    pps = PAGES_PER_STEP
    assert n_pages % pps == 0
    grid_spec = pltpu.PrefetchScalarGridSpec(
        num_scalar_prefetch=1,
        grid=(n_seq, n_pages // pps),
        in_specs=extra_specs + _page_specs(pps, slot, KV_DIM) + _page_specs(pps, slot, KV_DIM),
        out_specs=pl.BlockSpec((None, n_row, KV_DIM), lambda b, s, pt: (b, 0, 0)),
        scratch_shapes=scratch_shapes)
    return pl.pallas_call(
        kernel,
        grid_spec=grid_spec,
        out_shape=jax.ShapeDtypeStruct((n_seq, n_row, KV_DIM), F32),
        compiler_params=_params("parallel", "arbitrary"),
        name=name,
    )(page_table, *extra_inputs, *([cache_kt] * pps), *([cache_vt] * pps))


def _moba_sample(qbd, kn_pad, vn_pad, cache_kt, cache_vt, slot, page_table, n_tok):
    n_seq, n_pages = page_table.shape
    past = n_pages * PAGE_SIZE
    assert past % MOBA_BLOCK == 0 and MOBA_BLOCK == 2 * PAGE_SIZE and PAGES_PER_STEP % 2 == 0
    nblk = past // MOBA_BLOCK
    n_steps = n_pages // PAGES_PER_STEP
    n_row = qbd.shape[1]
    per_seq = lambda w: pl.BlockSpec((None, w[0], w[1]), lambda b, s, pt: (b, 0, 0))
    kernel = functools.partial(_moba_sample_kernel, n_tok=n_tok, n_sel=min(MOBA_TOPK, nblk), pps=PAGES_PER_STEP)
    stats = pltpu.VMEM((n_steps, n_row, LANES), F32)
    return _paged_attention_call(
        kernel, "moba_sample", slot, page_table, cache_kt, cache_vt,
        [qbd, kn_pad, vn_pad],
        [per_seq((n_row, KV_DIM)), per_seq((SUBLANES, KV_DIM)), per_seq((SUBLANES, KV_DIM))],
        [stats, stats, stats, pltpu.VMEM((nblk, n_row, KV_DIM), F32)],
        n_row)


def _dsa_index_kernel(pt_ref, qi_ref, w_ref, kin_ref, *refs, n_tok, k_top, ppi, n_bits):
    ki_pages = refs[:ppi]
    mask_ref = refs[ppi]
    key_scr = refs[ppi + 1]
    s = pl.program_id(1)
    n_steps = pl.num_programs(1)
    n_pages = n_steps * ppi
    qi = qi_ref[...]
    w = w_ref[...]
    fill = jnp.full((SUBLANES - n_tok, LANES), NEG_INF, F32)

    def scores(d):
        wd = (w * jnp.maximum(d, 0.0)).reshape(n_tok, IDX_HEADS, LANES)
        return jnp.sum(wd, axis=1)

    for r in range(ppi):
        sc = scores(jnp.dot(qi, ki_pages[r][...].astype(BF), preferred_element_type=F32))
        key_scr[s * ppi + r] = _float_key(jnp.concatenate([sc, fill], axis=0))

    @pl.when(s == n_steps - 1)
    def _():
        tok = lax.broadcasted_iota(I32, (n_tok, LANES), 0)
        lane_t = lax.broadcasted_iota(I32, (n_tok, LANES), 1)
        sc_new = scores(lax.dot_general(qi, kin_ref[...], NT, preferred_element_type=F32))
        sc_new = jnp.where(lane_t <= tok, sc_new, NEG_INF)
        key_scr[n_pages] = _float_key(jnp.concatenate([sc_new, fill], axis=0))
        tiles = (n_pages + 1, SUBLANES, LANES)
        idx = lax.broadcasted_iota(I32, tiles, 0) * PAGE_SIZE + lax.broadcasted_iota(I32, tiles, 2)

        def count(hit):
            return jnp.sum(jnp.sum(hit.astype(I32), axis=0), axis=1, keepdims=True)

        thr = _kth_largest_key(lambda t: count(key_scr[...] >= t), k_top, (SUBLANES, 1))
        need = k_top - count(key_scr[...] > thr)
        cutoff = _tie_cutoff(lambda c_: count((key_scr[...] == thr) & (idx < c_)), need, n_bits, (SUBLANES, 1))
        keys = key_scr[...]
        keep = (keys > thr) | ((keys == thr) & (idx <= cutoff))
        mask_ref[...] = jnp.where(keep & (keys != KEY_NEG_INF), 1.0, 0.0).astype(BF)


def _dsa_index(qi_rows, w_rows, ki_new, cache_idx, islot, page_table, n_tok):
    n_seq, n_pages = page_table.shape
    ppi = min(INDEX_PAGES_PER_STEP, n_pages)
    total = n_pages * PAGE_SIZE + n_tok
    k_top = min(DSA_TOPK_MAX, total // 4)
    assert n_pages % ppi == 0 and n_tok <= SUBLANES and n_pages * PAGE_SIZE >= k_top
    n_bits = max(1, ((n_pages + 1) * PAGE_SIZE - 1).bit_length())
    nr = n_tok * IDX_HEADS
    grid_spec = pltpu.PrefetchScalarGridSpec(
        num_scalar_prefetch=1,
        grid=(n_seq, n_pages // ppi),
        in_specs=[pl.BlockSpec((None, nr, IDX_DIM), lambda b, s, pt: (b, 0, 0)),
                  pl.BlockSpec((None, nr, 1), lambda b, s, pt: (b, 0, 0)),
                  pl.BlockSpec((None, LANES, IDX_DIM), lambda b, s, pt: (b, 0, 0))]
        + _page_specs(ppi, islot, IDX_DIM),
        out_specs=pl.BlockSpec((None, n_pages + 1, SUBLANES, LANES), lambda b, s, pt: (b, 0, 0, 0)),
        scratch_shapes=[pltpu.VMEM((n_pages + 1, SUBLANES, LANES), I32)])
    return pl.pallas_call(
        functools.partial(_dsa_index_kernel, n_tok=n_tok, k_top=k_top, ppi=ppi, n_bits=n_bits),
        grid_spec=grid_spec,
        out_shape=jax.ShapeDtypeStruct((n_seq, n_pages + 1, SUBLANES, LANES), BF),
        compiler_params=_params("parallel", "arbitrary"),
        name="dsa_index",
    )(page_table, qi_rows, w_rows, ki_new, *([cache_idx] * ppi))


def _dsa_sample_kernel(pt_ref, q_ref, kn_ref, vn_ref, mask_ref, mnew_ref, *refs, n_tok, pps):
    kt_pages = refs[:pps]
    vt_pages = refs[pps:2 * pps]
    o_ref = refs[2 * pps]
    m_scr, l_scr, acc_scr = refs[2 * pps + 1:]
    s = pl.program_id(1)
    q = q_ref[...]
    n_row = q.shape[0]
    expand = (lax.broadcasted_iota(I32, (n_row, SUBLANES), 1) == _row_token(n_row, n_tok)).astype(BF)

    @pl.when(s == 0)
    def _():
        m_scr[...] = jnp.full(m_scr.shape, NEG_INF, F32)
        l_scr[...] = jnp.zeros_like(l_scr)
        acc_scr[...] = jnp.zeros_like(acc_scr)

    def update(sc, keep, pv):
        sc = jnp.where(keep > 0.5, sc, NEG_INF)
        m_old = m_scr[...]
        m_new = jnp.maximum(m_old, jnp.max(sc, axis=1, keepdims=True))
        m_safe = jnp.where(m_new == NEG_INF, 0.0, m_new)
        alpha = jnp.exp(m_old - m_safe)
        e = jnp.exp(sc - m_safe)
        l_scr[...] = alpha * l_scr[...] + jnp.sum(e, axis=1, keepdims=True)
        acc_scr[...] = alpha * acc_scr[...] + pv(e.astype(BF))
        m_scr[...] = m_new

    for r0 in range(0, pps, DSA_PAGES_PER_UPDATE):
        rs = range(r0, r0 + DSA_PAGES_PER_UPDATE)
        kt = jnp.concatenate([kt_pages[r][...] for r in rs], axis=1).astype(BF)
        vt = jnp.concatenate([vt_pages[r][...] for r in rs], axis=1).astype(BF)
        sc = jnp.dot(q, kt, preferred_element_type=F32)
        keep = jnp.dot(expand, jnp.concatenate([mask_ref[r] for r in rs], axis=1), preferred_element_type=F32)
        update(sc, keep, lambda e, vt=vt: lax.dot_general(e, vt, NT, preferred_element_type=F32))

    @pl.when(s == pl.num_programs(1) - 1)
    def _():
        sc = lax.dot_general(q, kn_ref[...], NT, preferred_element_type=F32)
        keep = jnp.dot(expand, mnew_ref[...], preferred_element_type=F32)[:, :SUBLANES]
        update(sc, keep, lambda e: jnp.dot(e, vn_ref[...], preferred_element_type=F32))
        o_ref[...] = acc_scr[...] * (1.0 / l_scr[...])


def _dsa_sample(qbd, kn_pad, vn_pad, mask, cache_kt, cache_vt, slot, page_table, n_tok):
    n_seq, n_pages = page_table.shape
    n_row = qbd.shape[1]
    pps = PAGES_PER_STEP
    per_seq = lambda w: pl.BlockSpec((None, w[0], w[1]), lambda b, s, pt: (b, 0, 0))
    return _paged_attention_call(
        functools.partial(_dsa_sample_kernel, n_tok=n_tok, pps=pps), "dsa_sample", slot, page_table,
        cache_kt, cache_vt,
        [qbd, kn_pad, vn_pad, mask, mask],
        [per_seq((n_row, KV_DIM)), per_seq((SUBLANES, KV_DIM)), per_seq((SUBLANES, KV_DIM)),
         pl.BlockSpec((None, pps, SUBLANES, LANES), lambda b, s, pt: (b, s, 0, 0)),
         pl.BlockSpec((None, None, SUBLANES, LANES), lambda b, s, pt: (b, n_pages, 0, 0))],
        [pltpu.VMEM((n_row, 1), F32), pltpu.VMEM((n_row, 1), F32), pltpu.VMEM((n_row, KV_DIM), F32)],
        n_row)


def _swa_sample_kernel(q_ref, kt_ref, vt_ref, kn_ref, vn_ref, sink_ref, o_ref, *, n_tok):
    q = q_ref[...]
    n_row = q.shape[0]
    tok_r = _row_token(n_row, n_tok)
    wk = kt_ref.shape[1]
    s_buf = jnp.dot(q, kt_ref[...].astype(BF), preferred_element_type=F32)
    s_buf = jnp.where(lax.broadcasted_iota(I32, (n_row, wk), 1) >= tok_r, s_buf, NEG_INF)
    s_new = lax.dot_general(q, kn_ref[...], NT, preferred_element_type=F32)
    s_new = jnp.where(lax.broadcasted_iota(I32, (n_row, SUBLANES), 1) <= tok_r, s_new, NEG_INF)
    sink = sink_ref[...]
    m = jnp.maximum(jnp.maximum(jnp.max(s_buf, axis=1, keepdims=True), jnp.max(s_new, axis=1, keepdims=True)), sink)
    e_buf = jnp.exp(s_buf - m)
    e_new = jnp.exp(s_new - m)
    den = jnp.sum(e_buf, axis=1, keepdims=True) + jnp.sum(e_new, axis=1, keepdims=True) + jnp.exp(sink - m)
    num = (lax.dot_general(e_buf.astype(BF), vt_ref[...].astype(BF), NT, preferred_element_type=F32)
           + jnp.dot(e_new.astype(BF), vn_ref[...], preferred_element_type=F32))
    o_ref[...] = num * (1.0 / den)


def _swa_sample(qbd, kn_pad, vn_pad, buf_kt, buf_vt, sinks, n_tok):
    n_seq, _, wk = buf_kt.shape
    assert wk == WINDOW
    n_row = qbd.shape[1]
    sink_rows = jnp.repeat(sinks.reshape(N_KV_HEADS, 1, GROUP), n_tok, axis=1).reshape(n_row, 1)
    per_seq = lambda a, c: pl.BlockSpec((None, a, c), lambda b: (b, 0, 0))
    return pl.pallas_call(
        functools.partial(_swa_sample_kernel, n_tok=n_tok),
        grid=(n_seq,),
        in_specs=[per_seq(n_row, KV_DIM), per_seq(KV_DIM, wk), per_seq(KV_DIM, wk),
                  per_seq(SUBLANES, KV_DIM), per_seq(SUBLANES, KV_DIM),
                  pl.BlockSpec((n_row, 1), lambda b: (0, 0))],
        out_specs=per_seq(n_row, KV_DIM),
        out_shape=jax.ShapeDtypeStruct((n_seq, n_row, KV_DIM), F32),
        compiler_params=_params("parallel"),
        name="swa_sample",
    )(qbd, buf_kt, buf_vt, kn_pad, vn_pad, sink_rows)


def _rope_tables(pos):
    half = HEAD_DIM // 2
    inv = ROPE_THETA ** (-jnp.arange(half, dtype=F32) / half)
    ang = pos.astype(F32)[:, None] * inv[None, :]
    cos, sin = jnp.cos(ang), jnp.sin(ang)
    reps = LANES // HEAD_DIM
    return jnp.tile(jnp.concatenate([cos, cos], axis=1), (1, reps)), jnp.tile(jnp.concatenate([-sin, sin], axis=1), (1, reps))


def kernel(x_prompt, x_sample, cache_k, cache_v, cache_idx_k, state_swa_k, state_swa_v, page_table, c_prompt, c_sample, g_attn, g_mlp, w_mod, b_mod, w_in_moba, w_in_dsa, w_in_swa, swa_sinks, w_out, w_ff1, w_ff2, g_final):
    batch, seq, _ = x_prompt.shape
    n_seq, n_tok, _ = x_sample.shape
    depth = g_attn.shape[0]
    n_pages = page_table.shape[1]
    past = n_pages * PAGE_SIZE
    rows_p, rows_s = batch * seq, n_seq * n_tok

    n_cond = batch + n_seq
    pad_cond = -n_cond % SUBLANES
    c_all = jnp.pad(jnp.concatenate([c_prompt, c_sample], axis=0), ((0, pad_cond), (0, 0)))
    mod = _adaln(c_all, w_mod, b_mod)

    def mods(layer):
        parts = [mod[layer, :, r * D_MODEL:(r + 1) * D_MODEL] for r in range(6)]
        mp = [p[:batch].reshape(batch, 1, D_MODEL) for p in parts]
        ms = [jnp.repeat(p[batch:n_cond], n_tok, axis=0) for p in parts]
        return mp, ms

    cos_p, sin_p = _rope_tables(jnp.arange(seq, dtype=jnp.int32))
    cos_s, sin_s = _rope_tables(jnp.tile(past + jnp.arange(n_tok, dtype=jnp.int32), n_seq))
    cos_pt, sin_pt = cos_p[:, :HEAD_DIM].T, sin_p[:, :HEAD_DIM].T

    cache_kt, cache_vt, cache_it = _tokens_on_lanes(cache_k), _tokens_on_lanes(cache_v), _tokens_on_lanes(cache_idx_k)
    swa_kt, swa_vt = _tokens_on_lanes(state_swa_k), _tokens_on_lanes(state_swa_v)

    xp = x_prompt.reshape(rows_p, D_MODEL)
    xs = x_sample.reshape(rows_s, D_MODEL)
    kp_l, vp_l, ks_l, vs_l, ip_l, is_l = [], [], [], [], [], []
    skp_l, svp_l, sks_l, svs_l = [], [], [], []
    for i, (mixer, j, slot) in enumerate(_layer_plan(depth)):
        (sh_ap, sc_ap, gt_ap, sh_mp, sc_mp, gt_mp), (sh_as, sc_as, gt_as, sh_ms, sc_ms, gt_ms) = mods(i)
        dsa = mixer == 1
        if mixer == 0:
            w_in = w_in_moba[j]
        elif dsa:
            w_in = jnp.pad(w_in_dsa[j], ((0, 0), (0, -IN_DSA % LANES)))
        else:
            w_in = w_in_swa[j]
        w_in = w_in.astype(BF)
        g_a = g_attn[i].reshape(1, D_MODEL)
        outs_p = _project_t(xp, g_a, sc_ap, sh_ap, cos_p, sin_p, cos_pt, sin_pt, w_in, seq, dsa)
        qt_p, k_p, v_p, kz_p, vt_p = outs_p[:5]
        outs_s = _project(xs, g_a, sc_as, sh_as, cos_s, sin_s, w_in, n_tok, dsa)
        q_s, k_s, v_s, kb_s, vb_s = outs_s[:5]
        qbd = _block_diag_queries(q_s, n_seq, n_tok)
        kn_pad = _pad_rows(kb_s, n_seq, n_tok)
        vn_pad = _pad_rows(vb_s, n_seq, n_tok)
        k_p5 = k_p.reshape(batch, seq, N_KV_HEADS, HEAD_DIM)
        v_p5 = v_p.reshape(batch, seq, N_KV_HEADS, HEAD_DIM)
        k_s5 = k_s.reshape(n_seq, n_tok, N_KV_HEADS, HEAD_DIM)
        v_s5 = v_s.reshape(n_seq, n_tok, N_KV_HEADS, HEAD_DIM)
        if mixer == 0:
            o_p = _moba_prompt(qt_p, kz_p, vt_p, k_p, batch)
            o_t = _moba_sample(qbd, kn_pad, vn_pad, cache_kt, cache_vt, slot, page_table, n_tok)
        elif dsa:
            qit_p, wt_p, rest_p, kiz_p = outs_p[5:]
            qi_s, rest_s, restb_s = outs_s[5:]
            o_p = _dsa_prompt(qt_p, qit_p, wt_p, kiz_p, kz_p, vt_p, batch)
            qi_rows = qi_s.reshape(n_seq, n_tok * IDX_HEADS, IDX_DIM)
            w_rows = rest_s[:, IDX_DIM:IDX_DIM + IDX_HEADS].reshape(n_seq, n_tok * IDX_HEADS, 1)
            ki_new = jnp.pad(restb_s[:, :IDX_DIM].reshape(n_seq, n_tok, IDX_DIM),
                             ((0, 0), (0, LANES - n_tok), (0, 0)))
            mask = _dsa_index(qi_rows, w_rows, ki_new, cache_it, j, page_table, n_tok)
            o_t = _dsa_sample(qbd, kn_pad, vn_pad, mask, cache_kt, cache_vt, slot, page_table, n_tok)
            ip_l.append(rest_p[:, :IDX_DIM].reshape(batch, seq, IDX_DIM))
            is_l.append(rest_s[:, :IDX_DIM].reshape(n_seq, n_tok, IDX_DIM))
        else:
            o_p = _swa_prompt(qt_p, kz_p, vt_p, swa_sinks[j], batch)
            o_t = _swa_sample(qbd, kn_pad, vn_pad, swa_kt[j], swa_vt[j], swa_sinks[j], n_tok)
            keep_p = min(WINDOW, seq)
            skp_l.append(k_p5[:, seq - keep_p:])
            svp_l.append(v_p5[:, seq - keep_p:])
            wk = state_swa_k.shape[2]
            sks_l.append(jnp.concatenate([state_swa_k[j], k_s5], axis=1)[:, -wk:])
            svs_l.append(jnp.concatenate([state_swa_v[j], v_s5], axis=1)[:, -wk:])
        if mixer < 2:
            kp_l.append(k_p5)
            vp_l.append(v_p5)
            ks_l.append(k_s5)
            vs_l.append(v_s5)
        o_s = _undiag_outputs(o_t, n_seq, n_tok).astype(BF)
        wo, w1, w2 = w_out[i].astype(BF), w_ff1[i].astype(BF), w_ff2[i].astype(BF)
        g_m = g_mlp[i].reshape(1, D_MODEL)
        xp = _post(xp, o_p, wo, gt_ap, g_m, sc_mp, sh_mp, gt_mp, w1, w2, seq)
        xs = _post(xs, o_s, wo, gt_as, g_m, sc_ms, sh_ms, gt_ms, w1, w2, n_tok)
    g_f = g_final.reshape(1, D_MODEL)
    y_prompt = _final_norm(xp, g_f).reshape(batch, seq, D_MODEL)
    y_sample = _final_norm(xs, g_f).reshape(n_seq, n_tok, D_MODEL)
    return (y_prompt, y_sample, jnp.stack(kp_l), jnp.stack(vp_l), jnp.stack(ks_l), jnp.stack(vs_l),
            jnp.stack(ip_l), jnp.stack(is_l), jnp.stack(skp_l), jnp.stack(svp_l), jnp.stack(sks_l), jnp.stack(svs_l))
```

```python
import functools

import numpy as np
import jax
import jax.numpy as jnp
from jax import lax
from jax.experimental import pallas as pl
from jax.experimental.pallas import tpu as pltpu

D_MODEL = 1024
N_HEADS = 16
HEAD_DIM = D_MODEL // N_HEADS
N_KV_HEADS = 4
GROUP = N_HEADS // N_KV_HEADS
Q_DIM = N_HEADS * HEAD_DIM
KV_DIM = N_KV_HEADS * HEAD_DIM
D_FF = 4 * D_MODEL
ROPE_THETA = 10000.0
NORM_EPS = 1e-6
N_MIXERS = 3
PAGE_SIZE = 128
MOBA_BLOCK = 256
MOBA_TOPK = 3
IDX_HEADS = 8
IDX_DIM = 64
DSA_TOPK_MAX = 256
WINDOW = 128
IN_ATTN = Q_DIM + 2 * KV_DIM
IDX_Q_DIM = IDX_HEADS * IDX_DIM
IN_DSA = IN_ATTN + IDX_Q_DIM + IDX_DIM + IDX_HEADS
ATTN_SCALE = HEAD_DIM ** -0.5
IDX_W_SCALE = IDX_Q_DIM ** -0.5

LANES = 128
SUBLANES = 8
VMEM_LIMIT = 56 << 20
ROW_TILE = 512
FF_TILE = 1024
DSA_KEY_CHUNK = 512
DSA_QUERY_TILE = 256
PAGES_PER_STEP = 16
INDEX_PAGES_PER_STEP = 64
DSA_PAGES_PER_UPDATE = 4

BF = jnp.bfloat16
F32 = jnp.float32
I32 = jnp.int32
NEG_INF = float("-inf")
INT_MIN = -2 ** 31
KEY_NEG_INF = INT_MIN + 0x7FFFFF
NT = (((1,), (1,)), ((), ()))


def _params(*semantics):
    return pltpu.CompilerParams(dimension_semantics=semantics, vmem_limit_bytes=VMEM_LIMIT)


def _layer_plan(depth):
    plan, counts, n_paged = [], [0] * N_MIXERS, 0
    for i in range(depth):
        m = i % N_MIXERS
        slot = -1
        if m < 2:
            slot = n_paged
            n_paged += 1
        plan.append((m, counts[m], slot))
        counts[m] += 1
    return plan


def _adaln_kernel(c_ref, w_ref, b_ref, o_ref):
    c = c_ref[...]
    a = (c * (1.0 / (1.0 + jnp.exp(-c)))).astype(BF)
    o_ref[...] = jnp.dot(a, w_ref[...].astype(BF), preferred_element_type=F32) + b_ref[...]


def _adaln(c_all, w_mod, b_mod):
    depth, _, n_out = w_mod.shape
    nc = c_all.shape[0]
    tn = 1536
    return pl.pallas_call(
        _adaln_kernel,
        grid=(depth, n_out // tn),
        in_specs=[pl.BlockSpec((nc, D_MODEL), lambda l, j: (0, 0)),
                  pl.BlockSpec((None, D_MODEL, tn), lambda l, j: (l, 0, j)),
                  pl.BlockSpec((None, 1, tn), lambda l, j: (l, 0, j))],
        out_specs=pl.BlockSpec((None, nc, tn), lambda l, j: (l, 0, j)),
        out_shape=jax.ShapeDtypeStruct((depth, nc, n_out), F32),
        compiler_params=_params("parallel", "parallel"),
        name="adaln",
    )(c_all, w_mod, b_mod.reshape(depth, 1, n_out))


def _norm_mod(x, g, scale, shift):
    y = x * lax.rsqrt(jnp.mean(x * x, axis=-1, keepdims=True) + NORM_EPS)
    return (y * g) * (1.0 + scale) + shift


def _rope_chunk(z, cos, sin_signed, lo_half):
    partner = jnp.where(lo_half, pltpu.roll(z, LANES - HEAD_DIM // 2, 1), pltpu.roll(z, HEAD_DIM // 2, 1))
    return z * cos + partner * sin_signed


def _proj_kernel(x_ref, g_ref, sc_ref, sh_ref, cos_ref, sin_ref, w_ref, *out_refs, dsa):
    hb = _norm_mod(x_ref[...], g_ref[...], sc_ref[...], sh_ref[...]).astype(BF)
    cos = cos_ref[...]
    sin_s = sin_ref[...]
    lane = lax.broadcasted_iota(I32, (1, LANES), 1)
    lo_half = (lane % HEAD_DIM) < (HEAD_DIM // 2)
    q_ref, k_ref, v_ref, kb_ref, vb_ref = out_refs[:5]

    zq = jnp.dot(hb, w_ref[:, :Q_DIM], preferred_element_type=F32)
    for c in range(Q_DIM // LANES):
        sl = slice(c * LANES, (c + 1) * LANES)
        q_ref[:, sl] = (_rope_chunk(zq[:, sl], cos, sin_s, lo_half) * ATTN_SCALE).astype(BF)
    zk = jnp.dot(hb, w_ref[:, Q_DIM:Q_DIM + KV_DIM], preferred_element_type=F32)
    for c in range(KV_DIM // LANES):
        sl = slice(c * LANES, (c + 1) * LANES)
        kr = _rope_chunk(zk[:, sl], cos, sin_s, lo_half)
        k_ref[:, sl] = kr
        kb_ref[:, sl] = kr.astype(BF)
    zv = jnp.dot(hb, w_ref[:, Q_DIM + KV_DIM:IN_ATTN], preferred_element_type=F32)
    v_ref[...] = zv
    vb_ref[...] = zv.astype(BF)
    if dsa:
        qi_ref, rest_ref, restb_ref = out_refs[5:]
        zi = jnp.dot(hb, w_ref[:, IN_ATTN:IN_ATTN + IDX_Q_DIM], preferred_element_type=F32)
        for c in range(IDX_Q_DIM // LANES):
            sl = slice(c * LANES, (c + 1) * LANES)
            qi_ref[:, sl] = _rope_chunk(zi[:, sl], cos, sin_s, lo_half).astype(BF)
        zr = jnp.dot(hb, w_ref[:, IN_ATTN + IDX_Q_DIM:], preferred_element_type=F32)
        roped = _rope_chunk(zr, cos, sin_s, lo_half)
        rest = jnp.where(lane < IDX_DIM, roped, zr * IDX_W_SCALE)
        rest_ref[...] = rest
        restb_ref[...] = rest.astype(BF)


def _mod_spec(arr, tm, rows_per_seq):
    if arr.ndim == 3:
        return pl.BlockSpec((None, 1, D_MODEL), lambda i, *_: ((i * tm) // rows_per_seq, 0, 0))
    return pl.BlockSpec((tm, D_MODEL), lambda i, *_: (i, 0))


def _project(x, g, scale, shift, cos_tab, sin_tab, w, rows_per_seq, dsa):
    rows = x.shape[0]
    tm = min(ROW_TILE, rows)
    n_in = w.shape[1]
    tab_tiles = cos_tab.shape[0] // tm
    row_spec = lambda n: pl.BlockSpec((tm, n), lambda i: (i, 0))
    tab_spec = pl.BlockSpec((tm, LANES), lambda i: (i % tab_tiles, 0))
    out_shape = [jax.ShapeDtypeStruct((rows, Q_DIM), BF),
                 jax.ShapeDtypeStruct((rows, KV_DIM), F32), jax.ShapeDtypeStruct((rows, KV_DIM), F32),
                 jax.ShapeDtypeStruct((rows, KV_DIM), BF), jax.ShapeDtypeStruct((rows, KV_DIM), BF)]
    out_specs = [row_spec(Q_DIM), row_spec(KV_DIM), row_spec(KV_DIM), row_spec(KV_DIM), row_spec(KV_DIM)]
    if dsa:
        out_shape += [jax.ShapeDtypeStruct((rows, IDX_Q_DIM), BF),
                      jax.ShapeDtypeStruct((rows, LANES), F32), jax.ShapeDtypeStruct((rows, LANES), BF)]
        out_specs += [row_spec(IDX_Q_DIM), row_spec(LANES), row_spec(LANES)]
    return pl.pallas_call(
        functools.partial(_proj_kernel, dsa=dsa),
        grid=(rows // tm,),
        in_specs=[row_spec(D_MODEL),
                  pl.BlockSpec((1, D_MODEL), lambda i: (0, 0)),
                  _mod_spec(scale, tm, rows_per_seq), _mod_spec(shift, tm, rows_per_seq),
                  tab_spec, tab_spec,
                  pl.BlockSpec((D_MODEL, n_in), lambda i: (0, 0))],
        out_specs=out_specs,
        out_shape=out_shape,
        compiler_params=_params("parallel"),
        name="project_dsa" if dsa else "project",
    )(x, g, scale, shift, cos_tab, sin_tab, w)


def _rope_rows(zt, cos_t, sin_t, n_heads, scale):
    half = HEAD_DIM // 2
    out = []
    for h in range(n_heads):
        x = zt[h * HEAD_DIM:(h + 1) * HEAD_DIM]
        partner = jnp.concatenate([x[half:], x[:half]], axis=0)
        out.append(((x * cos_t + partner * sin_t) * scale).astype(BF))
    return out


def _proj_t_kernel(x_ref, g_ref, sc_ref, sh_ref, cos_ref, sin_ref, cost_ref, sint_ref, w_ref, wqt_ref, wvt_ref,
                   *refs, dsa):
    if dsa:
        wit_ref, wrt_ref = refs[:2]
        refs = refs[2:]
    qt_ref, k_ref, v_ref, kz_ref, vt_ref = refs[:5]
    hb = _norm_mod(x_ref[...], g_ref[...], sc_ref[...], sh_ref[...]).astype(BF)
    lane = lax.broadcasted_iota(I32, (1, LANES), 1)
    lo_half = (lane % HEAD_DIM) < (HEAD_DIM // 2)

    zqt = lax.dot_general(wqt_ref[...], hb, NT, preferred_element_type=F32)
    cos_t, sin_t = cost_ref[...], sint_ref[...]
    for h, qh in enumerate(_rope_rows(zqt, cos_t, sin_t, N_HEADS, ATTN_SCALE)):
        qt_ref[h * HEAD_DIM:(h + 1) * HEAD_DIM, :] = qh
    if dsa:
        qit_ref, wt_ref, rest_ref, kiz_ref = refs[5:]
        zit = lax.dot_general(wit_ref[...], hb, NT, preferred_element_type=F32)
        for h, qh in enumerate(_rope_rows(zit, cos_t, sin_t, IDX_HEADS, 1.0)):
            qit_ref[h * IDX_DIM:(h + 1) * IDX_DIM, :] = qh
        zr = jnp.dot(hb, w_ref[:, IN_ATTN + IDX_Q_DIM:], preferred_element_type=F32)
        roped = _rope_chunk(zr, cos_ref[...], sin_ref[...], lo_half)
        rest_ref[...] = jnp.where(lane < IDX_DIM, roped, zr * IDX_W_SCALE)
        kiz_ref[...] = jnp.where(lane < IDX_DIM, roped, 0.0).astype(BF)
        zrt = lax.dot_general(wrt_ref[...], hb, NT, preferred_element_type=F32)
        wt_ref[...] = zrt[IDX_DIM:IDX_DIM + IDX_HEADS] * IDX_W_SCALE
    zk = jnp.dot(hb, w_ref[:, Q_DIM:Q_DIM + KV_DIM], preferred_element_type=F32)
    for c in range(KV_DIM // LANES):
        sl = slice(c * LANES, (c + 1) * LANES)
        kr = _rope_chunk(zk[:, sl], cos_ref[...], sin_ref[...], lo_half)
        k_ref[:, sl] = kr
        kz_ref[:, 2 * c * LANES:(2 * c + 1) * LANES] = jnp.where(lane < HEAD_DIM, kr, 0.0).astype(BF)
        kz_ref[:, (2 * c + 1) * LANES:(2 * c + 2) * LANES] = jnp.where(
            lane < HEAD_DIM, pltpu.roll(kr, HEAD_DIM, 1), 0.0).astype(BF)
    v_ref[...] = jnp.dot(hb, w_ref[:, Q_DIM + KV_DIM:IN_ATTN], preferred_element_type=F32)
    vt_ref[...] = lax.dot_general(wvt_ref[...], hb, NT, preferred_element_type=F32).astype(BF)


def _project_t(x, g, scale, shift, cos_tab, sin_tab, cos_t, sin_t, w, rows_per_seq, dsa):
    rows = x.shape[0]
    tm = min(ROW_TILE, rows)
    tab_tiles = cos_tab.shape[0] // tm
    row_spec = lambda n: pl.BlockSpec((tm, n), lambda i: (i, 0))
    col_spec = lambda n: pl.BlockSpec((n, tm), lambda i: (0, i))
    tab_spec = pl.BlockSpec((tm, LANES), lambda i: (i % tab_tiles, 0))
    tabt_spec = pl.BlockSpec((HEAD_DIM, tm), lambda i: (0, i % tab_tiles))
    whole = lambda a: pl.BlockSpec(a.shape, lambda i: (0, 0))
    weights = [w, w[:, :Q_DIM].T, w[:, Q_DIM + KV_DIM:IN_ATTN].T]
    out_specs = [col_spec(Q_DIM), row_spec(KV_DIM), row_spec(KV_DIM), row_spec(2 * KV_DIM), col_spec(KV_DIM)]
    out_shape = [jax.ShapeDtypeStruct((Q_DIM, rows), BF),
                 jax.ShapeDtypeStruct((rows, KV_DIM), F32), jax.ShapeDtypeStruct((rows, KV_DIM), F32),
                 jax.ShapeDtypeStruct((rows, 2 * KV_DIM), BF), jax.ShapeDtypeStruct((KV_DIM, rows), BF)]
    if dsa:
        weights += [w[:, IN_ATTN:IN_ATTN + IDX_Q_DIM].T, w[:, IN_ATTN + IDX_Q_DIM:].T]
        out_specs[4] = pl.BlockSpec((None, KV_DIM, tm), lambda i: (i, 0, 0))
        out_shape[4] = jax.ShapeDtypeStruct((rows // tm, KV_DIM, tm), BF)
        out_specs += [col_spec(IDX_Q_DIM), col_spec(IDX_HEADS), row_spec(LANES), row_spec(LANES)]
        out_shape += [jax.ShapeDtypeStruct((IDX_Q_DIM, rows), BF), jax.ShapeDtypeStruct((IDX_HEADS, rows), F32),
                      jax.ShapeDtypeStruct((rows, LANES), F32), jax.ShapeDtypeStruct((rows, LANES), BF)]
    return pl.pallas_call(
        functools.partial(_proj_t_kernel, dsa=dsa),
        grid=(rows // tm,),
        in_specs=[row_spec(D_MODEL),
                  pl.BlockSpec((1, D_MODEL), lambda i: (0, 0)),
                  _mod_spec(scale, tm, rows_per_seq), _mod_spec(shift, tm, rows_per_seq),
                  tab_spec, tab_spec, tabt_spec, tabt_spec] + [whole(a) for a in weights],
        out_specs=out_specs,
        out_shape=out_shape,
        compiler_params=_params("parallel"),
        name="project_t_dsa" if dsa else "project_t",
    )(x, g, scale, shift, cos_tab, sin_tab, cos_t, sin_t, *weights)


def _post_kernel(x_ref, o_ref, wo_ref, ga_ref, g_ref, sc_ref, sh_ref, gm_ref, w1_ref, w2_ref, y_ref,
                 x1_scr, h_scr, acc_scr):
    j = pl.program_id(1)

    @pl.when(j == 0)
    def _():
        x1 = x_ref[...] + ga_ref[...] * jnp.dot(o_ref[...], wo_ref[...], preferred_element_type=F32)
        x1_scr[...] = x1
        h_scr[...] = _norm_mod(x1, g_ref[...], sc_ref[...], sh_ref[...]).astype(BF)
        acc_scr[...] = jnp.zeros_like(acc_scr)

    a = jnp.maximum(jnp.dot(h_scr[...], w1_ref[...], preferred_element_type=F32), 0.0)
    acc_scr[...] += jnp.dot((a * a).astype(BF), w2_ref[...], preferred_element_type=F32)

    @pl.when(j == pl.num_programs(1) - 1)
    def _():
        y_ref[...] = x1_scr[...] + gm_ref[...] * acc_scr[...]


def _post(x, o, w_out, gate_a, g_mlp, scale_m, shift_m, gate_m, w1, w2, rows_per_seq):
    rows = x.shape[0]
    tm = min(ROW_TILE, rows)
    tf = FF_TILE
    ms = lambda a: _mod_spec(a, tm, rows_per_seq)
    return pl.pallas_call(
        _post_kernel,
        grid=(rows // tm, D_FF // tf),
        in_specs=[pl.BlockSpec((tm, D_MODEL), lambda i, j: (i, 0)),
                  pl.BlockSpec((tm, Q_DIM), lambda i, j: (i, 0)),
                  pl.BlockSpec((Q_DIM, D_MODEL), lambda i, j: (0, 0)),
                  ms(gate_a),
                  pl.BlockSpec((1, D_MODEL), lambda i, j: (0, 0)),
                  ms(scale_m), ms(shift_m), ms(gate_m),
                  pl.BlockSpec((D_MODEL, tf), lambda i, j: (0, j)),
                  pl.BlockSpec((tf, D_MODEL), lambda i, j: (j, 0))],
        out_specs=pl.BlockSpec((tm, D_MODEL), lambda i, j: (i, 0)),
        out_shape=jax.ShapeDtypeStruct((rows, D_MODEL), F32),
        scratch_shapes=[pltpu.VMEM((tm, D_MODEL), F32), pltpu.VMEM((tm, D_MODEL), BF),
                        pltpu.VMEM((tm, D_MODEL), F32)],
        compiler_params=_params("parallel", "arbitrary"),
        name="post",
    )(x, o, w_out, gate_a, g_mlp, scale_m, shift_m, gate_m, w1, w2)


def _final_norm_kernel(x_ref, g_ref, y_ref):
    x = x_ref[...]
    y_ref[...] = x * lax.rsqrt(jnp.mean(x * x, axis=-1, keepdims=True) + NORM_EPS) * g_ref[...]


def _final_norm(x, g):
    rows = x.shape[0]
    tm = min(ROW_TILE, rows)
    return pl.pallas_call(
        _final_norm_kernel,
        grid=(rows // tm,),
        in_specs=[pl.BlockSpec((tm, D_MODEL), lambda i: (i, 0)), pl.BlockSpec((1, D_MODEL), lambda i: (0, 0))],
        out_specs=pl.BlockSpec((tm, D_MODEL), lambda i: (i, 0)),
        out_shape=jax.ShapeDtypeStruct((rows, D_MODEL), F32),
        compiler_params=_params("parallel"),
        name="final_norm",
    )(x, g)


def _block_mean_kernel(k_ref, o_ref):
    o_ref[...] = jnp.mean(k_ref[...], axis=0, keepdims=True)


def _block_means(k):
    nb = k.shape[0] // MOBA_BLOCK
    return pl.pallas_call(
        _block_mean_kernel,
        grid=(nb,),
        in_specs=[pl.BlockSpec((MOBA_BLOCK, KV_DIM), lambda i: (i, 0))],
        out_specs=pl.BlockSpec((None, 1, KV_DIM), lambda i: (i, 0, 0)),
        out_shape=jax.ShapeDtypeStruct((nb, 1, KV_DIM), F32),
        compiler_params=_params("parallel"),
        name="moba_block_means",
    )(k)


def _top_mask(gs, ids, n_sel, axis):
    picked = None
    for _ in range(n_sel):
        mx = jnp.max(gs, axis=axis, keepdims=True)
        am = jnp.min(jnp.where(gs == mx, ids, 2 ** 30), axis=axis, keepdims=True)
        hit = (ids == am) & (mx > NEG_INF)
        picked = hit if picked is None else (picked | hit)
        gs = jnp.where(ids == am, NEG_INF, gs)
    return picked


MOBA_ACC_ROWS = HEAD_DIM + 16
MOBA_MASKED = -1e30

def _moba_prompt_kernel(qi_tab, kp_tab, last_tab, qt_ref, kz_own_ref, vt_own_ref, kz_pair_ref, vt_pair_ref, kmz_ref,
                        o_ref, qz_scr, m_scr, acc_scr, *, n_sel):
    p = pl.program_id(1)
    i = qi_tab[p]
    kp = kp_tab[p]
    blk = MOBA_BLOCK

    @pl.when(kp < 0)
    def _():
        rows = lax.broadcasted_iota(I32, (LANES, blk), 0)
        earlier = (rows >= HEAD_DIM) & (rows < HEAD_DIM + i)
        for h in range(N_HEADS):
            qt = qt_ref[h * HEAD_DIM:(h + 1) * HEAD_DIM, :]
            gs = jnp.dot(kmz_ref[h // GROUP], qt.astype(F32), precision=lax.Precision.HIGHEST,
                         preferred_element_type=F32)
            picked = _top_mask(jnp.where(earlier, gs, NEG_INF), rows, n_sel, 0)
            usable = (earlier & picked) | (rows == HEAD_DIM + i)
            bias = jnp.where(usable, 0.0, MOBA_MASKED)
            qz_scr[h // GROUP, :, (h % GROUP) * blk:(h % GROUP + 1) * blk] = jnp.concatenate(
                [qt, bias[HEAD_DIM:].astype(BF)], axis=0)
        m_scr[...] = jnp.full(m_scr.shape, NEG_INF, F32)
        acc_scr[...] = jnp.zeros_like(acc_scr)

    def step(kz_ref, vt_ref, n_keys, block_of_key, causal):
        lane = lax.broadcasted_iota(I32, (n_keys, LANES), 1)
        ones_rows = (lax.broadcasted_iota(I32, (MOBA_ACC_ROWS - HEAD_DIM, n_keys), 0) == 0).astype(BF)
        sts = []
        for kv in range(N_KV_HEADS):
            kz = jnp.where(lane == HEAD_DIM + block_of_key, 1.0, kz_ref[:, kv * LANES:(kv + 1) * LANES]).astype(BF)
            sts.append(jnp.dot(kz, qz_scr[kv], preferred_element_type=F32))
        es, alphas = [], []
        for kv in range(N_KV_HEADS):
            st = sts[kv]
            if causal:
                key_i = lax.broadcasted_iota(I32, (n_keys, GROUP * blk), 0)
                qry_i = lax.broadcasted_iota(I32, (n_keys, GROUP * blk), 1) % blk
                st = jnp.where(key_i <= qry_i, st, NEG_INF)
            m_old = m_scr[kv]
            m_new = jnp.maximum(m_old, jnp.max(st, axis=0, keepdims=True))
            alphas.append(jnp.exp(m_old - m_new))
            es.append(jnp.exp(st - m_new).astype(BF))
            m_scr[kv] = m_new
        for kv in range(N_KV_HEADS):
            vaug = jnp.concatenate([vt_ref[kv * HEAD_DIM:(kv + 1) * HEAD_DIM, :], ones_rows], axis=0)
            acc_scr[kv] = alphas[kv] * acc_scr[kv] + jnp.dot(vaug, es[kv], preferred_element_type=F32)

    @pl.when(kp < 0)
    def _():
        step(kz_own_ref, vt_own_ref, blk, i, True)

    @pl.when(kp >= 0)
    def _():
        first = 2 * kp
        second = jnp.where(first + 1 < i, first + 1, i + 1)
        key_row = lax.broadcasted_iota(I32, (2 * blk, 1), 0)
        step(kz_pair_ref, vt_pair_ref, 2 * blk, jnp.where(key_row < blk, first, second), False)

    @pl.when(last_tab[p] == 1)
    def _():
        for c in range(N_HEADS // 2):
            halves = []
            for h in (2 * c, 2 * c + 1):
                a = acc_scr[h // GROUP, :, (h % GROUP) * blk:(h % GROUP + 1) * blk]
                halves.append(a[:HEAD_DIM] * (1.0 / a[HEAD_DIM:HEAD_DIM + 1]))
            o_ref[:, c * LANES:(c + 1) * LANES] = jnp.concatenate(halves, axis=0).T.astype(BF)


def _moba_prompt(qt, kz, vt, k, batch):
    rows = qt.shape[1]
    seq = rows // batch
    blk = MOBA_BLOCK
    assert seq % blk == 0
    nblk = seq // blk
    assert HEAD_DIM + nblk + 1 <= LANES and nblk % 2 == 0
    n_sel = min(MOBA_TOPK, nblk)
    kmean = _block_means(k).reshape(batch, nblk, N_KV_HEADS, HEAD_DIM).transpose(0, 2, 1, 3)
    kmz = jnp.pad(kmean, ((0, 0), (0, 0), (HEAD_DIM, LANES - HEAD_DIM - nblk), (0, 0)))
    qi, kp, last = [], [], []
    for i in range(nblk):
        order = [-1] + list(range((i + 1) // 2))
        qi += [i] * len(order)
        kp += order
        last += [0] * (len(order) - 1) + [1]
    tabs = [jnp.asarray(np.array(t, np.int32)) for t in (qi, kp, last)]
    npair = nblk // 2
    own_rows = lambda b, p, qi, kp, la: (b * nblk + qi[p], 0)
    own_cols = lambda b, p, qi, kp, la: (0, b * nblk + qi[p])
    grid_spec = pltpu.PrefetchScalarGridSpec(
        num_scalar_prefetch=3,
        grid=(batch, len(qi)),
        in_specs=[pl.BlockSpec((Q_DIM, blk), own_cols),
                  pl.BlockSpec((blk, 2 * KV_DIM), own_rows),
                  pl.BlockSpec((KV_DIM, blk), own_cols),
                  pl.BlockSpec((2 * blk, 2 * KV_DIM), lambda b, p, qi, kp, la: (b * npair + jnp.maximum(kp[p], 0), 0)),
                  pl.BlockSpec((KV_DIM, 2 * blk), lambda b, p, qi, kp, la: (0, b * npair + jnp.maximum(kp[p], 0))),
                  pl.BlockSpec((None, N_KV_HEADS, LANES, HEAD_DIM), lambda b, p, qi, kp, la: (b, 0, 0, 0))],
        out_specs=pl.BlockSpec((blk, Q_DIM), own_rows),
        scratch_shapes=[pltpu.VMEM((N_KV_HEADS, LANES, GROUP * blk), BF), pltpu.VMEM((N_KV_HEADS, 1, GROUP * blk), F32),
                        pltpu.VMEM((N_KV_HEADS, MOBA_ACC_ROWS, GROUP * blk), F32)])
    return pl.pallas_call(
        functools.partial(_moba_prompt_kernel, n_sel=n_sel),
        grid_spec=grid_spec,
        out_shape=jax.ShapeDtypeStruct((rows, Q_DIM), BF),
        compiler_params=_params("parallel", "arbitrary"),
        name="moba_prompt",
    )(*tabs, qt, kz, vt, kz, vt, kmz)


def _float_key(x):
    b = pltpu.bitcast(x, I32)
    return jnp.where(b < 0, b ^ 0x7FFFFFFF, b)


def _kth_largest_key(count_ge, k_top, shape):
    def body(it, cur):
        cand = cur | jnp.left_shift(jnp.int32(1), 31 - it)
        return jnp.where(count_ge(cand ^ INT_MIN) >= k_top, cand, cur)
    return lax.fori_loop(0, 32, body, jnp.zeros(shape, I32)) ^ INT_MIN


def _tie_cutoff(count_eq_below, need, n_bits, shape):
    def body(it, cur):
        cand = cur | jnp.left_shift(jnp.int32(1), n_bits - 1 - it)
        return jnp.where(count_eq_below(cand) < need, cand, cur)
    return lax.fori_loop(0, n_bits, body, jnp.zeros(shape, I32))


def _swap_halves(x):
    return jnp.concatenate([x[HEAD_DIM:], x[:HEAD_DIM]], axis=0)


def _dsa_prompt_kernel(qt_ref, qit_ref, wt_ref, kiz_ref, kz_ref, vt_ref, o_ref,
                       qp_scr, qip_scr, key_scr, m_scr, acc_scr, *, k_top, n_bits):
    i = pl.program_id(1)
    tq, ch = DSA_QUERY_TILE, DSA_KEY_CHUNK
    n_ch = ((i + 1) * tq + ch - 1) // ch
    qpos = i * tq + lax.broadcasted_iota(I32, (1, tq), 1)
    key_row = lax.broadcasted_iota(I32, (ch, tq), 0)
    ones_rows = (lax.broadcasted_iota(I32, (MOBA_ACC_ROWS - HEAD_DIM, ch), 0) == 0).astype(BF)

    for c in range(IDX_HEADS // 2):
        pair = qit_ref[c * LANES:(c + 1) * LANES, :]
        qip_scr[0, :, c * tq:(c + 1) * tq] = pair
        qip_scr[1, :, c * tq:(c + 1) * tq] = _swap_halves(pair)
    for c in range(N_HEADS // 2):
        pair = qt_ref[c * LANES:(c + 1) * LANES, :]
        cols = slice((c % 2) * tq, (c % 2 + 1) * tq)
        qp_scr[2 * (c // 2), :, cols] = pair
        qp_scr[2 * (c // 2) + 1, :, cols] = _swap_halves(pair)

    def score_body(c, carry):
        kiz = kiz_ref[pl.ds(pl.multiple_of(c * ch, ch), ch), :]
        sc = jnp.zeros((ch, tq), F32)
        for second in range(2):
            d = jnp.dot(kiz, qip_scr[second], preferred_element_type=F32)
            for u in range(IDX_HEADS // 2):
                h = 2 * u + second
                sc = sc + wt_ref[h:h + 1, :] * jnp.maximum(d[:, u * tq:(u + 1) * tq], 0.0)
        sc = jnp.where(c * ch + key_row <= qpos, sc, NEG_INF)
        key_scr[c] = _float_key(sc)
        return carry
    lax.fori_loop(0, n_ch, score_body, 0)

    def count(pred):
        def body(c, acc):
            return acc + jnp.sum(pred(key_scr[c], c * ch + key_row).astype(I32), axis=0, keepdims=True)
        return lax.fori_loop(0, n_ch, body, jnp.zeros((1, tq), I32))

    thr = _kth_largest_key(lambda t: count(lambda key, idx: key >= t), k_top, (1, tq))
    n_gt = count(lambda key, idx: key > thr)
    n_ge = count(lambda key, idx: key >= thr)
    need = k_top - n_gt
    tie_cols = ((n_ge > k_top) & (thr != KEY_NEG_INF)).astype(I32)
    cutoff = lax.cond(
        jnp.max(tie_cols) > 0,
        lambda: _tie_cutoff(lambda c_: count(lambda key, idx: (key == thr) & (idx < c_)), need, n_bits, (1, tq)),
        lambda: jnp.full((1, tq), 2 ** 30, I32))

    m_scr[...] = jnp.full(m_scr.shape, NEG_INF, F32)
    acc_scr[...] = jnp.zeros_like(acc_scr)

    def attn_body(c, carry):
        key = key_scr[c]
        idx = c * ch + key_row
        keep = ((key > thr) | ((key == thr) & (idx <= cutoff))) & (idx <= qpos)
        bias = jnp.where(keep, 0.0, NEG_INF)
        bias = jnp.concatenate([bias, bias], axis=1)
        rows = pl.ds(pl.multiple_of(c * ch, ch), ch)
        sts = []
        for kv in range(N_KV_HEADS):
            kz = kz_ref[rows, kv * LANES:(kv + 1) * LANES]
            for second in range(2):
                sts.append(jnp.dot(kz, qp_scr[2 * kv + second], preferred_element_type=F32))
        for kv in range(N_KV_HEADS):
            vaug = jnp.concatenate([vt_ref[c, kv * HEAD_DIM:(kv + 1) * HEAD_DIM, :], ones_rows], axis=0)
            for second in range(2):
                g = 2 * kv + second
                st = sts[g] + bias
                m_old = m_scr[g]
                m_new = jnp.maximum(m_old, jnp.max(st, axis=0, keepdims=True))
                m_safe = jnp.where(m_new == NEG_INF, 0.0, m_new)
                alpha = jnp.exp(m_old - m_safe)
                e = jnp.exp(st - m_safe).astype(BF)
                acc_scr[g] = alpha * acc_scr[g] + jnp.dot(vaug, e, preferred_element_type=F32)
                m_scr[g] = m_new
        return carry
    lax.fori_loop(0, n_ch, attn_body, 0)

    for kv in range(N_KV_HEADS):
        outs = []
        for second in range(2):
            a = acc_scr[2 * kv + second]
            outs.append(a[:HEAD_DIM] * (1.0 / a[HEAD_DIM:HEAD_DIM + 1]))
        for u in range(2):
            both = jnp.concatenate([outs[0][:, u * tq:(u + 1) * tq], outs[1][:, u * tq:(u + 1) * tq]], axis=0)
            o_ref[:, (2 * kv + u) * LANES:(2 * kv + u + 1) * LANES] = both.T.astype(BF)


def _dsa_prompt(qt, qit, wt, kiz, kz, vt3, batch):
    rows = qt.shape[1]
    seq = rows // batch
    tq, ch = DSA_QUERY_TILE, DSA_KEY_CHUNK
    k_top = min(DSA_TOPK_MAX, seq // 4)
    assert seq % ch == 0 and ch >= k_top and ch % tq == 0 and tq % LANES == 0 and vt3.shape[2] == ch
    nq, n_chunks = seq // tq, seq // ch
    n_bits = max(1, (seq - 1).bit_length())
    tile = lambda n: pl.BlockSpec((n, tq), lambda b, i: (0, b * nq + i))
    return pl.pallas_call(
        functools.partial(_dsa_prompt_kernel, k_top=k_top, n_bits=n_bits),
        grid=(batch, nq),
        in_specs=[tile(Q_DIM), tile(IDX_Q_DIM), tile(IDX_HEADS),
                  pl.BlockSpec((seq, LANES), lambda b, i: (b, 0)),
                  pl.BlockSpec((seq, 2 * KV_DIM), lambda b, i: (b, 0)),
                  pl.BlockSpec((n_chunks, KV_DIM, ch), lambda b, i: (b, 0, 0))],
        out_specs=pl.BlockSpec((tq, Q_DIM), lambda b, i: (b * nq + i, 0)),
        out_shape=jax.ShapeDtypeStruct((rows, Q_DIM), BF),
        scratch_shapes=[pltpu.VMEM((2 * N_KV_HEADS, LANES, 2 * tq), BF),
                        pltpu.VMEM((2, LANES, (IDX_HEADS // 2) * tq), BF),
                        pltpu.VMEM((n_chunks, ch, tq), I32),
                        pltpu.VMEM((2 * N_KV_HEADS, 1, 2 * tq), F32),
                        pltpu.VMEM((2 * N_KV_HEADS, MOBA_ACC_ROWS, 2 * tq), F32)],
        compiler_params=_params("parallel", "arbitrary"),
        name="dsa_prompt",
    )(qt, qit, wt, kiz, kz, vt3)


def _swa_prompt_kernel(qt_ref, kzp_ref, kzc_ref, vtp_ref, vtc_ref, sink_ref, o_ref):
    n = pl.program_id(1)
    w = WINDOW
    key_i = lax.broadcasted_iota(I32, (2 * w, w), 0)
    qry_i = lax.broadcasted_iota(I32, (2 * w, w), 1)
    keep = (key_i >= qry_i) & (key_i <= qry_i + w) & ((key_i >= w) | (n > 0))
    bias = jnp.where(keep, 0.0, NEG_INF)
    bias = jnp.concatenate([bias, bias], axis=1)
    ones_rows = (lax.broadcasted_iota(I32, (MOBA_ACC_ROWS - HEAD_DIM, 2 * w), 0) == 0).astype(BF)
    for kv in range(N_KV_HEADS):
        lanes = slice(kv * LANES, (kv + 1) * LANES)
        feats = slice(kv * HEAD_DIM, (kv + 1) * HEAD_DIM)
        kz = jnp.concatenate([kzp_ref[:, lanes], kzc_ref[:, lanes]], axis=0)
        vaug = jnp.concatenate([jnp.concatenate([vtp_ref[feats, :], vtc_ref[feats, :]], axis=1), ones_rows], axis=0)
        pairs = [qt_ref[(2 * kv + u) * LANES:(2 * kv + u + 1) * LANES, :] for u in range(2)]
        outs = []
        for second in range(2):
            ops = pairs if second == 0 else [_swap_halves(p) for p in pairs]
            st = jnp.dot(kz, jnp.concatenate(ops, axis=1), preferred_element_type=F32) + bias
            sink = sink_ref[2 * kv + second]
            m = jnp.maximum(jnp.max(st, axis=0, keepdims=True), sink)
            acc = jnp.dot(vaug, jnp.exp(st - m).astype(BF), preferred_element_type=F32)
            outs.append(acc[:HEAD_DIM] * (1.0 / (acc[HEAD_DIM:HEAD_DIM + 1] + jnp.exp(sink - m))))
        for u in range(2):
            both = jnp.concatenate([outs[0][:, u * w:(u + 1) * w], outs[1][:, u * w:(u + 1) * w]], axis=0)
            o_ref[:, (2 * kv + u) * LANES:(2 * kv + u + 1) * LANES] = both.T.astype(BF)


def _swa_prompt(qt, kz, vt, sinks, batch):
    rows = qt.shape[1]
    seq = rows // batch
    w = WINDOW
    assert seq % w == 0 and w == LANES
    nb = seq // w
    cur = lambda b, n: b * nb + n
    prev = lambda b, n: b * nb + jnp.maximum(n - 1, 0)
    sk = sinks.astype(F32).reshape(N_KV_HEADS, 2, 2)
    sink_cols = jnp.repeat(sk.transpose(0, 2, 1), w, axis=2).reshape(2 * N_KV_HEADS, 1, 2 * w)
    return pl.pallas_call(
        _swa_prompt_kernel,
        grid=(batch, nb),
        in_specs=[pl.BlockSpec((Q_DIM, w), lambda b, n: (0, cur(b, n))),
                  pl.BlockSpec((w, 2 * KV_DIM), lambda b, n: (prev(b, n), 0)),
                  pl.BlockSpec((w, 2 * KV_DIM), lambda b, n: (cur(b, n), 0)),
                  pl.BlockSpec((KV_DIM, w), lambda b, n: (0, prev(b, n))),
                  pl.BlockSpec((KV_DIM, w), lambda b, n: (0, cur(b, n))),
                  pl.BlockSpec((2 * N_KV_HEADS, 1, 2 * w), lambda b, n: (0, 0, 0))],
        out_specs=pl.BlockSpec((w, Q_DIM), lambda b, n: (cur(b, n), 0)),
        out_shape=jax.ShapeDtypeStruct((rows, Q_DIM), BF),
        compiler_params=_params("parallel", "parallel"),
        name="swa_prompt",
    )(qt, kz, kz, vt, vt, sink_cols)


def _row_token(n_rows, n_tok):
    r = lax.broadcasted_iota(I32, (n_rows, 1), 0)
    return (r // GROUP) % n_tok


def _block_diag_queries(q, n_seq, n_tok):
    qr = q.reshape(n_seq, n_tok, N_KV_HEADS, GROUP, HEAD_DIM).transpose(0, 2, 1, 3, 4)
    qr = qr.reshape(n_seq, N_KV_HEADS, n_tok * GROUP, HEAD_DIM)
    eye = jnp.eye(N_KV_HEADS, dtype=q.dtype)
    qbd = qr[:, :, :, None, :] * eye[None, :, None, :, None]
    return qbd.reshape(n_seq, N_KV_HEADS * n_tok * GROUP, KV_DIM)


def _undiag_outputs(out, n_seq, n_tok):
    o = out.reshape(n_seq, N_KV_HEADS, n_tok, GROUP, N_KV_HEADS, HEAD_DIM)
    o = jnp.stack([o[:, kv, :, :, kv] for kv in range(N_KV_HEADS)], axis=1)
    return o.transpose(0, 2, 1, 3, 4).reshape(n_seq * n_tok, Q_DIM)


def _tokens_on_lanes(cache):
    if cache.ndim == 5:
        t = jnp.transpose(cache, (0, 1, 3, 4, 2))
        return t.reshape(t.shape[:2] + (t.shape[2] * t.shape[3], t.shape[4]))
    return jnp.transpose(cache, (0, 1, 3, 2))


def _page_specs(n_pages_per_step, slot, width):
    def spec(r):
        return pl.BlockSpec((None, None, width, PAGE_SIZE),
                            lambda b, s, pt: (slot, pt[b, s * n_pages_per_step + r], 0, 0))
    return [spec(r) for r in range(n_pages_per_step)]


def _pad_rows(a, n_seq, n_tok):
    a = a.reshape(n_seq, n_tok, a.shape[-1])
    return jnp.pad(a, ((0, 0), (0, SUBLANES - n_tok), (0, 0)))


def _moba_sample_kernel(pt_ref, q_ref, kn_ref, vn_ref, *refs, n_tok, n_sel, pps):
    kt_pages = refs[:pps]
    vt_pages = refs[pps:2 * pps]
    o_ref = refs[2 * pps]
    m_scr, l_scr, g_scr, o_scr = refs[2 * pps + 1:]
    s = pl.program_id(1)
    n_steps = m_scr.shape[0]
    bps = pps // 2
    q = q_ref[...]
    n_row = q.shape[0]
    m_scr[s] = jnp.full(m_scr.shape[1:], NEG_INF, F32)
    l_scr[s] = jnp.zeros(l_scr.shape[1:], F32)
    g_scr[s] = jnp.full(g_scr.shape[1:], NEG_INF, F32)
    kt_all = jnp.concatenate([p[...] for p in kt_pages], axis=1).astype(BF)
    sc_all = jnp.dot(q, kt_all, preferred_element_type=F32)
    for r in range(bps):
        vt = jnp.concatenate([vt_pages[2 * r][...], vt_pages[2 * r + 1][...]], axis=1).astype(BF)
        sc = sc_all[:, r * MOBA_BLOCK:(r + 1) * MOBA_BLOCK]
        m = jnp.max(sc, axis=1, keepdims=True)
        e = jnp.exp(sc - m)
        m_scr[s, :, r:r + 1] = m
        l_scr[s, :, r:r + 1] = jnp.sum(e, axis=1, keepdims=True)
        g_scr[s, :, r:r + 1] = jnp.sum(sc, axis=1, keepdims=True)
        o_scr[s * bps + r] = lax.dot_general(e.astype(BF), vt, NT, preferred_element_type=F32)

    @pl.when(s == pl.num_programs(1) - 1)
    def _():
        lane = lax.broadcasted_iota(I32, (n_row, LANES), 1)
        ids = [t * bps + lane for t in range(n_steps)]
        gs = [g_scr[t] for t in range(n_steps)]
        picked = [None] * n_steps
        for _ in range(n_sel):
            mx = functools.reduce(jnp.maximum, [jnp.max(g, axis=1, keepdims=True) for g in gs])
            am = functools.reduce(jnp.minimum, [jnp.min(jnp.where(g == mx, i_, 2 ** 30), axis=1, keepdims=True)
                                                for g, i_ in zip(gs, ids)])
            for t in range(n_steps):
                hit = (ids[t] == am) & (mx > NEG_INF)
                picked[t] = hit if picked[t] is None else (picked[t] | hit)
                gs[t] = jnp.where(ids[t] == am, NEG_INF, gs[t])
        own_ok = lax.broadcasted_iota(I32, (n_row, SUBLANES), 1) <= _row_token(n_row, n_tok)
        s_own = jnp.where(own_ok, lax.dot_general(q, kn_ref[...], NT, preferred_element_type=F32), NEG_INF)
        m_tot = jnp.max(s_own, axis=1, keepdims=True)
        for t in range(n_steps):
            m_tot = jnp.maximum(m_tot, jnp.max(jnp.where(picked[t], m_scr[t], NEG_INF), axis=1, keepdims=True))
        e_own = jnp.exp(s_own - m_tot)
        den = jnp.sum(e_own, axis=1, keepdims=True)
        num = jnp.dot(e_own.astype(BF), vn_ref[...], preferred_element_type=F32)
        for t in range(n_steps):
            wgt = jnp.where(picked[t], jnp.exp(m_scr[t] - m_tot), 0.0)
            den = den + jnp.sum(wgt * l_scr[t], axis=1, keepdims=True)
            for r in range(bps):
                num = num + wgt[:, r:r + 1] * o_scr[t * bps + r]
        o_ref[...] = num * (1.0 / den)


def _paged_attention_call(kernel, name, slot, page_table, cache_kt, cache_vt, extra_inputs, extra_specs,
                          scratch_shapes, n_row):
    n_seq, n_pages = page_table.shape
    pps = PAGES_PER_STEP
    assert n_pages % pps == 0
    grid_spec = pltpu.PrefetchScalarGridSpec(
        num_scalar_prefetch=1,
        grid=(n_seq, n_pages // pps),
        in_specs=extra_specs + _page_specs(pps, slot, KV_DIM) + _page_specs(pps, slot, KV_DIM),
        out_specs=pl.BlockSpec((None, n_row, KV_DIM), lambda b, s, pt: (b, 0, 0)),
        scratch_shapes=scratch_shapes)
    return pl.pallas_call(
        kernel,
        grid_spec=grid_spec,
        out_shape=jax.ShapeDtypeStruct((n_seq, n_row, KV_DIM), F32),
        compiler_params=_params("parallel", "arbitrary"),
        name=name,
    )(page_table, *extra_inputs, *([cache_kt] * pps), *([cache_vt] * pps))


def _moba_sample(qbd, kn_pad, vn_pad, cache_kt, cache_vt, slot, page_table, n_tok):
    n_seq, n_pages = page_table.shape
    past = n_pages * PAGE_SIZE
    assert past % MOBA_BLOCK == 0 and MOBA_BLOCK == 2 * PAGE_SIZE and PAGES_PER_STEP % 2 == 0
    nblk = past // MOBA_BLOCK
    n_steps = n_pages // PAGES_PER_STEP
    n_row = qbd.shape[1]
    per_seq = lambda w: pl.BlockSpec((None, w[0], w[1]), lambda b, s, pt: (b, 0, 0))
    kernel = functools.partial(_moba_sample_kernel, n_tok=n_tok, n_sel=min(MOBA_TOPK, nblk), pps=PAGES_PER_STEP)
    stats = pltpu.VMEM((n_steps, n_row, LANES), F32)
    return _paged_attention_call(
        kernel, "moba_sample", slot, page_table, cache_kt, cache_vt,
        [qbd, kn_pad, vn_pad],
        [per_seq((n_row, KV_DIM)), per_seq((SUBLANES, KV_DIM)), per_seq((SUBLANES, KV_DIM))],
        [stats, stats, stats, pltpu.VMEM((nblk, n_row, KV_DIM), F32)],
        n_row)


def _dsa_index_kernel(pt_ref, qi_ref, w_ref, kin_ref, *refs, n_tok, k_top, ppi, n_bits):
    ki_pages = refs[:ppi]
    mask_ref = refs[ppi]
    key_scr = refs[ppi + 1]
    s = pl.program_id(1)
    n_steps = pl.num_programs(1)
    n_pages = n_steps * ppi
    qi = qi_ref[...]
    w = w_ref[...]
    fill = jnp.full((SUBLANES - n_tok, LANES), NEG_INF, F32)

    def scores(d):
        wd = (w * jnp.maximum(d, 0.0)).reshape(n_tok, IDX_HEADS, LANES)
        return jnp.sum(wd, axis=1)

    for r in range(ppi):
        sc = scores(jnp.dot(qi, ki_pages[r][...].astype(BF), preferred_element_type=F32))
        key_scr[s * ppi + r] = _float_key(jnp.concatenate([sc, fill], axis=0))

    @pl.when(s == n_steps - 1)
    def _():
        tok = lax.broadcasted_iota(I32, (n_tok, LANES), 0)
        lane_t = lax.broadcasted_iota(I32, (n_tok, LANES), 1)
        sc_new = scores(lax.dot_general(qi, kin_ref[...], NT, preferred_element_type=F32))
        sc_new = jnp.where(lane_t <= tok, sc_new, NEG_INF)
        key_scr[n_pages] = _float_key(jnp.concatenate([sc_new, fill], axis=0))
        tiles = (n_pages + 1, SUBLANES, LANES)
        idx = lax.broadcasted_iota(I32, tiles, 0) * PAGE_SIZE + lax.broadcasted_iota(I32, tiles, 2)

        def count(hit):
            return jnp.sum(jnp.sum(hit.astype(I32), axis=0), axis=1, keepdims=True)

        thr = _kth_largest_key(lambda t: count(key_scr[...] >= t), k_top, (SUBLANES, 1))
        need = k_top - count(key_scr[...] > thr)
        cutoff = _tie_cutoff(lambda c_: count((key_scr[...] == thr) & (idx < c_)), need, n_bits, (SUBLANES, 1))
        keys = key_scr[...]
        keep = (keys > thr) | ((keys == thr) & (idx <= cutoff))
        mask_ref[...] = jnp.where(keep & (keys != KEY_NEG_INF), 1.0, 0.0).astype(BF)


def _dsa_index(qi_rows, w_rows, ki_new, cache_idx, islot, page_table, n_tok):
    n_seq, n_pages = page_table.shape
    ppi = min(INDEX_PAGES_PER_STEP, n_pages)
    total = n_pages * PAGE_SIZE + n_tok
    k_top = min(DSA_TOPK_MAX, total // 4)
    assert n_pages % ppi == 0 and n_tok <= SUBLANES and n_pages * PAGE_SIZE >= k_top
    n_bits = max(1, ((n_pages + 1) * PAGE_SIZE - 1).bit_length())
    nr = n_tok * IDX_HEADS
    grid_spec = pltpu.PrefetchScalarGridSpec(
        num_scalar_prefetch=1,
        grid=(n_seq, n_pages // ppi),
        in_specs=[pl.BlockSpec((None, nr, IDX_DIM), lambda b, s, pt: (b, 0, 0)),
                  pl.BlockSpec((None, nr, 1), lambda b, s, pt: (b, 0, 0)),
                  pl.BlockSpec((None, LANES, IDX_DIM), lambda b, s, pt: (b, 0, 0))]
        + _page_specs(ppi, islot, IDX_DIM),
        out_specs=pl.BlockSpec((None, n_pages + 1, SUBLANES, LANES), lambda b, s, pt: (b, 0, 0, 0)),
        scratch_shapes=[pltpu.VMEM((n_pages + 1, SUBLANES, LANES), I32)])
    return pl.pallas_call(
        functools.partial(_dsa_index_kernel, n_tok=n_tok, k_top=k_top, ppi=ppi, n_bits=n_bits),
        grid_spec=grid_spec,
        out_shape=jax.ShapeDtypeStruct((n_seq, n_pages + 1, SUBLANES, LANES), BF),
        compiler_params=_params("parallel", "arbitrary"),
        name="dsa_index",
    )(page_table, qi_rows, w_rows, ki_new, *([cache_idx] * ppi))


def _dsa_sample_kernel(pt_ref, q_ref, kn_ref, vn_ref, mask_ref, mnew_ref, *refs, n_tok, pps):
    kt_pages = refs[:pps]
    vt_pages = refs[pps:2 * pps]
    o_ref = refs[2 * pps]
    m_scr, l_scr, acc_scr = refs[2 * pps + 1:]
    s = pl.program_id(1)
    q = q_ref[...]
    n_row = q.shape[0]
    expand = (lax.broadcasted_iota(I32, (n_row, SUBLANES), 1) == _row_token(n_row, n_tok)).astype(BF)

    @pl.when(s == 0)
    def _():
        m_scr[...] = jnp.full(m_scr.shape, NEG_INF, F32)
        l_scr[...] = jnp.zeros_like(l_scr)
        acc_scr[...] = jnp.zeros_like(acc_scr)

    def update(sc, keep, pv):
        sc = jnp.where(keep > 0.5, sc, NEG_INF)
        m_old = m_scr[...]
        m_new = jnp.maximum(m_old, jnp.max(sc, axis=1, keepdims=True))
        m_safe = jnp.where(m_new == NEG_INF, 0.0, m_new)
        alpha = jnp.exp(m_old - m_safe)
        e = jnp.exp(sc - m_safe)
        l_scr[...] = alpha * l_scr[...] + jnp.sum(e, axis=1, keepdims=True)
        acc_scr[...] = alpha * acc_scr[...] + pv(e.astype(BF))
        m_scr[...] = m_new

    kt_all = jnp.concatenate([p[...] for p in kt_pages], axis=1).astype(BF)
    sc_all = jnp.dot(q, kt_all, preferred_element_type=F32)
    keep_all = jnp.dot(expand, jnp.concatenate([mask_ref[r] for r in range(pps)], axis=1),
                       preferred_element_type=F32)
    width = DSA_PAGES_PER_UPDATE * PAGE_SIZE
    for u in range(pps // DSA_PAGES_PER_UPDATE):
        rs = range(u * DSA_PAGES_PER_UPDATE, (u + 1) * DSA_PAGES_PER_UPDATE)
        vt = jnp.concatenate([vt_pages[r][...] for r in rs], axis=1).astype(BF)
        cols = slice(u * width, (u + 1) * width)
        update(sc_all[:, cols], keep_all[:, cols], lambda e, vt=vt: lax.dot_general(e, vt, NT, preferred_element_type=F32))

    @pl.when(s == pl.num_programs(1) - 1)
    def _():
        sc = lax.dot_general(q, kn_ref[...], NT, preferred_element_type=F32)
        keep = jnp.dot(expand, mnew_ref[...], preferred_element_type=F32)[:, :SUBLANES]
        update(sc, keep, lambda e: jnp.dot(e, vn_ref[...], preferred_element_type=F32))
        o_ref[...] = acc_scr[...] * (1.0 / l_scr[...])


def _dsa_sample(qbd, kn_pad, vn_pad, mask, cache_kt, cache_vt, slot, page_table, n_tok):
    n_seq, n_pages = page_table.shape
    n_row = qbd.shape[1]
    pps = PAGES_PER_STEP
    per_seq = lambda w: pl.BlockSpec((None, w[0], w[1]), lambda b, s, pt: (b, 0, 0))
    return _paged_attention_call(
        functools.partial(_dsa_sample_kernel, n_tok=n_tok, pps=pps), "dsa_sample", slot, page_table,
        cache_kt, cache_vt,
        [qbd, kn_pad, vn_pad, mask, mask],
        [per_seq((n_row, KV_DIM)), per_seq((SUBLANES, KV_DIM)), per_seq((SUBLANES, KV_DIM)),
         pl.BlockSpec((None, pps, SUBLANES, LANES), lambda b, s, pt: (b, s, 0, 0)),
         pl.BlockSpec((None, None, SUBLANES, LANES), lambda b, s, pt: (b, n_pages, 0, 0))],
        [pltpu.VMEM((n_row, 1), F32), pltpu.VMEM((n_row, 1), F32), pltpu.VMEM((n_row, KV_DIM), F32)],
        n_row)


def _swa_sample_kernel(q_ref, kt_ref, vt_ref, kn_ref, vn_ref, sink_ref, o_ref, *, n_tok):
    q = q_ref[...]
    n_row = q.shape[0]
    tok_r = _row_token(n_row, n_tok)
    wk = kt_ref.shape[1]
    s_buf = jnp.dot(q, kt_ref[...].astype(BF), preferred_element_type=F32)
    s_buf = jnp.where(lax.broadcasted_iota(I32, (n_row, wk), 1) >= tok_r, s_buf, NEG_INF)
    s_new = lax.dot_general(q, kn_ref[...], NT, preferred_element_type=F32)
    s_new = jnp.where(lax.broadcasted_iota(I32, (n_row, SUBLANES), 1) <= tok_r, s_new, NEG_INF)
    sink = sink_ref[...]
    m = jnp.maximum(jnp.maximum(jnp.max(s_buf, axis=1, keepdims=True), jnp.max(s_new, axis=1, keepdims=True)), sink)
    e_buf = jnp.exp(s_buf - m)
    e_new = jnp.exp(s_new - m)
    den = jnp.sum(e_buf, axis=1, keepdims=True) + jnp.sum(e_new, axis=1, keepdims=True) + jnp.exp(sink - m)
    num = (lax.dot_general(e_buf.astype(BF), vt_ref[...].astype(BF), NT, preferred_element_type=F32)
           + jnp.dot(e_new.astype(BF), vn_ref[...], preferred_element_type=F32))
    o_ref[...] = num * (1.0 / den)


def _swa_sample(qbd, kn_pad, vn_pad, buf_kt, buf_vt, sinks, n_tok):
    n_seq, _, wk = buf_kt.shape
    assert wk == WINDOW
    n_row = qbd.shape[1]
    sink_rows = jnp.repeat(sinks.reshape(N_KV_HEADS, 1, GROUP), n_tok, axis=1).reshape(n_row, 1)
    per_seq = lambda a, c: pl.BlockSpec((None, a, c), lambda b: (b, 0, 0))
    return pl.pallas_call(
        functools.partial(_swa_sample_kernel, n_tok=n_tok),
        grid=(n_seq,),
        in_specs=[per_seq(n_row, KV_DIM), per_seq(KV_DIM, wk), per_seq(KV_DIM, wk),
                  per_seq(SUBLANES, KV_DIM), per_seq(SUBLANES, KV_DIM),
                  pl.BlockSpec((n_row, 1), lambda b: (0, 0))],
        out_specs=per_seq(n_row, KV_DIM),
        out_shape=jax.ShapeDtypeStruct((n_seq, n_row, KV_DIM), F32),
        compiler_params=_params("parallel"),
        name="swa_sample",
    )(qbd, buf_kt, buf_vt, kn_pad, vn_pad, sink_rows)


def _rope_tables(pos):
    half = HEAD_DIM // 2
    inv = ROPE_THETA ** (-jnp.arange(half, dtype=F32) / half)
    ang = pos.astype(F32)[:, None] * inv[None, :]
    cos, sin = jnp.cos(ang), jnp.sin(ang)
    reps = LANES // HEAD_DIM
    return jnp.tile(jnp.concatenate([cos, cos], axis=1), (1, reps)), jnp.tile(jnp.concatenate([-sin, sin], axis=1), (1, reps))


def kernel(x_prompt, x_sample, cache_k, cache_v, cache_idx_k, state_swa_k, state_swa_v, page_table, c_prompt, c_sample, g_attn, g_mlp, w_mod, b_mod, w_in_moba, w_in_dsa, w_in_swa, swa_sinks, w_out, w_ff1, w_ff2, g_final):
    batch, seq, _ = x_prompt.shape
    n_seq, n_tok, _ = x_sample.shape
    depth = g_attn.shape[0]
    n_pages = page_table.shape[1]
    past = n_pages * PAGE_SIZE
    rows_p, rows_s = batch * seq, n_seq * n_tok

    n_cond = batch + n_seq
    pad_cond = -n_cond % SUBLANES
    c_all = jnp.pad(jnp.concatenate([c_prompt, c_sample], axis=0), ((0, pad_cond), (0, 0)))
    mod = _adaln(c_all, w_mod, b_mod)

    def mods(layer):
        parts = [mod[layer, :, r * D_MODEL:(r + 1) * D_MODEL] for r in range(6)]
        mp = [p[:batch].reshape(batch, 1, D_MODEL) for p in parts]
        ms = [jnp.repeat(p[batch:n_cond], n_tok, axis=0) for p in parts]
        return mp, ms

    cos_p, sin_p = _rope_tables(jnp.arange(seq, dtype=jnp.int32))
    cos_s, sin_s = _rope_tables(jnp.tile(past + jnp.arange(n_tok, dtype=jnp.int32), n_seq))
    cos_pt, sin_pt = cos_p[:, :HEAD_DIM].T, sin_p[:, :HEAD_DIM].T

    cache_kt, cache_vt, cache_it = _tokens_on_lanes(cache_k), _tokens_on_lanes(cache_v), _tokens_on_lanes(cache_idx_k)
    swa_kt, swa_vt = _tokens_on_lanes(state_swa_k), _tokens_on_lanes(state_swa_v)

    xp = x_prompt.reshape(rows_p, D_MODEL)
    xs = x_sample.reshape(rows_s, D_MODEL)
    kp_l, vp_l, ks_l, vs_l, ip_l, is_l = [], [], [], [], [], []
    skp_l, svp_l, sks_l, svs_l = [], [], [], []
    for i, (mixer, j, slot) in enumerate(_layer_plan(depth)):
        (sh_ap, sc_ap, gt_ap, sh_mp, sc_mp, gt_mp), (sh_as, sc_as, gt_as, sh_ms, sc_ms, gt_ms) = mods(i)
        dsa = mixer == 1
        if mixer == 0:
            w_in = w_in_moba[j]
        elif dsa:
            w_in = jnp.pad(w_in_dsa[j], ((0, 0), (0, -IN_DSA % LANES)))
        else:
            w_in = w_in_swa[j]
        w_in = w_in.astype(BF)
        g_a = g_attn[i].reshape(1, D_MODEL)
        outs_p = _project_t(xp, g_a, sc_ap, sh_ap, cos_p, sin_p, cos_pt, sin_pt, w_in, seq, dsa)
        qt_p, k_p, v_p, kz_p, vt_p = outs_p[:5]
        outs_s = _project(xs, g_a, sc_as, sh_as, cos_s, sin_s, w_in, n_tok, dsa)
        q_s, k_s, v_s, kb_s, vb_s = outs_s[:5]
        qbd = _block_diag_queries(q_s, n_seq, n_tok)
        kn_pad = _pad_rows(kb_s, n_seq, n_tok)
        vn_pad = _pad_rows(vb_s, n_seq, n_tok)
        k_p5 = k_p.reshape(batch, seq, N_KV_HEADS, HEAD_DIM)
        v_p5 = v_p.reshape(batch, seq, N_KV_HEADS, HEAD_DIM)
        k_s5 = k_s.reshape(n_seq, n_tok, N_KV_HEADS, HEAD_DIM)
        v_s5 = v_s.reshape(n_seq, n_tok, N_KV_HEADS, HEAD_DIM)
        if mixer == 0:
            o_p = _moba_prompt(qt_p, kz_p, vt_p, k_p, batch)
            o_t = _moba_sample(qbd, kn_pad, vn_pad, cache_kt, cache_vt, slot, page_table, n_tok)
        elif dsa:
            qit_p, wt_p, rest_p, kiz_p = outs_p[5:]
            qi_s, rest_s, restb_s = outs_s[5:]
            o_p = _dsa_prompt(qt_p, qit_p, wt_p, kiz_p, kz_p, vt_p, batch)
            qi_rows = qi_s.reshape(n_seq, n_tok * IDX_HEADS, IDX_DIM)
            w_rows = rest_s[:, IDX_DIM:IDX_DIM + IDX_HEADS].reshape(n_seq, n_tok * IDX_HEADS, 1)
            ki_new = jnp.pad(restb_s[:, :IDX_DIM].reshape(n_seq, n_tok, IDX_DIM),
                             ((0, 0), (0, LANES - n_tok), (0, 0)))
            mask = _dsa_index(qi_rows, w_rows, ki_new, cache_it, j, page_table, n_tok)
            o_t = _dsa_sample(qbd, kn_pad, vn_pad, mask, cache_kt, cache_vt, slot, page_table, n_tok)
            ip_l.append(rest_p[:, :IDX_DIM].reshape(batch, seq, IDX_DIM))
            is_l.append(rest_s[:, :IDX_DIM].reshape(n_seq, n_tok, IDX_DIM))
        else:
            o_p = _swa_prompt(qt_p, kz_p, vt_p, swa_sinks[j], batch)
            o_t = _swa_sample(qbd, kn_pad, vn_pad, swa_kt[j], swa_vt[j], swa_sinks[j], n_tok)
            keep_p = min(WINDOW, seq)
            skp_l.append(k_p5[:, seq - keep_p:])
            svp_l.append(v_p5[:, seq - keep_p:])
            wk = state_swa_k.shape[2]
            sks_l.append(jnp.concatenate([state_swa_k[j], k_s5], axis=1)[:, -wk:])
            svs_l.append(jnp.concatenate([state_swa_v[j], v_s5], axis=1)[:, -wk:])
        if mixer < 2:
            kp_l.append(k_p5)
            vp_l.append(v_p5)
            ks_l.append(k_s5)
            vs_l.append(v_s5)
        o_s = _undiag_outputs(o_t, n_seq, n_tok).astype(BF)
        wo, w1, w2 = w_out[i].astype(BF), w_ff1[i].astype(BF), w_ff2[i].astype(BF)
        g_m = g_mlp[i].reshape(1, D_MODEL)
        xp = _post(xp, o_p, wo, gt_ap, g_m, sc_mp, sh_mp, gt_mp, w1, w2, seq)
        xs = _post(xs, o_s, wo, gt_as, g_m, sc_ms, sh_ms, gt_ms, w1, w2, n_tok)
    g_f = g_final.reshape(1, D_MODEL)
    y_prompt = _final_norm(xp, g_f).reshape(batch, seq, D_MODEL)
    y_sample = _final_norm(xs, g_f).reshape(n_seq, n_tok, D_MODEL)
    return (y_prompt, y_sample, jnp.stack(kp_l), jnp.stack(vp_l), jnp.stack(ks_l), jnp.stack(vs_l),
            jnp.stack(ip_l), jnp.stack(is_l), jnp.stack(skp_l), jnp.stack(svp_l), jnp.stack(sks_l), jnp.stack(svs_l))
```

```python
import functools

import numpy as np
import jax
import jax.numpy as jnp
from jax import lax
from jax.experimental import pallas as pl
from jax.experimental.pallas import tpu as pltpu

D_MODEL = 1024
N_HEADS = 16
HEAD_DIM = D_MODEL // N_HEADS
N_KV_HEADS = 4
GROUP = N_HEADS // N_KV_HEADS
Q_DIM = N_HEADS * HEAD_DIM
KV_DIM = N_KV_HEADS * HEAD_DIM
D_FF = 4 * D_MODEL
ROPE_THETA = 10000.0
NORM_EPS = 1e-6
N_MIXERS = 3
PAGE_SIZE = 128
MOBA_BLOCK = 256
MOBA_TOPK = 3
IDX_HEADS = 8
IDX_DIM = 64
DSA_TOPK_MAX = 256
WINDOW = 128
IN_ATTN = Q_DIM + 2 * KV_DIM
IDX_Q_DIM = IDX_HEADS * IDX_DIM
IN_DSA = IN_ATTN + IDX_Q_DIM + IDX_DIM + IDX_HEADS
ATTN_SCALE = HEAD_DIM ** -0.5
IDX_W_SCALE = IDX_Q_DIM ** -0.5

LANES = 128
SUBLANES = 8
VMEM_LIMIT = 56 << 20
ROW_TILE = 512
FF_TILE = 1024
DSA_KEY_CHUNK = 512
DSA_QUERY_TILE = 512
PAGES_PER_STEP = 16
INDEX_PAGES_PER_STEP = 64
DSA_PAGES_PER_UPDATE = 4

BF = jnp.bfloat16
F32 = jnp.float32
I32 = jnp.int32
NEG_INF = float("-inf")
INT_MIN = -2 ** 31
KEY_NEG_INF = INT_MIN + 0x7FFFFF
NT = (((1,), (1,)), ((), ()))


def _params(*semantics):
    return pltpu.CompilerParams(dimension_semantics=semantics, vmem_limit_bytes=VMEM_LIMIT)


def _layer_plan(depth):
    plan, counts, n_paged = [], [0] * N_MIXERS, 0
    for i in range(depth):
        m = i % N_MIXERS
        slot = -1
        if m < 2:
            slot = n_paged
            n_paged += 1
        plan.append((m, counts[m], slot))
        counts[m] += 1
    return plan


def _adaln_kernel(c_ref, w_ref, b_ref, o_ref):
    c = c_ref[...]
    a = (c * (1.0 / (1.0 + jnp.exp(-c)))).astype(BF)
    o_ref[...] = jnp.dot(a, w_ref[...].astype(BF), preferred_element_type=F32) + b_ref[...]


def _adaln(c_all, w_mod, b_mod):
    depth, _, n_out = w_mod.shape
    nc = c_all.shape[0]
    tn = 1536
    return pl.pallas_call(
        _adaln_kernel,
        grid=(depth, n_out // tn),
        in_specs=[pl.BlockSpec((nc, D_MODEL), lambda l, j: (0, 0)),
                  pl.BlockSpec((None, D_MODEL, tn), lambda l, j: (l, 0, j)),
                  pl.BlockSpec((None, 1, tn), lambda l, j: (l, 0, j))],
        out_specs=pl.BlockSpec((None, nc, tn), lambda l, j: (l, 0, j)),
        out_shape=jax.ShapeDtypeStruct((depth, nc, n_out), F32),
        compiler_params=_params("parallel", "parallel"),
        name="adaln",
    )(c_all, w_mod, b_mod.reshape(depth, 1, n_out))


def _norm_mod(x, g, scale, shift):
    y = x * lax.rsqrt(jnp.mean(x * x, axis=-1, keepdims=True) + NORM_EPS)
    return (y * g) * (1.0 + scale) + shift


def _rope_chunk(z, cos, sin_signed, lo_half):
    partner = jnp.where(lo_half, pltpu.roll(z, LANES - HEAD_DIM // 2, 1), pltpu.roll(z, HEAD_DIM // 2, 1))
    return z * cos + partner * sin_signed


def _proj_kernel(x_ref, g_ref, sc_ref, sh_ref, cos_ref, sin_ref, w_ref, *out_refs, dsa):
    hb = _norm_mod(x_ref[...], g_ref[...], sc_ref[...], sh_ref[...]).astype(BF)
    cos = cos_ref[...]
    sin_s = sin_ref[...]
    lane = lax.broadcasted_iota(I32, (1, LANES), 1)
    lo_half = (lane % HEAD_DIM) < (HEAD_DIM // 2)
    q_ref, k_ref, v_ref, kb_ref, vb_ref = out_refs[:5]

    zq = jnp.dot(hb, w_ref[:, :Q_DIM], preferred_element_type=F32)
    for c in range(Q_DIM // LANES):
        sl = slice(c * LANES, (c + 1) * LANES)
        q_ref[:, sl] = (_rope_chunk(zq[:, sl], cos, sin_s, lo_half) * ATTN_SCALE).astype(BF)
    zk = jnp.dot(hb, w_ref[:, Q_DIM:Q_DIM + KV_DIM], preferred_element_type=F32)
    for c in range(KV_DIM // LANES):
        sl = slice(c * LANES, (c + 1) * LANES)
        kr = _rope_chunk(zk[:, sl], cos, sin_s, lo_half)
        k_ref[:, sl] = kr
        kb_ref[:, sl] = kr.astype(BF)
    zv = jnp.dot(hb, w_ref[:, Q_DIM + KV_DIM:IN_ATTN], preferred_element_type=F32)
    v_ref[...] = zv
    vb_ref[...] = zv.astype(BF)
    if dsa:
        qi_ref, rest_ref, restb_ref = out_refs[5:]
        zi = jnp.dot(hb, w_ref[:, IN_ATTN:IN_ATTN + IDX_Q_DIM], preferred_element_type=F32)
        for c in range(IDX_Q_DIM // LANES):
            sl = slice(c * LANES, (c + 1) * LANES)
            qi_ref[:, sl] = _rope_chunk(zi[:, sl], cos, sin_s, lo_half).astype(BF)
        zr = jnp.dot(hb, w_ref[:, IN_ATTN + IDX_Q_DIM:], preferred_element_type=F32)
        roped = _rope_chunk(zr, cos, sin_s, lo_half)
        rest = jnp.where(lane < IDX_DIM, roped, zr * IDX_W_SCALE)
        rest_ref[...] = rest
        restb_ref[...] = rest.astype(BF)


def _mod_spec(arr, tm, rows_per_seq):
    if arr.ndim == 3:
        return pl.BlockSpec((None, 1, D_MODEL), lambda i, *_: ((i * tm) // rows_per_seq, 0, 0))
    return pl.BlockSpec((tm, D_MODEL), lambda i, *_: (i, 0))


def _project(x, g, scale, shift, cos_tab, sin_tab, w, rows_per_seq, dsa):
    rows = x.shape[0]
    tm = min(ROW_TILE, rows)
    n_in = w.shape[1]
    tab_tiles = cos_tab.shape[0] // tm
    row_spec = lambda n: pl.BlockSpec((tm, n), lambda i: (i, 0))
    tab_spec = pl.BlockSpec((tm, LANES), lambda i: (i % tab_tiles, 0))
    out_shape = [jax.ShapeDtypeStruct((rows, Q_DIM), BF),
                 jax.ShapeDtypeStruct((rows, KV_DIM), F32), jax.ShapeDtypeStruct((rows, KV_DIM), F32),
                 jax.ShapeDtypeStruct((rows, KV_DIM), BF), jax.ShapeDtypeStruct((rows, KV_DIM), BF)]
    out_specs = [row_spec(Q_DIM), row_spec(KV_DIM), row_spec(KV_DIM), row_spec(KV_DIM), row_spec(KV_DIM)]
    if dsa:
        out_shape += [jax.ShapeDtypeStruct((rows, IDX_Q_DIM), BF),
                      jax.ShapeDtypeStruct((rows, LANES), F32), jax.ShapeDtypeStruct((rows, LANES), BF)]
        out_specs += [row_spec(IDX_Q_DIM), row_spec(LANES), row_spec(LANES)]
    return pl.pallas_call(
        functools.partial(_proj_kernel, dsa=dsa),
        grid=(rows // tm,),
        in_specs=[row_spec(D_MODEL),
                  pl.BlockSpec((1, D_MODEL), lambda i: (0, 0)),
                  _mod_spec(scale, tm, rows_per_seq), _mod_spec(shift, tm, rows_per_seq),
                  tab_spec, tab_spec,
                  pl.BlockSpec((D_MODEL, n_in), lambda i: (0, 0))],
        out_specs=out_specs,
        out_shape=out_shape,
        compiler_params=_params("parallel"),
        name="project_dsa" if dsa else "project",
    )(x, g, scale, shift, cos_tab, sin_tab, w)


def _rope_rows(zt, cos_t, sin_t, n_heads, scale, dtype):
    half = HEAD_DIM // 2
    out = []
    for h in range(n_heads):
        x = zt[h * HEAD_DIM:(h + 1) * HEAD_DIM]
        partner = jnp.concatenate([x[half:], x[:half]], axis=0)
        out.append(((x * cos_t + partner * sin_t) * scale).astype(dtype))
    return out


def _proj_t_kernel(x_ref, g_ref, sc_ref, sh_ref, cos_ref, sin_ref, cost_ref, sint_ref, w_ref, wqt_ref, wkt_ref, wvt_ref,
                   *refs, dsa):
    if dsa:
        wit_ref, wrt_ref = refs[:2]
        refs = refs[2:]
    qt_ref, kt_ref, vtf_ref, kz_ref, vt_ref, kmean_ref = refs[:6]
    hb = _norm_mod(x_ref[...], g_ref[...], sc_ref[...], sh_ref[...]).astype(BF)
    lane = lax.broadcasted_iota(I32, (1, LANES), 1)
    lo_half = (lane % HEAD_DIM) < (HEAD_DIM // 2)

    zqt = lax.dot_general(wqt_ref[...], hb, NT, preferred_element_type=F32)
    cos_t, sin_t = cost_ref[...], sint_ref[...]
    for h, qh in enumerate(_rope_rows(zqt, cos_t, sin_t, N_HEADS, ATTN_SCALE, BF)):
        qt_ref[h * HEAD_DIM:(h + 1) * HEAD_DIM, :] = qh
    zkt = lax.dot_general(wkt_ref[...], hb, NT, preferred_element_type=F32)
    for h, kh in enumerate(_rope_rows(zkt, cos_t, sin_t, N_KV_HEADS, 1.0, F32)):
        kt_ref[h * HEAD_DIM:(h + 1) * HEAD_DIM, :] = kh
    zvt = lax.dot_general(wvt_ref[...], hb, NT, preferred_element_type=F32)
    vtf_ref[...] = zvt
    vt_ref[...] = zvt.astype(BF)
    if dsa:
        qit_ref, wt_ref, kit_ref, kiz_ref = refs[6:]
        zit = lax.dot_general(wit_ref[...], hb, NT, preferred_element_type=F32)
        for h, qh in enumerate(_rope_rows(zit, cos_t, sin_t, IDX_HEADS, 1.0, BF)):
            qit_ref[h * IDX_DIM:(h + 1) * IDX_DIM, :] = qh
        zr = jnp.dot(hb, w_ref[:, IN_ATTN + IDX_Q_DIM:], preferred_element_type=F32)
        kiz_ref[...] = jnp.where(lane < IDX_DIM, _rope_chunk(zr, cos_ref[...], sin_ref[...], lo_half), 0.0).astype(BF)
        zrt = lax.dot_general(wrt_ref[...], hb, NT, preferred_element_type=F32)
        kit_ref[...] = _rope_rows(zrt, cos_t, sin_t, 1, 1.0, F32)[0]
        wt_ref[...] = zrt[IDX_DIM:IDX_DIM + IDX_HEADS] * IDX_W_SCALE
    zk = jnp.dot(hb, w_ref[:, Q_DIM:Q_DIM + KV_DIM], preferred_element_type=F32)
    tm = zk.shape[0]
    for c in range(KV_DIM // LANES):
        sl = slice(c * LANES, (c + 1) * LANES)
        kr = _rope_chunk(zk[:, sl], cos_ref[...], sin_ref[...], lo_half)
        kmean_ref[:, :, sl] = jnp.mean(kr.reshape(tm // MOBA_BLOCK, MOBA_BLOCK, LANES), axis=1, keepdims=True)
        kz_ref[:, 2 * c * LANES:(2 * c + 1) * LANES] = jnp.where(lane < HEAD_DIM, kr, 0.0).astype(BF)
        kz_ref[:, (2 * c + 1) * LANES:(2 * c + 2) * LANES] = jnp.where(
            lane < HEAD_DIM, pltpu.roll(kr, HEAD_DIM, 1), 0.0).astype(BF)


def _project_t(x, g, scale, shift, cos_tab, sin_tab, cos_t, sin_t, w, rows_per_seq, dsa):
    rows = x.shape[0]
    tm = min(ROW_TILE, rows)
    assert rows_per_seq % tm == 0 and tm % MOBA_BLOCK == 0
    tab_tiles = cos_tab.shape[0] // tm
    tiles_per_seq = rows_per_seq // tm
    row_spec = lambda n: pl.BlockSpec((tm, n), lambda i: (i, 0))
    col_spec = lambda n: pl.BlockSpec((n, tm), lambda i: (0, i))
    seq_spec = lambda n: pl.BlockSpec((None, n, tm), lambda i: (i // tiles_per_seq, 0, i % tiles_per_seq))
    seq_shape = lambda n: jax.ShapeDtypeStruct((rows // rows_per_seq, n, rows_per_seq), F32)
    tab_spec = pl.BlockSpec((tm, LANES), lambda i: (i % tab_tiles, 0))
    tabt_spec = pl.BlockSpec((HEAD_DIM, tm), lambda i: (0, i % tab_tiles))
    whole = lambda a: pl.BlockSpec(a.shape, lambda i: (0, 0))
    weights = [w, w[:, :Q_DIM].T, w[:, Q_DIM:Q_DIM + KV_DIM].T, w[:, Q_DIM + KV_DIM:IN_ATTN].T]
    out_specs = [col_spec(Q_DIM), seq_spec(KV_DIM), seq_spec(KV_DIM), row_spec(2 * KV_DIM), col_spec(KV_DIM),
                 pl.BlockSpec((tm // MOBA_BLOCK, 1, KV_DIM), lambda i: (i, 0, 0))]
    out_shape = [jax.ShapeDtypeStruct((Q_DIM, rows), BF), seq_shape(KV_DIM), seq_shape(KV_DIM),
                 jax.ShapeDtypeStruct((rows, 2 * KV_DIM), BF), jax.ShapeDtypeStruct((KV_DIM, rows), BF),
                 jax.ShapeDtypeStruct((rows // MOBA_BLOCK, 1, KV_DIM), F32)]
    if dsa:
        weights += [w[:, IN_ATTN:IN_ATTN + IDX_Q_DIM].T, w[:, IN_ATTN + IDX_Q_DIM:].T]
        out_specs[4] = pl.BlockSpec((None, KV_DIM, tm), lambda i: (i, 0, 0))
        out_shape[4] = jax.ShapeDtypeStruct((rows // tm, KV_DIM, tm), BF)
        out_specs += [col_spec(IDX_Q_DIM), col_spec(IDX_HEADS), seq_spec(IDX_DIM), row_spec(LANES)]
        out_shape += [jax.ShapeDtypeStruct((IDX_Q_DIM, rows), BF), jax.ShapeDtypeStruct((IDX_HEADS, rows), F32),
                      seq_shape(IDX_DIM), jax.ShapeDtypeStruct((rows, LANES), BF)]
    return pl.pallas_call(
        functools.partial(_proj_t_kernel, dsa=dsa),
        grid=(rows // tm,),
        in_specs=[row_spec(D_MODEL),
                  pl.BlockSpec((1, D_MODEL), lambda i: (0, 0)),
                  _mod_spec(scale, tm, rows_per_seq), _mod_spec(shift, tm, rows_per_seq),
                  tab_spec, tab_spec, tabt_spec, tabt_spec] + [whole(a) for a in weights],
        out_specs=out_specs,
        out_shape=out_shape,
        compiler_params=_params("parallel"),
        name="project_t_dsa" if dsa else "project_t",
    )(x, g, scale, shift, cos_tab, sin_tab, cos_t, sin_t, *weights)


def _post_kernel(x_ref, o_ref, wo_ref, ga_ref, g_ref, sc_ref, sh_ref, gm_ref, w1_ref, w2_ref, y_ref,
                 x1_scr, h_scr, acc_scr):
    j = pl.program_id(1)

    @pl.when(j == 0)
    def _():
        x1 = x_ref[...] + ga_ref[...] * jnp.dot(o_ref[...], wo_ref[...], preferred_element_type=F32)
        x1_scr[...] = x1
        h_scr[...] = _norm_mod(x1, g_ref[...], sc_ref[...], sh_ref[...]).astype(BF)
        acc_scr[...] = jnp.zeros_like(acc_scr)

    a = jnp.maximum(jnp.dot(h_scr[...], w1_ref[...], preferred_element_type=F32), 0.0)
    acc_scr[...] += jnp.dot((a * a).astype(BF), w2_ref[...], preferred_element_type=F32)

    @pl.when(j == pl.num_programs(1) - 1)
    def _():
        y_ref[...] = x1_scr[...] + gm_ref[...] * acc_scr[...]


def _post(x, o, w_out, gate_a, g_mlp, scale_m, shift_m, gate_m, w1, w2, rows_per_seq):
    rows = x.shape[0]
    tm = min(ROW_TILE, rows)
    tf = FF_TILE
    ms = lambda a: _mod_spec(a, tm, rows_per_seq)
    return pl.pallas_call(
        _post_kernel,
        grid=(rows // tm, D_FF // tf),
        in_specs=[pl.BlockSpec((tm, D_MODEL), lambda i, j: (i, 0)),
                  pl.BlockSpec((tm, Q_DIM), lambda i, j: (i, 0)),
                  pl.BlockSpec((Q_DIM, D_MODEL), lambda i, j: (0, 0)),
                  ms(gate_a),
                  pl.BlockSpec((1, D_MODEL), lambda i, j: (0, 0)),
                  ms(scale_m), ms(shift_m), ms(gate_m),
                  pl.BlockSpec((D_MODEL, tf), lambda i, j: (0, j)),
                  pl.BlockSpec((tf, D_MODEL), lambda i, j: (j, 0))],
        out_specs=pl.BlockSpec((tm, D_MODEL), lambda i, j: (i, 0)),
        out_shape=jax.ShapeDtypeStruct((rows, D_MODEL), F32),
        scratch_shapes=[pltpu.VMEM((tm, D_MODEL), F32), pltpu.VMEM((tm, D_MODEL), BF),
                        pltpu.VMEM((tm, D_MODEL), F32)],
        compiler_params=_params("parallel", "arbitrary"),
        name="post",
    )(x, o, w_out, gate_a, g_mlp, scale_m, shift_m, gate_m, w1, w2)


def _final_norm_kernel(x_ref, g_ref, y_ref):
    x = x_ref[...]
    y_ref[...] = x * lax.rsqrt(jnp.mean(x * x, axis=-1, keepdims=True) + NORM_EPS) * g_ref[...]


def _final_norm(x, g):
    rows = x.shape[0]
    tm = min(ROW_TILE, rows)
    return pl.pallas_call(
        _final_norm_kernel,
        grid=(rows // tm,),
        in_specs=[pl.BlockSpec((tm, D_MODEL), lambda i: (i, 0)), pl.BlockSpec((1, D_MODEL), lambda i: (0, 0))],
        out_specs=pl.BlockSpec((tm, D_MODEL), lambda i: (i, 0)),
        out_shape=jax.ShapeDtypeStruct((rows, D_MODEL), F32),
        compiler_params=_params("parallel"),
        name="final_norm",
    )(x, g)


def _top_mask(gs, ids, n_sel, axis):
    picked = None
    for _ in range(n_sel):
        mx = jnp.max(gs, axis=axis, keepdims=True)
        am = jnp.min(jnp.where(gs == mx, ids, 2 ** 30), axis=axis, keepdims=True)
        hit = (ids == am) & (mx > NEG_INF)
        picked = hit if picked is None else (picked | hit)
        gs = jnp.where(ids == am, NEG_INF, gs)
    return picked


MOBA_ACC_ROWS = HEAD_DIM + 16
MOBA_MASKED = -1e30

def _moba_prompt_kernel(qi_tab, kp_tab, last_tab, qt_ref, kz_own_ref, vt_own_ref, kz_pair_ref, vt_pair_ref, kmz_ref,
                        o_ref, qz_scr, m_scr, acc_scr, *, n_sel):
    p = pl.program_id(1)
    i = qi_tab[p]
    kp = kp_tab[p]
    blk = MOBA_BLOCK

    @pl.when(kp < 0)
    def _():
        rows = lax.broadcasted_iota(I32, (LANES, blk), 0)
        earlier = (rows >= HEAD_DIM) & (rows < HEAD_DIM + i)
        for h in range(N_HEADS):
            qt = qt_ref[h * HEAD_DIM:(h + 1) * HEAD_DIM, :]
            gs = jnp.dot(kmz_ref[h // GROUP], qt.astype(F32), precision=lax.Precision.HIGHEST,
                         preferred_element_type=F32)
            picked = _top_mask(jnp.where(earlier, gs, NEG_INF), rows, n_sel, 0)
            usable = (earlier & picked) | (rows == HEAD_DIM + i)
            bias = jnp.where(usable, 0.0, MOBA_MASKED)
            qz_scr[h // GROUP, :, (h % GROUP) * blk:(h % GROUP + 1) * blk] = jnp.concatenate(
                [qt, bias[HEAD_DIM:].astype(BF)], axis=0)
        m_scr[...] = jnp.full(m_scr.shape, NEG_INF, F32)
        acc_scr[...] = jnp.zeros_like(acc_scr)

    def step(kz_ref, vt_ref, n_keys, block_of_key, causal):
        lane = lax.broadcasted_iota(I32, (n_keys, LANES), 1)
        ones_rows = (lax.broadcasted_iota(I32, (MOBA_ACC_ROWS - HEAD_DIM, n_keys), 0) == 0).astype(BF)
        sts = []
        for kv in range(N_KV_HEADS):
            kz = jnp.where(lane == HEAD_DIM + block_of_key, 1.0, kz_ref[:, kv * LANES:(kv + 1) * LANES]).astype(BF)
            sts.append(jnp.dot(kz, qz_scr[kv], preferred_element_type=F32))
        es, alphas = [], []
        for kv in range(N_KV_HEADS):
            st = sts[kv]
            if causal:
                key_i = lax.broadcasted_iota(I32, (n_keys, GROUP * blk), 0)
                qry_i = lax.broadcasted_iota(I32, (n_keys, GROUP * blk), 1) % blk
                st = jnp.where(key_i <= qry_i, st, NEG_INF)
            m_old = m_scr[kv]
            m_new = jnp.maximum(m_old, jnp.max(st, axis=0, keepdims=True))
            alphas.append(jnp.exp(m_old - m_new))
            es.append(jnp.exp(st - m_new).astype(BF))
            m_scr[kv] = m_new
        for kv in range(N_KV_HEADS):
            vaug = jnp.concatenate([vt_ref[kv * HEAD_DIM:(kv + 1) * HEAD_DIM, :], ones_rows], axis=0)
            acc_scr[kv] = alphas[kv] * acc_scr[kv] + jnp.dot(vaug, es[kv], preferred_element_type=F32)

    @pl.when(kp < 0)
    def _():
        step(kz_own_ref, vt_own_ref, blk, i, True)

    @pl.when(kp >= 0)
    def _():
        first = 2 * kp
        second = jnp.where(first + 1 < i, first + 1, i + 1)
        key_row = lax.broadcasted_iota(I32, (2 * blk, 1), 0)
        step(kz_pair_ref, vt_pair_ref, 2 * blk, jnp.where(key_row < blk, first, second), False)

    @pl.when(last_tab[p] == 1)
    def _():
        for c in range(N_HEADS // 2):
            halves = []
            for h in (2 * c, 2 * c + 1):
                a = acc_scr[h // GROUP, :, (h % GROUP) * blk:(h % GROUP + 1) * blk]
                halves.append(a[:HEAD_DIM] * (1.0 / a[HEAD_DIM:HEAD_DIM + 1]))
            o_ref[:, c * LANES:(c + 1) * LANES] = jnp.concatenate(halves, axis=0).T.astype(BF)


def _moba_prompt(qt, kz, vt, kmeans, batch):
    rows = qt.shape[1]
    seq = rows // batch
    blk = MOBA_BLOCK
    assert seq % blk == 0
    nblk = seq // blk
    assert HEAD_DIM + nblk + 1 <= LANES and nblk % 2 == 0
    n_sel = min(MOBA_TOPK, nblk)
    kmean = kmeans.reshape(batch, nblk, N_KV_HEADS, HEAD_DIM).transpose(0, 2, 1, 3)
    kmz = jnp.pad(kmean, ((0, 0), (0, 0), (HEAD_DIM, LANES - HEAD_DIM - nblk), (0, 0)))
    qi, kp, last = [], [], []
    for i in range(nblk):
        order = [-1] + list(range((i + 1) // 2))
        qi += [i] * len(order)
        kp += order
        last += [0] * (len(order) - 1) + [1]
    tabs = [jnp.asarray(np.array(t, np.int32)) for t in (qi, kp, last)]
    npair = nblk // 2
    own_rows = lambda b, p, qi, kp, la: (b * nblk + qi[p], 0)
    own_cols = lambda b, p, qi, kp, la: (0, b * nblk + qi[p])
    grid_spec = pltpu.PrefetchScalarGridSpec(
        num_scalar_prefetch=3,
        grid=(batch, len(qi)),
        in_specs=[pl.BlockSpec((Q_DIM, blk), own_cols),
                  pl.BlockSpec((blk, 2 * KV_DIM), own_rows),
                  pl.BlockSpec((KV_DIM, blk), own_cols),
                  pl.BlockSpec((2 * blk, 2 * KV_DIM), lambda b, p, qi, kp, la: (b * npair + jnp.maximum(kp[p], 0), 0)),
                  pl.BlockSpec((KV_DIM, 2 * blk), lambda b, p, qi, kp, la: (0, b * npair + jnp.maximum(kp[p], 0))),
                  pl.BlockSpec((None, N_KV_HEADS, LANES, HEAD_DIM), lambda b, p, qi, kp, la: (b, 0, 0, 0))],
        out_specs=pl.BlockSpec((blk, Q_DIM), own_rows),
        scratch_shapes=[pltpu.VMEM((N_KV_HEADS, LANES, GROUP * blk), BF), pltpu.VMEM((N_KV_HEADS, 1, GROUP * blk), F32),
                        pltpu.VMEM((N_KV_HEADS, MOBA_ACC_ROWS, GROUP * blk), F32)])
    return pl.pallas_call(
        functools.partial(_moba_prompt_kernel, n_sel=n_sel),
        grid_spec=grid_spec,
        out_shape=jax.ShapeDtypeStruct((rows, Q_DIM), BF),
        compiler_params=_params("parallel", "arbitrary"),
        name="moba_prompt",
    )(*tabs, qt, kz, vt, kz, vt, kmz)


def _float_key(x):
    b = pltpu.bitcast(x, I32)
    return jnp.where(b < 0, b ^ 0x7FFFFFFF, b)


def _kth_largest_key(count_ge, k_top, shape, bits_per_step=1):
    assert 32 % bits_per_step == 0
    def body(it, cur):
        shift = 32 - bits_per_step * (it + 1)
        best = cur
        for v in range(1, 2 ** bits_per_step):
            cand = cur | jnp.left_shift(jnp.int32(v), shift)
            best = jnp.where(count_ge(cand ^ INT_MIN) >= k_top, cand, best)
        return best
    return lax.fori_loop(0, 32 // bits_per_step, body, jnp.zeros(shape, I32)) ^ INT_MIN


def _tie_cutoff(count_eq_below, need, n_bits, shape):
    def body(it, cur):
        cand = cur | jnp.left_shift(jnp.int32(1), n_bits - 1 - it)
        return jnp.where(count_eq_below(cand) < need, cand, cur)
    return lax.fori_loop(0, n_bits, body, jnp.zeros(shape, I32))


def _swap_halves(x):
    return jnp.concatenate([x[HEAD_DIM:], x[:HEAD_DIM]], axis=0)


def _dsa_prompt_kernel(qt_ref, qit_ref, wt_ref, kiz_ref, kz_ref, vt_ref, o_ref,
                       qp_scr, qip_scr, key_scr, m_scr, acc_scr, *, k_top, n_bits):
    i = pl.program_id(1)
    tq, ch = DSA_QUERY_TILE, DSA_KEY_CHUNK
    n_ch = ((i + 1) * tq + ch - 1) // ch
    qpos = i * tq + lax.broadcasted_iota(I32, (1, tq), 1)
    key_row = lax.broadcasted_iota(I32, (ch, tq), 0)
    ones_rows = (lax.broadcasted_iota(I32, (MOBA_ACC_ROWS - HEAD_DIM, ch), 0) == 0).astype(BF)

    for c in range(IDX_HEADS // 2):
        pair = qit_ref[c * LANES:(c + 1) * LANES, :]
        qip_scr[0, :, c * tq:(c + 1) * tq] = pair
        qip_scr[1, :, c * tq:(c + 1) * tq] = _swap_halves(pair)
    for c in range(N_HEADS // 2):
        pair = qt_ref[c * LANES:(c + 1) * LANES, :]
        cols = slice((c % 2) * tq, (c % 2 + 1) * tq)
        qp_scr[2 * (c // 2), :, cols] = pair
        qp_scr[2 * (c // 2) + 1, :, cols] = _swap_halves(pair)

    def score_body(c, carry):
        kiz = kiz_ref[pl.ds(pl.multiple_of(c * ch, ch), ch), :]
        sc = jnp.zeros((ch, tq), F32)
        for second in range(2):
            d = jnp.dot(kiz, qip_scr[second], preferred_element_type=F32)
            for u in range(IDX_HEADS // 2):
                h = 2 * u + second
                sc = sc + wt_ref[h:h + 1, :] * jnp.maximum(d[:, u * tq:(u + 1) * tq], 0.0)
        sc = jnp.where(c * ch + key_row <= qpos, sc, NEG_INF)
        key_scr[c] = _float_key(sc)
        return carry
    lax.fori_loop(0, n_ch, score_body, 0)

    def count(pred):
        def body(c, acc):
            return acc + jnp.sum(pred(key_scr[c], c * ch + key_row).astype(I32), axis=0, keepdims=True)
        return lax.fori_loop(0, n_ch, body, jnp.zeros((1, tq), I32))

    thr = _kth_largest_key(lambda t: count(lambda key, idx: key >= t), k_top, (1, tq))
    n_gt = count(lambda key, idx: key > thr)
    n_ge = count(lambda key, idx: key >= thr)
    need = k_top - n_gt
    tie_cols = ((n_ge > k_top) & (thr != KEY_NEG_INF)).astype(I32)
    cutoff = lax.cond(
        jnp.max(tie_cols) > 0,
        lambda: _tie_cutoff(lambda c_: count(lambda key, idx: (key == thr) & (idx < c_)), need, n_bits, (1, tq)),
        lambda: jnp.full((1, tq), 2 ** 30, I32))

    m_scr[...] = jnp.full(m_scr.shape, NEG_INF, F32)
    acc_scr[...] = jnp.zeros_like(acc_scr)

    def attn_body(c, carry):
        key = key_scr[c]
        idx = c * ch + key_row
        keep = ((key > thr) | ((key == thr) & (idx <= cutoff))) & (idx <= qpos)
        bias = jnp.where(keep, 0.0, NEG_INF)
        bias = jnp.concatenate([bias, bias], axis=1)
        rows = pl.ds(pl.multiple_of(c * ch, ch), ch)
        sts = []
        for kv in range(N_KV_HEADS):
            kz = kz_ref[rows, kv * LANES:(kv + 1) * LANES]
            for second in range(2):
                sts.append(jnp.dot(kz, qp_scr[2 * kv + second], preferred_element_type=F32))
        for kv in range(N_KV_HEADS):
            vaug = jnp.concatenate([vt_ref[c, kv * HEAD_DIM:(kv + 1) * HEAD_DIM, :], ones_rows], axis=0)
            for second in range(2):
                g = 2 * kv + second
                st = sts[g] + bias
                m_old = m_scr[g]
                m_new = jnp.maximum(m_old, jnp.max(st, axis=0, keepdims=True))
                m_safe = jnp.where(m_new == NEG_INF, 0.0, m_new)
                alpha = jnp.exp(m_old - m_safe)
                e = jnp.exp(st - m_safe).astype(BF)
                acc_scr[g] = alpha * acc_scr[g] + jnp.dot(vaug, e, preferred_element_type=F32)
                m_scr[g] = m_new
        return carry
    lax.fori_loop(0, n_ch, attn_body, 0)

    for kv in range(N_KV_HEADS):
        outs = []
        for second in range(2):
            a = acc_scr[2 * kv + second]
            outs.append(a[:HEAD_DIM] * (1.0 / a[HEAD_DIM:HEAD_DIM + 1]))
        for u in range(2):
            both = jnp.concatenate([outs[0][:, u * tq:(u + 1) * tq], outs[1][:, u * tq:(u + 1) * tq]], axis=0)
            o_ref[:, (2 * kv + u) * LANES:(2 * kv + u + 1) * LANES] = both.T.astype(BF)


def _dsa_prompt(qt, qit, wt, kiz, kz, vt3, batch):
    rows = qt.shape[1]
    seq = rows // batch
    tq, ch = DSA_QUERY_TILE, DSA_KEY_CHUNK
    k_top = min(DSA_TOPK_MAX, seq // 4)
    assert seq % ch == 0 and ch >= k_top and ch % tq == 0 and tq % LANES == 0 and vt3.shape[2] == ch
    nq, n_chunks = seq // tq, seq // ch
    n_bits = max(1, (seq - 1).bit_length())
    tile = lambda n: pl.BlockSpec((n, tq), lambda b, i: (0, b * nq + i))
    return pl.pallas_call(
        functools.partial(_dsa_prompt_kernel, k_top=k_top, n_bits=n_bits),
        grid=(batch, nq),
        in_specs=[tile(Q_DIM), tile(IDX_Q_DIM), tile(IDX_HEADS),
                  pl.BlockSpec((seq, LANES), lambda b, i: (b, 0), pipeline_mode=pl.Buffered(1)),
                  pl.BlockSpec((seq, 2 * KV_DIM), lambda b, i: (b, 0), pipeline_mode=pl.Buffered(1)),
                  pl.BlockSpec((n_chunks, KV_DIM, ch), lambda b, i: (b, 0, 0), pipeline_mode=pl.Buffered(1))],
        out_specs=pl.BlockSpec((tq, Q_DIM), lambda b, i: (b * nq + i, 0)),
        out_shape=jax.ShapeDtypeStruct((rows, Q_DIM), BF),
        scratch_shapes=[pltpu.VMEM((2 * N_KV_HEADS, LANES, 2 * tq), BF),
                        pltpu.VMEM((2, LANES, (IDX_HEADS // 2) * tq), BF),
                        pltpu.VMEM((n_chunks, ch, tq), I32),
                        pltpu.VMEM((2 * N_KV_HEADS, 1, 2 * tq), F32),
                        pltpu.VMEM((2 * N_KV_HEADS, MOBA_ACC_ROWS, 2 * tq), F32)],
        compiler_params=_params("parallel", "arbitrary"),
        name="dsa_prompt",
    )(qt, qit, wt, kiz, kz, vt3)


def _swa_prompt_kernel(qt_ref, kzp_ref, kzc_ref, vtp_ref, vtc_ref, sink_ref, o_ref):
    n = pl.program_id(1)
    w = WINDOW
    key_i = lax.broadcasted_iota(I32, (2 * w, w), 0)
    qry_i = lax.broadcasted_iota(I32, (2 * w, w), 1)
    keep = (key_i >= qry_i) & (key_i <= qry_i + w) & ((key_i >= w) | (n > 0))
    bias = jnp.where(keep, 0.0, NEG_INF)
    bias = jnp.concatenate([bias, bias], axis=1)
    ones_rows = (lax.broadcasted_iota(I32, (MOBA_ACC_ROWS - HEAD_DIM, 2 * w), 0) == 0).astype(BF)
    for kv in range(N_KV_HEADS):
        lanes = slice(kv * LANES, (kv + 1) * LANES)
        feats = slice(kv * HEAD_DIM, (kv + 1) * HEAD_DIM)
        kz = jnp.concatenate([kzp_ref[:, lanes], kzc_ref[:, lanes]], axis=0)
        vaug = jnp.concatenate([jnp.concatenate([vtp_ref[feats, :], vtc_ref[feats, :]], axis=1), ones_rows], axis=0)
        pairs = [qt_ref[(2 * kv + u) * LANES:(2 * kv + u + 1) * LANES, :] for u in range(2)]
        outs = []
        for second in range(2):
            ops = pairs if second == 0 else [_swap_halves(p) for p in pairs]
            st = jnp.dot(kz, jnp.concatenate(ops, axis=1), preferred_element_type=F32) + bias
            sink = sink_ref[2 * kv + second]
            m = jnp.maximum(jnp.max(st, axis=0, keepdims=True), sink)
            acc = jnp.dot(vaug, jnp.exp(st - m).astype(BF), preferred_element_type=F32)
            outs.append(acc[:HEAD_DIM] * (1.0 / (acc[HEAD_DIM:HEAD_DIM + 1] + jnp.exp(sink - m))))
        for u in range(2):
            both = jnp.concatenate([outs[0][:, u * w:(u + 1) * w], outs[1][:, u * w:(u + 1) * w]], axis=0)
            o_ref[:, (2 * kv + u) * LANES:(2 * kv + u + 1) * LANES] = both.T.astype(BF)


def _swa_prompt(qt, kz, vt, sinks, batch):
    rows = qt.shape[1]
    seq = rows // batch
    w = WINDOW
    assert seq % w == 0 and w == LANES
    nb = seq // w
    cur = lambda b, n: b * nb + n
    prev = lambda b, n: b * nb + jnp.maximum(n - 1, 0)
    sk = sinks.astype(F32).reshape(N_KV_HEADS, 2, 2)
    sink_cols = jnp.repeat(sk.transpose(0, 2, 1), w, axis=2).reshape(2 * N_KV_HEADS, 1, 2 * w)
    return pl.pallas_call(
        _swa_prompt_kernel,
        grid=(batch, nb),
        in_specs=[pl.BlockSpec((Q_DIM, w), lambda b, n: (0, cur(b, n))),
                  pl.BlockSpec((w, 2 * KV_DIM), lambda b, n: (prev(b, n), 0)),
                  pl.BlockSpec((w, 2 * KV_DIM), lambda b, n: (cur(b, n), 0)),
                  pl.BlockSpec((KV_DIM, w), lambda b, n: (0, prev(b, n))),
                  pl.BlockSpec((KV_DIM, w), lambda b, n: (0, cur(b, n))),
                  pl.BlockSpec((2 * N_KV_HEADS, 1, 2 * w), lambda b, n: (0, 0, 0))],
        out_specs=pl.BlockSpec((w, Q_DIM), lambda b, n: (cur(b, n), 0)),
        out_shape=jax.ShapeDtypeStruct((rows, Q_DIM), BF),
        compiler_params=_params("parallel", "parallel"),
        name="swa_prompt",
    )(qt, kz, kz, vt, vt, sink_cols)


def _row_token(n_rows, n_tok):
    r = lax.broadcasted_iota(I32, (n_rows, 1), 0)
    return (r // GROUP) % n_tok


def _block_diag_queries(q, n_seq, n_tok):
    qr = q.reshape(n_seq, n_tok, N_KV_HEADS, GROUP, HEAD_DIM).transpose(0, 2, 1, 3, 4)
    qr = qr.reshape(n_seq, N_KV_HEADS, n_tok * GROUP, HEAD_DIM)
    eye = jnp.eye(N_KV_HEADS, dtype=q.dtype)
    qbd = qr[:, :, :, None, :] * eye[None, :, None, :, None]
    return qbd.reshape(n_seq, N_KV_HEADS * n_tok * GROUP, KV_DIM)


def _undiag_outputs(out, n_seq, n_tok):
    o = out.reshape(n_seq, N_KV_HEADS, n_tok, GROUP, N_KV_HEADS, HEAD_DIM)
    o = jnp.stack([o[:, kv, :, :, kv] for kv in range(N_KV_HEADS)], axis=1)
    return o.transpose(0, 2, 1, 3, 4).reshape(n_seq * n_tok, Q_DIM)


def _tokens_on_lanes(cache):
    if cache.ndim == 5:
        t = jnp.transpose(cache, (0, 1, 3, 4, 2))
        return t.reshape(t.shape[:2] + (t.shape[2] * t.shape[3], t.shape[4]))
    return jnp.transpose(cache, (0, 1, 3, 2))


def _page_specs(n_pages_per_step, slot, width):
    def spec(r):
        return pl.BlockSpec((None, None, width, PAGE_SIZE),
                            lambda b, s, pt: (slot, pt[b, s * n_pages_per_step + r], 0, 0))
    return [spec(r) for r in range(n_pages_per_step)]


def _pad_rows(a, n_seq, n_tok):
    a = a.reshape(n_seq, n_tok, a.shape[-1])
    return jnp.pad(a, ((0, 0), (0, SUBLANES - n_tok), (0, 0)))


def _moba_sample_kernel(pt_ref, q_ref, kn_ref, vn_ref, *refs, n_tok, n_sel, pps):
    kt_pages = refs[:pps]
    vt_pages = refs[pps:2 * pps]
    o_ref = refs[2 * pps]
    m_scr, l_scr, g_scr, o_scr = refs[2 * pps + 1:]
    s = pl.program_id(1)
    n_steps = m_scr.shape[0]
    bps = pps // 2
    q = q_ref[...]
    n_row = q.shape[0]
    m_scr[s] = jnp.full(m_scr.shape[1:], NEG_INF, F32)
    l_scr[s] = jnp.zeros(l_scr.shape[1:], F32)
    g_scr[s] = jnp.full(g_scr.shape[1:], NEG_INF, F32)
    kt_all = jnp.concatenate([p[...] for p in kt_pages], axis=1).astype(BF)
    sc_all = jnp.dot(q, kt_all, preferred_element_type=F32)
    for r in range(bps):
        vt = jnp.concatenate([vt_pages[2 * r][...], vt_pages[2 * r + 1][...]], axis=1).astype(BF)
        sc = sc_all[:, r * MOBA_BLOCK:(r + 1) * MOBA_BLOCK]
        m = jnp.max(sc, axis=1, keepdims=True)
        e = jnp.exp(sc - m)
        m_scr[s, :, r:r + 1] = m
        l_scr[s, :, r:r + 1] = jnp.sum(e, axis=1, keepdims=True)
        g_scr[s, :, r:r + 1] = jnp.sum(sc, axis=1, keepdims=True)
        o_scr[s * bps + r] = lax.dot_general(e.astype(BF), vt, NT, preferred_element_type=F32)

    @pl.when(s == pl.num_programs(1) - 1)
    def _():
        lane = lax.broadcasted_iota(I32, (n_row, LANES), 1)
        ids = [t * bps + lane for t in range(n_steps)]
        gs = [g_scr[t] for t in range(n_steps)]
        picked = [None] * n_steps
        for _ in range(n_sel):
            mx = functools.reduce(jnp.maximum, [jnp.max(g, axis=1, keepdims=True) for g in gs])
            am = functools.reduce(jnp.minimum, [jnp.min(jnp.where(g == mx, i_, 2 ** 30), axis=1, keepdims=True)
                                                for g, i_ in zip(gs, ids)])
            for t in range(n_steps):
                hit = (ids[t] == am) & (mx > NEG_INF)
                picked[t] = hit if picked[t] is None else (picked[t] | hit)
                gs[t] = jnp.where(ids[t] == am, NEG_INF, gs[t])
        own_ok = lax.broadcasted_iota(I32, (n_row, SUBLANES), 1) <= _row_token(n_row, n_tok)
        s_own = jnp.where(own_ok, lax.dot_general(q, kn_ref[...], NT, preferred_element_type=F32), NEG_INF)
        m_tot = jnp.max(s_own, axis=1, keepdims=True)
        for t in range(n_steps):
            m_tot = jnp.maximum(m_tot, jnp.max(jnp.where(picked[t], m_scr[t], NEG_INF), axis=1, keepdims=True))
        e_own = jnp.exp(s_own - m_tot)
        den = jnp.sum(e_own, axis=1, keepdims=True)
        num = jnp.dot(e_own.astype(BF), vn_ref[...], preferred_element_type=F32)
        for t in range(n_steps):
            wgt = jnp.where(picked[t], jnp.exp(m_scr[t] - m_tot), 0.0)
            den = den + jnp.sum(wgt * l_scr[t], axis=1, keepdims=True)
            for r in range(bps):
                num = num + wgt[:, r:r + 1] * o_scr[t * bps + r]
        o_ref[...] = num * (1.0 / den)


def _paged_attention_call(kernel, name, slot, page_table, cache_kt, cache_vt, extra_inputs, extra_specs,
                          scratch_shapes, n_row):
    n_seq, n_pages = page_table.shape
    pps = PAGES_PER_STEP
    assert n_pages % pps == 0
    grid_spec = pltpu.PrefetchScalarGridSpec(
        num_scalar_prefetch=1,
        grid=(n_seq, n_pages // pps),
        in_specs=extra_specs + _page_specs(pps, slot, KV_DIM) + _page_specs(pps, slot, KV_DIM),
        out_specs=pl.BlockSpec((None, n_row, KV_DIM), lambda b, s, pt: (b, 0, 0)),
        scratch_shapes=scratch_shapes)
    return pl.pallas_call(
        kernel,
        grid_spec=grid_spec,
        out_shape=jax.ShapeDtypeStruct((n_seq, n_row, KV_DIM), F32),
        compiler_params=_params("parallel", "arbitrary"),
        name=name,
    )(page_table, *extra_inputs, *([cache_kt] * pps), *([cache_vt] * pps))


def _moba_sample(qbd, kn_pad, vn_pad, cache_kt, cache_vt, slot, page_table, n_tok):
    n_seq, n_pages = page_table.shape
    past = n_pages * PAGE_SIZE
    assert past % MOBA_BLOCK == 0 and MOBA_BLOCK == 2 * PAGE_SIZE and PAGES_PER_STEP % 2 == 0
    nblk = past // MOBA_BLOCK
    n_steps = n_pages // PAGES_PER_STEP
    n_row = qbd.shape[1]
    per_seq = lambda w: pl.BlockSpec((None, w[0], w[1]), lambda b, s, pt: (b, 0, 0))
    kernel = functools.partial(_moba_sample_kernel, n_tok=n_tok, n_sel=min(MOBA_TOPK, nblk), pps=PAGES_PER_STEP)
    stats = pltpu.VMEM((n_steps, n_row, LANES), F32)
    return _paged_attention_call(
        kernel, "moba_sample", slot, page_table, cache_kt, cache_vt,
        [qbd, kn_pad, vn_pad],
        [per_seq((n_row, KV_DIM)), per_seq((SUBLANES, KV_DIM)), per_seq((SUBLANES, KV_DIM))],
        [stats, stats, stats, pltpu.VMEM((nblk, n_row, KV_DIM), F32)],
        n_row)


def _dsa_index_kernel(pt_ref, qi_ref, w_ref, kin_ref, *refs, n_tok, k_top, ppi, n_bits):
    ki_pages = refs[:ppi]
    mask_ref = refs[ppi]
    key_scr = refs[ppi + 1]
    s = pl.program_id(1)
    n_steps = pl.num_programs(1)
    n_pages = n_steps * ppi
    qi = qi_ref[...]
    w = w_ref[...]
    fill = jnp.full((SUBLANES - n_tok, LANES), NEG_INF, F32)

    def scores(d):
        wd = (w * jnp.maximum(d, 0.0)).reshape(n_tok, IDX_HEADS, LANES)
        return jnp.sum(wd, axis=1)

    for r in range(ppi):
        sc = scores(jnp.dot(qi, ki_pages[r][...].astype(BF), preferred_element_type=F32))
        key_scr[s * ppi + r] = _float_key(jnp.concatenate([sc, fill], axis=0))

    @pl.when(s == n_steps - 1)
    def _():
        tok = lax.broadcasted_iota(I32, (n_tok, LANES), 0)
        lane_t = lax.broadcasted_iota(I32, (n_tok, LANES), 1)
        sc_new = scores(lax.dot_general(qi, kin_ref[...], NT, preferred_element_type=F32))
        sc_new = jnp.where(lane_t <= tok, sc_new, NEG_INF)
        key_scr[n_pages] = _float_key(jnp.concatenate([sc_new, fill], axis=0))
        tiles = (n_pages + 1, SUBLANES, LANES)
        idx = lax.broadcasted_iota(I32, tiles, 0) * PAGE_SIZE + lax.broadcasted_iota(I32, tiles, 2)

        def count(hit):
            return jnp.sum(jnp.sum(hit.astype(I32), axis=0), axis=1, keepdims=True)

        thr = _kth_largest_key(lambda t: count(key_scr[...] >= t), k_top, (SUBLANES, 1), bits_per_step=2)
        need = k_top - count(key_scr[...] > thr)
        tie_rows = ((count(key_scr[...] >= thr) > k_top) & (thr != KEY_NEG_INF)).astype(I32)
        cutoff = lax.cond(
            jnp.max(tie_rows) > 0,
            lambda: _tie_cutoff(lambda c_: count((key_scr[...] == thr) & (idx < c_)), need, n_bits, (SUBLANES, 1)),
            lambda: jnp.full((SUBLANES, 1), 2 ** 30, I32))
        keys = key_scr[...]
        keep = (keys > thr) | ((keys == thr) & (idx <= cutoff))
        mask_ref[...] = jnp.where(keep & (keys != KEY_NEG_INF), 1.0, 0.0).astype(BF)


def _dsa_index(qi_rows, w_rows, ki_new, cache_idx, islot, page_table, n_tok):
    n_seq, n_pages = page_table.shape
    ppi = min(INDEX_PAGES_PER_STEP, n_pages)
    total = n_pages * PAGE_SIZE + n_tok
    k_top = min(DSA_TOPK_MAX, total // 4)
    assert n_pages % ppi == 0 and n_tok <= SUBLANES and n_pages * PAGE_SIZE >= k_top
    n_bits = max(1, ((n_pages + 1) * PAGE_SIZE - 1).bit_length())
    nr = n_tok * IDX_HEADS
    grid_spec = pltpu.PrefetchScalarGridSpec(
        num_scalar_prefetch=1,
        grid=(n_seq, n_pages // ppi),
        in_specs=[pl.BlockSpec((None, nr, IDX_DIM), lambda b, s, pt: (b, 0, 0)),
                  pl.BlockSpec((None, nr, 1), lambda b, s, pt: (b, 0, 0)),
                  pl.BlockSpec((None, LANES, IDX_DIM), lambda b, s, pt: (b, 0, 0))]
        + _page_specs(ppi, islot, IDX_DIM),
        out_specs=pl.BlockSpec((None, n_pages + 1, SUBLANES, LANES), lambda b, s, pt: (b, 0, 0, 0)),
        scratch_shapes=[pltpu.VMEM((n_pages + 1, SUBLANES, LANES), I32)])
    return pl.pallas_call(
        functools.partial(_dsa_index_kernel, n_tok=n_tok, k_top=k_top, ppi=ppi, n_bits=n_bits),
        grid_spec=grid_spec,
        out_shape=jax.ShapeDtypeStruct((n_seq, n_pages + 1, SUBLANES, LANES), BF),
        compiler_params=_params("parallel", "arbitrary"),
        name="dsa_index",
    )(page_table, qi_rows, w_rows, ki_new, *([cache_idx] * ppi))


def _dsa_sample_kernel(pt_ref, q_ref, kn_ref, vn_ref, mask_ref, mnew_ref, *refs, n_tok, pps):
    kt_pages = refs[:pps]
    vt_pages = refs[pps:2 * pps]
    o_ref = refs[2 * pps]
    m_scr, l_scr, acc_scr = refs[2 * pps + 1:]
    s = pl.program_id(1)
    q = q_ref[...]
    n_row = q.shape[0]
    expand = (lax.broadcasted_iota(I32, (n_row, SUBLANES), 1) == _row_token(n_row, n_tok)).astype(BF)

    @pl.when(s == 0)
    def _():
        m_scr[...] = jnp.full(m_scr.shape, NEG_INF, F32)
        l_scr[...] = jnp.zeros_like(l_scr)
        acc_scr[...] = jnp.zeros_like(acc_scr)

    def update(sc, keep, pv):
        sc = jnp.where(keep > 0.5, sc, NEG_INF)
        m_old = m_scr[...]
        m_new = jnp.maximum(m_old, jnp.max(sc, axis=1, keepdims=True))
        m_safe = jnp.where(m_new == NEG_INF, 0.0, m_new)
        alpha = jnp.exp(m_old - m_safe)
        e = jnp.exp(sc - m_safe)
        l_scr[...] = alpha * l_scr[...] + jnp.sum(e, axis=1, keepdims=True)
        acc_scr[...] = alpha * acc_scr[...] + pv(e.astype(BF))
        m_scr[...] = m_new

    kt_all = jnp.concatenate([p[...] for p in kt_pages], axis=1).astype(BF)
    sc_all = jnp.dot(q, kt_all, preferred_element_type=F32)
    keep_all = jnp.dot(expand, jnp.concatenate([mask_ref[r] for r in range(pps)], axis=1),
                       preferred_element_type=F32)
    width = DSA_PAGES_PER_UPDATE * PAGE_SIZE
    for u in range(pps // DSA_PAGES_PER_UPDATE):
        rs = range(u * DSA_PAGES_PER_UPDATE, (u + 1) * DSA_PAGES_PER_UPDATE)
        vt = jnp.concatenate([vt_pages[r][...] for r in rs], axis=1).astype(BF)
        cols = slice(u * width, (u + 1) * width)
        update(sc_all[:, cols], keep_all[:, cols], lambda e, vt=vt: lax.dot_general(e, vt, NT, preferred_element_type=F32))

    @pl.when(s == pl.num_programs(1) - 1)
    def _():
        sc = lax.dot_general(q, kn_ref[...], NT, preferred_element_type=F32)
        keep = jnp.dot(expand, mnew_ref[...], preferred_element_type=F32)[:, :SUBLANES]
        update(sc, keep, lambda e: jnp.dot(e, vn_ref[...], preferred_element_type=F32))
        o_ref[...] = acc_scr[...] * (1.0 / l_scr[...])


def _dsa_sample(qbd, kn_pad, vn_pad, mask, cache_kt, cache_vt, slot, page_table, n_tok):
    n_seq, n_pages = page_table.shape
    n_row = qbd.shape[1]
    pps = PAGES_PER_STEP
    per_seq = lambda w: pl.BlockSpec((None, w[0], w[1]), lambda b, s, pt: (b, 0, 0))
    return _paged_attention_call(
        functools.partial(_dsa_sample_kernel, n_tok=n_tok, pps=pps), "dsa_sample", slot, page_table,
        cache_kt, cache_vt,
        [qbd, kn_pad, vn_pad, mask, mask],
        [per_seq((n_row, KV_DIM)), per_seq((SUBLANES, KV_DIM)), per_seq((SUBLANES, KV_DIM)),
         pl.BlockSpec((None, pps, SUBLANES, LANES), lambda b, s, pt: (b, s, 0, 0)),
         pl.BlockSpec((None, None, SUBLANES, LANES), lambda b, s, pt: (b, n_pages, 0, 0))],
        [pltpu.VMEM((n_row, 1), F32), pltpu.VMEM((n_row, 1), F32), pltpu.VMEM((n_row, KV_DIM), F32)],
        n_row)


def _swa_sample_kernel(q_ref, kt_ref, vt_ref, kn_ref, vn_ref, sink_ref, o_ref, *, n_tok):
    q = q_ref[...]
    n_row = q.shape[0]
    tok_r = _row_token(n_row, n_tok)
    wk = kt_ref.shape[1]
    s_buf = jnp.dot(q, kt_ref[...].astype(BF), preferred_element_type=F32)
    s_buf = jnp.where(lax.broadcasted_iota(I32, (n_row, wk), 1) >= tok_r, s_buf, NEG_INF)
    s_new = lax.dot_general(q, kn_ref[...], NT, preferred_element_type=F32)
    s_new = jnp.where(lax.broadcasted_iota(I32, (n_row, SUBLANES), 1) <= tok_r, s_new, NEG_INF)
    sink = sink_ref[...]
    m = jnp.maximum(jnp.maximum(jnp.max(s_buf, axis=1, keepdims=True), jnp.max(s_new, axis=1, keepdims=True)), sink)
    e_buf = jnp.exp(s_buf - m)
    e_new = jnp.exp(s_new - m)
    den = jnp.sum(e_buf, axis=1, keepdims=True) + jnp.sum(e_new, axis=1, keepdims=True) + jnp.exp(sink - m)
    num = (lax.dot_general(e_buf.astype(BF), vt_ref[...].astype(BF), NT, preferred_element_type=F32)
           + jnp.dot(e_new.astype(BF), vn_ref[...], preferred_element_type=F32))
    o_ref[...] = num * (1.0 / den)


def _swa_sample(qbd, kn_pad, vn_pad, buf_kt, buf_vt, sinks, n_tok):
    n_seq, _, wk = buf_kt.shape
    assert wk == WINDOW
    n_row = qbd.shape[1]
    sink_rows = jnp.repeat(sinks.reshape(N_KV_HEADS, 1, GROUP), n_tok, axis=1).reshape(n_row, 1)
    per_seq = lambda a, c: pl.BlockSpec((None, a, c), lambda b: (b, 0, 0))
    return pl.pallas_call(
        functools.partial(_swa_sample_kernel, n_tok=n_tok),
        grid=(n_seq,),
        in_specs=[per_seq(n_row, KV_DIM), per_seq(KV_DIM, wk), per_seq(KV_DIM, wk),
                  per_seq(SUBLANES, KV_DIM), per_seq(SUBLANES, KV_DIM),
                  pl.BlockSpec((n_row, 1), lambda b: (0, 0))],
        out_specs=per_seq(n_row, KV_DIM),
        out_shape=jax.ShapeDtypeStruct((n_seq, n_row, KV_DIM), F32),
        compiler_params=_params("parallel"),
        name="swa_sample",
    )(qbd, buf_kt, buf_vt, kn_pad, vn_pad, sink_rows)


def _rope_tables(pos):
    half = HEAD_DIM // 2
    inv = ROPE_THETA ** (-jnp.arange(half, dtype=F32) / half)
    ang = pos.astype(F32)[:, None] * inv[None, :]
    cos, sin = jnp.cos(ang), jnp.sin(ang)
    reps = LANES // HEAD_DIM
    return jnp.tile(jnp.concatenate([cos, cos], axis=1), (1, reps)), jnp.tile(jnp.concatenate([-sin, sin], axis=1), (1, reps))


def kernel(x_prompt, x_sample, cache_k, cache_v, cache_idx_k, state_swa_k, state_swa_v, page_table, c_prompt, c_sample, g_attn, g_mlp, w_mod, b_mod, w_in_moba, w_in_dsa, w_in_swa, swa_sinks, w_out, w_ff1, w_ff2, g_final):
    batch, seq, _ = x_prompt.shape
    n_seq, n_tok, _ = x_sample.shape
    depth = g_attn.shape[0]
    n_pages = page_table.shape[1]
    past = n_pages * PAGE_SIZE
    rows_p, rows_s = batch * seq, n_seq * n_tok

    n_cond = batch + n_seq
    pad_cond = -n_cond % SUBLANES
    c_all = jnp.pad(jnp.concatenate([c_prompt, c_sample], axis=0), ((0, pad_cond), (0, 0)))
    mod = _adaln(c_all, w_mod, b_mod)

    def mods(layer):
        parts = [mod[layer, :, r * D_MODEL:(r + 1) * D_MODEL] for r in range(6)]
        mp = [p[:batch].reshape(batch, 1, D_MODEL) for p in parts]
        ms = [jnp.repeat(p[batch:n_cond], n_tok, axis=0) for p in parts]
        return mp, ms

    cos_p, sin_p = _rope_tables(jnp.arange(seq, dtype=jnp.int32))
    cos_s, sin_s = _rope_tables(jnp.tile(past + jnp.arange(n_tok, dtype=jnp.int32), n_seq))
    cos_pt, sin_pt = cos_p[:, :HEAD_DIM].T, sin_p[:, :HEAD_DIM].T

    cache_kt, cache_vt, cache_it = _tokens_on_lanes(cache_k), _tokens_on_lanes(cache_v), _tokens_on_lanes(cache_idx_k)
    swa_kt, swa_vt = _tokens_on_lanes(state_swa_k), _tokens_on_lanes(state_swa_v)

    xp = x_prompt.reshape(rows_p, D_MODEL)
    xs = x_sample.reshape(rows_s, D_MODEL)
    kp_l, vp_l, ks_l, vs_l, ip_l, is_l = [], [], [], [], [], []
    skp_l, svp_l, sks_l, svs_l = [], [], [], []
    for i, (mixer, j, slot) in enumerate(_layer_plan(depth)):
        (sh_ap, sc_ap, gt_ap, sh_mp, sc_mp, gt_mp), (sh_as, sc_as, gt_as, sh_ms, sc_ms, gt_ms) = mods(i)
        dsa = mixer == 1
        if mixer == 0:
            w_in = w_in_moba[j]
        elif dsa:
            w_in = jnp.pad(w_in_dsa[j], ((0, 0), (0, -IN_DSA % LANES)))
        else:
            w_in = w_in_swa[j]
        w_in = w_in.astype(BF)
        g_a = g_attn[i].reshape(1, D_MODEL)
        outs_p = _project_t(xp, g_a, sc_ap, sh_ap, cos_p, sin_p, cos_pt, sin_pt, w_in, seq, dsa)
        qt_p, kt_p, vtf_p, kz_p, vt_p, kmeans_p = outs_p[:6]
        outs_s = _project(xs, g_a, sc_as, sh_as, cos_s, sin_s, w_in, n_tok, dsa)
        q_s, k_s, v_s, kb_s, vb_s = outs_s[:5]
        qbd = _block_diag_queries(q_s, n_seq, n_tok)
        kn_pad = _pad_rows(kb_s, n_seq, n_tok)
        vn_pad = _pad_rows(vb_s, n_seq, n_tok)
        k_p5 = kt_p.reshape(batch, N_KV_HEADS, HEAD_DIM, seq).transpose(0, 3, 1, 2)
        v_p5 = vtf_p.reshape(batch, N_KV_HEADS, HEAD_DIM, seq).transpose(0, 3, 1, 2)
        k_s5 = k_s.reshape(n_seq, n_tok, N_KV_HEADS, HEAD_DIM)
        v_s5 = v_s.reshape(n_seq, n_tok, N_KV_HEADS, HEAD_DIM)
        if mixer == 0:
            o_p = _moba_prompt(qt_p, kz_p, vt_p, kmeans_p, batch)
            o_t = _moba_sample(qbd, kn_pad, vn_pad, cache_kt, cache_vt, slot, page_table, n_tok)
        elif dsa:
            qit_p, wt_p, kit_p, kiz_p = outs_p[6:]
            qi_s, rest_s, restb_s = outs_s[5:]
            o_p = _dsa_prompt(qt_p, qit_p, wt_p, kiz_p, kz_p, vt_p, batch)
            qi_rows = qi_s.reshape(n_seq, n_tok * IDX_HEADS, IDX_DIM)
            w_rows = rest_s[:, IDX_DIM:IDX_DIM + IDX_HEADS].reshape(n_seq, n_tok * IDX_HEADS, 1)
            ki_new = jnp.pad(restb_s[:, :IDX_DIM].reshape(n_seq, n_tok, IDX_DIM),
                             ((0, 0), (0, LANES - n_tok), (0, 0)))
            mask = _dsa_index(qi_rows, w_rows, ki_new, cache_it, j, page_table, n_tok)
            o_t = _dsa_sample(qbd, kn_pad, vn_pad, mask, cache_kt, cache_vt, slot, page_table, n_tok)
            ip_l.append(kit_p.transpose(0, 2, 1))
            is_l.append(rest_s[:, :IDX_DIM].reshape(n_seq, n_tok, IDX_DIM))
        else:
            o_p = _swa_prompt(qt_p, kz_p, vt_p, swa_sinks[j], batch)
            o_t = _swa_sample(qbd, kn_pad, vn_pad, swa_kt[j], swa_vt[j], swa_sinks[j], n_tok)
            keep_p = min(WINDOW, seq)
            skp_l.append(k_p5[:, seq - keep_p:])
            svp_l.append(v_p5[:, seq - keep_p:])
            wk = state_swa_k.shape[2]
            sks_l.append(jnp.concatenate([state_swa_k[j], k_s5], axis=1)[:, -wk:])
            svs_l.append(jnp.concatenate([state_swa_v[j], v_s5], axis=1)[:, -wk:])
        if mixer < 2:
            kp_l.append(k_p5)
            vp_l.append(v_p5)
            ks_l.append(k_s5)
            vs_l.append(v_s5)
        o_s = _undiag_outputs(o_t, n_seq, n_tok).astype(BF)
        wo, w1, w2 = w_out[i].astype(BF), w_ff1[i].astype(BF), w_ff2[i].astype(BF)
        g_m = g_mlp[i].reshape(1, D_MODEL)
        xp = _post(xp, o_p, wo, gt_ap, g_m, sc_mp, sh_mp, gt_mp, w1, w2, seq)
        xs = _post(xs, o_s, wo, gt_as, g_m, sc_ms, sh_ms, gt_ms, w1, w2, n_tok)
    g_f = g_final.reshape(1, D_MODEL)
    y_prompt = _final_norm(xp, g_f).reshape(batch, seq, D_MODEL)
    y_sample = _final_norm(xs, g_f).reshape(n_seq, n_tok, D_MODEL)
    return (y_prompt, y_sample, jnp.stack(kp_l), jnp.stack(vp_l), jnp.stack(ks_l), jnp.stack(vs_l),
            jnp.stack(ip_l), jnp.stack(is_l), jnp.stack(skp_l), jnp.stack(svp_l), jnp.stack(sks_l), jnp.stack(svs_l))
```

```python
import functools

import numpy as np
import jax
import jax.numpy as jnp
from jax import lax
from jax.experimental import pallas as pl
from jax.experimental.pallas import tpu as pltpu

D_MODEL = 1024
N_HEADS = 16
HEAD_DIM = D_MODEL // N_HEADS
N_KV_HEADS = 4
GROUP = N_HEADS // N_KV_HEADS
Q_DIM = N_HEADS * HEAD_DIM
KV_DIM = N_KV_HEADS * HEAD_DIM
D_FF = 4 * D_MODEL
ROPE_THETA = 10000.0
NORM_EPS = 1e-6
N_MIXERS = 3
PAGE_SIZE = 128
MOBA_BLOCK = 256
MOBA_TOPK = 3
IDX_HEADS = 8
IDX_DIM = 64
DSA_TOPK_MAX = 256
WINDOW = 128
IN_ATTN = Q_DIM + 2 * KV_DIM
IDX_Q_DIM = IDX_HEADS * IDX_DIM
IN_DSA = IN_ATTN + IDX_Q_DIM + IDX_DIM + IDX_HEADS
ATTN_SCALE = HEAD_DIM ** -0.5
LOG2E = 1.4426950408889634
IDX_W_SCALE = IDX_Q_DIM ** -0.5

LANES = 128
SUBLANES = 8
VMEM_LIMIT = 56 << 20
ROW_TILE = 512
FF_TILE = 1024
DSA_KEY_CHUNK = 512
DSA_QUERY_TILE = 512
PAGES_PER_STEP = 16
INDEX_PAGES_PER_STEP = 64
DSA_PAGES_PER_UPDATE = 4

BF = jnp.bfloat16
F32 = jnp.float32
I32 = jnp.int32
NEG_INF = float("-inf")
INT_MIN = -2 ** 31
KEY_NEG_INF = INT_MIN + 0x7FFFFF
NT = (((1,), (1,)), ((), ()))


def _params(*semantics):
    return pltpu.CompilerParams(dimension_semantics=semantics, vmem_limit_bytes=VMEM_LIMIT)


def _layer_plan(depth):
    plan, counts, n_paged = [], [0] * N_MIXERS, 0
    for i in range(depth):
        m = i % N_MIXERS
        slot = -1
        if m < 2:
            slot = n_paged
            n_paged += 1
        plan.append((m, counts[m], slot))
        counts[m] += 1
    return plan


def _adaln_kernel(c_ref, w_ref, b_ref, o_ref):
    c = c_ref[...]
    a = (c * (1.0 / (1.0 + jnp.exp(-c)))).astype(BF)
    o_ref[...] = jnp.dot(a, w_ref[...].astype(BF), preferred_element_type=F32) + b_ref[...]


def _adaln(c_all, w_mod, b_mod):
    depth, _, n_out = w_mod.shape
    nc = c_all.shape[0]
    tn = 1536
    return pl.pallas_call(
        _adaln_kernel,
        grid=(depth, n_out // tn),
        in_specs=[pl.BlockSpec((nc, D_MODEL), lambda l, j: (0, 0)),
                  pl.BlockSpec((None, D_MODEL, tn), lambda l, j: (l, 0, j)),
                  pl.BlockSpec((None, 1, tn), lambda l, j: (l, 0, j))],
        out_specs=pl.BlockSpec((None, nc, tn), lambda l, j: (l, 0, j)),
        out_shape=jax.ShapeDtypeStruct((depth, nc, n_out), F32),
        compiler_params=_params("parallel", "parallel"),
        name="adaln",
    )(c_all, w_mod, b_mod.reshape(depth, 1, n_out))


def _norm_mod(x, g, scale, shift):
    y = x * lax.rsqrt(jnp.mean(x * x, axis=-1, keepdims=True) + NORM_EPS)
    return (y * g) * (1.0 + scale) + shift


def _rope_chunk(z, cos, sin_signed, lo_half):
    partner = jnp.where(lo_half, pltpu.roll(z, LANES - HEAD_DIM // 2, 1), pltpu.roll(z, HEAD_DIM // 2, 1))
    return z * cos + partner * sin_signed


def _proj_kernel(x_ref, g_ref, sc_ref, sh_ref, cos_ref, sin_ref, w_ref, *out_refs, dsa):
    hb = _norm_mod(x_ref[...], g_ref[...], sc_ref[...], sh_ref[...]).astype(BF)
    cos = cos_ref[...]
    sin_s = sin_ref[...]
    lane = lax.broadcasted_iota(I32, (1, LANES), 1)
    lo_half = (lane % HEAD_DIM) < (HEAD_DIM // 2)
    q_ref, k_ref, v_ref, kb_ref, vb_ref = out_refs[:5]

    zq = jnp.dot(hb, w_ref[:, :Q_DIM], preferred_element_type=F32)
    for c in range(Q_DIM // LANES):
        sl = slice(c * LANES, (c + 1) * LANES)
        q_ref[:, sl] = (_rope_chunk(zq[:, sl], cos, sin_s, lo_half) * ATTN_SCALE).astype(BF)
    zk = jnp.dot(hb, w_ref[:, Q_DIM:Q_DIM + KV_DIM], preferred_element_type=F32)
    for c in range(KV_DIM // LANES):
        sl = slice(c * LANES, (c + 1) * LANES)
        kr = _rope_chunk(zk[:, sl], cos, sin_s, lo_half)
        k_ref[:, sl] = kr
        kb_ref[:, sl] = kr.astype(BF)
    zv = jnp.dot(hb, w_ref[:, Q_DIM + KV_DIM:IN_ATTN], preferred_element_type=F32)
    v_ref[...] = zv
    vb_ref[...] = zv.astype(BF)
    if dsa:
        qi_ref, rest_ref, restb_ref = out_refs[5:]
        zi = jnp.dot(hb, w_ref[:, IN_ATTN:IN_ATTN + IDX_Q_DIM], preferred_element_type=F32)
        for c in range(IDX_Q_DIM // LANES):
            sl = slice(c * LANES, (c + 1) * LANES)
            qi_ref[:, sl] = _rope_chunk(zi[:, sl], cos, sin_s, lo_half).astype(BF)
        zr = jnp.dot(hb, w_ref[:, IN_ATTN + IDX_Q_DIM:], preferred_element_type=F32)
        roped = _rope_chunk(zr, cos, sin_s, lo_half)
        rest = jnp.where(lane < IDX_DIM, roped, zr * IDX_W_SCALE)
        rest_ref[...] = rest
        restb_ref[...] = rest.astype(BF)


def _mod_spec(arr, tm, rows_per_seq):
    if arr.ndim == 3:
        return pl.BlockSpec((None, 1, D_MODEL), lambda i, *_: ((i * tm) // rows_per_seq, 0, 0))
    return pl.BlockSpec((tm, D_MODEL), lambda i, *_: (i, 0))


def _project(x, g, scale, shift, cos_tab, sin_tab, w, rows_per_seq, dsa):
    rows = x.shape[0]
    tm = min(ROW_TILE, rows)
    n_in = w.shape[1]
    tab_tiles = cos_tab.shape[0] // tm
    row_spec = lambda n: pl.BlockSpec((tm, n), lambda i: (i, 0))
    tab_spec = pl.BlockSpec((tm, LANES), lambda i: (i % tab_tiles, 0))
    out_shape = [jax.ShapeDtypeStruct((rows, Q_DIM), BF),
                 jax.ShapeDtypeStruct((rows, KV_DIM), F32), jax.ShapeDtypeStruct((rows, KV_DIM), F32),
                 jax.ShapeDtypeStruct((rows, KV_DIM), BF), jax.ShapeDtypeStruct((rows, KV_DIM), BF)]
    out_specs = [row_spec(Q_DIM), row_spec(KV_DIM), row_spec(KV_DIM), row_spec(KV_DIM), row_spec(KV_DIM)]
    if dsa:
        out_shape += [jax.ShapeDtypeStruct((rows, IDX_Q_DIM), BF),
                      jax.ShapeDtypeStruct((rows, LANES), F32), jax.ShapeDtypeStruct((rows, LANES), BF)]
        out_specs += [row_spec(IDX_Q_DIM), row_spec(LANES), row_spec(LANES)]
    return pl.pallas_call(
        functools.partial(_proj_kernel, dsa=dsa),
        grid=(rows // tm,),
        in_specs=[row_spec(D_MODEL),
                  pl.BlockSpec((1, D_MODEL), lambda i: (0, 0)),
                  _mod_spec(scale, tm, rows_per_seq), _mod_spec(shift, tm, rows_per_seq),
                  tab_spec, tab_spec,
                  pl.BlockSpec((D_MODEL, n_in), lambda i: (0, 0))],
        out_specs=out_specs,
        out_shape=out_shape,
        compiler_params=_params("parallel"),
        name="project_dsa" if dsa else "project",
    )(x, g, scale, shift, cos_tab, sin_tab, w)


def _rope_rows(zt, cos_t, sin_t, n_heads, scale, dtype):
    half = HEAD_DIM // 2
    out = []
    for h in range(n_heads):
        x = zt[h * HEAD_DIM:(h + 1) * HEAD_DIM]
        partner = jnp.concatenate([x[half:], x[:half]], axis=0)
        out.append(((x * cos_t + partner * sin_t) * scale).astype(dtype))
    return out


def _proj_t_kernel(x_ref, g_ref, sc_ref, sh_ref, cos_ref, sin_ref, cost_ref, sint_ref, w_ref, wqt_ref, wkt_ref, wvt_ref,
                   *refs, dsa):
    if dsa:
        wit_ref, wrt_ref = refs[:2]
        refs = refs[2:]
    qt_ref, kt_ref, vtf_ref, kz_ref, vt_ref, kmean_ref = refs[:6]
    hb = _norm_mod(x_ref[...], g_ref[...], sc_ref[...], sh_ref[...]).astype(BF)
    lane = lax.broadcasted_iota(I32, (1, LANES), 1)
    lo_half = (lane % HEAD_DIM) < (HEAD_DIM // 2)

    zqt = lax.dot_general(wqt_ref[...], hb, NT, preferred_element_type=F32)
    cos_t, sin_t = cost_ref[...], sint_ref[...]
    for h, qh in enumerate(_rope_rows(zqt, cos_t, sin_t, N_HEADS, ATTN_SCALE * LOG2E, BF)):
        qt_ref[h * HEAD_DIM:(h + 1) * HEAD_DIM, :] = qh
    zkt = lax.dot_general(wkt_ref[...], hb, NT, preferred_element_type=F32)
    for h, kh in enumerate(_rope_rows(zkt, cos_t, sin_t, N_KV_HEADS, 1.0, F32)):
        kt_ref[h * HEAD_DIM:(h + 1) * HEAD_DIM, :] = kh
    zvt = lax.dot_general(wvt_ref[...], hb, NT, preferred_element_type=F32)
    vtf_ref[...] = zvt
    vt_ref[...] = zvt.astype(BF)
    if dsa:
        qit_ref, wt_ref, kit_ref, kiz_ref = refs[6:]
        zit = lax.dot_general(wit_ref[...], hb, NT, preferred_element_type=F32)
        for h, qh in enumerate(_rope_rows(zit, cos_t, sin_t, IDX_HEADS, 1.0, BF)):
            qit_ref[h * IDX_DIM:(h + 1) * IDX_DIM, :] = qh
        zr = jnp.dot(hb, w_ref[:, IN_ATTN + IDX_Q_DIM:], preferred_element_type=F32)
        kiz_ref[...] = jnp.where(lane < IDX_DIM, _rope_chunk(zr, cos_ref[...], sin_ref[...], lo_half), 0.0).astype(BF)
        zrt = lax.dot_general(wrt_ref[...], hb, NT, preferred_element_type=F32)
        kit_ref[...] = _rope_rows(zrt, cos_t, sin_t, 1, 1.0, F32)[0]
        wt_ref[...] = zrt[IDX_DIM:IDX_DIM + IDX_HEADS] * IDX_W_SCALE
    zk = jnp.dot(hb, w_ref[:, Q_DIM:Q_DIM + KV_DIM], preferred_element_type=F32)
    tm = zk.shape[0]
    for c in range(KV_DIM // LANES):
        sl = slice(c * LANES, (c + 1) * LANES)
        kr = _rope_chunk(zk[:, sl], cos_ref[...], sin_ref[...], lo_half)
        kmean_ref[:, :, sl] = jnp.mean(kr.reshape(tm // MOBA_BLOCK, MOBA_BLOCK, LANES), axis=1, keepdims=True)
        kz_ref[:, 2 * c * LANES:(2 * c + 1) * LANES] = jnp.where(lane < HEAD_DIM, kr, 0.0).astype(BF)
        kz_ref[:, (2 * c + 1) * LANES:(2 * c + 2) * LANES] = jnp.where(
            lane < HEAD_DIM, pltpu.roll(kr, HEAD_DIM, 1), 0.0).astype(BF)


def _project_t(x, g, scale, shift, cos_tab, sin_tab, cos_t, sin_t, w, rows_per_seq, dsa):
    rows = x.shape[0]
    tm = min(ROW_TILE, rows)
    assert rows_per_seq % tm == 0 and tm % MOBA_BLOCK == 0
    tab_tiles = cos_tab.shape[0] // tm
    tiles_per_seq = rows_per_seq // tm
    row_spec = lambda n: pl.BlockSpec((tm, n), lambda i: (i, 0))
    col_spec = lambda n: pl.BlockSpec((n, tm), lambda i: (0, i))
    seq_spec = lambda n: pl.BlockSpec((None, n, tm), lambda i: (i // tiles_per_seq, 0, i % tiles_per_seq))
    seq_shape = lambda n: jax.ShapeDtypeStruct((rows // rows_per_seq, n, rows_per_seq), F32)
    tab_spec = pl.BlockSpec((tm, LANES), lambda i: (i % tab_tiles, 0))
    tabt_spec = pl.BlockSpec((HEAD_DIM, tm), lambda i: (0, i % tab_tiles))
    whole = lambda a: pl.BlockSpec(a.shape, lambda i: (0, 0))
    weights = [w, w[:, :Q_DIM].T, w[:, Q_DIM:Q_DIM + KV_DIM].T, w[:, Q_DIM + KV_DIM:IN_ATTN].T]
    out_specs = [col_spec(Q_DIM), seq_spec(KV_DIM), seq_spec(KV_DIM), row_spec(2 * KV_DIM), col_spec(KV_DIM),
                 pl.BlockSpec((tm // MOBA_BLOCK, 1, KV_DIM), lambda i: (i, 0, 0))]
    out_shape = [jax.ShapeDtypeStruct((Q_DIM, rows), BF), seq_shape(KV_DIM), seq_shape(KV_DIM),
                 jax.ShapeDtypeStruct((rows, 2 * KV_DIM), BF), jax.ShapeDtypeStruct((KV_DIM, rows), BF),
                 jax.ShapeDtypeStruct((rows // MOBA_BLOCK, 1, KV_DIM), F32)]
    if dsa:
        weights += [w[:, IN_ATTN:IN_ATTN + IDX_Q_DIM].T, w[:, IN_ATTN + IDX_Q_DIM:].T]
        out_specs[4] = pl.BlockSpec((None, KV_DIM, tm), lambda i: (i, 0, 0))
        out_shape[4] = jax.ShapeDtypeStruct((rows // tm, KV_DIM, tm), BF)
        out_specs += [col_spec(IDX_Q_DIM), col_spec(IDX_HEADS), seq_spec(IDX_DIM), row_spec(LANES)]
        out_shape += [jax.ShapeDtypeStruct((IDX_Q_DIM, rows), BF), jax.ShapeDtypeStruct((IDX_HEADS, rows), F32),
                      seq_shape(IDX_DIM), jax.ShapeDtypeStruct((rows, LANES), BF)]
    return pl.pallas_call(
        functools.partial(_proj_t_kernel, dsa=dsa),
        grid=(rows // tm,),
        in_specs=[row_spec(D_MODEL),
                  pl.BlockSpec((1, D_MODEL), lambda i: (0, 0)),
                  _mod_spec(scale, tm, rows_per_seq), _mod_spec(shift, tm, rows_per_seq),
                  tab_spec, tab_spec, tabt_spec, tabt_spec] + [whole(a) for a in weights],
        out_specs=out_specs,
        out_shape=out_shape,
        compiler_params=_params("parallel"),
        name="project_t_dsa" if dsa else "project_t",
    )(x, g, scale, shift, cos_tab, sin_tab, cos_t, sin_t, *weights)


def _post_kernel(x_ref, o_ref, wo_ref, ga_ref, g_ref, sc_ref, sh_ref, gm_ref, w1_ref, w2_ref, y_ref,
                 x1_scr, h_scr, acc_scr):
    j = pl.program_id(1)

    @pl.when(j == 0)
    def _():
        x1 = x_ref[...] + ga_ref[...] * jnp.dot(o_ref[...], wo_ref[...], preferred_element_type=F32)
        x1_scr[...] = x1
        h_scr[...] = _norm_mod(x1, g_ref[...], sc_ref[...], sh_ref[...]).astype(BF)
        acc_scr[...] = jnp.zeros_like(acc_scr)

    a = jnp.maximum(jnp.dot(h_scr[...], w1_ref[...], preferred_element_type=F32), 0.0)
    acc_scr[...] += jnp.dot((a * a).astype(BF), w2_ref[...], preferred_element_type=F32)

    @pl.when(j == pl.num_programs(1) - 1)
    def _():
        y_ref[...] = x1_scr[...] + gm_ref[...] * acc_scr[...]


def _post(x, o, w_out, gate_a, g_mlp, scale_m, shift_m, gate_m, w1, w2, rows_per_seq):
    rows = x.shape[0]
    tm = min(ROW_TILE, rows)
    tf = FF_TILE
    ms = lambda a: _mod_spec(a, tm, rows_per_seq)
    return pl.pallas_call(
        _post_kernel,
        grid=(rows // tm, D_FF // tf),
        in_specs=[pl.BlockSpec((tm, D_MODEL), lambda i, j: (i, 0)),
                  pl.BlockSpec((tm, Q_DIM), lambda i, j: (i, 0)),
                  pl.BlockSpec((Q_DIM, D_MODEL), lambda i, j: (0, 0)),
                  ms(gate_a),
                  pl.BlockSpec((1, D_MODEL), lambda i, j: (0, 0)),
                  ms(scale_m), ms(shift_m), ms(gate_m),
                  pl.BlockSpec((D_MODEL, tf), lambda i, j: (0, j)),
                  pl.BlockSpec((tf, D_MODEL), lambda i, j: (j, 0))],
        out_specs=pl.BlockSpec((tm, D_MODEL), lambda i, j: (i, 0)),
        out_shape=jax.ShapeDtypeStruct((rows, D_MODEL), F32),
        scratch_shapes=[pltpu.VMEM((tm, D_MODEL), F32), pltpu.VMEM((tm, D_MODEL), BF),
                        pltpu.VMEM((tm, D_MODEL), F32)],
        compiler_params=_params("parallel", "arbitrary"),
        name="post",
    )(x, o, w_out, gate_a, g_mlp, scale_m, shift_m, gate_m, w1, w2)


def _final_norm_kernel(x_ref, g_ref, y_ref):
    x = x_ref[...]
    y_ref[...] = x * lax.rsqrt(jnp.mean(x * x, axis=-1, keepdims=True) + NORM_EPS) * g_ref[...]


def _final_norm(x, g):
    rows = x.shape[0]
    tm = min(ROW_TILE, rows)
    return pl.pallas_call(
        _final_norm_kernel,
        grid=(rows // tm,),
        in_specs=[pl.BlockSpec((tm, D_MODEL), lambda i: (i, 0)), pl.BlockSpec((1, D_MODEL), lambda i: (0, 0))],
        out_specs=pl.BlockSpec((tm, D_MODEL), lambda i: (i, 0)),
        out_shape=jax.ShapeDtypeStruct((rows, D_MODEL), F32),
        compiler_params=_params("parallel"),
        name="final_norm",
    )(x, g)


def _top_mask(gs, ids, n_sel, axis):
    picked = None
    for _ in range(n_sel):
        mx = jnp.max(gs, axis=axis, keepdims=True)
        am = jnp.min(jnp.where(gs == mx, ids, 2 ** 30), axis=axis, keepdims=True)
        hit = (ids == am) & (mx > NEG_INF)
        picked = hit if picked is None else (picked | hit)
        gs = jnp.where(ids == am, NEG_INF, gs)
    return picked


MOBA_ACC_ROWS = HEAD_DIM + 16
MOBA_MASKED = -1e30

def _moba_prompt_kernel(qi_tab, kp_tab, last_tab, qt_ref, kz_own_ref, vt_own_ref, kz_pair_ref, vt_pair_ref, kmz_ref,
                        o_ref, qz_scr, m_scr, acc_scr, *, n_sel):
    p = pl.program_id(1)
    i = qi_tab[p]
    kp = kp_tab[p]
    blk = MOBA_BLOCK

    @pl.when(kp < 0)
    def _():
        blocks = lax.broadcasted_iota(I32, (LANES - HEAD_DIM, blk), 0)
        earlier = blocks < i
        for h in range(N_HEADS):
            qt = qt_ref[h * HEAD_DIM:(h + 1) * HEAD_DIM, :]
            gs = jnp.dot(kmz_ref[h // GROUP, HEAD_DIM:, :], qt.astype(F32), precision=lax.Precision.HIGHEST,
                         preferred_element_type=F32)
            picked = _top_mask(jnp.where(earlier, gs, NEG_INF), blocks, n_sel, 0)
            usable = (earlier & picked) | (blocks == i)
            bias = jnp.where(usable, 0.0, MOBA_MASKED)
            qz_scr[h // GROUP, :, (h % GROUP) * blk:(h % GROUP + 1) * blk] = jnp.concatenate(
                [qt, bias.astype(BF)], axis=0)
        m_scr[...] = jnp.full(m_scr.shape, NEG_INF, F32)
        acc_scr[...] = jnp.zeros_like(acc_scr)

    def step(kz_ref, vt_ref, n_keys, block_of_key, causal):
        lane = lax.broadcasted_iota(I32, (n_keys, LANES), 1)
        ones_rows = (lax.broadcasted_iota(I32, (MOBA_ACC_ROWS - HEAD_DIM, n_keys), 0) == 0).astype(BF)
        sts = []
        for kv in range(N_KV_HEADS):
            kz = jnp.where(lane == HEAD_DIM + block_of_key, 1.0, kz_ref[:, kv * LANES:(kv + 1) * LANES]).astype(BF)
            sts.append(jnp.dot(kz, qz_scr[kv], preferred_element_type=F32))
        es, alphas = [], []
        for kv in range(N_KV_HEADS):
            st = sts[kv]
            if causal:
                key_i = lax.broadcasted_iota(I32, (n_keys, GROUP * blk), 0)
                qry_i = lax.broadcasted_iota(I32, (n_keys, GROUP * blk), 1) % blk
                st = jnp.where(key_i <= qry_i, st, NEG_INF)
            m_old = m_scr[kv]
            m_new = jnp.maximum(m_old, jnp.max(st, axis=0, keepdims=True))
            alphas.append(jnp.exp2(m_old - m_new))
            es.append(jnp.exp2(st - m_new).astype(BF))
            m_scr[kv] = m_new
        for kv in range(N_KV_HEADS):
            vaug = jnp.concatenate([vt_ref[kv * HEAD_DIM:(kv + 1) * HEAD_DIM, :], ones_rows], axis=0)
            acc_scr[kv] = alphas[kv] * acc_scr[kv] + jnp.dot(vaug, es[kv], preferred_element_type=F32)

    @pl.when(kp < 0)
    def _():
        step(kz_own_ref, vt_own_ref, blk, i, True)

    @pl.when(kp >= 0)
    def _():
        first = 2 * kp
        second = jnp.where(first + 1 < i, first + 1, i + 1)
        key_row = lax.broadcasted_iota(I32, (2 * blk, 1), 0)
        step(kz_pair_ref, vt_pair_ref, 2 * blk, jnp.where(key_row < blk, first, second), False)

    @pl.when(last_tab[p] == 1)
    def _():
        for c in range(N_HEADS // 2):
            halves = []
            for h in (2 * c, 2 * c + 1):
                a = acc_scr[h // GROUP, :, (h % GROUP) * blk:(h % GROUP + 1) * blk]
                halves.append(a[:HEAD_DIM] * (1.0 / a[HEAD_DIM:HEAD_DIM + 1]))
            o_ref[:, c * LANES:(c + 1) * LANES] = jnp.concatenate(halves, axis=0).T.astype(BF)


def _moba_prompt(qt, kz, vt, kmeans, batch):
    rows = qt.shape[1]
    seq = rows // batch
    blk = MOBA_BLOCK
    assert seq % blk == 0
    nblk = seq // blk
    assert HEAD_DIM + nblk + 1 <= LANES and nblk % 2 == 0
    n_sel = min(MOBA_TOPK, nblk)
    kmean = kmeans.reshape(batch, nblk, N_KV_HEADS, HEAD_DIM).transpose(0, 2, 1, 3)
    kmz = jnp.pad(kmean, ((0, 0), (0, 0), (HEAD_DIM, LANES - HEAD_DIM - nblk), (0, 0)))
    qi, kp, last = [], [], []
    for i in range(nblk):
        order = [-1] + list(range((i + 1) // 2))
        qi += [i] * len(order)
        kp += order
        last += [0] * (len(order) - 1) + [1]
    tabs = [jnp.asarray(np.array(t, np.int32)) for t in (qi, kp, last)]
    npair = nblk // 2
    own_rows = lambda b, p, qi, kp, la: (b * nblk + qi[p], 0)
    own_cols = lambda b, p, qi, kp, la: (0, b * nblk + qi[p])
    grid_spec = pltpu.PrefetchScalarGridSpec(
        num_scalar_prefetch=3,
        grid=(batch, len(qi)),
        in_specs=[pl.BlockSpec((Q_DIM, blk), own_cols),
                  pl.BlockSpec((blk, 2 * KV_DIM), own_rows),
                  pl.BlockSpec((KV_DIM, blk), own_cols),
                  pl.BlockSpec((2 * blk, 2 * KV_DIM), lambda b, p, qi, kp, la: (b * npair + jnp.maximum(kp[p], 0), 0)),
                  pl.BlockSpec((KV_DIM, 2 * blk), lambda b, p, qi, kp, la: (0, b * npair + jnp.maximum(kp[p], 0))),
                  pl.BlockSpec((None, N_KV_HEADS, LANES, HEAD_DIM), lambda b, p, qi, kp, la: (b, 0, 0, 0))],
        out_specs=pl.BlockSpec((blk, Q_DIM), own_rows),
        scratch_shapes=[pltpu.VMEM((N_KV_HEADS, LANES, GROUP * blk), BF), pltpu.VMEM((N_KV_HEADS, 1, GROUP * blk), F32),
                        pltpu.VMEM((N_KV_HEADS, MOBA_ACC_ROWS, GROUP * blk), F32)])
    return pl.pallas_call(
        functools.partial(_moba_prompt_kernel, n_sel=n_sel),
        grid_spec=grid_spec,
        out_shape=jax.ShapeDtypeStruct((rows, Q_DIM), BF),
        compiler_params=_params("parallel", "arbitrary"),
        name="moba_prompt",
    )(*tabs, qt, kz, vt, kz, vt, kmz)


def _float_key(x):
    b = pltpu.bitcast(x, I32)
    return jnp.where(b < 0, b ^ 0x7FFFFFFF, b)


def _kth_largest_key(count_ge, k_top, shape, bits_per_step=1):
    assert 32 % bits_per_step == 0
    def body(it, cur):
        shift = 32 - bits_per_step * (it + 1)
        best = cur
        for v in range(1, 2 ** bits_per_step):
            cand = cur | jnp.left_shift(jnp.int32(v), shift)
            best = jnp.where(count_ge(cand ^ INT_MIN) >= k_top, cand, best)
        return best
    return lax.fori_loop(0, 32 // bits_per_step, body, jnp.zeros(shape, I32)) ^ INT_MIN


def _tie_cutoff(count_eq_below, need, n_bits, shape):
    def body(it, cur):
        cand = cur | jnp.left_shift(jnp.int32(1), n_bits - 1 - it)
        return jnp.where(count_eq_below(cand) < need, cand, cur)
    return lax.fori_loop(0, n_bits, body, jnp.zeros(shape, I32))


def _swap_halves(x):
    return jnp.concatenate([x[HEAD_DIM:], x[:HEAD_DIM]], axis=0)


def _dsa_prompt_kernel(qt_ref, qit_ref, wt_ref, kiz_ref, kz_ref, vt_ref, o_ref,
                       qp_scr, qip_scr, key_scr, m_scr, acc_scr, *, k_top, n_bits):
    i = pl.program_id(1)
    tq, ch = DSA_QUERY_TILE, DSA_KEY_CHUNK
    n_ch = ((i + 1) * tq + ch - 1) // ch
    qpos = i * tq + lax.broadcasted_iota(I32, (1, tq), 1)
    key_row = lax.broadcasted_iota(I32, (ch, tq), 0)
    ones_rows = (lax.broadcasted_iota(I32, (MOBA_ACC_ROWS - HEAD_DIM, ch), 0) == 0).astype(BF)

    for c in range(IDX_HEADS // 2):
        pair = qit_ref[c * LANES:(c + 1) * LANES, :]
        qip_scr[0, :, c * tq:(c + 1) * tq] = pair
        qip_scr[1, :, c * tq:(c + 1) * tq] = _swap_halves(pair)
    for c in range(N_HEADS // 2):
        pair = qt_ref[c * LANES:(c + 1) * LANES, :]
        cols = slice((c % 2) * tq, (c % 2 + 1) * tq)
        qp_scr[2 * (c // 2), :, cols] = pair
        qp_scr[2 * (c // 2) + 1, :, cols] = _swap_halves(pair)

    def score_body(c, carry):
        kiz = kiz_ref[pl.ds(pl.multiple_of(c * ch, ch), ch), :]
        sc = jnp.zeros((ch, tq), F32)
        for second in range(2):
            d = jnp.dot(kiz, qip_scr[second], preferred_element_type=F32)
            for u in range(IDX_HEADS // 2):
                h = 2 * u + second
                sc = sc + wt_ref[h:h + 1, :] * jnp.maximum(d[:, u * tq:(u + 1) * tq], 0.0)
        sc = jnp.where(c * ch + key_row <= qpos, sc, NEG_INF)
        key_scr[c] = _float_key(sc)
        return carry
    lax.fori_loop(0, n_ch, score_body, 0)

    def count(pred):
        def body(c, acc):
            return acc + jnp.sum(pred(key_scr[c], c * ch + key_row).astype(I32), axis=0, keepdims=True)
        return lax.fori_loop(0, n_ch, body, jnp.zeros((1, tq), I32))

    thr = _kth_largest_key(lambda t: count(lambda key, idx: key >= t), k_top, (1, tq))
    n_gt = count(lambda key, idx: key > thr)
    n_ge = count(lambda key, idx: key >= thr)
    need = k_top - n_gt
    tie_cols = ((n_ge > k_top) & (thr != KEY_NEG_INF)).astype(I32)
    cutoff = lax.cond(
        jnp.max(tie_cols) > 0,
        lambda: _tie_cutoff(lambda c_: count(lambda key, idx: (key == thr) & (idx < c_)), need, n_bits, (1, tq)),
        lambda: jnp.full((1, tq), 2 ** 30, I32))

    m_scr[...] = jnp.full(m_scr.shape, NEG_INF, F32)
    acc_scr[...] = jnp.zeros_like(acc_scr)

    def attn_body(c, carry):
        key = key_scr[c]
        idx = c * ch + key_row
        keep = ((key > thr) | ((key == thr) & (idx <= cutoff))) & (idx <= qpos)
        bias = jnp.where(keep, 0.0, NEG_INF)
        bias = jnp.concatenate([bias, bias], axis=1)
        rows = pl.ds(pl.multiple_of(c * ch, ch), ch)
        sts = []
        for kv in range(N_KV_HEADS):
            kz = kz_ref[rows, kv * LANES:(kv + 1) * LANES]
            for second in range(2):
                sts.append(jnp.dot(kz, qp_scr[2 * kv + second], preferred_element_type=F32))
        for kv in range(N_KV_HEADS):
            vaug = jnp.concatenate([vt_ref[c, kv * HEAD_DIM:(kv + 1) * HEAD_DIM, :], ones_rows], axis=0)
            for second in range(2):
                g = 2 * kv + second
                st = sts[g] + bias
                m_old = m_scr[g]
                m_new = jnp.maximum(m_old, jnp.max(st, axis=0, keepdims=True))
                m_safe = jnp.where(m_new == NEG_INF, 0.0, m_new)
                alpha = jnp.exp2(m_old - m_safe)
                e = jnp.exp2(st - m_safe).astype(BF)
                acc_scr[g] = alpha * acc_scr[g] + jnp.dot(vaug, e, preferred_element_type=F32)
                m_scr[g] = m_new
        return carry
    lax.fori_loop(0, n_ch, attn_body, 0)

    for kv in range(N_KV_HEADS):
        outs = []
        for second in range(2):
            a = acc_scr[2 * kv + second]
            outs.append(a[:HEAD_DIM] * (1.0 / a[HEAD_DIM:HEAD_DIM + 1]))
        for u in range(2):
            both = jnp.concatenate([outs[0][:, u * tq:(u + 1) * tq], outs[1][:, u * tq:(u + 1) * tq]], axis=0)
            o_ref[:, (2 * kv + u) * LANES:(2 * kv + u + 1) * LANES] = both.T.astype(BF)


def _dsa_prompt(qt, qit, wt, kiz, kz, vt3, batch):
    rows = qt.shape[1]
    seq = rows // batch
    tq, ch = DSA_QUERY_TILE, DSA_KEY_CHUNK
    k_top = min(DSA_TOPK_MAX, seq // 4)
    assert seq % ch == 0 and ch >= k_top and ch % tq == 0 and tq % LANES == 0 and vt3.shape[2] == ch
    nq, n_chunks = seq // tq, seq // ch
    n_bits = max(1, (seq - 1).bit_length())
    tile = lambda n: pl.BlockSpec((n, tq), lambda b, i: (0, b * nq + i))
    return pl.pallas_call(
        functools.partial(_dsa_prompt_kernel, k_top=k_top, n_bits=n_bits),
        grid=(batch, nq),
        in_specs=[tile(Q_DIM), tile(IDX_Q_DIM), tile(IDX_HEADS),
                  pl.BlockSpec((seq, LANES), lambda b, i: (b, 0), pipeline_mode=pl.Buffered(1)),
                  pl.BlockSpec((seq, 2 * KV_DIM), lambda b, i: (b, 0), pipeline_mode=pl.Buffered(1)),
                  pl.BlockSpec((n_chunks, KV_DIM, ch), lambda b, i: (b, 0, 0), pipeline_mode=pl.Buffered(1))],
        out_specs=pl.BlockSpec((tq, Q_DIM), lambda b, i: (b * nq + i, 0)),
        out_shape=jax.ShapeDtypeStruct((rows, Q_DIM), BF),
        scratch_shapes=[pltpu.VMEM((2 * N_KV_HEADS, LANES, 2 * tq), BF),
                        pltpu.VMEM((2, LANES, (IDX_HEADS // 2) * tq), BF),
                        pltpu.VMEM((n_chunks, ch, tq), I32),
                        pltpu.VMEM((2 * N_KV_HEADS, 1, 2 * tq), F32),
                        pltpu.VMEM((2 * N_KV_HEADS, MOBA_ACC_ROWS, 2 * tq), F32)],
        compiler_params=_params("parallel", "arbitrary"),
        name="dsa_prompt",
    )(qt, qit, wt, kiz, kz, vt3)


def _swa_prompt_kernel(qt_ref, kzp_ref, kzc_ref, vtp_ref, vtc_ref, sink_ref, o_ref):
    n = pl.program_id(1)
    w = WINDOW
    key_i = lax.broadcasted_iota(I32, (2 * w, w), 0)
    qry_i = lax.broadcasted_iota(I32, (2 * w, w), 1)
    keep = (key_i >= qry_i) & (key_i <= qry_i + w) & ((key_i >= w) | (n > 0))
    bias = jnp.where(keep, 0.0, NEG_INF)
    bias = jnp.concatenate([bias, bias], axis=1)
    ones_rows = (lax.broadcasted_iota(I32, (MOBA_ACC_ROWS - HEAD_DIM, 2 * w), 0) == 0).astype(BF)
    for kv in range(N_KV_HEADS):
        lanes = slice(kv * LANES, (kv + 1) * LANES)
        feats = slice(kv * HEAD_DIM, (kv + 1) * HEAD_DIM)
        kz = jnp.concatenate([kzp_ref[:, lanes], kzc_ref[:, lanes]], axis=0)
        vaug = jnp.concatenate([jnp.concatenate([vtp_ref[feats, :], vtc_ref[feats, :]], axis=1), ones_rows], axis=0)
        pairs = [qt_ref[(2 * kv + u) * LANES:(2 * kv + u + 1) * LANES, :] for u in range(2)]
        outs = []
        for second in range(2):
            ops = pairs if second == 0 else [_swap_halves(p) for p in pairs]
            st = jnp.dot(kz, jnp.concatenate(ops, axis=1), preferred_element_type=F32) + bias
            sink = sink_ref[2 * kv + second]
            m = jnp.maximum(jnp.max(st, axis=0, keepdims=True), sink)
            acc = jnp.dot(vaug, jnp.exp2(st - m).astype(BF), preferred_element_type=F32)
            outs.append(acc[:HEAD_DIM] * (1.0 / (acc[HEAD_DIM:HEAD_DIM + 1] + jnp.exp2(sink - m))))
        for u in range(2):
            both = jnp.concatenate([outs[0][:, u * w:(u + 1) * w], outs[1][:, u * w:(u + 1) * w]], axis=0)
            o_ref[:, (2 * kv + u) * LANES:(2 * kv + u + 1) * LANES] = both.T.astype(BF)


def _swa_prompt(qt, kz, vt, sinks, batch):
    rows = qt.shape[1]
    seq = rows // batch
    w = WINDOW
    assert seq % w == 0 and w == LANES
    nb = seq // w
    cur = lambda b, n: b * nb + n
    prev = lambda b, n: b * nb + jnp.maximum(n - 1, 0)
    sk = sinks.astype(F32).reshape(N_KV_HEADS, 2, 2)
    sink_cols = jnp.repeat(sk.transpose(0, 2, 1) * LOG2E, w, axis=2).reshape(2 * N_KV_HEADS, 1, 2 * w)
    return pl.pallas_call(
        _swa_prompt_kernel,
        grid=(batch, nb),
        in_specs=[pl.BlockSpec((Q_DIM, w), lambda b, n: (0, cur(b, n))),
                  pl.BlockSpec((w, 2 * KV_DIM), lambda b, n: (prev(b, n), 0)),
                  pl.BlockSpec((w, 2 * KV_DIM), lambda b, n: (cur(b, n), 0)),
                  pl.BlockSpec((KV_DIM, w), lambda b, n: (0, prev(b, n))),
                  pl.BlockSpec((KV_DIM, w), lambda b, n: (0, cur(b, n))),
                  pl.BlockSpec((2 * N_KV_HEADS, 1, 2 * w), lambda b, n: (0, 0, 0))],
        out_specs=pl.BlockSpec((w, Q_DIM), lambda b, n: (cur(b, n), 0)),
        out_shape=jax.ShapeDtypeStruct((rows, Q_DIM), BF),
        compiler_params=_params("parallel", "parallel"),
        name="swa_prompt",
    )(qt, kz, kz, vt, vt, sink_cols)


def _row_token(n_rows, n_tok):
    r = lax.broadcasted_iota(I32, (n_rows, 1), 0)
    return (r // GROUP) % n_tok


def _block_diag_queries(q, n_seq, n_tok):
    qr = q.reshape(n_seq, n_tok, N_KV_HEADS, GROUP, HEAD_DIM).transpose(0, 2, 1, 3, 4)
    qr = qr.reshape(n_seq, N_KV_HEADS, n_tok * GROUP, HEAD_DIM)
    eye = jnp.eye(N_KV_HEADS, dtype=q.dtype)
    qbd = qr[:, :, :, None, :] * eye[None, :, None, :, None]
    return qbd.reshape(n_seq, N_KV_HEADS * n_tok * GROUP, KV_DIM)


def _undiag_outputs(out, n_seq, n_tok):
    o = out.reshape(n_seq, N_KV_HEADS, n_tok, GROUP, N_KV_HEADS, HEAD_DIM)
    o = jnp.stack([o[:, kv, :, :, kv] for kv in range(N_KV_HEADS)], axis=1)
    return o.transpose(0, 2, 1, 3, 4).reshape(n_seq * n_tok, Q_DIM)


def _tokens_on_lanes(cache):
    if cache.ndim == 5:
        t = jnp.transpose(cache, (0, 1, 3, 4, 2))
        return t.reshape(t.shape[:2] + (t.shape[2] * t.shape[3], t.shape[4]))
    return jnp.transpose(cache, (0, 1, 3, 2))


def _page_specs(n_pages_per_step, slot, width):
    def spec(r):
        return pl.BlockSpec((None, None, width, PAGE_SIZE),
                            lambda b, s, pt: (slot, pt[b, s * n_pages_per_step + r], 0, 0))
    return [spec(r) for r in range(n_pages_per_step)]


def _pad_rows(a, n_seq, n_tok):
    a = a.reshape(n_seq, n_tok, a.shape[-1])
    return jnp.pad(a, ((0, 0), (0, SUBLANES - n_tok), (0, 0)))


def _moba_sample_kernel(pt_ref, q_ref, kn_ref, vn_ref, *refs, n_tok, n_sel, pps):
    kt_pages = refs[:pps]
    vt_pages = refs[pps:2 * pps]
    o_ref = refs[2 * pps]
    m_scr, l_scr, g_scr, o_scr = refs[2 * pps + 1:]
    s = pl.program_id(1)
    n_steps = m_scr.shape[0]
    bps = pps // 2
    q = q_ref[...]
    n_row = q.shape[0]
    m_scr[s] = jnp.full(m_scr.shape[1:], NEG_INF, F32)
    l_scr[s] = jnp.zeros(l_scr.shape[1:], F32)
    g_scr[s] = jnp.full(g_scr.shape[1:], NEG_INF, F32)
    kt_all = jnp.concatenate([p[...] for p in kt_pages], axis=1).astype(BF)
    sc_all = jnp.dot(q, kt_all, preferred_element_type=F32)
    for r in range(bps):
        vt = jnp.concatenate([vt_pages[2 * r][...], vt_pages[2 * r + 1][...]], axis=1).astype(BF)
        sc = sc_all[:, r * MOBA_BLOCK:(r + 1) * MOBA_BLOCK]
        m = jnp.max(sc, axis=1, keepdims=True)
        e = jnp.exp(sc - m)
        m_scr[s, :, r:r + 1] = m
        l_scr[s, :, r:r + 1] = jnp.sum(e, axis=1, keepdims=True)
        g_scr[s, :, r:r + 1] = jnp.sum(sc, axis=1, keepdims=True)
        o_scr[s * bps + r] = lax.dot_general(e.astype(BF), vt, NT, preferred_element_type=F32)

    @pl.when(s == pl.num_programs(1) - 1)
    def _():
        lane = lax.broadcasted_iota(I32, (n_row, LANES), 1)
        ids = [t * bps + lane for t in range(n_steps)]
        gs = [g_scr[t] for t in range(n_steps)]
        picked = [None] * n_steps
        for _ in range(n_sel):
            mx = functools.reduce(jnp.maximum, [jnp.max(g, axis=1, keepdims=True) for g in gs])
            am = functools.reduce(jnp.minimum, [jnp.min(jnp.where(g == mx, i_, 2 ** 30), axis=1, keepdims=True)
                                                for g, i_ in zip(gs, ids)])
            for t in range(n_steps):
                hit = (ids[t] == am) & (mx > NEG_INF)
                picked[t] = hit if picked[t] is None else (picked[t] | hit)
                gs[t] = jnp.where(ids[t] == am, NEG_INF, gs[t])
        own_ok = lax.broadcasted_iota(I32, (n_row, SUBLANES), 1) <= _row_token(n_row, n_tok)
        s_own = jnp.where(own_ok, lax.dot_general(q, kn_ref[...], NT, preferred_element_type=F32), NEG_INF)
        m_tot = jnp.max(s_own, axis=1, keepdims=True)
        for t in range(n_steps):
            m_tot = jnp.maximum(m_tot, jnp.max(jnp.where(picked[t], m_scr[t], NEG_INF), axis=1, keepdims=True))
        e_own = jnp.exp(s_own - m_tot)
        den = jnp.sum(e_own, axis=1, keepdims=True)
        num = jnp.dot(e_own.astype(BF), vn_ref[...], preferred_element_type=F32)
        for t in range(n_steps):
            wgt = jnp.where(picked[t], jnp.exp(m_scr[t] - m_tot), 0.0)
            den = den + jnp.sum(wgt * l_scr[t], axis=1, keepdims=True)
            for r in range(bps):
                num = num + wgt[:, r:r + 1] * o_scr[t * bps + r]
        o_ref[...] = num * (1.0 / den)


def _paged_attention_call(kernel, name, slot, page_table, cache_kt, cache_vt, extra_inputs, extra_specs,
                          scratch_shapes, n_row):
    n_seq, n_pages = page_table.shape
    pps = PAGES_PER_STEP
    assert n_pages % pps == 0
    grid_spec = pltpu.PrefetchScalarGridSpec(
        num_scalar_prefetch=1,
        grid=(n_seq, n_pages // pps),
        in_specs=extra_specs + _page_specs(pps, slot, KV_DIM) + _page_specs(pps, slot, KV_DIM),
        out_specs=pl.BlockSpec((None, n_row, KV_DIM), lambda b, s, pt: (b, 0, 0)),
        scratch_shapes=scratch_shapes)
    return pl.pallas_call(
        kernel,
        grid_spec=grid_spec,
        out_shape=jax.ShapeDtypeStruct((n_seq, n_row, KV_DIM), F32),
        compiler_params=_params("parallel", "arbitrary"),
        name=name,
    )(page_table, *extra_inputs, *([cache_kt] * pps), *([cache_vt] * pps))


def _moba_sample(qbd, kn_pad, vn_pad, cache_kt, cache_vt, slot, page_table, n_tok):
    n_seq, n_pages = page_table.shape
    past = n_pages * PAGE_SIZE
    assert past % MOBA_BLOCK == 0 and MOBA_BLOCK == 2 * PAGE_SIZE and PAGES_PER_STEP % 2 == 0
    nblk = past // MOBA_BLOCK
    n_steps = n_pages // PAGES_PER_STEP
    n_row = qbd.shape[1]
    per_seq = lambda w: pl.BlockSpec((None, w[0], w[1]), lambda b, s, pt: (b, 0, 0))
    kernel = functools.partial(_moba_sample_kernel, n_tok=n_tok, n_sel=min(MOBA_TOPK, nblk), pps=PAGES_PER_STEP)
    stats = pltpu.VMEM((n_steps, n_row, LANES), F32)
    return _paged_attention_call(
        kernel, "moba_sample", slot, page_table, cache_kt, cache_vt,
        [qbd, kn_pad, vn_pad],
        [per_seq((n_row, KV_DIM)), per_seq((SUBLANES, KV_DIM)), per_seq((SUBLANES, KV_DIM))],
        [stats, stats, stats, pltpu.VMEM((nblk, n_row, KV_DIM), F32)],
        n_row)


def _dsa_index_kernel(pt_ref, qi_ref, w_ref, kin_ref, *refs, n_tok, k_top, ppi, n_bits):
    ki_pages = refs[:ppi]
    mask_ref = refs[ppi]
    key_scr = refs[ppi + 1]
    s = pl.program_id(1)
    n_steps = pl.num_programs(1)
    n_pages = n_steps * ppi
    qi = qi_ref[...]
    w = w_ref[...]
    fill = jnp.full((SUBLANES - n_tok, LANES), NEG_INF, F32)

    def scores(d):
        wd = (w * jnp.maximum(d, 0.0)).reshape(n_tok, IDX_HEADS, LANES)
        return jnp.sum(wd, axis=1)

    for r in range(ppi):
        sc = scores(jnp.dot(qi, ki_pages[r][...].astype(BF), preferred_element_type=F32))
        key_scr[s * ppi + r] = _float_key(jnp.concatenate([sc, fill], axis=0))

    @pl.when(s == n_steps - 1)
    def _():
        tok = lax.broadcasted_iota(I32, (n_tok, LANES), 0)
        lane_t = lax.broadcasted_iota(I32, (n_tok, LANES), 1)
        sc_new = scores(lax.dot_general(qi, kin_ref[...], NT, preferred_element_type=F32))
        sc_new = jnp.where(lane_t <= tok, sc_new, NEG_INF)
        key_scr[n_pages] = _float_key(jnp.concatenate([sc_new, fill], axis=0))
        tiles = (n_pages + 1, SUBLANES, LANES)
        idx = lax.broadcasted_iota(I32, tiles, 0) * PAGE_SIZE + lax.broadcasted_iota(I32, tiles, 2)

        def count(hit):
            return jnp.sum(jnp.sum(hit.astype(I32), axis=0), axis=1, keepdims=True)

        thr = _kth_largest_key(lambda t: count(key_scr[...] >= t), k_top, (SUBLANES, 1), bits_per_step=2)
        need = k_top - count(key_scr[...] > thr)
        tie_rows = ((count(key_scr[...] >= thr) > k_top) & (thr != KEY_NEG_INF)).astype(I32)
        cutoff = lax.cond(
            jnp.max(tie_rows) > 0,
            lambda: _tie_cutoff(lambda c_: count((key_scr[...] == thr) & (idx < c_)), need, n_bits, (SUBLANES, 1)),
            lambda: jnp.full((SUBLANES, 1), 2 ** 30, I32))
        keys = key_scr[...]
        keep = (keys > thr) | ((keys == thr) & (idx <= cutoff))
        mask_ref[...] = jnp.where(keep & (keys != KEY_NEG_INF), 1.0, 0.0).astype(BF)


def _dsa_index(qi_rows, w_rows, ki_new, cache_idx, islot, page_table, n_tok):
    n_seq, n_pages = page_table.shape
    ppi = min(INDEX_PAGES_PER_STEP, n_pages)
    total = n_pages * PAGE_SIZE + n_tok
    k_top = min(DSA_TOPK_MAX, total // 4)
    assert n_pages % ppi == 0 and n_tok <= SUBLANES and n_pages * PAGE_SIZE >= k_top
    n_bits = max(1, ((n_pages + 1) * PAGE_SIZE - 1).bit_length())
    nr = n_tok * IDX_HEADS
    grid_spec = pltpu.PrefetchScalarGridSpec(
        num_scalar_prefetch=1,
        grid=(n_seq, n_pages // ppi),
        in_specs=[pl.BlockSpec((None, nr, IDX_DIM), lambda b, s, pt: (b, 0, 0)),
                  pl.BlockSpec((None, nr, 1), lambda b, s, pt: (b, 0, 0)),
                  pl.BlockSpec((None, LANES, IDX_DIM), lambda b, s, pt: (b, 0, 0))]
        + _page_specs(ppi, islot, IDX_DIM),
        out_specs=pl.BlockSpec((None, n_pages + 1, SUBLANES, LANES), lambda b, s, pt: (b, 0, 0, 0)),
        scratch_shapes=[pltpu.VMEM((n_pages + 1, SUBLANES, LANES), I32)])
    return pl.pallas_call(
        functools.partial(_dsa_index_kernel, n_tok=n_tok, k_top=k_top, ppi=ppi, n_bits=n_bits),
        grid_spec=grid_spec,
        out_shape=jax.ShapeDtypeStruct((n_seq, n_pages + 1, SUBLANES, LANES), BF),
        compiler_params=_params("parallel", "arbitrary"),
        name="dsa_index",
    )(page_table, qi_rows, w_rows, ki_new, *([cache_idx] * ppi))


def _dsa_sample_kernel(pt_ref, q_ref, kn_ref, vn_ref, mask_ref, mnew_ref, *refs, n_tok, pps):
    kt_pages = refs[:pps]
    vt_pages = refs[pps:2 * pps]
    o_ref = refs[2 * pps]
    m_scr, l_scr, acc_scr = refs[2 * pps + 1:]
    s = pl.program_id(1)
    q = q_ref[...]
    n_row = q.shape[0]
    expand = (lax.broadcasted_iota(I32, (n_row, SUBLANES), 1) == _row_token(n_row, n_tok)).astype(BF)

    @pl.when(s == 0)
    def _():
        m_scr[...] = jnp.full(m_scr.shape, NEG_INF, F32)
        l_scr[...] = jnp.zeros_like(l_scr)
        acc_scr[...] = jnp.zeros_like(acc_scr)

    def update(sc, keep, pv):
        sc = jnp.where(keep > 0.5, sc, NEG_INF)
        m_old = m_scr[...]
        m_new = jnp.maximum(m_old, jnp.max(sc, axis=1, keepdims=True))
        m_safe = jnp.where(m_new == NEG_INF, 0.0, m_new)
        alpha = jnp.exp(m_old - m_safe)
        e = jnp.exp(sc - m_safe)
        l_scr[...] = alpha * l_scr[...] + jnp.sum(e, axis=1, keepdims=True)
        acc_scr[...] = alpha * acc_scr[...] + pv(e.astype(BF))
        m_scr[...] = m_new

    kt_all = jnp.concatenate([p[...] for p in kt_pages], axis=1).astype(BF)
    sc_all = jnp.dot(q, kt_all, preferred_element_type=F32)
    keep_all = jnp.dot(expand, jnp.concatenate([mask_ref[r] for r in range(pps)], axis=1),
                       preferred_element_type=F32)
    width = DSA_PAGES_PER_UPDATE * PAGE_SIZE
    for u in range(pps // DSA_PAGES_PER_UPDATE):
        rs = range(u * DSA_PAGES_PER_UPDATE, (u + 1) * DSA_PAGES_PER_UPDATE)
        vt = jnp.concatenate([vt_pages[r][...] for r in rs], axis=1).astype(BF)
        cols = slice(u * width, (u + 1) * width)
        update(sc_all[:, cols], keep_all[:, cols], lambda e, vt=vt: lax.dot_general(e, vt, NT, preferred_element_type=F32))

    @pl.when(s == pl.num_programs(1) - 1)
    def _():
        sc = lax.dot_general(q, kn_ref[...], NT, preferred_element_type=F32)
        keep = jnp.dot(expand, mnew_ref[...], preferred_element_type=F32)[:, :SUBLANES]
        update(sc, keep, lambda e: jnp.dot(e, vn_ref[...], preferred_element_type=F32))
        o_ref[...] = acc_scr[...] * (1.0 / l_scr[...])


def _dsa_sample(qbd, kn_pad, vn_pad, mask, cache_kt, cache_vt, slot, page_table, n_tok):
    n_seq, n_pages = page_table.shape
    n_row = qbd.shape[1]
    pps = PAGES_PER_STEP
    per_seq = lambda w: pl.BlockSpec((None, w[0], w[1]), lambda b, s, pt: (b, 0, 0))
    return _paged_attention_call(
        functools.partial(_dsa_sample_kernel, n_tok=n_tok, pps=pps), "dsa_sample", slot, page_table,
        cache_kt, cache_vt,
        [qbd, kn_pad, vn_pad, mask, mask],
        [per_seq((n_row, KV_DIM)), per_seq((SUBLANES, KV_DIM)), per_seq((SUBLANES, KV_DIM)),
         pl.BlockSpec((None, pps, SUBLANES, LANES), lambda b, s, pt: (b, s, 0, 0)),
         pl.BlockSpec((None, None, SUBLANES, LANES), lambda b, s, pt: (b, n_pages, 0, 0))],
        [pltpu.VMEM((n_row, 1), F32), pltpu.VMEM((n_row, 1), F32), pltpu.VMEM((n_row, KV_DIM), F32)],
        n_row)


def _swa_sample_kernel(q_ref, kt_ref, vt_ref, kn_ref, vn_ref, sink_ref, o_ref, *, n_tok):
    q = q_ref[...]
    n_row = q.shape[0]
    tok_r = _row_token(n_row, n_tok)
    wk = kt_ref.shape[1]
    s_buf = jnp.dot(q, kt_ref[...].astype(BF), preferred_element_type=F32)
    s_buf = jnp.where(lax.broadcasted_iota(I32, (n_row, wk), 1) >= tok_r, s_buf, NEG_INF)
    s_new = lax.dot_general(q, kn_ref[...], NT, preferred_element_type=F32)
    s_new = jnp.where(lax.broadcasted_iota(I32, (n_row, SUBLANES), 1) <= tok_r, s_new, NEG_INF)
    sink = sink_ref[...]
    m = jnp.maximum(jnp.maximum(jnp.max(s_buf, axis=1, keepdims=True), jnp.max(s_new, axis=1, keepdims=True)), sink)
    e_buf = jnp.exp(s_buf - m)
    e_new = jnp.exp(s_new - m)
    den = jnp.sum(e_buf, axis=1, keepdims=True) + jnp.sum(e_new, axis=1, keepdims=True) + jnp.exp(sink - m)
    num = (lax.dot_general(e_buf.astype(BF), vt_ref[...].astype(BF), NT, preferred_element_type=F32)
           + jnp.dot(e_new.astype(BF), vn_ref[...], preferred_element_type=F32))
    o_ref[...] = num * (1.0 / den)


def _swa_sample(qbd, kn_pad, vn_pad, buf_kt, buf_vt, sinks, n_tok):
    n_seq, _, wk = buf_kt.shape
    assert wk == WINDOW
    n_row = qbd.shape[1]
    sink_rows = jnp.repeat(sinks.reshape(N_KV_HEADS, 1, GROUP), n_tok, axis=1).reshape(n_row, 1)
    per_seq = lambda a, c: pl.BlockSpec((None, a, c), lambda b: (b, 0, 0))
    return pl.pallas_call(
        functools.partial(_swa_sample_kernel, n_tok=n_tok),
        grid=(n_seq,),
        in_specs=[per_seq(n_row, KV_DIM), per_seq(KV_DIM, wk), per_seq(KV_DIM, wk),
                  per_seq(SUBLANES, KV_DIM), per_seq(SUBLANES, KV_DIM),
                  pl.BlockSpec((n_row, 1), lambda b: (0, 0))],
        out_specs=per_seq(n_row, KV_DIM),
        out_shape=jax.ShapeDtypeStruct((n_seq, n_row, KV_DIM), F32),
        compiler_params=_params("parallel"),
        name="swa_sample",
    )(qbd, buf_kt, buf_vt, kn_pad, vn_pad, sink_rows)


def _rope_tables(pos):
    half = HEAD_DIM // 2
    inv = ROPE_THETA ** (-jnp.arange(half, dtype=F32) / half)
    ang = pos.astype(F32)[:, None] * inv[None, :]
    cos, sin = jnp.cos(ang), jnp.sin(ang)
    reps = LANES // HEAD_DIM
    return jnp.tile(jnp.concatenate([cos, cos], axis=1), (1, reps)), jnp.tile(jnp.concatenate([-sin, sin], axis=1), (1, reps))


def kernel(x_prompt, x_sample, cache_k, cache_v, cache_idx_k, state_swa_k, state_swa_v, page_table, c_prompt, c_sample, g_attn, g_mlp, w_mod, b_mod, w_in_moba, w_in_dsa, w_in_swa, swa_sinks, w_out, w_ff1, w_ff2, g_final):
    batch, seq, _ = x_prompt.shape
    n_seq, n_tok, _ = x_sample.shape
    depth = g_attn.shape[0]
    n_pages = page_table.shape[1]
    past = n_pages * PAGE_SIZE
    rows_p, rows_s = batch * seq, n_seq * n_tok

    n_cond = batch + n_seq
    pad_cond = -n_cond % SUBLANES
    c_all = jnp.pad(jnp.concatenate([c_prompt, c_sample], axis=0), ((0, pad_cond), (0, 0)))
    mod = _adaln(c_all, w_mod, b_mod)

    def mods(layer):
        parts = [mod[layer, :, r * D_MODEL:(r + 1) * D_MODEL] for r in range(6)]
        mp = [p[:batch].reshape(batch, 1, D_MODEL) for p in parts]
        ms = [jnp.repeat(p[batch:n_cond], n_tok, axis=0) for p in parts]
        return mp, ms

    cos_p, sin_p = _rope_tables(jnp.arange(seq, dtype=jnp.int32))
    cos_s, sin_s = _rope_tables(jnp.tile(past + jnp.arange(n_tok, dtype=jnp.int32), n_seq))
    cos_pt, sin_pt = cos_p[:, :HEAD_DIM].T, sin_p[:, :HEAD_DIM].T

    cache_kt, cache_vt, cache_it = _tokens_on_lanes(cache_k), _tokens_on_lanes(cache_v), _tokens_on_lanes(cache_idx_k)
    swa_kt, swa_vt = _tokens_on_lanes(state_swa_k), _tokens_on_lanes(state_swa_v)

    xp = x_prompt.reshape(rows_p, D_MODEL)
    xs = x_sample.reshape(rows_s, D_MODEL)
    kp_l, vp_l, ks_l, vs_l, ip_l, is_l = [], [], [], [], [], []
    skp_l, svp_l, sks_l, svs_l = [], [], [], []
    for i, (mixer, j, slot) in enumerate(_layer_plan(depth)):
        (sh_ap, sc_ap, gt_ap, sh_mp, sc_mp, gt_mp), (sh_as, sc_as, gt_as, sh_ms, sc_ms, gt_ms) = mods(i)
        dsa = mixer == 1
        if mixer == 0:
            w_in = w_in_moba[j]
        elif dsa:
            w_in = jnp.pad(w_in_dsa[j], ((0, 0), (0, -IN_DSA % LANES)))
        else:
            w_in = w_in_swa[j]
        w_in = w_in.astype(BF)
        g_a = g_attn[i].reshape(1, D_MODEL)
        outs_p = _project_t(xp, g_a, sc_ap, sh_ap, cos_p, sin_p, cos_pt, sin_pt, w_in, seq, dsa)
        qt_p, kt_p, vtf_p, kz_p, vt_p, kmeans_p = outs_p[:6]
        outs_s = _project(xs, g_a, sc_as, sh_as, cos_s, sin_s, w_in, n_tok, dsa)
        q_s, k_s, v_s, kb_s, vb_s = outs_s[:5]
        qbd = _block_diag_queries(q_s, n_seq, n_tok)
        kn_pad = _pad_rows(kb_s, n_seq, n_tok)
        vn_pad = _pad_rows(vb_s, n_seq, n_tok)
        k_p5 = kt_p.reshape(batch, N_KV_HEADS, HEAD_DIM, seq).transpose(0, 3, 1, 2)
        v_p5 = vtf_p.reshape(batch, N_KV_HEADS, HEAD_DIM, seq).transpose(0, 3, 1, 2)
        k_s5 = k_s.reshape(n_seq, n_tok, N_KV_HEADS, HEAD_DIM)
        v_s5 = v_s.reshape(n_seq, n_tok, N_KV_HEADS, HEAD_DIM)
        if mixer == 0:
            o_p = _moba_prompt(qt_p, kz_p, vt_p, kmeans_p, batch)
            o_t = _moba_sample(qbd, kn_pad, vn_pad, cache_kt, cache_vt, slot, page_table, n_tok)
        elif dsa:
            qit_p, wt_p, kit_p, kiz_p = outs_p[6:]
            qi_s, rest_s, restb_s = outs_s[5:]
            o_p = _dsa_prompt(qt_p, qit_p, wt_p, kiz_p, kz_p, vt_p, batch)
            qi_rows = qi_s.reshape(n_seq, n_tok * IDX_HEADS, IDX_DIM)
            w_rows = rest_s[:, IDX_DIM:IDX_DIM + IDX_HEADS].reshape(n_seq, n_tok * IDX_HEADS, 1)
            ki_new = jnp.pad(restb_s[:, :IDX_DIM].reshape(n_seq, n_tok, IDX_DIM),
                             ((0, 0), (0, LANES - n_tok), (0, 0)))
            mask = _dsa_index(qi_rows, w_rows, ki_new, cache_it, j, page_table, n_tok)
            o_t = _dsa_sample(qbd, kn_pad, vn_pad, mask, cache_kt, cache_vt, slot, page_table, n_tok)
            ip_l.append(kit_p.transpose(0, 2, 1))
            is_l.append(rest_s[:, :IDX_DIM].reshape(n_seq, n_tok, IDX_DIM))
        else:
            o_p = _swa_prompt(qt_p, kz_p, vt_p, swa_sinks[j], batch)
            o_t = _swa_sample(qbd, kn_pad, vn_pad, swa_kt[j], swa_vt[j], swa_sinks[j], n_tok)
            keep_p = min(WINDOW, seq)
            skp_l.append(k_p5[:, seq - keep_p:])
            svp_l.append(v_p5[:, seq - keep_p:])
            wk = state_swa_k.shape[2]
            sks_l.append(jnp.concatenate([state_swa_k[j], k_s5], axis=1)[:, -wk:])
            svs_l.append(jnp.concatenate([state_swa_v[j], v_s5], axis=1)[:, -wk:])
        if mixer < 2:
            kp_l.append(k_p5)
            vp_l.append(v_p5)
            ks_l.append(k_s5)
            vs_l.append(v_s5)
        o_s = _undiag_outputs(o_t, n_seq, n_tok).astype(BF)
        wo, w1, w2 = w_out[i].astype(BF), w_ff1[i].astype(BF), w_ff2[i].astype(BF)
        g_m = g_mlp[i].reshape(1, D_MODEL)
        xp = _post(xp, o_p, wo, gt_ap, g_m, sc_mp, sh_mp, gt_mp, w1, w2, seq)
        xs = _post(xs, o_s, wo, gt_as, g_m, sc_ms, sh_ms, gt_ms, w1, w2, n_tok)
    g_f = g_final.reshape(1, D_MODEL)
    y_prompt = _final_norm(xp, g_f).reshape(batch, seq, D_MODEL)
    y_sample = _final_norm(xs, g_f).reshape(n_seq, n_tok, D_MODEL)
    return (y_prompt, y_sample, jnp.stack(kp_l), jnp.stack(vp_l), jnp.stack(ks_l), jnp.stack(vs_l),
            jnp.stack(ip_l), jnp.stack(is_l), jnp.stack(skp_l), jnp.stack(svp_l), jnp.stack(sks_l), jnp.stack(svs_l))
```

```python
import functools

import numpy as np
import jax
import jax.numpy as jnp
from jax import lax
from jax.experimental import pallas as pl
from jax.experimental.pallas import tpu as pltpu

D_MODEL = 1024
N_HEADS = 16
HEAD_DIM = D_MODEL // N_HEADS
N_KV_HEADS = 4
GROUP = N_HEADS // N_KV_HEADS
Q_DIM = N_HEADS * HEAD_DIM
KV_DIM = N_KV_HEADS * HEAD_DIM
D_FF = 4 * D_MODEL
ROPE_THETA = 10000.0
NORM_EPS = 1e-6
N_MIXERS = 3
PAGE_SIZE = 128
MOBA_BLOCK = 256
MOBA_TOPK = 3
IDX_HEADS = 8
IDX_DIM = 64
DSA_TOPK_MAX = 256
WINDOW = 128
IN_ATTN = Q_DIM + 2 * KV_DIM
IDX_Q_DIM = IDX_HEADS * IDX_DIM
IN_DSA = IN_ATTN + IDX_Q_DIM + IDX_DIM + IDX_HEADS
ATTN_SCALE = HEAD_DIM ** -0.5
LOG2E = 1.4426950408889634
IDX_W_SCALE = IDX_Q_DIM ** -0.5

LANES = 128
SUBLANES = 8
VMEM_LIMIT = 56 << 20
ROW_TILE = 512
POST_ROW_TILE = 1024
FF_TILE = 1024
DSA_KEY_CHUNK = 512
DSA_QUERY_TILE = 512
PAGES_PER_STEP = 32
INDEX_PAGES_PER_STEP = 64
DSA_PAGES_PER_UPDATE = 4

BF = jnp.bfloat16
F32 = jnp.float32
I32 = jnp.int32
NEG_INF = float("-inf")
INT_MIN = -2 ** 31
KEY_NEG_INF = INT_MIN + 0x7FFFFF
NT = (((1,), (1,)), ((), ()))


def _params(*semantics):
    return pltpu.CompilerParams(dimension_semantics=semantics, vmem_limit_bytes=VMEM_LIMIT)


def _layer_plan(depth):
    plan, counts, n_paged = [], [0] * N_MIXERS, 0
    for i in range(depth):
        m = i % N_MIXERS
        slot = -1
        if m < 2:
            slot = n_paged
            n_paged += 1
        plan.append((m, counts[m], slot))
        counts[m] += 1
    return plan


def _adaln_kernel(c_ref, w_ref, b_ref, o_ref):
    c = c_ref[...]
    a = (c * (1.0 / (1.0 + jnp.exp(-c)))).astype(BF)
    o_ref[...] = jnp.dot(a, w_ref[...].astype(BF), preferred_element_type=F32) + b_ref[...]


def _adaln(c_all, w_mod, b_mod):
    depth, _, n_out = w_mod.shape
    nc = c_all.shape[0]
    tn = 1536
    return pl.pallas_call(
        _adaln_kernel,
        grid=(depth, n_out // tn),
        in_specs=[pl.BlockSpec((nc, D_MODEL), lambda l, j: (0, 0)),
                  pl.BlockSpec((None, D_MODEL, tn), lambda l, j: (l, 0, j)),
                  pl.BlockSpec((None, 1, tn), lambda l, j: (l, 0, j))],
        out_specs=pl.BlockSpec((None, nc, tn), lambda l, j: (l, 0, j)),
        out_shape=jax.ShapeDtypeStruct((depth, nc, n_out), F32),
        compiler_params=_params("parallel", "parallel"),
        name="adaln",
    )(c_all, w_mod, b_mod.reshape(depth, 1, n_out))


def _norm_mod(x, g, scale, shift):
    y = x * lax.rsqrt(jnp.mean(x * x, axis=-1, keepdims=True) + NORM_EPS)
    return (y * g) * (1.0 + scale) + shift


def _rope_chunk(z, cos, sin_signed, lo_half):
    partner = jnp.where(lo_half, pltpu.roll(z, LANES - HEAD_DIM // 2, 1), pltpu.roll(z, HEAD_DIM // 2, 1))
    return z * cos + partner * sin_signed


def _proj_kernel(x_ref, g_ref, sc_ref, sh_ref, cos_ref, sin_ref, w_ref, *out_refs, dsa):
    hb = _norm_mod(x_ref[...], g_ref[...], sc_ref[...], sh_ref[...]).astype(BF)
    cos = cos_ref[...]
    sin_s = sin_ref[...]
    lane = lax.broadcasted_iota(I32, (1, LANES), 1)
    lo_half = (lane % HEAD_DIM) < (HEAD_DIM // 2)
    q_ref, k_ref, v_ref, kb_ref, vb_ref = out_refs[:5]

    zq = jnp.dot(hb, w_ref[:, :Q_DIM], preferred_element_type=F32)
    for c in range(Q_DIM // LANES):
        sl = slice(c * LANES, (c + 1) * LANES)
        q_ref[:, sl] = (_rope_chunk(zq[:, sl], cos, sin_s, lo_half) * ATTN_SCALE).astype(BF)
    zk = jnp.dot(hb, w_ref[:, Q_DIM:Q_DIM + KV_DIM], preferred_element_type=F32)
    for c in range(KV_DIM // LANES):
        sl = slice(c * LANES, (c + 1) * LANES)
        kr = _rope_chunk(zk[:, sl], cos, sin_s, lo_half)
        k_ref[:, sl] = kr
        kb_ref[:, sl] = kr.astype(BF)
    zv = jnp.dot(hb, w_ref[:, Q_DIM + KV_DIM:IN_ATTN], preferred_element_type=F32)
    v_ref[...] = zv
    vb_ref[...] = zv.astype(BF)
    if dsa:
        qi_ref, rest_ref, restb_ref = out_refs[5:]
        zi = jnp.dot(hb, w_ref[:, IN_ATTN:IN_ATTN + IDX_Q_DIM], preferred_element_type=F32)
        for c in range(IDX_Q_DIM // LANES):
            sl = slice(c * LANES, (c + 1) * LANES)
            qi_ref[:, sl] = _rope_chunk(zi[:, sl], cos, sin_s, lo_half).astype(BF)
        zr = jnp.dot(hb, w_ref[:, IN_ATTN + IDX_Q_DIM:], preferred_element_type=F32)
        roped = _rope_chunk(zr, cos, sin_s, lo_half)
        rest = jnp.where(lane < IDX_DIM, roped, zr * IDX_W_SCALE)
        rest_ref[...] = rest
        restb_ref[...] = rest.astype(BF)


def _mod_spec(arr, tm, rows_per_seq):
    if arr.ndim == 3:
        return pl.BlockSpec((None, 1, D_MODEL), lambda i, *_: ((i * tm) // rows_per_seq, 0, 0))
    return pl.BlockSpec((tm, D_MODEL), lambda i, *_: (i, 0))


def _project(x, g, scale, shift, cos_tab, sin_tab, w, rows_per_seq, dsa):
    rows = x.shape[0]
    tm = min(ROW_TILE, rows)
    n_in = w.shape[1]
    tab_tiles = cos_tab.shape[0] // tm
    row_spec = lambda n: pl.BlockSpec((tm, n), lambda i: (i, 0))
    tab_spec = pl.BlockSpec((tm, LANES), lambda i: (i % tab_tiles, 0))
    out_shape = [jax.ShapeDtypeStruct((rows, Q_DIM), BF),
                 jax.ShapeDtypeStruct((rows, KV_DIM), F32), jax.ShapeDtypeStruct((rows, KV_DIM), F32),
                 jax.ShapeDtypeStruct((rows, KV_DIM), BF), jax.ShapeDtypeStruct((rows, KV_DIM), BF)]
    out_specs = [row_spec(Q_DIM), row_spec(KV_DIM), row_spec(KV_DIM), row_spec(KV_DIM), row_spec(KV_DIM)]
    if dsa:
        out_shape += [jax.ShapeDtypeStruct((rows, IDX_Q_DIM), BF),
                      jax.ShapeDtypeStruct((rows, LANES), F32), jax.ShapeDtypeStruct((rows, LANES), BF)]
        out_specs += [row_spec(IDX_Q_DIM), row_spec(LANES), row_spec(LANES)]
    return pl.pallas_call(
        functools.partial(_proj_kernel, dsa=dsa),
        grid=(rows // tm,),
        in_specs=[row_spec(D_MODEL),
                  pl.BlockSpec((1, D_MODEL), lambda i: (0, 0)),
                  _mod_spec(scale, tm, rows_per_seq), _mod_spec(shift, tm, rows_per_seq),
                  tab_spec, tab_spec,
                  pl.BlockSpec((D_MODEL, n_in), lambda i: (0, 0))],
        out_specs=out_specs,
        out_shape=out_shape,
        compiler_params=_params("parallel"),
        name="project_dsa" if dsa else "project",
    )(x, g, scale, shift, cos_tab, sin_tab, w)


def _rope_rows(zt, cos_t, sin_t, n_heads, scale, dtype):
    half = HEAD_DIM // 2
    out = []
    for h in range(n_heads):
        x = zt[h * HEAD_DIM:(h + 1) * HEAD_DIM]
        partner = jnp.concatenate([x[half:], x[:half]], axis=0)
        out.append(((x * cos_t + partner * sin_t) * scale).astype(dtype))
    return out


def _proj_t_kernel(x_ref, g_ref, sc_ref, sh_ref, cos_ref, sin_ref, cost_ref, sint_ref, w_ref, wqt_ref, wkt_ref, wvt_ref,
                   *refs, dsa):
    if dsa:
        wit_ref, wrt_ref = refs[:2]
        refs = refs[2:]
    qt_ref, kt_ref, vtf_ref, kz_ref, vt_ref, kmean_ref = refs[:6]
    hb = _norm_mod(x_ref[...], g_ref[...], sc_ref[...], sh_ref[...]).astype(BF)
    lane = lax.broadcasted_iota(I32, (1, LANES), 1)
    lo_half = (lane % HEAD_DIM) < (HEAD_DIM // 2)

    zqt = lax.dot_general(wqt_ref[...], hb, NT, preferred_element_type=F32)
    cos_t, sin_t = cost_ref[...], sint_ref[...]
    for h, qh in enumerate(_rope_rows(zqt, cos_t, sin_t, N_HEADS, ATTN_SCALE * LOG2E, BF)):
        qt_ref[h * HEAD_DIM:(h + 1) * HEAD_DIM, :] = qh
    zkt = lax.dot_general(wkt_ref[...], hb, NT, preferred_element_type=F32)
    for h, kh in enumerate(_rope_rows(zkt, cos_t, sin_t, N_KV_HEADS, 1.0, F32)):
        kt_ref[h * HEAD_DIM:(h + 1) * HEAD_DIM, :] = kh
    zvt = lax.dot_general(wvt_ref[...], hb, NT, preferred_element_type=F32)
    vtf_ref[...] = zvt
    vt_ref[...] = zvt.astype(BF)
    if dsa:
        qit_ref, wt_ref, kit_ref, kiz_ref = refs[6:]
        zit = lax.dot_general(wit_ref[...], hb, NT, preferred_element_type=F32)
        for h, qh in enumerate(_rope_rows(zit, cos_t, sin_t, IDX_HEADS, 1.0, BF)):
            qit_ref[h * IDX_DIM:(h + 1) * IDX_DIM, :] = qh
        zr = jnp.dot(hb, w_ref[:, IN_ATTN + IDX_Q_DIM:], preferred_element_type=F32)
        kiz_ref[...] = jnp.where(lane < IDX_DIM, _rope_chunk(zr, cos_ref[...], sin_ref[...], lo_half), 0.0).astype(BF)
        zrt = lax.dot_general(wrt_ref[...], hb, NT, preferred_element_type=F32)
        kit_ref[...] = _rope_rows(zrt, cos_t, sin_t, 1, 1.0, F32)[0]
        wt_ref[...] = zrt[IDX_DIM:IDX_DIM + IDX_HEADS] * IDX_W_SCALE
    zk = jnp.dot(hb, w_ref[:, Q_DIM:Q_DIM + KV_DIM], preferred_element_type=F32)
    tm = zk.shape[0]
    for c in range(KV_DIM // LANES):
        sl = slice(c * LANES, (c + 1) * LANES)
        kr = _rope_chunk(zk[:, sl], cos_ref[...], sin_ref[...], lo_half)
        kmean_ref[:, :, sl] = jnp.mean(kr.reshape(tm // MOBA_BLOCK, MOBA_BLOCK, LANES), axis=1, keepdims=True)
        kz_ref[:, 2 * c * LANES:(2 * c + 1) * LANES] = jnp.where(lane < HEAD_DIM, kr, 0.0).astype(BF)
        kz_ref[:, (2 * c + 1) * LANES:(2 * c + 2) * LANES] = jnp.where(
            lane < HEAD_DIM, pltpu.roll(kr, HEAD_DIM, 1), 0.0).astype(BF)


def _project_t(x, g, scale, shift, cos_tab, sin_tab, cos_t, sin_t, w, rows_per_seq, dsa):
    rows = x.shape[0]
    tm = min(ROW_TILE, rows)
    assert rows_per_seq % tm == 0 and tm % MOBA_BLOCK == 0
    tab_tiles = cos_tab.shape[0] // tm
    tiles_per_seq = rows_per_seq // tm
    row_spec = lambda n: pl.BlockSpec((tm, n), lambda i: (i, 0))
    col_spec = lambda n: pl.BlockSpec((n, tm), lambda i: (0, i))
    seq_spec = lambda n: pl.BlockSpec((None, n, tm), lambda i: (i // tiles_per_seq, 0, i % tiles_per_seq))
    seq_shape = lambda n: jax.ShapeDtypeStruct((rows // rows_per_seq, n, rows_per_seq), F32)
    tab_spec = pl.BlockSpec((tm, LANES), lambda i: (i % tab_tiles, 0))
    tabt_spec = pl.BlockSpec((HEAD_DIM, tm), lambda i: (0, i % tab_tiles))
    whole = lambda a: pl.BlockSpec(a.shape, lambda i: (0, 0))
    weights = [w, w[:, :Q_DIM].T, w[:, Q_DIM:Q_DIM + KV_DIM].T, w[:, Q_DIM + KV_DIM:IN_ATTN].T]
    out_specs = [col_spec(Q_DIM), seq_spec(KV_DIM), seq_spec(KV_DIM), row_spec(2 * KV_DIM), col_spec(KV_DIM),
                 pl.BlockSpec((tm // MOBA_BLOCK, 1, KV_DIM), lambda i: (i, 0, 0))]
    out_shape = [jax.ShapeDtypeStruct((Q_DIM, rows), BF), seq_shape(KV_DIM), seq_shape(KV_DIM),
                 jax.ShapeDtypeStruct((rows, 2 * KV_DIM), BF), jax.ShapeDtypeStruct((KV_DIM, rows), BF),
                 jax.ShapeDtypeStruct((rows // MOBA_BLOCK, 1, KV_DIM), F32)]
    if dsa:
        weights += [w[:, IN_ATTN:IN_ATTN + IDX_Q_DIM].T, w[:, IN_ATTN + IDX_Q_DIM:].T]
        out_specs[4] = pl.BlockSpec((None, KV_DIM, tm), lambda i: (i, 0, 0))
        out_shape[4] = jax.ShapeDtypeStruct((rows // tm, KV_DIM, tm), BF)
        out_specs += [col_spec(IDX_Q_DIM), col_spec(IDX_HEADS), seq_spec(IDX_DIM), row_spec(LANES)]
        out_shape += [jax.ShapeDtypeStruct((IDX_Q_DIM, rows), BF), jax.ShapeDtypeStruct((IDX_HEADS, rows), F32),
                      seq_shape(IDX_DIM), jax.ShapeDtypeStruct((rows, LANES), BF)]
    return pl.pallas_call(
        functools.partial(_proj_t_kernel, dsa=dsa),
        grid=(rows // tm,),
        in_specs=[row_spec(D_MODEL),
                  pl.BlockSpec((1, D_MODEL), lambda i: (0, 0)),
                  _mod_spec(scale, tm, rows_per_seq), _mod_spec(shift, tm, rows_per_seq),
                  tab_spec, tab_spec, tabt_spec, tabt_spec] + [whole(a) for a in weights],
        out_specs=out_specs,
        out_shape=out_shape,
        compiler_params=_params("parallel"),
        name="project_t_dsa" if dsa else "project_t",
    )(x, g, scale, shift, cos_tab, sin_tab, cos_t, sin_t, *weights)


def _post_kernel(x_ref, o_ref, wo_ref, ga_ref, g_ref, sc_ref, sh_ref, gm_ref, w1_ref, w2_ref, y_ref,
                 x1_scr, h_scr, acc_scr):
    j = pl.program_id(1)

    @pl.when(j == 0)
    def _():
        x1 = x_ref[...] + ga_ref[...] * jnp.dot(o_ref[...], wo_ref[...], preferred_element_type=F32)
        x1_scr[...] = x1
        h_scr[...] = _norm_mod(x1, g_ref[...], sc_ref[...], sh_ref[...]).astype(BF)
        acc_scr[...] = jnp.zeros_like(acc_scr)

    a = jnp.maximum(jnp.dot(h_scr[...], w1_ref[...], preferred_element_type=F32), 0.0)
    acc_scr[...] += jnp.dot((a * a).astype(BF), w2_ref[...], preferred_element_type=F32)

    @pl.when(j == pl.num_programs(1) - 1)
    def _():
        y_ref[...] = x1_scr[...] + gm_ref[...] * acc_scr[...]


def _post(x, o, w_out, gate_a, g_mlp, scale_m, shift_m, gate_m, w1, w2, rows_per_seq):
    rows = x.shape[0]
    tm = min(POST_ROW_TILE, rows)
    tf = FF_TILE
    ms = lambda a: _mod_spec(a, tm, rows_per_seq)
    return pl.pallas_call(
        _post_kernel,
        grid=(rows // tm, D_FF // tf),
        in_specs=[pl.BlockSpec((tm, D_MODEL), lambda i, j: (i, 0)),
                  pl.BlockSpec((tm, Q_DIM), lambda i, j: (i, 0)),
                  pl.BlockSpec((Q_DIM, D_MODEL), lambda i, j: (0, 0)),
                  ms(gate_a),
                  pl.BlockSpec((1, D_MODEL), lambda i, j: (0, 0)),
                  ms(scale_m), ms(shift_m), ms(gate_m),
                  pl.BlockSpec((D_MODEL, tf), lambda i, j: (0, j)),
                  pl.BlockSpec((tf, D_MODEL), lambda i, j: (j, 0))],
        out_specs=pl.BlockSpec((tm, D_MODEL), lambda i, j: (i, 0)),
        out_shape=jax.ShapeDtypeStruct((rows, D_MODEL), F32),
        scratch_shapes=[pltpu.VMEM((tm, D_MODEL), F32), pltpu.VMEM((tm, D_MODEL), BF),
                        pltpu.VMEM((tm, D_MODEL), F32)],
        compiler_params=_params("parallel", "arbitrary"),
        name="post",
    )(x, o, w_out, gate_a, g_mlp, scale_m, shift_m, gate_m, w1, w2)


def _final_norm_kernel(x_ref, g_ref, y_ref):
    x = x_ref[...]
    y_ref[...] = x * lax.rsqrt(jnp.mean(x * x, axis=-1, keepdims=True) + NORM_EPS) * g_ref[...]


def _final_norm(x, g):
    rows = x.shape[0]
    tm = min(ROW_TILE, rows)
    return pl.pallas_call(
        _final_norm_kernel,
        grid=(rows // tm,),
        in_specs=[pl.BlockSpec((tm, D_MODEL), lambda i: (i, 0)), pl.BlockSpec((1, D_MODEL), lambda i: (0, 0))],
        out_specs=pl.BlockSpec((tm, D_MODEL), lambda i: (i, 0)),
        out_shape=jax.ShapeDtypeStruct((rows, D_MODEL), F32),
        compiler_params=_params("parallel"),
        name="final_norm",
    )(x, g)


def _top_mask(gs, ids, n_sel, axis):
    picked = None
    for _ in range(n_sel):
        mx = jnp.max(gs, axis=axis, keepdims=True)
        am = jnp.min(jnp.where(gs == mx, ids, 2 ** 30), axis=axis, keepdims=True)
        hit = (ids == am) & (mx > NEG_INF)
        picked = hit if picked is None else (picked | hit)
        gs = jnp.where(ids == am, NEG_INF, gs)
    return picked


MOBA_ACC_ROWS = HEAD_DIM + 16
MOBA_MASKED = -1e30

def _moba_prompt_kernel(qi_tab, kp_tab, last_tab, qt_ref, kz_own_ref, vt_own_ref, kz_pair_ref, vt_pair_ref, kmz_ref,
                        o_ref, qz_scr, m_scr, acc_scr, *, n_sel):
    p = pl.program_id(1)
    i = qi_tab[p]
    kp = kp_tab[p]
    blk = MOBA_BLOCK

    @pl.when(kp < 0)
    def _():
        blocks = lax.broadcasted_iota(I32, (LANES - HEAD_DIM, blk), 0)
        earlier = blocks < i
        for h in range(N_HEADS):
            qt = qt_ref[h * HEAD_DIM:(h + 1) * HEAD_DIM, :]
            gs = jnp.dot(kmz_ref[h // GROUP, HEAD_DIM:, :], qt.astype(F32), precision=lax.Precision.HIGHEST,
                         preferred_element_type=F32)
            picked = _top_mask(jnp.where(earlier, gs, NEG_INF), blocks, n_sel, 0)
            usable = (earlier & picked) | (blocks == i)
            bias = jnp.where(usable, 0.0, MOBA_MASKED)
            qz_scr[h // GROUP, :, (h % GROUP) * blk:(h % GROUP + 1) * blk] = jnp.concatenate(
                [qt, bias.astype(BF)], axis=0)
        m_scr[...] = jnp.full(m_scr.shape, NEG_INF, F32)
        acc_scr[...] = jnp.zeros_like(acc_scr)

    def step(kz_ref, vt_ref, n_keys, block_of_key, causal):
        lane = lax.broadcasted_iota(I32, (n_keys, LANES), 1)
        ones_rows = (lax.broadcasted_iota(I32, (MOBA_ACC_ROWS - HEAD_DIM, n_keys), 0) == 0).astype(BF)
        sts = []
        for kv in range(N_KV_HEADS):
            kz = jnp.where(lane == HEAD_DIM + block_of_key, 1.0, kz_ref[:, kv * LANES:(kv + 1) * LANES]).astype(BF)
            sts.append(jnp.dot(kz, qz_scr[kv], preferred_element_type=F32))
        es, alphas = [], []
        for kv in range(N_KV_HEADS):
            st = sts[kv]
            if causal:
                key_i = lax.broadcasted_iota(I32, (n_keys, GROUP * blk), 0)
                qry_i = lax.broadcasted_iota(I32, (n_keys, GROUP * blk), 1) % blk
                st = jnp.where(key_i <= qry_i, st, NEG_INF)
            m_old = m_scr[kv]
            m_new = jnp.maximum(m_old, jnp.max(st, axis=0, keepdims=True))
            alphas.append(jnp.exp2(m_old - m_new))
            es.append(jnp.exp2(st - m_new).astype(BF))
            m_scr[kv] = m_new
        for kv in range(N_KV_HEADS):
            vaug = jnp.concatenate([vt_ref[kv * HEAD_DIM:(kv + 1) * HEAD_DIM, :], ones_rows], axis=0)
            acc_scr[kv] = alphas[kv] * acc_scr[kv] + jnp.dot(vaug, es[kv], preferred_element_type=F32)

    @pl.when(kp < 0)
    def _():
        step(kz_own_ref, vt_own_ref, blk, i, True)

    @pl.when(kp >= 0)
    def _():
        first = 2 * kp
        second = jnp.where(first + 1 < i, first + 1, i + 1)
        key_row = lax.broadcasted_iota(I32, (2 * blk, 1), 0)
        step(kz_pair_ref, vt_pair_ref, 2 * blk, jnp.where(key_row < blk, first, second), False)

    @pl.when(last_tab[p] == 1)
    def _():
        for c in range(N_HEADS // 2):
            halves = []
            for h in (2 * c, 2 * c + 1):
                a = acc_scr[h // GROUP, :, (h % GROUP) * blk:(h % GROUP + 1) * blk]
                halves.append(a[:HEAD_DIM] * (1.0 / a[HEAD_DIM:HEAD_DIM + 1]))
            o_ref[:, c * LANES:(c + 1) * LANES] = jnp.concatenate(halves, axis=0).T.astype(BF)


def _moba_prompt(qt, kz, vt, kmeans, batch):
    rows = qt.shape[1]
    seq = rows // batch
    blk = MOBA_BLOCK
    assert seq % blk == 0
    nblk = seq // blk
    assert HEAD_DIM + nblk + 1 <= LANES and nblk % 2 == 0
    n_sel = min(MOBA_TOPK, nblk)
    kmean = kmeans.reshape(batch, nblk, N_KV_HEADS, HEAD_DIM).transpose(0, 2, 1, 3)
    kmz = jnp.pad(kmean, ((0, 0), (0, 0), (HEAD_DIM, LANES - HEAD_DIM - nblk), (0, 0)))
    qi, kp, last = [], [], []
    for i in range(nblk):
        order = [-1] + list(range((i + 1) // 2))
        qi += [i] * len(order)
        kp += order
        last += [0] * (len(order) - 1) + [1]
    tabs = [jnp.asarray(np.array(t, np.int32)) for t in (qi, kp, last)]
    npair = nblk // 2
    own_rows = lambda b, p, qi, kp, la: (b * nblk + qi[p], 0)
    own_cols = lambda b, p, qi, kp, la: (0, b * nblk + qi[p])
    grid_spec = pltpu.PrefetchScalarGridSpec(
        num_scalar_prefetch=3,
        grid=(batch, len(qi)),
        in_specs=[pl.BlockSpec((Q_DIM, blk), own_cols),
                  pl.BlockSpec((blk, 2 * KV_DIM), own_rows),
                  pl.BlockSpec((KV_DIM, blk), own_cols),
                  pl.BlockSpec((2 * blk, 2 * KV_DIM), lambda b, p, qi, kp, la: (b * npair + jnp.maximum(kp[p], 0), 0)),
                  pl.BlockSpec((KV_DIM, 2 * blk), lambda b, p, qi, kp, la: (0, b * npair + jnp.maximum(kp[p], 0))),
                  pl.BlockSpec((None, N_KV_HEADS, LANES, HEAD_DIM), lambda b, p, qi, kp, la: (b, 0, 0, 0))],
        out_specs=pl.BlockSpec((blk, Q_DIM), own_rows),
        scratch_shapes=[pltpu.VMEM((N_KV_HEADS, LANES, GROUP * blk), BF), pltpu.VMEM((N_KV_HEADS, 1, GROUP * blk), F32),
                        pltpu.VMEM((N_KV_HEADS, MOBA_ACC_ROWS, GROUP * blk), F32)])
    return pl.pallas_call(
        functools.partial(_moba_prompt_kernel, n_sel=n_sel),
        grid_spec=grid_spec,
        out_shape=jax.ShapeDtypeStruct((rows, Q_DIM), BF),
        compiler_params=_params("parallel", "arbitrary"),
        name="moba_prompt",
    )(*tabs, qt, kz, vt, kz, vt, kmz)


def _float_key(x):
    b = pltpu.bitcast(x, I32)
    return jnp.where(b < 0, b ^ 0x7FFFFFFF, b)


def _kth_largest_key(count_ge, k_top, shape, bits_per_step=1):
    assert 32 % bits_per_step == 0
    def body(it, cur):
        shift = 32 - bits_per_step * (it + 1)
        best = cur
        for v in range(1, 2 ** bits_per_step):
            cand = cur | jnp.left_shift(jnp.int32(v), shift)
            best = jnp.where(count_ge(cand ^ INT_MIN) >= k_top, cand, best)
        return best
    return lax.fori_loop(0, 32 // bits_per_step, body, jnp.zeros(shape, I32)) ^ INT_MIN


def _tie_cutoff(count_eq_below, need, n_bits, shape):
    def body(it, cur):
        cand = cur | jnp.left_shift(jnp.int32(1), n_bits - 1 - it)
        return jnp.where(count_eq_below(cand) < need, cand, cur)
    return lax.fori_loop(0, n_bits, body, jnp.zeros(shape, I32))


def _swap_halves(x):
    return jnp.concatenate([x[HEAD_DIM:], x[:HEAD_DIM]], axis=0)


def _dsa_prompt_kernel(qt_ref, qit_ref, wt_ref, kiz_ref, kz_ref, vt_ref, o_ref,
                       qp_scr, qip_scr, key_scr, m_scr, acc_scr, *, k_top, n_bits):
    i = pl.program_id(1)
    tq, ch = DSA_QUERY_TILE, DSA_KEY_CHUNK
    n_ch = ((i + 1) * tq + ch - 1) // ch
    qpos = i * tq + lax.broadcasted_iota(I32, (1, tq), 1)
    key_row = lax.broadcasted_iota(I32, (ch, tq), 0)
    ones_rows = (lax.broadcasted_iota(I32, (MOBA_ACC_ROWS - HEAD_DIM, ch), 0) == 0).astype(BF)

    for c in range(IDX_HEADS // 2):
        pair = qit_ref[c * LANES:(c + 1) * LANES, :]
        qip_scr[0, :, c * tq:(c + 1) * tq] = pair
        qip_scr[1, :, c * tq:(c + 1) * tq] = _swap_halves(pair)
    for c in range(N_HEADS // 2):
        pair = qt_ref[c * LANES:(c + 1) * LANES, :]
        cols = slice((c % 2) * tq, (c % 2 + 1) * tq)
        qp_scr[2 * (c // 2), :, cols] = pair
        qp_scr[2 * (c // 2) + 1, :, cols] = _swap_halves(pair)

    def score_body(c, carry):
        kiz = kiz_ref[pl.ds(pl.multiple_of(c * ch, ch), ch), :]
        sc = jnp.zeros((ch, tq), F32)
        for second in range(2):
            d = jnp.dot(kiz, qip_scr[second], preferred_element_type=F32)
            for u in range(IDX_HEADS // 2):
                h = 2 * u + second
                sc = sc + wt_ref[h:h + 1, :] * jnp.maximum(d[:, u * tq:(u + 1) * tq], 0.0)
        sc = jnp.where(c * ch + key_row <= qpos, sc, NEG_INF)
        key_scr[c] = _float_key(sc)
        return carry
    lax.fori_loop(0, n_ch, score_body, 0)

    def count(pred):
        def body(c, acc):
            return acc + jnp.sum(pred(key_scr[c], c * ch + key_row).astype(I32), axis=0, keepdims=True)
        return lax.fori_loop(0, n_ch, body, jnp.zeros((1, tq), I32))

    thr = _kth_largest_key(lambda t: count(lambda key, idx: key >= t), k_top, (1, tq))
    n_gt = count(lambda key, idx: key > thr)
    n_ge = count(lambda key, idx: key >= thr)
    need = k_top - n_gt
    tie_cols = ((n_ge > k_top) & (thr != KEY_NEG_INF)).astype(I32)
    cutoff = lax.cond(
        jnp.max(tie_cols) > 0,
        lambda: _tie_cutoff(lambda c_: count(lambda key, idx: (key == thr) & (idx < c_)), need, n_bits, (1, tq)),
        lambda: jnp.full((1, tq), 2 ** 30, I32))

    m_scr[...] = jnp.full(m_scr.shape, NEG_INF, F32)
    acc_scr[...] = jnp.zeros_like(acc_scr)

    def attn_body(c, carry):
        key = key_scr[c]
        idx = c * ch + key_row
        keep = ((key > thr) | ((key == thr) & (idx <= cutoff))) & (idx <= qpos)
        bias = jnp.where(keep, 0.0, NEG_INF)
        bias = jnp.concatenate([bias, bias], axis=1)
        rows = pl.ds(pl.multiple_of(c * ch, ch), ch)
        sts = []
        for kv in range(N_KV_HEADS):
            kz = kz_ref[rows, kv * LANES:(kv + 1) * LANES]
            for second in range(2):
                sts.append(jnp.dot(kz, qp_scr[2 * kv + second], preferred_element_type=F32))
        for kv in range(N_KV_HEADS):
            vaug = jnp.concatenate([vt_ref[c, kv * HEAD_DIM:(kv + 1) * HEAD_DIM, :], ones_rows], axis=0)
            for second in range(2):
                g = 2 * kv + second
                st = sts[g] + bias
                m_old = m_scr[g]
                m_new = jnp.maximum(m_old, jnp.max(st, axis=0, keepdims=True))
                m_safe = jnp.where(m_new == NEG_INF, 0.0, m_new)
                alpha = jnp.exp2(m_old - m_safe)
                e = jnp.exp2(st - m_safe).astype(BF)
                acc_scr[g] = alpha * acc_scr[g] + jnp.dot(vaug, e, preferred_element_type=F32)
                m_scr[g] = m_new
        return carry
    lax.fori_loop(0, n_ch, attn_body, 0)

    for kv in range(N_KV_HEADS):
        outs = []
        for second in range(2):
            a = acc_scr[2 * kv + second]
            outs.append(a[:HEAD_DIM] * (1.0 / a[HEAD_DIM:HEAD_DIM + 1]))
        for u in range(2):
            both = jnp.concatenate([outs[0][:, u * tq:(u + 1) * tq], outs[1][:, u * tq:(u + 1) * tq]], axis=0)
            o_ref[:, (2 * kv + u) * LANES:(2 * kv + u + 1) * LANES] = both.T.astype(BF)


def _dsa_prompt(qt, qit, wt, kiz, kz, vt3, batch):
    rows = qt.shape[1]
    seq = rows // batch
    tq, ch = DSA_QUERY_TILE, DSA_KEY_CHUNK
    k_top = min(DSA_TOPK_MAX, seq // 4)
    assert seq % ch == 0 and ch >= k_top and ch % tq == 0 and tq % LANES == 0 and vt3.shape[2] == ch
    nq, n_chunks = seq // tq, seq // ch
    n_bits = max(1, (seq - 1).bit_length())
    tile = lambda n: pl.BlockSpec((n, tq), lambda b, i: (0, b * nq + i))
    return pl.pallas_call(
        functools.partial(_dsa_prompt_kernel, k_top=k_top, n_bits=n_bits),
        grid=(batch, nq),
        in_specs=[tile(Q_DIM), tile(IDX_Q_DIM), tile(IDX_HEADS),
                  pl.BlockSpec((seq, LANES), lambda b, i: (b, 0), pipeline_mode=pl.Buffered(1)),
                  pl.BlockSpec((seq, 2 * KV_DIM), lambda b, i: (b, 0), pipeline_mode=pl.Buffered(1)),
                  pl.BlockSpec((n_chunks, KV_DIM, ch), lambda b, i: (b, 0, 0), pipeline_mode=pl.Buffered(1))],
        out_specs=pl.BlockSpec((tq, Q_DIM), lambda b, i: (b * nq + i, 0)),
        out_shape=jax.ShapeDtypeStruct((rows, Q_DIM), BF),
        scratch_shapes=[pltpu.VMEM((2 * N_KV_HEADS, LANES, 2 * tq), BF),
                        pltpu.VMEM((2, LANES, (IDX_HEADS // 2) * tq), BF),
                        pltpu.VMEM((n_chunks, ch, tq), I32),
                        pltpu.VMEM((2 * N_KV_HEADS, 1, 2 * tq), F32),
                        pltpu.VMEM((2 * N_KV_HEADS, MOBA_ACC_ROWS, 2 * tq), F32)],
        compiler_params=_params("parallel", "arbitrary"),
        name="dsa_prompt",
    )(qt, qit, wt, kiz, kz, vt3)


def _swa_prompt_kernel(qt_ref, kzp_ref, kzc_ref, vtp_ref, vtc_ref, sink_ref, o_ref):
    n = pl.program_id(1)
    w = WINDOW
    key_i = lax.broadcasted_iota(I32, (2 * w, w), 0)
    qry_i = lax.broadcasted_iota(I32, (2 * w, w), 1)
    keep = (key_i >= qry_i) & (key_i <= qry_i + w) & ((key_i >= w) | (n > 0))
    bias = jnp.where(keep, 0.0, NEG_INF)
    bias = jnp.concatenate([bias, bias], axis=1)
    ones_rows = (lax.broadcasted_iota(I32, (MOBA_ACC_ROWS - HEAD_DIM, 2 * w), 0) == 0).astype(BF)
    for kv in range(N_KV_HEADS):
        lanes = slice(kv * LANES, (kv + 1) * LANES)
        feats = slice(kv * HEAD_DIM, (kv + 1) * HEAD_DIM)
        kz = jnp.concatenate([kzp_ref[:, lanes], kzc_ref[:, lanes]], axis=0)
        vaug = jnp.concatenate([jnp.concatenate([vtp_ref[feats, :], vtc_ref[feats, :]], axis=1), ones_rows], axis=0)
        pairs = [qt_ref[(2 * kv + u) * LANES:(2 * kv + u + 1) * LANES, :] for u in range(2)]
        outs = []
        for second in range(2):
            ops = pairs if second == 0 else [_swap_halves(p) for p in pairs]
            st = jnp.dot(kz, jnp.concatenate(ops, axis=1), preferred_element_type=F32) + bias
            sink = sink_ref[2 * kv + second]
            m = jnp.maximum(jnp.max(st, axis=0, keepdims=True), sink)
            acc = jnp.dot(vaug, jnp.exp2(st - m).astype(BF), preferred_element_type=F32)
            outs.append(acc[:HEAD_DIM] * (1.0 / (acc[HEAD_DIM:HEAD_DIM + 1] + jnp.exp2(sink - m))))
        for u in range(2):
            both = jnp.concatenate([outs[0][:, u * w:(u + 1) * w], outs[1][:, u * w:(u + 1) * w]], axis=0)
            o_ref[:, (2 * kv + u) * LANES:(2 * kv + u + 1) * LANES] = both.T.astype(BF)


def _swa_prompt(qt, kz, vt, sinks, batch):
    rows = qt.shape[1]
    seq = rows // batch
    w = WINDOW
    assert seq % w == 0 and w == LANES
    nb = seq // w
    cur = lambda b, n: b * nb + n
    prev = lambda b, n: b * nb + jnp.maximum(n - 1, 0)
    sk = sinks.astype(F32).reshape(N_KV_HEADS, 2, 2)
    sink_cols = jnp.repeat(sk.transpose(0, 2, 1) * LOG2E, w, axis=2).reshape(2 * N_KV_HEADS, 1, 2 * w)
    return pl.pallas_call(
        _swa_prompt_kernel,
        grid=(batch, nb),
        in_specs=[pl.BlockSpec((Q_DIM, w), lambda b, n: (0, cur(b, n))),
                  pl.BlockSpec((w, 2 * KV_DIM), lambda b, n: (prev(b, n), 0)),
                  pl.BlockSpec((w, 2 * KV_DIM), lambda b, n: (cur(b, n), 0)),
                  pl.BlockSpec((KV_DIM, w), lambda b, n: (0, prev(b, n))),
                  pl.BlockSpec((KV_DIM, w), lambda b, n: (0, cur(b, n))),
                  pl.BlockSpec((2 * N_KV_HEADS, 1, 2 * w), lambda b, n: (0, 0, 0))],
        out_specs=pl.BlockSpec((w, Q_DIM), lambda b, n: (cur(b, n), 0)),
        out_shape=jax.ShapeDtypeStruct((rows, Q_DIM), BF),
        compiler_params=_params("parallel", "parallel"),
        name="swa_prompt",
    )(qt, kz, kz, vt, vt, sink_cols)


def _row_token(n_rows, n_tok):
    r = lax.broadcasted_iota(I32, (n_rows, 1), 0)
    return (r // GROUP) % n_tok


def _block_diag_queries(q, n_seq, n_tok):
    qr = q.reshape(n_seq, n_tok, N_KV_HEADS, GROUP, HEAD_DIM).transpose(0, 2, 1, 3, 4)
    qr = qr.reshape(n_seq, N_KV_HEADS, n_tok * GROUP, HEAD_DIM)
    eye = jnp.eye(N_KV_HEADS, dtype=q.dtype)
    qbd = qr[:, :, :, None, :] * eye[None, :, None, :, None]
    return qbd.reshape(n_seq, N_KV_HEADS * n_tok * GROUP, KV_DIM)


def _undiag_outputs(out, n_seq, n_tok):
    o = out.reshape(n_seq, N_KV_HEADS, n_tok, GROUP, N_KV_HEADS, HEAD_DIM)
    o = jnp.stack([o[:, kv, :, :, kv] for kv in range(N_KV_HEADS)], axis=1)
    return o.transpose(0, 2, 1, 3, 4).reshape(n_seq * n_tok, Q_DIM)


def _tokens_on_lanes(cache):
    if cache.ndim == 5:
        t = jnp.transpose(cache, (0, 1, 3, 4, 2))
        return t.reshape(t.shape[:2] + (t.shape[2] * t.shape[3], t.shape[4]))
    return jnp.transpose(cache, (0, 1, 3, 2))


def _page_specs(n_pages_per_step, slot, width):
    def spec(r):
        return pl.BlockSpec((None, None, width, PAGE_SIZE),
                            lambda b, s, pt: (slot, pt[b, s * n_pages_per_step + r], 0, 0))
    return [spec(r) for r in range(n_pages_per_step)]


def _pad_rows(a, n_seq, n_tok):
    a = a.reshape(n_seq, n_tok, a.shape[-1])
    return jnp.pad(a, ((0, 0), (0, SUBLANES - n_tok), (0, 0)))


def _moba_sample_kernel(pt_ref, q_ref, kn_ref, vn_ref, *refs, n_tok, n_sel, pps):
    kt_pages = refs[:pps]
    vt_pages = refs[pps:2 * pps]
    o_ref = refs[2 * pps]
    m_scr, l_scr, g_scr, o_scr = refs[2 * pps + 1:]
    s = pl.program_id(1)
    n_steps = m_scr.shape[0]
    bps = pps // 2
    q = q_ref[...]
    n_row = q.shape[0]
    m_scr[s] = jnp.full(m_scr.shape[1:], NEG_INF, F32)
    l_scr[s] = jnp.zeros(l_scr.shape[1:], F32)
    g_scr[s] = jnp.full(g_scr.shape[1:], NEG_INF, F32)
    kt_all = jnp.concatenate([p[...] for p in kt_pages], axis=1).astype(BF)
    sc_all = jnp.dot(q, kt_all, preferred_element_type=F32)
    for r in range(bps):
        vt = jnp.concatenate([vt_pages[2 * r][...], vt_pages[2 * r + 1][...]], axis=1).astype(BF)
        sc = sc_all[:, r * MOBA_BLOCK:(r + 1) * MOBA_BLOCK]
        m = jnp.max(sc, axis=1, keepdims=True)
        e = jnp.exp(sc - m)
        m_scr[s, :, r:r + 1] = m
        l_scr[s, :, r:r + 1] = jnp.sum(e, axis=1, keepdims=True)
        g_scr[s, :, r:r + 1] = jnp.sum(sc, axis=1, keepdims=True)
        o_scr[s * bps + r] = lax.dot_general(e.astype(BF), vt, NT, preferred_element_type=F32)

    @pl.when(s == pl.num_programs(1) - 1)
    def _():
        lane = lax.broadcasted_iota(I32, (n_row, LANES), 1)
        ids = [t * bps + lane for t in range(n_steps)]
        gs = [g_scr[t] for t in range(n_steps)]
        picked = [None] * n_steps
        for _ in range(n_sel):
            mx = functools.reduce(jnp.maximum, [jnp.max(g, axis=1, keepdims=True) for g in gs])
            am = functools.reduce(jnp.minimum, [jnp.min(jnp.where(g == mx, i_, 2 ** 30), axis=1, keepdims=True)
                                                for g, i_ in zip(gs, ids)])
            for t in range(n_steps):
                hit = (ids[t] == am) & (mx > NEG_INF)
                picked[t] = hit if picked[t] is None else (picked[t] | hit)
                gs[t] = jnp.where(ids[t] == am, NEG_INF, gs[t])
        own_ok = lax.broadcasted_iota(I32, (n_row, SUBLANES), 1) <= _row_token(n_row, n_tok)
        s_own = jnp.where(own_ok, lax.dot_general(q, kn_ref[...], NT, preferred_element_type=F32), NEG_INF)
        m_tot = jnp.max(s_own, axis=1, keepdims=True)
        for t in range(n_steps):
            m_tot = jnp.maximum(m_tot, jnp.max(jnp.where(picked[t], m_scr[t], NEG_INF), axis=1, keepdims=True))
        e_own = jnp.exp(s_own - m_tot)
        den = jnp.sum(e_own, axis=1, keepdims=True)
        num = jnp.dot(e_own.astype(BF), vn_ref[...], preferred_element_type=F32)
        for t in range(n_steps):
            wgt = jnp.where(picked[t], jnp.exp(m_scr[t] - m_tot), 0.0)
            den = den + jnp.sum(wgt * l_scr[t], axis=1, keepdims=True)
            for r in range(bps):
                num = num + wgt[:, r:r + 1] * o_scr[t * bps + r]
        o_ref[...] = num * (1.0 / den)


def _paged_attention_call(kernel, name, slot, page_table, cache_kt, cache_vt, extra_inputs, extra_specs,
                          scratch_shapes, n_row):
    n_seq, n_pages = page_table.shape
    pps = PAGES_PER_STEP
    assert n_pages % pps == 0
    grid_spec = pltpu.PrefetchScalarGridSpec(
        num_scalar_prefetch=1,
        grid=(n_seq, n_pages // pps),
        in_specs=extra_specs + _page_specs(pps, slot, KV_DIM) + _page_specs(pps, slot, KV_DIM),
        out_specs=pl.BlockSpec((None, n_row, KV_DIM), lambda b, s, pt: (b, 0, 0)),
        scratch_shapes=scratch_shapes)
    return pl.pallas_call(
        kernel,
        grid_spec=grid_spec,
        out_shape=jax.ShapeDtypeStruct((n_seq, n_row, KV_DIM), F32),
        compiler_params=_params("parallel", "arbitrary"),
        name=name,
    )(page_table, *extra_inputs, *([cache_kt] * pps), *([cache_vt] * pps))


def _moba_sample(qbd, kn_pad, vn_pad, cache_kt, cache_vt, slot, page_table, n_tok):
    n_seq, n_pages = page_table.shape
    past = n_pages * PAGE_SIZE
    assert past % MOBA_BLOCK == 0 and MOBA_BLOCK == 2 * PAGE_SIZE and PAGES_PER_STEP % 2 == 0
    nblk = past // MOBA_BLOCK
    n_steps = n_pages // PAGES_PER_STEP
    n_row = qbd.shape[1]
    per_seq = lambda w: pl.BlockSpec((None, w[0], w[1]), lambda b, s, pt: (b, 0, 0))
    kernel = functools.partial(_moba_sample_kernel, n_tok=n_tok, n_sel=min(MOBA_TOPK, nblk), pps=PAGES_PER_STEP)
    stats = pltpu.VMEM((n_steps, n_row, LANES), F32)
    return _paged_attention_call(
        kernel, "moba_sample", slot, page_table, cache_kt, cache_vt,
        [qbd, kn_pad, vn_pad],
        [per_seq((n_row, KV_DIM)), per_seq((SUBLANES, KV_DIM)), per_seq((SUBLANES, KV_DIM))],
        [stats, stats, stats, pltpu.VMEM((nblk, n_row, KV_DIM), F32)],
        n_row)


def _dsa_index_kernel(pt_ref, qi_ref, w_ref, kin_ref, *refs, n_tok, k_top, ppi, n_bits):
    ki_pages = refs[:ppi]
    mask_ref = refs[ppi]
    key_scr = refs[ppi + 1]
    s = pl.program_id(1)
    n_steps = pl.num_programs(1)
    n_pages = n_steps * ppi
    qi = qi_ref[...]
    w = w_ref[...]
    fill = jnp.full((SUBLANES - n_tok, LANES), NEG_INF, F32)

    def scores(d):
        wd = (w * jnp.maximum(d, 0.0)).reshape(n_tok, IDX_HEADS, LANES)
        return jnp.sum(wd, axis=1)

    for r in range(ppi):
        sc = scores(jnp.dot(qi, ki_pages[r][...].astype(BF), preferred_element_type=F32))
        key_scr[s * ppi + r] = _float_key(jnp.concatenate([sc, fill], axis=0))

    @pl.when(s == n_steps - 1)
    def _():
        tok = lax.broadcasted_iota(I32, (n_tok, LANES), 0)
        lane_t = lax.broadcasted_iota(I32, (n_tok, LANES), 1)
        sc_new = scores(lax.dot_general(qi, kin_ref[...], NT, preferred_element_type=F32))
        sc_new = jnp.where(lane_t <= tok, sc_new, NEG_INF)
        key_scr[n_pages] = _float_key(jnp.concatenate([sc_new, fill], axis=0))
        tiles = (n_pages + 1, SUBLANES, LANES)
        idx = lax.broadcasted_iota(I32, tiles, 0) * PAGE_SIZE + lax.broadcasted_iota(I32, tiles, 2)

        def count(hit):
            return jnp.sum(jnp.sum(hit.astype(I32), axis=0), axis=1, keepdims=True)

        thr = _kth_largest_key(lambda t: count(key_scr[...] >= t), k_top, (SUBLANES, 1), bits_per_step=2)
        need = k_top - count(key_scr[...] > thr)
        tie_rows = ((count(key_scr[...] >= thr) > k_top) & (thr != KEY_NEG_INF)).astype(I32)
        cutoff = lax.cond(
            jnp.max(tie_rows) > 0,
            lambda: _tie_cutoff(lambda c_: count((key_scr[...] == thr) & (idx < c_)), need, n_bits, (SUBLANES, 1)),
            lambda: jnp.full((SUBLANES, 1), 2 ** 30, I32))
        keys = key_scr[...]
        keep = (keys > thr) | ((keys == thr) & (idx <= cutoff))
        mask_ref[...] = jnp.where(keep & (keys != KEY_NEG_INF), 1.0, 0.0).astype(BF)


def _dsa_index(qi_rows, w_rows, ki_new, cache_idx, islot, page_table, n_tok):
    n_seq, n_pages = page_table.shape
    ppi = min(INDEX_PAGES_PER_STEP, n_pages)
    total = n_pages * PAGE_SIZE + n_tok
    k_top = min(DSA_TOPK_MAX, total // 4)
    assert n_pages % ppi == 0 and n_tok <= SUBLANES and n_pages * PAGE_SIZE >= k_top
    n_bits = max(1, ((n_pages + 1) * PAGE_SIZE - 1).bit_length())
    nr = n_tok * IDX_HEADS
    grid_spec = pltpu.PrefetchScalarGridSpec(
        num_scalar_prefetch=1,
        grid=(n_seq, n_pages // ppi),
        in_specs=[pl.BlockSpec((None, nr, IDX_DIM), lambda b, s, pt: (b, 0, 0)),
                  pl.BlockSpec((None, nr, 1), lambda b, s, pt: (b, 0, 0)),
                  pl.BlockSpec((None, LANES, IDX_DIM), lambda b, s, pt: (b, 0, 0))]
        + _page_specs(ppi, islot, IDX_DIM),
        out_specs=pl.BlockSpec((None, n_pages + 1, SUBLANES, LANES), lambda b, s, pt: (b, 0, 0, 0)),
        scratch_shapes=[pltpu.VMEM((n_pages + 1, SUBLANES, LANES), I32)])
    return pl.pallas_call(
        functools.partial(_dsa_index_kernel, n_tok=n_tok, k_top=k_top, ppi=ppi, n_bits=n_bits),
        grid_spec=grid_spec,
        out_shape=jax.ShapeDtypeStruct((n_seq, n_pages + 1, SUBLANES, LANES), BF),
        compiler_params=_params("parallel", "arbitrary"),
        name="dsa_index",
    )(page_table, qi_rows, w_rows, ki_new, *([cache_idx] * ppi))


def _dsa_sample_kernel(pt_ref, q_ref, kn_ref, vn_ref, mask_ref, mnew_ref, *refs, n_tok, pps):
    kt_pages = refs[:pps]
    vt_pages = refs[pps:2 * pps]
    o_ref = refs[2 * pps]
    m_scr, l_scr, acc_scr = refs[2 * pps + 1:]
    s = pl.program_id(1)
    q = q_ref[...]
    n_row = q.shape[0]
    expand = (lax.broadcasted_iota(I32, (n_row, SUBLANES), 1) == _row_token(n_row, n_tok)).astype(BF)

    @pl.when(s == 0)
    def _():
        m_scr[...] = jnp.full(m_scr.shape, NEG_INF, F32)
        l_scr[...] = jnp.zeros_like(l_scr)
        acc_scr[...] = jnp.zeros_like(acc_scr)

    def update(sc, keep, pv):
        sc = jnp.where(keep > 0.5, sc, NEG_INF)
        m_old = m_scr[...]
        m_new = jnp.maximum(m_old, jnp.max(sc, axis=1, keepdims=True))
        m_safe = jnp.where(m_new == NEG_INF, 0.0, m_new)
        alpha = jnp.exp(m_old - m_safe)
        e = jnp.exp(sc - m_safe)
        l_scr[...] = alpha * l_scr[...] + jnp.sum(e, axis=1, keepdims=True)
        acc_scr[...] = alpha * acc_scr[...] + pv(e.astype(BF))
        m_scr[...] = m_new

    kt_all = jnp.concatenate([p[...] for p in kt_pages], axis=1).astype(BF)
    sc_all = jnp.dot(q, kt_all, preferred_element_type=F32)
    keep_all = jnp.dot(expand, jnp.concatenate([mask_ref[r] for r in range(pps)], axis=1),
                       preferred_element_type=F32)
    width = DSA_PAGES_PER_UPDATE * PAGE_SIZE
    for u in range(pps // DSA_PAGES_PER_UPDATE):
        rs = range(u * DSA_PAGES_PER_UPDATE, (u + 1) * DSA_PAGES_PER_UPDATE)
        vt = jnp.concatenate([vt_pages[r][...] for r in rs], axis=1).astype(BF)
        cols = slice(u * width, (u + 1) * width)
        update(sc_all[:, cols], keep_all[:, cols], lambda e, vt=vt: lax.dot_general(e, vt, NT, preferred_element_type=F32))

    @pl.when(s == pl.num_programs(1) - 1)
    def _():
        sc = lax.dot_general(q, kn_ref[...], NT, preferred_element_type=F32)
        keep = jnp.dot(expand, mnew_ref[...], preferred_element_type=F32)[:, :SUBLANES]
        update(sc, keep, lambda e: jnp.dot(e, vn_ref[...], preferred_element_type=F32))
        o_ref[...] = acc_scr[...] * (1.0 / l_scr[...])


def _dsa_sample(qbd, kn_pad, vn_pad, mask, cache_kt, cache_vt, slot, page_table, n_tok):
    n_seq, n_pages = page_table.shape
    n_row = qbd.shape[1]
    pps = PAGES_PER_STEP
    per_seq = lambda w: pl.BlockSpec((None, w[0], w[1]), lambda b, s, pt: (b, 0, 0))
    return _paged_attention_call(
        functools.partial(_dsa_sample_kernel, n_tok=n_tok, pps=pps), "dsa_sample", slot, page_table,
        cache_kt, cache_vt,
        [qbd, kn_pad, vn_pad, mask, mask],
        [per_seq((n_row, KV_DIM)), per_seq((SUBLANES, KV_DIM)), per_seq((SUBLANES, KV_DIM)),
         pl.BlockSpec((None, pps, SUBLANES, LANES), lambda b, s, pt: (b, s, 0, 0)),
         pl.BlockSpec((None, None, SUBLANES, LANES), lambda b, s, pt: (b, n_pages, 0, 0))],
        [pltpu.VMEM((n_row, 1), F32), pltpu.VMEM((n_row, 1), F32), pltpu.VMEM((n_row, KV_DIM), F32)],
        n_row)


def _swa_sample_kernel(q_ref, kt_ref, vt_ref, kn_ref, vn_ref, sink_ref, o_ref, *, n_tok):
    q = q_ref[...]
    n_row = q.shape[0]
    tok_r = _row_token(n_row, n_tok)
    wk = kt_ref.shape[1]
    s_buf = jnp.dot(q, kt_ref[...].astype(BF), preferred_element_type=F32)
    s_buf = jnp.where(lax.broadcasted_iota(I32, (n_row, wk), 1) >= tok_r, s_buf, NEG_INF)
    s_new = lax.dot_general(q, kn_ref[...], NT, preferred_element_type=F32)
    s_new = jnp.where(lax.broadcasted_iota(I32, (n_row, SUBLANES), 1) <= tok_r, s_new, NEG_INF)
    sink = sink_ref[...]
    m = jnp.maximum(jnp.maximum(jnp.max(s_buf, axis=1, keepdims=True), jnp.max(s_new, axis=1, keepdims=True)), sink)
    e_buf = jnp.exp(s_buf - m)
    e_new = jnp.exp(s_new - m)
    den = jnp.sum(e_buf, axis=1, keepdims=True) + jnp.sum(e_new, axis=1, keepdims=True) + jnp.exp(sink - m)
    num = (lax.dot_general(e_buf.astype(BF), vt_ref[...].astype(BF), NT, preferred_element_type=F32)
           + jnp.dot(e_new.astype(BF), vn_ref[...], preferred_element_type=F32))
    o_ref[...] = num * (1.0 / den)


def _swa_sample(qbd, kn_pad, vn_pad, buf_kt, buf_vt, sinks, n_tok):
    n_seq, _, wk = buf_kt.shape
    assert wk == WINDOW
    n_row = qbd.shape[1]
    sink_rows = jnp.repeat(sinks.reshape(N_KV_HEADS, 1, GROUP), n_tok, axis=1).reshape(n_row, 1)
    per_seq = lambda a, c: pl.BlockSpec((None, a, c), lambda b: (b, 0, 0))
    return pl.pallas_call(
        functools.partial(_swa_sample_kernel, n_tok=n_tok),
        grid=(n_seq,),
        in_specs=[per_seq(n_row, KV_DIM), per_seq(KV_DIM, wk), per_seq(KV_DIM, wk),
                  per_seq(SUBLANES, KV_DIM), per_seq(SUBLANES, KV_DIM),
                  pl.BlockSpec((n_row, 1), lambda b: (0, 0))],
        out_specs=per_seq(n_row, KV_DIM),
        out_shape=jax.ShapeDtypeStruct((n_seq, n_row, KV_DIM), F32),
        compiler_params=_params("parallel"),
        name="swa_sample",
    )(qbd, buf_kt, buf_vt, kn_pad, vn_pad, sink_rows)


def _rope_tables(pos):
    half = HEAD_DIM // 2
    inv = ROPE_THETA ** (-jnp.arange(half, dtype=F32) / half)
    ang = pos.astype(F32)[:, None] * inv[None, :]
    cos, sin = jnp.cos(ang), jnp.sin(ang)
    reps = LANES // HEAD_DIM
    return jnp.tile(jnp.concatenate([cos, cos], axis=1), (1, reps)), jnp.tile(jnp.concatenate([-sin, sin], axis=1), (1, reps))


def kernel(x_prompt, x_sample, cache_k, cache_v, cache_idx_k, state_swa_k, state_swa_v, page_table, c_prompt, c_sample, g_attn, g_mlp, w_mod, b_mod, w_in_moba, w_in_dsa, w_in_swa, swa_sinks, w_out, w_ff1, w_ff2, g_final):
    batch, seq, _ = x_prompt.shape
    n_seq, n_tok, _ = x_sample.shape
    depth = g_attn.shape[0]
    n_pages = page_table.shape[1]
    past = n_pages * PAGE_SIZE
    rows_p, rows_s = batch * seq, n_seq * n_tok

    n_cond = batch + n_seq
    pad_cond = -n_cond % SUBLANES
    c_all = jnp.pad(jnp.concatenate([c_prompt, c_sample], axis=0), ((0, pad_cond), (0, 0)))
    mod = _adaln(c_all, w_mod, b_mod)

    def mods(layer):
        parts = [mod[layer, :, r * D_MODEL:(r + 1) * D_MODEL] for r in range(6)]
        mp = [p[:batch].reshape(batch, 1, D_MODEL) for p in parts]
        ms = [jnp.repeat(p[batch:n_cond], n_tok, axis=0) for p in parts]
        return mp, ms

    cos_p, sin_p = _rope_tables(jnp.arange(seq, dtype=jnp.int32))
    cos_s, sin_s = _rope_tables(jnp.tile(past + jnp.arange(n_tok, dtype=jnp.int32), n_seq))
    cos_pt, sin_pt = cos_p[:, :HEAD_DIM].T, sin_p[:, :HEAD_DIM].T

    cache_kt, cache_vt, cache_it = _tokens_on_lanes(cache_k), _tokens_on_lanes(cache_v), _tokens_on_lanes(cache_idx_k)
    swa_kt, swa_vt = _tokens_on_lanes(state_swa_k), _tokens_on_lanes(state_swa_v)

    xp = x_prompt.reshape(rows_p, D_MODEL)
    xs = x_sample.reshape(rows_s, D_MODEL)
    kp_l, vp_l, ks_l, vs_l, ip_l, is_l = [], [], [], [], [], []
    skp_l, svp_l, sks_l, svs_l = [], [], [], []
    for i, (mixer, j, slot) in enumerate(_layer_plan(depth)):
        (sh_ap, sc_ap, gt_ap, sh_mp, sc_mp, gt_mp), (sh_as, sc_as, gt_as, sh_ms, sc_ms, gt_ms) = mods(i)
        dsa = mixer == 1
        if mixer == 0:
            w_in = w_in_moba[j]
        elif dsa:
            w_in = jnp.pad(w_in_dsa[j], ((0, 0), (0, -IN_DSA % LANES)))
        else:
            w_in = w_in_swa[j]
        w_in = w_in.astype(BF)
        g_a = g_attn[i].reshape(1, D_MODEL)
        outs_p = _project_t(xp, g_a, sc_ap, sh_ap, cos_p, sin_p, cos_pt, sin_pt, w_in, seq, dsa)
        qt_p, kt_p, vtf_p, kz_p, vt_p, kmeans_p = outs_p[:6]
        outs_s = _project(xs, g_a, sc_as, sh_as, cos_s, sin_s, w_in, n_tok, dsa)
        q_s, k_s, v_s, kb_s, vb_s = outs_s[:5]
        qbd = _block_diag_queries(q_s, n_seq, n_tok)
        kn_pad = _pad_rows(kb_s, n_seq, n_tok)
        vn_pad = _pad_rows(vb_s, n_seq, n_tok)
        k_p5 = kt_p.reshape(batch, N_KV_HEADS, HEAD_DIM, seq).transpose(0, 3, 1, 2)
        v_p5 = vtf_p.reshape(batch, N_KV_HEADS, HEAD_DIM, seq).transpose(0, 3, 1, 2)
        k_s5 = k_s.reshape(n_seq, n_tok, N_KV_HEADS, HEAD_DIM)
        v_s5 = v_s.reshape(n_seq, n_tok, N_KV_HEADS, HEAD_DIM)
        if mixer == 0:
            o_p = _moba_prompt(qt_p, kz_p, vt_p, kmeans_p, batch)
            o_t = _moba_sample(qbd, kn_pad, vn_pad, cache_kt, cache_vt, slot, page_table, n_tok)
        elif dsa:
            qit_p, wt_p, kit_p, kiz_p = outs_p[6:]
            qi_s, rest_s, restb_s = outs_s[5:]
            o_p = _dsa_prompt(qt_p, qit_p, wt_p, kiz_p, kz_p, vt_p, batch)
            qi_rows = qi_s.reshape(n_seq, n_tok * IDX_HEADS, IDX_DIM)
            w_rows = rest_s[:, IDX_DIM:IDX_DIM + IDX_HEADS].reshape(n_seq, n_tok * IDX_HEADS, 1)
            ki_new = jnp.pad(restb_s[:, :IDX_DIM].reshape(n_seq, n_tok, IDX_DIM),
                             ((0, 0), (0, LANES - n_tok), (0, 0)))
            mask = _dsa_index(qi_rows, w_rows, ki_new, cache_it, j, page_table, n_tok)
            o_t = _dsa_sample(qbd, kn_pad, vn_pad, mask, cache_kt, cache_vt, slot, page_table, n_tok)
            ip_l.append(kit_p.transpose(0, 2, 1))
            is_l.append(rest_s[:, :IDX_DIM].reshape(n_seq, n_tok, IDX_DIM))
        else:
            o_p = _swa_prompt(qt_p, kz_p, vt_p, swa_sinks[j], batch)
            o_t = _swa_sample(qbd, kn_pad, vn_pad, swa_kt[j], swa_vt[j], swa_sinks[j], n_tok)
            keep_p = min(WINDOW, seq)
            skp_l.append(k_p5[:, seq - keep_p:])
            svp_l.append(v_p5[:, seq - keep_p:])
            wk = state_swa_k.shape[2]
            sks_l.append(jnp.concatenate([state_swa_k[j], k_s5], axis=1)[:, -wk:])
            svs_l.append(jnp.concatenate([state_swa_v[j], v_s5], axis=1)[:, -wk:])
        if mixer < 2:
            kp_l.append(k_p5)
            vp_l.append(v_p5)
            ks_l.append(k_s5)
            vs_l.append(v_s5)
        o_s = _undiag_outputs(o_t, n_seq, n_tok).astype(BF)
        wo, w1, w2 = w_out[i].astype(BF), w_ff1[i].astype(BF), w_ff2[i].astype(BF)
        g_m = g_mlp[i].reshape(1, D_MODEL)
        xp = _post(xp, o_p, wo, gt_ap, g_m, sc_mp, sh_mp, gt_mp, w1, w2, seq)
        xs = _post(xs, o_s, wo, gt_as, g_m, sc_ms, sh_ms, gt_ms, w1, w2, n_tok)
    g_f = g_final.reshape(1, D_MODEL)
    y_prompt = _final_norm(xp, g_f).reshape(batch, seq, D_MODEL)
    y_sample = _final_norm(xs, g_f).reshape(n_seq, n_tok, D_MODEL)
    return (y_prompt, y_sample, jnp.stack(kp_l), jnp.stack(vp_l), jnp.stack(ks_l), jnp.stack(vs_l),
            jnp.stack(ip_l), jnp.stack(is_l), jnp.stack(skp_l), jnp.stack(svp_l), jnp.stack(sks_l), jnp.stack(svs_l))
```

```python
import functools

import numpy as np
import jax
import jax.numpy as jnp
from jax import lax
from jax.experimental import pallas as pl
from jax.experimental.pallas import tpu as pltpu

D_MODEL = 1024
N_HEADS = 16
HEAD_DIM = D_MODEL // N_HEADS
N_KV_HEADS = 4
GROUP = N_HEADS // N_KV_HEADS
Q_DIM = N_HEADS * HEAD_DIM
KV_DIM = N_KV_HEADS * HEAD_DIM
D_FF = 4 * D_MODEL
ROPE_THETA = 10000.0
NORM_EPS = 1e-6
N_MIXERS = 3
PAGE_SIZE = 128
MOBA_BLOCK = 256
MOBA_TOPK = 3
IDX_HEADS = 8
IDX_DIM = 64
DSA_TOPK_MAX = 256
WINDOW = 128
IN_ATTN = Q_DIM + 2 * KV_DIM
IDX_Q_DIM = IDX_HEADS * IDX_DIM
IN_DSA = IN_ATTN + IDX_Q_DIM + IDX_DIM + IDX_HEADS
ATTN_SCALE = HEAD_DIM ** -0.5
LOG2E = 1.4426950408889634
IDX_W_SCALE = IDX_Q_DIM ** -0.5

LANES = 128
SUBLANES = 8
BF16_SUBLANES = 16
ACC_ROWS = HEAD_DIM + BF16_SUBLANES
VMEM_LIMIT = 56 << 20
ROW_TILE = 512
POST_ROW_TILE = 1024
FF_TILE = 1024
DSA_KEY_CHUNK = 512
DSA_QUERY_TILE = 512
PAGES_PER_STEP = 64
INDEX_PAGES_PER_STEP = 64
DSA_PAGES_PER_UPDATE = 4

BF = jnp.bfloat16
F32 = jnp.float32
I32 = jnp.int32
NEG_INF = float("-inf")
INT_MIN = -2 ** 31
KEY_NEG_INF = INT_MIN + 0x7FFFFF
NT = (((1,), (1,)), ((), ()))


def _params(*semantics):
    return pltpu.CompilerParams(dimension_semantics=semantics, vmem_limit_bytes=VMEM_LIMIT)


def _layer_plan(depth):
    plan, counts, n_paged = [], [0] * N_MIXERS, 0
    for i in range(depth):
        m = i % N_MIXERS
        slot = -1
        if m < 2:
            slot = n_paged
            n_paged += 1
        plan.append((m, counts[m], slot))
        counts[m] += 1
    return plan


def _adaln_kernel(c_ref, w_ref, b_ref, o_ref):
    c = c_ref[...]
    a = (c * (1.0 / (1.0 + jnp.exp(-c)))).astype(BF)
    o_ref[...] = jnp.dot(a, w_ref[...].astype(BF), preferred_element_type=F32) + b_ref[...]


def _adaln(c_all, w_mod, b_mod):
    depth, _, n_out = w_mod.shape
    nc = c_all.shape[0]
    tn = 1536
    return pl.pallas_call(
        _adaln_kernel,
        grid=(depth, n_out // tn),
        in_specs=[pl.BlockSpec((nc, D_MODEL), lambda l, j: (0, 0)),
                  pl.BlockSpec((None, D_MODEL, tn), lambda l, j: (l, 0, j)),
                  pl.BlockSpec((None, 1, tn), lambda l, j: (l, 0, j))],
        out_specs=pl.BlockSpec((None, nc, tn), lambda l, j: (l, 0, j)),
        out_shape=jax.ShapeDtypeStruct((depth, nc, n_out), F32),
        compiler_params=_params("parallel", "parallel"),
        name="adaln",
    )(c_all, w_mod, b_mod.reshape(depth, 1, n_out))


def _norm_mod(x, g, scale, shift):
    y = x * lax.rsqrt(jnp.mean(x * x, axis=-1, keepdims=True) + NORM_EPS)
    return (y * g) * (1.0 + scale) + shift


def _rope_chunk(z, cos, sin_signed, lo_half):
    partner = jnp.where(lo_half, pltpu.roll(z, LANES - HEAD_DIM // 2, 1), pltpu.roll(z, HEAD_DIM // 2, 1))
    return z * cos + partner * sin_signed


def _proj_kernel(x_ref, g_ref, sc_ref, sh_ref, cos_ref, sin_ref, w_ref, *out_refs, dsa):
    hb = _norm_mod(x_ref[...], g_ref[...], sc_ref[...], sh_ref[...]).astype(BF)
    cos = cos_ref[...]
    sin_s = sin_ref[...]
    lane = lax.broadcasted_iota(I32, (1, LANES), 1)
    lo_half = (lane % HEAD_DIM) < (HEAD_DIM // 2)
    q_ref, k_ref, v_ref, kb_ref, vb_ref = out_refs[:5]

    zq = jnp.dot(hb, w_ref[:, :Q_DIM], preferred_element_type=F32)
    for c in range(Q_DIM // LANES):
        sl = slice(c * LANES, (c + 1) * LANES)
        q_ref[:, sl] = (_rope_chunk(zq[:, sl], cos, sin_s, lo_half) * ATTN_SCALE).astype(BF)
    zk = jnp.dot(hb, w_ref[:, Q_DIM:Q_DIM + KV_DIM], preferred_element_type=F32)
    for c in range(KV_DIM // LANES):
        sl = slice(c * LANES, (c + 1) * LANES)
        kr = _rope_chunk(zk[:, sl], cos, sin_s, lo_half)
        k_ref[:, sl] = kr
        kb_ref[:, sl] = kr.astype(BF)
    zv = jnp.dot(hb, w_ref[:, Q_DIM + KV_DIM:IN_ATTN], preferred_element_type=F32)
    v_ref[...] = zv
    vb_ref[...] = zv.astype(BF)
    if dsa:
        qi_ref, rest_ref, restb_ref = out_refs[5:]
        zi = jnp.dot(hb, w_ref[:, IN_ATTN:IN_ATTN + IDX_Q_DIM], preferred_element_type=F32)
        for c in range(IDX_Q_DIM // LANES):
            sl = slice(c * LANES, (c + 1) * LANES)
            qi_ref[:, sl] = _rope_chunk(zi[:, sl], cos, sin_s, lo_half).astype(BF)
        zr = jnp.dot(hb, w_ref[:, IN_ATTN + IDX_Q_DIM:], preferred_element_type=F32)
        roped = _rope_chunk(zr, cos, sin_s, lo_half)
        rest = jnp.where(lane < IDX_DIM, roped, zr * IDX_W_SCALE)
        rest_ref[...] = rest
        restb_ref[...] = rest.astype(BF)


def _mod_spec(arr, tm, rows_per_seq):
    if arr.ndim == 3:
        return pl.BlockSpec((None, 1, D_MODEL), lambda i, *_: ((i * tm) // rows_per_seq, 0, 0))
    return pl.BlockSpec((tm, D_MODEL), lambda i, *_: (i, 0))


def _project(x, g, scale, shift, cos_tab, sin_tab, w, rows_per_seq, dsa):
    rows = x.shape[0]
    tm = min(ROW_TILE, rows)
    n_in = w.shape[1]
    tab_tiles = cos_tab.shape[0] // tm
    row_spec = lambda n: pl.BlockSpec((tm, n), lambda i: (i, 0))
    tab_spec = pl.BlockSpec((tm, LANES), lambda i: (i % tab_tiles, 0))
    out_shape = [jax.ShapeDtypeStruct((rows, Q_DIM), BF),
                 jax.ShapeDtypeStruct((rows, KV_DIM), F32), jax.ShapeDtypeStruct((rows, KV_DIM), F32),
                 jax.ShapeDtypeStruct((rows, KV_DIM), BF), jax.ShapeDtypeStruct((rows, KV_DIM), BF)]
    out_specs = [row_spec(Q_DIM), row_spec(KV_DIM), row_spec(KV_DIM), row_spec(KV_DIM), row_spec(KV_DIM)]
    if dsa:
        out_shape += [jax.ShapeDtypeStruct((rows, IDX_Q_DIM), BF),
                      jax.ShapeDtypeStruct((rows, LANES), F32), jax.ShapeDtypeStruct((rows, LANES), BF)]
        out_specs += [row_spec(IDX_Q_DIM), row_spec(LANES), row_spec(LANES)]
    return pl.pallas_call(
        functools.partial(_proj_kernel, dsa=dsa),
        grid=(rows // tm,),
        in_specs=[row_spec(D_MODEL),
                  pl.BlockSpec((1, D_MODEL), lambda i: (0, 0)),
                  _mod_spec(scale, tm, rows_per_seq), _mod_spec(shift, tm, rows_per_seq),
                  tab_spec, tab_spec,
                  pl.BlockSpec((D_MODEL, n_in), lambda i: (0, 0))],
        out_specs=out_specs,
        out_shape=out_shape,
        compiler_params=_params("parallel"),
        name="project_dsa" if dsa else "project",
    )(x, g, scale, shift, cos_tab, sin_tab, w)


def _rope_rows(zt, cos_t, sin_t, n_heads, scale, dtype):
    half = HEAD_DIM // 2
    out = []
    for h in range(n_heads):
        x = zt[h * HEAD_DIM:(h + 1) * HEAD_DIM]
        partner = jnp.concatenate([x[half:], x[:half]], axis=0)
        out.append(((x * cos_t + partner * sin_t) * scale).astype(dtype))
    return out


def _proj_t_kernel(x_ref, g_ref, sc_ref, sh_ref, cos_ref, sin_ref, cost_ref, sint_ref, w_ref, wqt_ref, wkt_ref, wvt_ref,
                   *refs, dsa):
    if dsa:
        wit_ref, wrt_ref = refs[:2]
        refs = refs[2:]
    qt_ref, kt_ref, vtf_ref, kz_ref, vt_ref, kmean_ref = refs[:6]
    hb = _norm_mod(x_ref[...], g_ref[...], sc_ref[...], sh_ref[...]).astype(BF)
    lane = lax.broadcasted_iota(I32, (1, LANES), 1)
    lo_half = (lane % HEAD_DIM) < (HEAD_DIM // 2)

    zqt = lax.dot_general(wqt_ref[...], hb, NT, preferred_element_type=F32)
    cos_t, sin_t = cost_ref[...], sint_ref[...]
    for h, qh in enumerate(_rope_rows(zqt, cos_t, sin_t, N_HEADS, ATTN_SCALE * LOG2E, BF)):
        qt_ref[h * HEAD_DIM:(h + 1) * HEAD_DIM, :] = qh
    zkt = lax.dot_general(wkt_ref[...], hb, NT, preferred_element_type=F32)
    for h, kh in enumerate(_rope_rows(zkt, cos_t, sin_t, N_KV_HEADS, 1.0, F32)):
        kt_ref[h * HEAD_DIM:(h + 1) * HEAD_DIM, :] = kh
    zvt = lax.dot_general(wvt_ref[...], hb, NT, preferred_element_type=F32)
    vtf_ref[...] = zvt
    vt_ref[...] = zvt.astype(BF)
    if dsa:
        qit_ref, wt_ref, kit_ref, kiz_ref = refs[6:]
        zit = lax.dot_general(wit_ref[...], hb, NT, preferred_element_type=F32)
        for h, qh in enumerate(_rope_rows(zit, cos_t, sin_t, IDX_HEADS, 1.0, BF)):
            qit_ref[h * IDX_DIM:(h + 1) * IDX_DIM, :] = qh
        zr = jnp.dot(hb, w_ref[:, IN_ATTN + IDX_Q_DIM:], preferred_element_type=F32)
        kiz_ref[...] = jnp.where(lane < IDX_DIM, _rope_chunk(zr, cos_ref[...], sin_ref[...], lo_half), 0.0).astype(BF)
        zrt = lax.dot_general(wrt_ref[...], hb, NT, preferred_element_type=F32)
        kit_ref[...] = _rope_rows(zrt, cos_t, sin_t, 1, 1.0, F32)[0]
        wt_ref[...] = zrt[IDX_DIM:IDX_DIM + IDX_HEADS] * IDX_W_SCALE
    zk = jnp.dot(hb, w_ref[:, Q_DIM:Q_DIM + KV_DIM], preferred_element_type=F32)
    tm = zk.shape[0]
    for c in range(KV_DIM // LANES):
        sl = slice(c * LANES, (c + 1) * LANES)
        kr = _rope_chunk(zk[:, sl], cos_ref[...], sin_ref[...], lo_half)
        kmean_ref[:, :, sl] = jnp.mean(kr.reshape(tm // MOBA_BLOCK, MOBA_BLOCK, LANES), axis=1, keepdims=True)
        kz_ref[:, 2 * c * LANES:(2 * c + 1) * LANES] = jnp.where(lane < HEAD_DIM, kr, 0.0).astype(BF)
        kz_ref[:, (2 * c + 1) * LANES:(2 * c + 2) * LANES] = jnp.where(
            lane < HEAD_DIM, pltpu.roll(kr, HEAD_DIM, 1), 0.0).astype(BF)


def _project_t(x, g, scale, shift, cos_tab, sin_tab, cos_t, sin_t, w, rows_per_seq, dsa):
    rows = x.shape[0]
    tm = min(ROW_TILE, rows)
    assert rows_per_seq % tm == 0 and tm % MOBA_BLOCK == 0
    tab_tiles = cos_tab.shape[0] // tm
    tiles_per_seq = rows_per_seq // tm
    row_spec = lambda n: pl.BlockSpec((tm, n), lambda i: (i, 0))
    col_spec = lambda n: pl.BlockSpec((n, tm), lambda i: (0, i))
    seq_spec = lambda n: pl.BlockSpec((None, n, tm), lambda i: (i // tiles_per_seq, 0, i % tiles_per_seq))
    seq_shape = lambda n: jax.ShapeDtypeStruct((rows // rows_per_seq, n, rows_per_seq), F32)
    tab_spec = pl.BlockSpec((tm, LANES), lambda i: (i % tab_tiles, 0))
    tabt_spec = pl.BlockSpec((HEAD_DIM, tm), lambda i: (0, i % tab_tiles))
    whole = lambda a: pl.BlockSpec(a.shape, lambda i: (0, 0))
    weights = [w, w[:, :Q_DIM].T, w[:, Q_DIM:Q_DIM + KV_DIM].T, w[:, Q_DIM + KV_DIM:IN_ATTN].T]
    out_specs = [col_spec(Q_DIM), seq_spec(KV_DIM), seq_spec(KV_DIM), row_spec(2 * KV_DIM), col_spec(KV_DIM),
                 pl.BlockSpec((tm // MOBA_BLOCK, 1, KV_DIM), lambda i: (i, 0, 0))]
    out_shape = [jax.ShapeDtypeStruct((Q_DIM, rows), BF), seq_shape(KV_DIM), seq_shape(KV_DIM),
                 jax.ShapeDtypeStruct((rows, 2 * KV_DIM), BF), jax.ShapeDtypeStruct((KV_DIM, rows), BF),
                 jax.ShapeDtypeStruct((rows // MOBA_BLOCK, 1, KV_DIM), F32)]
    if dsa:
        weights += [w[:, IN_ATTN:IN_ATTN + IDX_Q_DIM].T, w[:, IN_ATTN + IDX_Q_DIM:].T]
        out_specs[4] = pl.BlockSpec((None, KV_DIM, tm), lambda i: (i, 0, 0))
        out_shape[4] = jax.ShapeDtypeStruct((rows // tm, KV_DIM, tm), BF)
        out_specs += [col_spec(IDX_Q_DIM), col_spec(IDX_HEADS), seq_spec(IDX_DIM), row_spec(LANES)]
        out_shape += [jax.ShapeDtypeStruct((IDX_Q_DIM, rows), BF), jax.ShapeDtypeStruct((IDX_HEADS, rows), F32),
                      seq_shape(IDX_DIM), jax.ShapeDtypeStruct((rows, LANES), BF)]
    return pl.pallas_call(
        functools.partial(_proj_t_kernel, dsa=dsa),
        grid=(rows // tm,),
        in_specs=[row_spec(D_MODEL),
                  pl.BlockSpec((1, D_MODEL), lambda i: (0, 0)),
                  _mod_spec(scale, tm, rows_per_seq), _mod_spec(shift, tm, rows_per_seq),
                  tab_spec, tab_spec, tabt_spec, tabt_spec] + [whole(a) for a in weights],
        out_specs=out_specs,
        out_shape=out_shape,
        compiler_params=_params("parallel"),
        name="project_t_dsa" if dsa else "project_t",
    )(x, g, scale, shift, cos_tab, sin_tab, cos_t, sin_t, *weights)


def _post_kernel(x_ref, o_ref, wo_ref, ga_ref, g_ref, sc_ref, sh_ref, gm_ref, w1_ref, w2_ref, y_ref,
                 x1_scr, h_scr, acc_scr):
    j = pl.program_id(1)

    @pl.when(j == 0)
    def _():
        x1 = x_ref[...] + ga_ref[...] * jnp.dot(o_ref[...], wo_ref[...], preferred_element_type=F32)
        x1_scr[...] = x1
        h_scr[...] = _norm_mod(x1, g_ref[...], sc_ref[...], sh_ref[...]).astype(BF)
        acc_scr[...] = jnp.zeros_like(acc_scr)

    a = jnp.maximum(jnp.dot(h_scr[...], w1_ref[...], preferred_element_type=F32), 0.0)
    acc_scr[...] += jnp.dot((a * a).astype(BF), w2_ref[...], preferred_element_type=F32)

    @pl.when(j == pl.num_programs(1) - 1)
    def _():
        y_ref[...] = x1_scr[...] + gm_ref[...] * acc_scr[...]


def _post(x, o, w_out, gate_a, g_mlp, scale_m, shift_m, gate_m, w1, w2, rows_per_seq):
    rows = x.shape[0]
    tm = min(POST_ROW_TILE, rows)
    tf = FF_TILE
    ms = lambda a: _mod_spec(a, tm, rows_per_seq)
    return pl.pallas_call(
        _post_kernel,
        grid=(rows // tm, D_FF // tf),
        in_specs=[pl.BlockSpec((tm, D_MODEL), lambda i, j: (i, 0)),
                  pl.BlockSpec((tm, Q_DIM), lambda i, j: (i, 0)),
                  pl.BlockSpec((Q_DIM, D_MODEL), lambda i, j: (0, 0)),
                  ms(gate_a),
                  pl.BlockSpec((1, D_MODEL), lambda i, j: (0, 0)),
                  ms(scale_m), ms(shift_m), ms(gate_m),
                  pl.BlockSpec((D_MODEL, tf), lambda i, j: (0, j)),
                  pl.BlockSpec((tf, D_MODEL), lambda i, j: (j, 0))],
        out_specs=pl.BlockSpec((tm, D_MODEL), lambda i, j: (i, 0)),
        out_shape=jax.ShapeDtypeStruct((rows, D_MODEL), F32),
        scratch_shapes=[pltpu.VMEM((tm, D_MODEL), F32), pltpu.VMEM((tm, D_MODEL), BF),
                        pltpu.VMEM((tm, D_MODEL), F32)],
        compiler_params=_params("parallel", "arbitrary"),
        name="post",
    )(x, o, w_out, gate_a, g_mlp, scale_m, shift_m, gate_m, w1, w2)


def _final_norm_kernel(x_ref, g_ref, y_ref):
    x = x_ref[...]
    y_ref[...] = x * lax.rsqrt(jnp.mean(x * x, axis=-1, keepdims=True) + NORM_EPS) * g_ref[...]


def _final_norm(x, g):
    rows = x.shape[0]
    tm = min(ROW_TILE, rows)
    return pl.pallas_call(
        _final_norm_kernel,
        grid=(rows // tm,),
        in_specs=[pl.BlockSpec((tm, D_MODEL), lambda i: (i, 0)), pl.BlockSpec((1, D_MODEL), lambda i: (0, 0))],
        out_specs=pl.BlockSpec((tm, D_MODEL), lambda i: (i, 0)),
        out_shape=jax.ShapeDtypeStruct((rows, D_MODEL), F32),
        compiler_params=_params("parallel"),
        name="final_norm",
    )(x, g)


def _top_mask(gs, ids, n_sel, axis):
    picked = None
    for _ in range(n_sel):
        mx = jnp.max(gs, axis=axis, keepdims=True)
        am = jnp.min(jnp.where(gs == mx, ids, 2 ** 30), axis=axis, keepdims=True)
        hit = (ids == am) & (mx > NEG_INF)
        picked = hit if picked is None else (picked | hit)
        gs = jnp.where(ids == am, NEG_INF, gs)
    return picked


MOBA_MASKED = -1e30

def _moba_prompt_kernel(qi_tab, kp_tab, last_tab, qt_ref, kz_own_ref, vt_own_ref, kz_pair_ref, vt_pair_ref, kmz_ref,
                        o_ref, qz_scr, m_scr, acc_scr, *, n_sel):
    p = pl.program_id(1)
    i = qi_tab[p]
    kp = kp_tab[p]
    blk = MOBA_BLOCK

    @pl.when(kp < 0)
    def _():
        blocks = lax.broadcasted_iota(I32, (LANES - HEAD_DIM, blk), 0)
        earlier = blocks < i
        for h in range(N_HEADS):
            qt = qt_ref[h * HEAD_DIM:(h + 1) * HEAD_DIM, :]
            gs = jnp.dot(kmz_ref[h // GROUP, HEAD_DIM:, :], qt.astype(F32), precision=lax.Precision.HIGHEST,
                         preferred_element_type=F32)
            picked = _top_mask(jnp.where(earlier, gs, NEG_INF), blocks, n_sel, 0)
            usable = (earlier & picked) | (blocks == i)
            bias = jnp.where(usable, 0.0, MOBA_MASKED)
            qz_scr[h // GROUP, :, (h % GROUP) * blk:(h % GROUP + 1) * blk] = jnp.concatenate(
                [qt, bias.astype(BF)], axis=0)
        m_scr[...] = jnp.full(m_scr.shape, NEG_INF, F32)
        acc_scr[...] = jnp.zeros_like(acc_scr)

    def step(kz_ref, vt_ref, n_keys, block_of_key, causal):
        lane = lax.broadcasted_iota(I32, (n_keys, LANES), 1)
        ones_rows = (lax.broadcasted_iota(I32, (ACC_ROWS - HEAD_DIM, n_keys), 0) == 0).astype(BF)
        sts = []
        for kv in range(N_KV_HEADS):
            kz = jnp.where(lane == HEAD_DIM + block_of_key, 1.0, kz_ref[:, kv * LANES:(kv + 1) * LANES]).astype(BF)
            sts.append(jnp.dot(kz, qz_scr[kv], preferred_element_type=F32))
        es, alphas = [], []
        for kv in range(N_KV_HEADS):
            st = sts[kv]
            if causal:
                key_i = lax.broadcasted_iota(I32, (n_keys, GROUP * blk), 0)
                qry_i = lax.broadcasted_iota(I32, (n_keys, GROUP * blk), 1) % blk
                st = jnp.where(key_i <= qry_i, st, NEG_INF)
            m_old = m_scr[kv]
            m_new = jnp.maximum(m_old, jnp.max(st, axis=0, keepdims=True))
            alphas.append(jnp.exp2(m_old - m_new))
            es.append(jnp.exp2(st - m_new).astype(BF))
            m_scr[kv] = m_new
        for kv in range(N_KV_HEADS):
            vaug = jnp.concatenate([vt_ref[kv * HEAD_DIM:(kv + 1) * HEAD_DIM, :], ones_rows], axis=0)
            acc_scr[kv] = alphas[kv] * acc_scr[kv] + jnp.dot(vaug, es[kv], preferred_element_type=F32)

    @pl.when(kp < 0)
    def _():
        step(kz_own_ref, vt_own_ref, blk, i, True)

    @pl.when(kp >= 0)
    def _():
        first = 2 * kp
        second = jnp.where(first + 1 < i, first + 1, i + 1)
        key_row = lax.broadcasted_iota(I32, (2 * blk, 1), 0)
        step(kz_pair_ref, vt_pair_ref, 2 * blk, jnp.where(key_row < blk, first, second), False)

    @pl.when(last_tab[p] == 1)
    def _():
        for c in range(N_HEADS // 2):
            halves = []
            for h in (2 * c, 2 * c + 1):
                a = acc_scr[h // GROUP, :, (h % GROUP) * blk:(h % GROUP + 1) * blk]
                halves.append(a[:HEAD_DIM] * (1.0 / a[HEAD_DIM:HEAD_DIM + 1]))
            o_ref[:, c * LANES:(c + 1) * LANES] = jnp.concatenate(halves, axis=0).T.astype(BF)


def _moba_prompt(qt, kz, vt, kmeans, batch):
    rows = qt.shape[1]
    seq = rows // batch
    blk = MOBA_BLOCK
    assert seq % blk == 0
    nblk = seq // blk
    assert HEAD_DIM + nblk + 1 <= LANES and nblk % 2 == 0
    n_sel = min(MOBA_TOPK, nblk)
    kmean = kmeans.reshape(batch, nblk, N_KV_HEADS, HEAD_DIM).transpose(0, 2, 1, 3)
    kmz = jnp.pad(kmean, ((0, 0), (0, 0), (HEAD_DIM, LANES - HEAD_DIM - nblk), (0, 0)))
    qi, kp, last = [], [], []
    for i in range(nblk):
        order = [-1] + list(range((i + 1) // 2))
        qi += [i] * len(order)
        kp += order
        last += [0] * (len(order) - 1) + [1]
    tabs = [jnp.asarray(np.array(t, np.int32)) for t in (qi, kp, last)]
    npair = nblk // 2
    own_rows = lambda b, p, qi, kp, la: (b * nblk + qi[p], 0)
    own_cols = lambda b, p, qi, kp, la: (0, b * nblk + qi[p])
    grid_spec = pltpu.PrefetchScalarGridSpec(
        num_scalar_prefetch=3,
        grid=(batch, len(qi)),
        in_specs=[pl.BlockSpec((Q_DIM, blk), own_cols),
                  pl.BlockSpec((blk, 2 * KV_DIM), own_rows),
                  pl.BlockSpec((KV_DIM, blk), own_cols),
                  pl.BlockSpec((2 * blk, 2 * KV_DIM), lambda b, p, qi, kp, la: (b * npair + jnp.maximum(kp[p], 0), 0)),
                  pl.BlockSpec((KV_DIM, 2 * blk), lambda b, p, qi, kp, la: (0, b * npair + jnp.maximum(kp[p], 0))),
                  pl.BlockSpec((None, N_KV_HEADS, LANES, HEAD_DIM), lambda b, p, qi, kp, la: (b, 0, 0, 0))],
        out_specs=pl.BlockSpec((blk, Q_DIM), own_rows),
        scratch_shapes=[pltpu.VMEM((N_KV_HEADS, LANES, GROUP * blk), BF), pltpu.VMEM((N_KV_HEADS, 1, GROUP * blk), F32),
                        pltpu.VMEM((N_KV_HEADS, ACC_ROWS, GROUP * blk), F32)])
    return pl.pallas_call(
        functools.partial(_moba_prompt_kernel, n_sel=n_sel),
        grid_spec=grid_spec,
        out_shape=jax.ShapeDtypeStruct((rows, Q_DIM), BF),
        compiler_params=_params("parallel", "arbitrary"),
        name="moba_prompt",
    )(*tabs, qt, kz, vt, kz, vt, kmz)


def _float_key(x):
    b = pltpu.bitcast(x, I32)
    return jnp.where(b < 0, b ^ 0x7FFFFFFF, b)


def _kth_largest_key(count_ge, k_top, shape, bits_per_step=1):
    assert 32 % bits_per_step == 0
    def body(it, cur):
        shift = 32 - bits_per_step * (it + 1)
        best = cur
        for v in range(1, 2 ** bits_per_step):
            cand = cur | jnp.left_shift(jnp.int32(v), shift)
            best = jnp.where(count_ge(cand ^ INT_MIN) >= k_top, cand, best)
        return best
    return lax.fori_loop(0, 32 // bits_per_step, body, jnp.zeros(shape, I32)) ^ INT_MIN


def _tie_cutoff(count_eq_below, need, n_bits, shape):
    def body(it, cur):
        cand = cur | jnp.left_shift(jnp.int32(1), n_bits - 1 - it)
        return jnp.where(count_eq_below(cand) < need, cand, cur)
    return lax.fori_loop(0, n_bits, body, jnp.zeros(shape, I32))


def _swap_halves(x):
    return jnp.concatenate([x[HEAD_DIM:], x[:HEAD_DIM]], axis=0)


def _dsa_prompt_kernel(qt_ref, qit_ref, wt_ref, kiz_ref, kz_ref, vt_ref, o_ref,
                       qp_scr, qip_scr, key_scr, m_scr, acc_scr, *, k_top, n_bits):
    i = pl.program_id(1)
    tq, ch = DSA_QUERY_TILE, DSA_KEY_CHUNK
    n_ch = ((i + 1) * tq + ch - 1) // ch
    qpos = i * tq + lax.broadcasted_iota(I32, (1, tq), 1)
    key_row = lax.broadcasted_iota(I32, (ch, tq), 0)
    ones_rows = (lax.broadcasted_iota(I32, (ACC_ROWS - HEAD_DIM, ch), 0) == 0).astype(BF)

    for c in range(IDX_HEADS // 2):
        pair = qit_ref[c * LANES:(c + 1) * LANES, :]
        qip_scr[0, :, c * tq:(c + 1) * tq] = pair
        qip_scr[1, :, c * tq:(c + 1) * tq] = _swap_halves(pair)
    for c in range(N_HEADS // 2):
        pair = qt_ref[c * LANES:(c + 1) * LANES, :]
        cols = slice((c % 2) * tq, (c % 2 + 1) * tq)
        qp_scr[2 * (c // 2), :, cols] = pair
        qp_scr[2 * (c // 2) + 1, :, cols] = _swap_halves(pair)

    def score_body(c, carry):
        kiz = kiz_ref[pl.ds(pl.multiple_of(c * ch, ch), ch), :]
        sc = jnp.zeros((ch, tq), F32)
        for second in range(2):
            d = jnp.dot(kiz, qip_scr[second], preferred_element_type=F32)
            for u in range(IDX_HEADS // 2):
                h = 2 * u + second
                sc = sc + wt_ref[h:h + 1, :] * jnp.maximum(d[:, u * tq:(u + 1) * tq], 0.0)
        sc = jnp.where(c * ch + key_row <= qpos, sc, NEG_INF)
        key_scr[c] = _float_key(sc)
        return carry
    lax.fori_loop(0, n_ch, score_body, 0)

    def count(pred):
        def body(c, acc):
            return acc + jnp.sum(pred(key_scr[c], c * ch + key_row).astype(I32), axis=0, keepdims=True)
        return lax.fori_loop(0, n_ch, body, jnp.zeros((1, tq), I32))

    thr = _kth_largest_key(lambda t: count(lambda key, idx: key >= t), k_top, (1, tq))
    n_gt = count(lambda key, idx: key > thr)
    n_ge = count(lambda key, idx: key >= thr)
    need = k_top - n_gt
    tie_cols = ((n_ge > k_top) & (thr != KEY_NEG_INF)).astype(I32)
    cutoff = lax.cond(
        jnp.max(tie_cols) > 0,
        lambda: _tie_cutoff(lambda c_: count(lambda key, idx: (key == thr) & (idx < c_)), need, n_bits, (1, tq)),
        lambda: jnp.full((1, tq), 2 ** 30, I32))

    m_scr[...] = jnp.full(m_scr.shape, NEG_INF, F32)
    acc_scr[...] = jnp.zeros_like(acc_scr)

    def attn_body(c, carry):
        key = key_scr[c]
        idx = c * ch + key_row
        keep = ((key > thr) | ((key == thr) & (idx <= cutoff))) & (idx <= qpos)
        bias = jnp.where(keep, 0.0, NEG_INF)
        bias = jnp.concatenate([bias, bias], axis=1)
        rows = pl.ds(pl.multiple_of(c * ch, ch), ch)
        sts = []
        for kv in range(N_KV_HEADS):
            kz = kz_ref[rows, kv * LANES:(kv + 1) * LANES]
            for second in range(2):
                sts.append(jnp.dot(kz, qp_scr[2 * kv + second], preferred_element_type=F32))
        for kv in range(N_KV_HEADS):
            vaug = jnp.concatenate([vt_ref[c, kv * HEAD_DIM:(kv + 1) * HEAD_DIM, :], ones_rows], axis=0)
            for second in range(2):
                g = 2 * kv + second
                st = sts[g] + bias
                m_old = m_scr[g]
                m_new = jnp.maximum(m_old, jnp.max(st, axis=0, keepdims=True))
                m_safe = jnp.where(m_new == NEG_INF, 0.0, m_new)
                alpha = jnp.exp2(m_old - m_safe)
                e = jnp.exp2(st - m_safe).astype(BF)
                acc_scr[g] = alpha * acc_scr[g] + jnp.dot(vaug, e, preferred_element_type=F32)
                m_scr[g] = m_new
        return carry
    lax.fori_loop(0, n_ch, attn_body, 0)

    for kv in range(N_KV_HEADS):
        outs = []
        for second in range(2):
            a = acc_scr[2 * kv + second]
            outs.append(a[:HEAD_DIM] * (1.0 / a[HEAD_DIM:HEAD_DIM + 1]))
        for u in range(2):
            both = jnp.concatenate([outs[0][:, u * tq:(u + 1) * tq], outs[1][:, u * tq:(u + 1) * tq]], axis=0)
            o_ref[:, (2 * kv + u) * LANES:(2 * kv + u + 1) * LANES] = both.T.astype(BF)


def _dsa_prompt(qt, qit, wt, kiz, kz, vt3, batch):
    rows = qt.shape[1]
    seq = rows // batch
    tq, ch = DSA_QUERY_TILE, DSA_KEY_CHUNK
    k_top = min(DSA_TOPK_MAX, seq // 4)
    assert seq % ch == 0 and ch >= k_top and ch % tq == 0 and tq % LANES == 0 and vt3.shape[2] == ch
    nq, n_chunks = seq // tq, seq // ch
    n_bits = max(1, (seq - 1).bit_length())
    tile = lambda n: pl.BlockSpec((n, tq), lambda b, i: (0, b * nq + i))
    return pl.pallas_call(
        functools.partial(_dsa_prompt_kernel, k_top=k_top, n_bits=n_bits),
        grid=(batch, nq),
        in_specs=[tile(Q_DIM), tile(IDX_Q_DIM), tile(IDX_HEADS),
                  pl.BlockSpec((seq, LANES), lambda b, i: (b, 0), pipeline_mode=pl.Buffered(1)),
                  pl.BlockSpec((seq, 2 * KV_DIM), lambda b, i: (b, 0), pipeline_mode=pl.Buffered(1)),
                  pl.BlockSpec((n_chunks, KV_DIM, ch), lambda b, i: (b, 0, 0), pipeline_mode=pl.Buffered(1))],
        out_specs=pl.BlockSpec((tq, Q_DIM), lambda b, i: (b * nq + i, 0)),
        out_shape=jax.ShapeDtypeStruct((rows, Q_DIM), BF),
        scratch_shapes=[pltpu.VMEM((2 * N_KV_HEADS, LANES, 2 * tq), BF),
                        pltpu.VMEM((2, LANES, (IDX_HEADS // 2) * tq), BF),
                        pltpu.VMEM((n_chunks, ch, tq), I32),
                        pltpu.VMEM((2 * N_KV_HEADS, 1, 2 * tq), F32),
                        pltpu.VMEM((2 * N_KV_HEADS, ACC_ROWS, 2 * tq), F32)],
        compiler_params=_params("parallel", "arbitrary"),
        name="dsa_prompt",
    )(qt, qit, wt, kiz, kz, vt3)


def _swa_prompt_kernel(qt_ref, kzp_ref, kzc_ref, vtp_ref, vtc_ref, sink_ref, o_ref):
    n = pl.program_id(1)
    w = WINDOW
    key_i = lax.broadcasted_iota(I32, (2 * w, w), 0)
    qry_i = lax.broadcasted_iota(I32, (2 * w, w), 1)
    keep = (key_i >= qry_i) & (key_i <= qry_i + w) & ((key_i >= w) | (n > 0))
    bias = jnp.where(keep, 0.0, NEG_INF)
    bias = jnp.concatenate([bias, bias], axis=1)
    ones_rows = (lax.broadcasted_iota(I32, (ACC_ROWS - HEAD_DIM, 2 * w), 0) == 0).astype(BF)
    for kv in range(N_KV_HEADS):
        lanes = slice(kv * LANES, (kv + 1) * LANES)
        feats = slice(kv * HEAD_DIM, (kv + 1) * HEAD_DIM)
        kz = jnp.concatenate([kzp_ref[:, lanes], kzc_ref[:, lanes]], axis=0)
        vaug = jnp.concatenate([jnp.concatenate([vtp_ref[feats, :], vtc_ref[feats, :]], axis=1), ones_rows], axis=0)
        pairs = [qt_ref[(2 * kv + u) * LANES:(2 * kv + u + 1) * LANES, :] for u in range(2)]
        outs = []
        for second in range(2):
            ops = pairs if second == 0 else [_swap_halves(p) for p in pairs]
            st = jnp.dot(kz, jnp.concatenate(ops, axis=1), preferred_element_type=F32) + bias
            sink = sink_ref[2 * kv + second]
            m = jnp.maximum(jnp.max(st, axis=0, keepdims=True), sink)
            acc = jnp.dot(vaug, jnp.exp2(st - m).astype(BF), preferred_element_type=F32)
            outs.append(acc[:HEAD_DIM] * (1.0 / (acc[HEAD_DIM:HEAD_DIM + 1] + jnp.exp2(sink - m))))
        for u in range(2):
            both = jnp.concatenate([outs[0][:, u * w:(u + 1) * w], outs[1][:, u * w:(u + 1) * w]], axis=0)
            o_ref[:, (2 * kv + u) * LANES:(2 * kv + u + 1) * LANES] = both.T.astype(BF)


def _swa_prompt(qt, kz, vt, sinks, batch):
    rows = qt.shape[1]
    seq = rows // batch
    w = WINDOW
    assert seq % w == 0 and w == LANES
    nb = seq // w
    cur = lambda b, n: b * nb + n
    prev = lambda b, n: b * nb + jnp.maximum(n - 1, 0)
    sk = sinks.astype(F32).reshape(N_KV_HEADS, 2, 2)
    sink_cols = jnp.repeat(sk.transpose(0, 2, 1) * LOG2E, w, axis=2).reshape(2 * N_KV_HEADS, 1, 2 * w)
    return pl.pallas_call(
        _swa_prompt_kernel,
        grid=(batch, nb),
        in_specs=[pl.BlockSpec((Q_DIM, w), lambda b, n: (0, cur(b, n))),
                  pl.BlockSpec((w, 2 * KV_DIM), lambda b, n: (prev(b, n), 0)),
                  pl.BlockSpec((w, 2 * KV_DIM), lambda b, n: (cur(b, n), 0)),
                  pl.BlockSpec((KV_DIM, w), lambda b, n: (0, prev(b, n))),
                  pl.BlockSpec((KV_DIM, w), lambda b, n: (0, cur(b, n))),
                  pl.BlockSpec((2 * N_KV_HEADS, 1, 2 * w), lambda b, n: (0, 0, 0))],
        out_specs=pl.BlockSpec((w, Q_DIM), lambda b, n: (cur(b, n), 0)),
        out_shape=jax.ShapeDtypeStruct((rows, Q_DIM), BF),
        compiler_params=_params("parallel", "parallel"),
        name="swa_prompt",
    )(qt, kz, kz, vt, vt, sink_cols)


def _row_token(n_rows, n_tok):
    r = lax.broadcasted_iota(I32, (n_rows, 1), 0)
    return (r // GROUP) % n_tok


def _block_diag_queries(q, n_seq, n_tok):
    qr = q.reshape(n_seq, n_tok, N_KV_HEADS, GROUP, HEAD_DIM).transpose(0, 2, 1, 3, 4)
    qr = qr.reshape(n_seq, N_KV_HEADS, n_tok * GROUP, HEAD_DIM)
    eye = jnp.eye(N_KV_HEADS, dtype=q.dtype)
    qbd = qr[:, :, :, None, :] * eye[None, :, None, :, None]
    return qbd.reshape(n_seq, N_KV_HEADS * n_tok * GROUP, KV_DIM)


def _undiag_outputs(out, n_seq, n_tok):
    o = out.reshape(n_seq, N_KV_HEADS, n_tok, GROUP, N_KV_HEADS, HEAD_DIM)
    o = jnp.stack([o[:, kv, :, :, kv] for kv in range(N_KV_HEADS)], axis=1)
    return o.transpose(0, 2, 1, 3, 4).reshape(n_seq * n_tok, Q_DIM)


def _tokens_on_lanes(cache):
    if cache.ndim == 5:
        t = jnp.transpose(cache, (0, 1, 3, 4, 2))
        return t.reshape(t.shape[:2] + (t.shape[2] * t.shape[3], t.shape[4]))
    return jnp.transpose(cache, (0, 1, 3, 2))


def _page_specs(n_pages_per_step, slot, width):
    def spec(r):
        return pl.BlockSpec((None, None, width, PAGE_SIZE),
                            lambda b, s, pt: (slot, pt[b, s * n_pages_per_step + r], 0, 0))
    return [spec(r) for r in range(n_pages_per_step)]


def _pad_rows(a, n_seq, n_tok):
    a = a.reshape(n_seq, n_tok, a.shape[-1])
    return jnp.pad(a, ((0, 0), (0, SUBLANES - n_tok), (0, 0)))


def _moba_sample_kernel(pt_ref, q_ref, kn_ref, vn_ref, *refs, n_tok, n_sel, pps):
    kt_pages = refs[:pps]
    vt_pages = refs[pps:2 * pps]
    o_ref = refs[2 * pps]
    m_scr, l_scr, g_scr, o_scr = refs[2 * pps + 1:]
    s = pl.program_id(1)
    n_steps = m_scr.shape[0]
    bps = pps // 2
    q = q_ref[...]
    n_row = q.shape[0]
    m_scr[s] = jnp.full(m_scr.shape[1:], NEG_INF, F32)
    l_scr[s] = jnp.zeros(l_scr.shape[1:], F32)
    g_scr[s] = jnp.full(g_scr.shape[1:], NEG_INF, F32)
    kt_all = jnp.concatenate([p[...] for p in kt_pages], axis=1).astype(BF)
    sc_all = jnp.dot(q, kt_all, preferred_element_type=F32)
    for r in range(bps):
        vt = jnp.concatenate([vt_pages[2 * r][...], vt_pages[2 * r + 1][...]], axis=1).astype(BF)
        sc = sc_all[:, r * MOBA_BLOCK:(r + 1) * MOBA_BLOCK]
        m = jnp.max(sc, axis=1, keepdims=True)
        e = jnp.exp(sc - m)
        m_scr[s, :, r:r + 1] = m
        l_scr[s, :, r:r + 1] = jnp.sum(e, axis=1, keepdims=True)
        g_scr[s, :, r:r + 1] = jnp.sum(sc, axis=1, keepdims=True)
        o_scr[s * bps + r] = lax.dot_general(e.astype(BF), vt, NT, preferred_element_type=F32)

    @pl.when(s == pl.num_programs(1) - 1)
    def _():
        lane = lax.broadcasted_iota(I32, (n_row, LANES), 1)
        ids = [t * bps + lane for t in range(n_steps)]
        gs = [g_scr[t] for t in range(n_steps)]
        picked = [None] * n_steps
        for _ in range(n_sel):
            mx = functools.reduce(jnp.maximum, [jnp.max(g, axis=1, keepdims=True) for g in gs])
            am = functools.reduce(jnp.minimum, [jnp.min(jnp.where(g == mx, i_, 2 ** 30), axis=1, keepdims=True)
                                                for g, i_ in zip(gs, ids)])
            for t in range(n_steps):
                hit = (ids[t] == am) & (mx > NEG_INF)
                picked[t] = hit if picked[t] is None else (picked[t] | hit)
                gs[t] = jnp.where(ids[t] == am, NEG_INF, gs[t])
        own_ok = lax.broadcasted_iota(I32, (n_row, SUBLANES), 1) <= _row_token(n_row, n_tok)
        s_own = jnp.where(own_ok, lax.dot_general(q, kn_ref[...], NT, preferred_element_type=F32), NEG_INF)
        m_tot = jnp.max(s_own, axis=1, keepdims=True)
        for t in range(n_steps):
            m_tot = jnp.maximum(m_tot, jnp.max(jnp.where(picked[t], m_scr[t], NEG_INF), axis=1, keepdims=True))
        e_own = jnp.exp(s_own - m_tot)
        den = jnp.sum(e_own, axis=1, keepdims=True)
        num = jnp.dot(e_own.astype(BF), vn_ref[...], preferred_element_type=F32)
        for t in range(n_steps):
            wgt = jnp.where(picked[t], jnp.exp(m_scr[t] - m_tot), 0.0)
            den = den + jnp.sum(wgt * l_scr[t], axis=1, keepdims=True)
            for r in range(bps):
                num = num + wgt[:, r:r + 1] * o_scr[t * bps + r]
        o_ref[...] = num * (1.0 / den)


def _paged_attention_call(kernel, name, slot, page_table, cache_kt, cache_vt, extra_inputs, extra_specs,
                          scratch_shapes, n_row):
    n_seq, n_pages = page_table.shape
    pps = PAGES_PER_STEP
    assert n_pages % pps == 0
    grid_spec = pltpu.PrefetchScalarGridSpec(
        num_scalar_prefetch=1,
        grid=(n_seq, n_pages // pps),
        in_specs=extra_specs + _page_specs(pps, slot, KV_DIM) + _page_specs(pps, slot, KV_DIM),
        out_specs=pl.BlockSpec((None, n_row, KV_DIM), lambda b, s, pt: (b, 0, 0)),
        scratch_shapes=scratch_shapes)
    return pl.pallas_call(
        kernel,
        grid_spec=grid_spec,
        out_shape=jax.ShapeDtypeStruct((n_seq, n_row, KV_DIM), F32),
        compiler_params=_params("parallel", "arbitrary"),
        name=name,
    )(page_table, *extra_inputs, *([cache_kt] * pps), *([cache_vt] * pps))


def _moba_sample(qbd, kn_pad, vn_pad, cache_kt, cache_vt, slot, page_table, n_tok):
    n_seq, n_pages = page_table.shape
    past = n_pages * PAGE_SIZE
    assert past % MOBA_BLOCK == 0 and MOBA_BLOCK == 2 * PAGE_SIZE and PAGES_PER_STEP % 2 == 0
    nblk = past // MOBA_BLOCK
    n_steps = n_pages // PAGES_PER_STEP
    n_row = qbd.shape[1]
    per_seq = lambda w: pl.BlockSpec((None, w[0], w[1]), lambda b, s, pt: (b, 0, 0))
    kernel = functools.partial(_moba_sample_kernel, n_tok=n_tok, n_sel=min(MOBA_TOPK, nblk), pps=PAGES_PER_STEP)
    stats = pltpu.VMEM((n_steps, n_row, LANES), F32)
    return _paged_attention_call(
        kernel, "moba_sample", slot, page_table, cache_kt, cache_vt,
        [qbd, kn_pad, vn_pad],
        [per_seq((n_row, KV_DIM)), per_seq((SUBLANES, KV_DIM)), per_seq((SUBLANES, KV_DIM))],
        [stats, stats, stats, pltpu.VMEM((nblk, n_row, KV_DIM), F32)],
        n_row)


def _dsa_index_kernel(pt_ref, qi_ref, w_ref, kin_ref, *refs, n_tok, k_top, ppi, n_bits):
    ki_pages = refs[:ppi]
    mask_ref = refs[ppi]
    key_scr = refs[ppi + 1]
    s = pl.program_id(1)
    n_steps = pl.num_programs(1)
    n_pages = n_steps * ppi
    qi = qi_ref[...]
    w = w_ref[...]
    fill = jnp.full((SUBLANES - n_tok, LANES), NEG_INF, F32)

    def scores(d):
        wd = (w * jnp.maximum(d, 0.0)).reshape(n_tok, IDX_HEADS, LANES)
        return jnp.sum(wd, axis=1)

    for r in range(ppi):
        sc = scores(jnp.dot(qi, ki_pages[r][...].astype(BF), preferred_element_type=F32))
        key_scr[s * ppi + r] = _float_key(jnp.concatenate([sc, fill], axis=0))

    @pl.when(s == n_steps - 1)
    def _():
        tok = lax.broadcasted_iota(I32, (n_tok, LANES), 0)
        lane_t = lax.broadcasted_iota(I32, (n_tok, LANES), 1)
        sc_new = scores(lax.dot_general(qi, kin_ref[...], NT, preferred_element_type=F32))
        sc_new = jnp.where(lane_t <= tok, sc_new, NEG_INF)
        key_scr[n_pages] = _float_key(jnp.concatenate([sc_new, fill], axis=0))
        tiles = (n_pages + 1, SUBLANES, LANES)
        idx = lax.broadcasted_iota(I32, tiles, 0) * PAGE_SIZE + lax.broadcasted_iota(I32, tiles, 2)

        def count(hit):
            return jnp.sum(jnp.sum(hit.astype(I32), axis=0), axis=1, keepdims=True)

        thr = _kth_largest_key(lambda t: count(key_scr[...] >= t), k_top, (SUBLANES, 1), bits_per_step=2)
        need = k_top - count(key_scr[...] > thr)
        tie_rows = ((count(key_scr[...] >= thr) > k_top) & (thr != KEY_NEG_INF)).astype(I32)
        cutoff = lax.cond(
            jnp.max(tie_rows) > 0,
            lambda: _tie_cutoff(lambda c_: count((key_scr[...] == thr) & (idx < c_)), need, n_bits, (SUBLANES, 1)),
            lambda: jnp.full((SUBLANES, 1), 2 ** 30, I32))
        keys = key_scr[...]
        keep = (keys > thr) | ((keys == thr) & (idx <= cutoff))
        mask_ref[...] = jnp.where(keep & (keys != KEY_NEG_INF), 1.0, 0.0).astype(BF)


def _dsa_index(qi_rows, w_rows, ki_new, cache_idx, islot, page_table, n_tok):
    n_seq, n_pages = page_table.shape
    ppi = min(INDEX_PAGES_PER_STEP, n_pages)
    total = n_pages * PAGE_SIZE + n_tok
    k_top = min(DSA_TOPK_MAX, total // 4)
    assert n_pages % ppi == 0 and n_tok <= SUBLANES and n_pages * PAGE_SIZE >= k_top
    n_bits = max(1, ((n_pages + 1) * PAGE_SIZE - 1).bit_length())
    nr = n_tok * IDX_HEADS
    grid_spec = pltpu.PrefetchScalarGridSpec(
        num_scalar_prefetch=1,
        grid=(n_seq, n_pages // ppi),
        in_specs=[pl.BlockSpec((None, nr, IDX_DIM), lambda b, s, pt: (b, 0, 0)),
                  pl.BlockSpec((None, nr, 1), lambda b, s, pt: (b, 0, 0)),
                  pl.BlockSpec((None, LANES, IDX_DIM), lambda b, s, pt: (b, 0, 0))]
        + _page_specs(ppi, islot, IDX_DIM),
        out_specs=pl.BlockSpec((None, n_pages + 1, SUBLANES, LANES), lambda b, s, pt: (b, 0, 0, 0)),
        scratch_shapes=[pltpu.VMEM((n_pages + 1, SUBLANES, LANES), I32)])
    return pl.pallas_call(
        functools.partial(_dsa_index_kernel, n_tok=n_tok, k_top=k_top, ppi=ppi, n_bits=n_bits),
        grid_spec=grid_spec,
        out_shape=jax.ShapeDtypeStruct((n_seq, n_pages + 1, SUBLANES, LANES), BF),
        compiler_params=_params("parallel", "arbitrary"),
        name="dsa_index",
    )(page_table, qi_rows, w_rows, ki_new, *([cache_idx] * ppi))


def _dsa_sample_kernel(pt_ref, q_ref, kn_ref, vn_ref, mask_ref, mnew_ref, *refs, n_tok, pps):
    kt_pages = refs[:pps]
    vt_pages = refs[pps:2 * pps]
    o_ref = refs[2 * pps]
    m_scr, l_scr, acc_scr = refs[2 * pps + 1:]
    s = pl.program_id(1)
    q = q_ref[...]
    n_row = q.shape[0]
    expand = (lax.broadcasted_iota(I32, (n_row, SUBLANES), 1) == _row_token(n_row, n_tok)).astype(BF)

    @pl.when(s == 0)
    def _():
        m_scr[...] = jnp.full(m_scr.shape, NEG_INF, F32)
        l_scr[...] = jnp.zeros_like(l_scr)
        acc_scr[...] = jnp.zeros_like(acc_scr)

    def update(sc, keep, pv):
        sc = jnp.where(keep > 0.5, sc, NEG_INF)
        m_old = m_scr[...]
        m_new = jnp.maximum(m_old, jnp.max(sc, axis=1, keepdims=True))
        m_safe = jnp.where(m_new == NEG_INF, 0.0, m_new)
        alpha = jnp.exp(m_old - m_safe)
        e = jnp.exp(sc - m_safe)
        l_scr[...] = alpha * l_scr[...] + jnp.sum(e, axis=1, keepdims=True)
        acc_scr[...] = alpha * acc_scr[...] + pv(e.astype(BF))
        m_scr[...] = m_new

    kt_all = jnp.concatenate([p[...] for p in kt_pages], axis=1).astype(BF)
    sc_all = jnp.dot(q, kt_all, preferred_element_type=F32)
    keep_all = jnp.dot(expand, jnp.concatenate([mask_ref[r] for r in range(pps)], axis=1),
                       preferred_element_type=F32)
    width = DSA_PAGES_PER_UPDATE * PAGE_SIZE
    for u in range(pps // DSA_PAGES_PER_UPDATE):
        rs = range(u * DSA_PAGES_PER_UPDATE, (u + 1) * DSA_PAGES_PER_UPDATE)
        vt = jnp.concatenate([vt_pages[r][...] for r in rs], axis=1).astype(BF)
        cols = slice(u * width, (u + 1) * width)
        update(sc_all[:, cols], keep_all[:, cols], lambda e, vt=vt: lax.dot_general(e, vt, NT, preferred_element_type=F32))

    @pl.when(s == pl.num_programs(1) - 1)
    def _():
        sc = lax.dot_general(q, kn_ref[...], NT, preferred_element_type=F32)
        keep = jnp.dot(expand, mnew_ref[...], preferred_element_type=F32)[:, :SUBLANES]
        update(sc, keep, lambda e: jnp.dot(e, vn_ref[...], preferred_element_type=F32))
        o_ref[...] = acc_scr[...] * (1.0 / l_scr[...])


def _dsa_sample(qbd, kn_pad, vn_pad, mask, cache_kt, cache_vt, slot, page_table, n_tok):
    n_seq, n_pages = page_table.shape
    n_row = qbd.shape[1]
    pps = PAGES_PER_STEP
    per_seq = lambda w: pl.BlockSpec((None, w[0], w[1]), lambda b, s, pt: (b, 0, 0))
    return _paged_attention_call(
        functools.partial(_dsa_sample_kernel, n_tok=n_tok, pps=pps), "dsa_sample", slot, page_table,
        cache_kt, cache_vt,
        [qbd, kn_pad, vn_pad, mask, mask],
        [per_seq((n_row, KV_DIM)), per_seq((SUBLANES, KV_DIM)), per_seq((SUBLANES, KV_DIM)),
         pl.BlockSpec((None, pps, SUBLANES, LANES), lambda b, s, pt: (b, s, 0, 0)),
         pl.BlockSpec((None, None, SUBLANES, LANES), lambda b, s, pt: (b, n_pages, 0, 0))],
        [pltpu.VMEM((n_row, 1), F32), pltpu.VMEM((n_row, 1), F32), pltpu.VMEM((n_row, KV_DIM), F32)],
        n_row)


def _swa_sample_kernel(q_ref, kt_ref, vt_ref, kn_ref, vn_ref, sink_ref, o_ref, *, n_tok):
    q = q_ref[...]
    n_row = q.shape[0]
    tok_r = _row_token(n_row, n_tok)
    wk = kt_ref.shape[1]
    s_buf = jnp.dot(q, kt_ref[...].astype(BF), preferred_element_type=F32)
    s_buf = jnp.where(lax.broadcasted_iota(I32, (n_row, wk), 1) >= tok_r, s_buf, NEG_INF)
    s_new = lax.dot_general(q, kn_ref[...], NT, preferred_element_type=F32)
    s_new = jnp.where(lax.broadcasted_iota(I32, (n_row, SUBLANES), 1) <= tok_r, s_new, NEG_INF)
    sink = sink_ref[...]
    m = jnp.maximum(jnp.maximum(jnp.max(s_buf, axis=1, keepdims=True), jnp.max(s_new, axis=1, keepdims=True)), sink)
    e_buf = jnp.exp(s_buf - m)
    e_new = jnp.exp(s_new - m)
    den = jnp.sum(e_buf, axis=1, keepdims=True) + jnp.sum(e_new, axis=1, keepdims=True) + jnp.exp(sink - m)
    num = (lax.dot_general(e_buf.astype(BF), vt_ref[...].astype(BF), NT, preferred_element_type=F32)
           + jnp.dot(e_new.astype(BF), vn_ref[...], preferred_element_type=F32))
    o_ref[...] = num * (1.0 / den)


def _swa_sample(qbd, kn_pad, vn_pad, buf_kt, buf_vt, sinks, n_tok):
    n_seq, _, wk = buf_kt.shape
    assert wk == WINDOW
    n_row = qbd.shape[1]
    sink_rows = jnp.repeat(sinks.reshape(N_KV_HEADS, 1, GROUP), n_tok, axis=1).reshape(n_row, 1)
    per_seq = lambda a, c: pl.BlockSpec((None, a, c), lambda b: (b, 0, 0))
    return pl.pallas_call(
        functools.partial(_swa_sample_kernel, n_tok=n_tok),
        grid=(n_seq,),
        in_specs=[per_seq(n_row, KV_DIM), per_seq(KV_DIM, wk), per_seq(KV_DIM, wk),
                  per_seq(SUBLANES, KV_DIM), per_seq(SUBLANES, KV_DIM),
                  pl.BlockSpec((n_row, 1), lambda b: (0, 0))],
        out_specs=per_seq(n_row, KV_DIM),
        out_shape=jax.ShapeDtypeStruct((n_seq, n_row, KV_DIM), F32),
        compiler_params=_params("parallel"),
        name="swa_sample",
    )(qbd, buf_kt, buf_vt, kn_pad, vn_pad, sink_rows)


def _rope_tables(pos):
    half = HEAD_DIM // 2
    inv = ROPE_THETA ** (-jnp.arange(half, dtype=F32) / half)
    ang = pos.astype(F32)[:, None] * inv[None, :]
    cos, sin = jnp.cos(ang), jnp.sin(ang)
    reps = LANES // HEAD_DIM
    return jnp.tile(jnp.concatenate([cos, cos], axis=1), (1, reps)), jnp.tile(jnp.concatenate([-sin, sin], axis=1), (1, reps))


def kernel(x_prompt, x_sample, cache_k, cache_v, cache_idx_k, state_swa_k, state_swa_v, page_table, c_prompt, c_sample, g_attn, g_mlp, w_mod, b_mod, w_in_moba, w_in_dsa, w_in_swa, swa_sinks, w_out, w_ff1, w_ff2, g_final):
    batch, seq, _ = x_prompt.shape
    n_seq, n_tok, _ = x_sample.shape
    depth = g_attn.shape[0]
    n_pages = page_table.shape[1]
    past = n_pages * PAGE_SIZE
    rows_p, rows_s = batch * seq, n_seq * n_tok

    n_cond = batch + n_seq
    pad_cond = -n_cond % SUBLANES
    c_all = jnp.pad(jnp.concatenate([c_prompt, c_sample], axis=0), ((0, pad_cond), (0, 0)))
    mod = _adaln(c_all, w_mod, b_mod)

    def mods(layer):
        parts = [mod[layer, :, r * D_MODEL:(r + 1) * D_MODEL] for r in range(6)]
        mp = [p[:batch].reshape(batch, 1, D_MODEL) for p in parts]
        ms = [jnp.repeat(p[batch:n_cond], n_tok, axis=0) for p in parts]
        return mp, ms

    cos_p, sin_p = _rope_tables(jnp.arange(seq, dtype=jnp.int32))
    cos_s, sin_s = _rope_tables(jnp.tile(past + jnp.arange(n_tok, dtype=jnp.int32), n_seq))
    cos_pt, sin_pt = cos_p[:, :HEAD_DIM].T, sin_p[:, :HEAD_DIM].T

    cache_kt, cache_vt, cache_it = _tokens_on_lanes(cache_k), _tokens_on_lanes(cache_v), _tokens_on_lanes(cache_idx_k)
    swa_kt, swa_vt = _tokens_on_lanes(state_swa_k), _tokens_on_lanes(state_swa_v)

    xp = x_prompt.reshape(rows_p, D_MODEL)
    xs = x_sample.reshape(rows_s, D_MODEL)
    kp_l, vp_l, ks_l, vs_l, ip_l, is_l = [], [], [], [], [], []
    skp_l, svp_l, sks_l, svs_l = [], [], [], []
    for i, (mixer, j, slot) in enumerate(_layer_plan(depth)):
        (sh_ap, sc_ap, gt_ap, sh_mp, sc_mp, gt_mp), (sh_as, sc_as, gt_as, sh_ms, sc_ms, gt_ms) = mods(i)
        dsa = mixer == 1
        if mixer == 0:
            w_in = w_in_moba[j]
        elif dsa:
            w_in = jnp.pad(w_in_dsa[j], ((0, 0), (0, -IN_DSA % LANES)))
        else:
            w_in = w_in_swa[j]
        w_in = w_in.astype(BF)
        g_a = g_attn[i].reshape(1, D_MODEL)
        outs_p = _project_t(xp, g_a, sc_ap, sh_ap, cos_p, sin_p, cos_pt, sin_pt, w_in, seq, dsa)
        qt_p, kt_p, vtf_p, kz_p, vt_p, kmeans_p = outs_p[:6]
        outs_s = _project(xs, g_a, sc_as, sh_as, cos_s, sin_s, w_in, n_tok, dsa)
        q_s, k_s, v_s, kb_s, vb_s = outs_s[:5]
        qbd = _block_diag_queries(q_s, n_seq, n_tok)
        kn_pad = _pad_rows(kb_s, n_seq, n_tok)
        vn_pad = _pad_rows(vb_s, n_seq, n_tok)
        k_p5 = kt_p.reshape(batch, N_KV_HEADS, HEAD_DIM, seq).transpose(0, 3, 1, 2)
        v_p5 = vtf_p.reshape(batch, N_KV_HEADS, HEAD_DIM, seq).transpose(0, 3, 1, 2)
        k_s5 = k_s.reshape(n_seq, n_tok, N_KV_HEADS, HEAD_DIM)
        v_s5 = v_s.reshape(n_seq, n_tok, N_KV_HEADS, HEAD_DIM)
        if mixer == 0:
            o_p = _moba_prompt(qt_p, kz_p, vt_p, kmeans_p, batch)
            o_t = _moba_sample(qbd, kn_pad, vn_pad, cache_kt, cache_vt, slot, page_table, n_tok)
        elif dsa:
            qit_p, wt_p, kit_p, kiz_p = outs_p[6:]
            qi_s, rest_s, restb_s = outs_s[5:]
            o_p = _dsa_prompt(qt_p, qit_p, wt_p, kiz_p, kz_p, vt_p, batch)
            qi_rows = qi_s.reshape(n_seq, n_tok * IDX_HEADS, IDX_DIM)
            w_rows = rest_s[:, IDX_DIM:IDX_DIM + IDX_HEADS].reshape(n_seq, n_tok * IDX_HEADS, 1)
            ki_new = jnp.pad(restb_s[:, :IDX_DIM].reshape(n_seq, n_tok, IDX_DIM),
                             ((0, 0), (0, LANES - n_tok), (0, 0)))
            mask = _dsa_index(qi_rows, w_rows, ki_new, cache_it, j, page_table, n_tok)
            o_t = _dsa_sample(qbd, kn_pad, vn_pad, mask, cache_kt, cache_vt, slot, page_table, n_tok)
            ip_l.append(kit_p.transpose(0, 2, 1))
            is_l.append(rest_s[:, :IDX_DIM].reshape(n_seq, n_tok, IDX_DIM))
        else:
            o_p = _swa_prompt(qt_p, kz_p, vt_p, swa_sinks[j], batch)
            o_t = _swa_sample(qbd, kn_pad, vn_pad, swa_kt[j], swa_vt[j], swa_sinks[j], n_tok)
            keep_p = min(WINDOW, seq)
            skp_l.append(k_p5[:, seq - keep_p:])
            svp_l.append(v_p5[:, seq - keep_p:])
            wk = state_swa_k.shape[2]
            sks_l.append(jnp.concatenate([state_swa_k[j], k_s5], axis=1)[:, -wk:])
            svs_l.append(jnp.concatenate([state_swa_v[j], v_s5], axis=1)[:, -wk:])
        if mixer < 2:
            kp_l.append(k_p5)
            vp_l.append(v_p5)
            ks_l.append(k_s5)
            vs_l.append(v_s5)
        o_s = _undiag_outputs(o_t, n_seq, n_tok).astype(BF)
        wo, w1, w2 = w_out[i].astype(BF), w_ff1[i].astype(BF), w_ff2[i].astype(BF)
        g_m = g_mlp[i].reshape(1, D_MODEL)
        xp = _post(xp, o_p, wo, gt_ap, g_m, sc_mp, sh_mp, gt_mp, w1, w2, seq)
        xs = _post(xs, o_s, wo, gt_as, g_m, sc_ms, sh_ms, gt_ms, w1, w2, n_tok)
    g_f = g_final.reshape(1, D_MODEL)
    y_prompt = _final_norm(xp, g_f).reshape(batch, seq, D_MODEL)
    y_sample = _final_norm(xs, g_f).reshape(n_seq, n_tok, D_MODEL)
    return (y_prompt, y_sample, jnp.stack(kp_l), jnp.stack(vp_l), jnp.stack(ks_l), jnp.stack(vs_l),
            jnp.stack(ip_l), jnp.stack(is_l), jnp.stack(skp_l), jnp.stack(svp_l), jnp.stack(sks_l), jnp.stack(svs_l))
```

```python
import functools

import numpy as np
import jax
import jax.numpy as jnp
from jax import lax
from jax.experimental import pallas as pl
from jax.experimental.pallas import tpu as pltpu

D_MODEL = 1024
N_HEADS = 16
HEAD_DIM = D_MODEL // N_HEADS
N_KV_HEADS = 4
GROUP = N_HEADS // N_KV_HEADS
Q_DIM = N_HEADS * HEAD_DIM
KV_DIM = N_KV_HEADS * HEAD_DIM
D_FF = 4 * D_MODEL
ROPE_THETA = 10000.0
NORM_EPS = 1e-6
N_MIXERS = 3
PAGE_SIZE = 128
MOBA_BLOCK = 256
MOBA_TOPK = 3
IDX_HEADS = 8
IDX_DIM = 64
DSA_TOPK_MAX = 256
WINDOW = 128
IN_ATTN = Q_DIM + 2 * KV_DIM
IDX_Q_DIM = IDX_HEADS * IDX_DIM
IN_DSA = IN_ATTN + IDX_Q_DIM + IDX_DIM + IDX_HEADS
ATTN_SCALE = HEAD_DIM ** -0.5
LOG2E = 1.4426950408889634
IDX_W_SCALE = IDX_Q_DIM ** -0.5

LANES = 128
SUBLANES = 8
BF16_SUBLANES = 16
ACC_ROWS = HEAD_DIM + BF16_SUBLANES
VMEM_LIMIT = 56 << 20
ROW_TILE = 512
POST_ROW_TILE = 1024
FF_TILE = 1024
DSA_KEY_CHUNK = 512
DSA_QUERY_TILE = 512
PAGES_PER_STEP = 64
INDEX_PAGES_PER_STEP = 64
SEQS_PER_INDEX_STEP = 2
DSA_PAGES_PER_UPDATE = 4

BF = jnp.bfloat16
F32 = jnp.float32
I32 = jnp.int32
NEG_INF = float("-inf")
INT_MIN = -2 ** 31
KEY_NEG_INF = INT_MIN + 0x7FFFFF
NT = (((1,), (1,)), ((), ()))


def _params(*semantics):
    return pltpu.CompilerParams(dimension_semantics=semantics, vmem_limit_bytes=VMEM_LIMIT)


def _layer_plan(depth):
    plan, counts, n_paged = [], [0] * N_MIXERS, 0
    for i in range(depth):
        m = i % N_MIXERS
        slot = -1
        if m < 2:
            slot = n_paged
            n_paged += 1
        plan.append((m, counts[m], slot))
        counts[m] += 1
    return plan


def _adaln_kernel(c_ref, w_ref, b_ref, o_ref):
    c = c_ref[...]
    a = (c * (1.0 / (1.0 + jnp.exp(-c)))).astype(BF)
    o_ref[...] = jnp.dot(a, w_ref[...].astype(BF), preferred_element_type=F32) + b_ref[...]


def _adaln(c_all, w_mod, b_mod):
    depth, _, n_out = w_mod.shape
    nc = c_all.shape[0]
    tn = 1536
    return pl.pallas_call(
        _adaln_kernel,
        grid=(depth, n_out // tn),
        in_specs=[pl.BlockSpec((nc, D_MODEL), lambda l, j: (0, 0)),
                  pl.BlockSpec((None, D_MODEL, tn), lambda l, j: (l, 0, j)),
                  pl.BlockSpec((None, 1, tn), lambda l, j: (l, 0, j))],
        out_specs=pl.BlockSpec((None, nc, tn), lambda l, j: (l, 0, j)),
        out_shape=jax.ShapeDtypeStruct((depth, nc, n_out), F32),
        compiler_params=_params("parallel", "parallel"),
        name="adaln",
    )(c_all, w_mod, b_mod.reshape(depth, 1, n_out))


def _norm_mod(x, g, scale, shift):
    y = x * lax.rsqrt(jnp.mean(x * x, axis=-1, keepdims=True) + NORM_EPS)
    return (y * g) * (1.0 + scale) + shift


def _rope_chunk(z, cos, sin_signed, lo_half):
    partner = jnp.where(lo_half, pltpu.roll(z, LANES - HEAD_DIM // 2, 1), pltpu.roll(z, HEAD_DIM // 2, 1))
    return z * cos + partner * sin_signed


def _proj_kernel(x_ref, g_ref, sc_ref, sh_ref, cos_ref, sin_ref, w_ref, *out_refs, dsa):
    hb = _norm_mod(x_ref[...], g_ref[...], sc_ref[...], sh_ref[...]).astype(BF)
    cos = cos_ref[...]
    sin_s = sin_ref[...]
    lane = lax.broadcasted_iota(I32, (1, LANES), 1)
    lo_half = (lane % HEAD_DIM) < (HEAD_DIM // 2)
    q_ref, k_ref, v_ref, kb_ref, vb_ref = out_refs[:5]

    zq = jnp.dot(hb, w_ref[:, :Q_DIM], preferred_element_type=F32)
    for c in range(Q_DIM // LANES):
        sl = slice(c * LANES, (c + 1) * LANES)
        q_ref[:, sl] = (_rope_chunk(zq[:, sl], cos, sin_s, lo_half) * ATTN_SCALE).astype(BF)
    zk = jnp.dot(hb, w_ref[:, Q_DIM:Q_DIM + KV_DIM], preferred_element_type=F32)
    for c in range(KV_DIM // LANES):
        sl = slice(c * LANES, (c + 1) * LANES)
        kr = _rope_chunk(zk[:, sl], cos, sin_s, lo_half)
        k_ref[:, sl] = kr
        kb_ref[:, sl] = kr.astype(BF)
    zv = jnp.dot(hb, w_ref[:, Q_DIM + KV_DIM:IN_ATTN], preferred_element_type=F32)
    v_ref[...] = zv
    vb_ref[...] = zv.astype(BF)
    if dsa:
        qi_ref, rest_ref, restb_ref = out_refs[5:]
        zi = jnp.dot(hb, w_ref[:, IN_ATTN:IN_ATTN + IDX_Q_DIM], preferred_element_type=F32)
        for c in range(IDX_Q_DIM // LANES):
            sl = slice(c * LANES, (c + 1) * LANES)
            qi_ref[:, sl] = _rope_chunk(zi[:, sl], cos, sin_s, lo_half).astype(BF)
        zr = jnp.dot(hb, w_ref[:, IN_ATTN + IDX_Q_DIM:], preferred_element_type=F32)
        roped = _rope_chunk(zr, cos, sin_s, lo_half)
        rest = jnp.where(lane < IDX_DIM, roped, zr * IDX_W_SCALE)
        rest_ref[...] = rest
        restb_ref[...] = rest.astype(BF)


def _mod_spec(arr, tm, rows_per_seq):
    if arr.ndim == 3:
        return pl.BlockSpec((None, 1, D_MODEL), lambda i, *_: ((i * tm) // rows_per_seq, 0, 0))
    return pl.BlockSpec((tm, D_MODEL), lambda i, *_: (i, 0))


def _project(x, g, scale, shift, cos_tab, sin_tab, w, rows_per_seq, dsa):
    rows = x.shape[0]
    tm = min(ROW_TILE, rows)
    n_in = w.shape[1]
    tab_tiles = cos_tab.shape[0] // tm
    row_spec = lambda n: pl.BlockSpec((tm, n), lambda i: (i, 0))
    tab_spec = pl.BlockSpec((tm, LANES), lambda i: (i % tab_tiles, 0))
    out_shape = [jax.ShapeDtypeStruct((rows, Q_DIM), BF),
                 jax.ShapeDtypeStruct((rows, KV_DIM), F32), jax.ShapeDtypeStruct((rows, KV_DIM), F32),
                 jax.ShapeDtypeStruct((rows, KV_DIM), BF), jax.ShapeDtypeStruct((rows, KV_DIM), BF)]
    out_specs = [row_spec(Q_DIM), row_spec(KV_DIM), row_spec(KV_DIM), row_spec(KV_DIM), row_spec(KV_DIM)]
    if dsa:
        out_shape += [jax.ShapeDtypeStruct((rows, IDX_Q_DIM), BF),
                      jax.ShapeDtypeStruct((rows, LANES), F32), jax.ShapeDtypeStruct((rows, LANES), BF)]
        out_specs += [row_spec(IDX_Q_DIM), row_spec(LANES), row_spec(LANES)]
    return pl.pallas_call(
        functools.partial(_proj_kernel, dsa=dsa),
        grid=(rows // tm,),
        in_specs=[row_spec(D_MODEL),
                  pl.BlockSpec((1, D_MODEL), lambda i: (0, 0)),
                  _mod_spec(scale, tm, rows_per_seq), _mod_spec(shift, tm, rows_per_seq),
                  tab_spec, tab_spec,
                  pl.BlockSpec((D_MODEL, n_in), lambda i: (0, 0))],
        out_specs=out_specs,
        out_shape=out_shape,
        compiler_params=_params("parallel"),
        name="project_dsa" if dsa else "project",
    )(x, g, scale, shift, cos_tab, sin_tab, w)


def _rope_rows(zt, cos_t, sin_t, n_heads, scale, dtype):
    half = HEAD_DIM // 2
    out = []
    for h in range(n_heads):
        x = zt[h * HEAD_DIM:(h + 1) * HEAD_DIM]
        partner = jnp.concatenate([x[half:], x[:half]], axis=0)
        out.append(((x * cos_t + partner * sin_t) * scale).astype(dtype))
    return out


def _proj_t_kernel(x_ref, g_ref, sc_ref, sh_ref, cos_ref, sin_ref, cost_ref, sint_ref, w_ref, wqt_ref, wkt_ref, wvt_ref,
                   *refs, dsa):
    if dsa:
        wit_ref, wrt_ref = refs[:2]
        refs = refs[2:]
    qt_ref, kt_ref, vtf_ref, kz_ref, vt_ref, kmean_ref = refs[:6]
    hb = _norm_mod(x_ref[...], g_ref[...], sc_ref[...], sh_ref[...]).astype(BF)
    lane = lax.broadcasted_iota(I32, (1, LANES), 1)
    lo_half = (lane % HEAD_DIM) < (HEAD_DIM // 2)

    zqt = lax.dot_general(wqt_ref[...], hb, NT, preferred_element_type=F32)
    cos_t, sin_t = cost_ref[...], sint_ref[...]
    for h, qh in enumerate(_rope_rows(zqt, cos_t, sin_t, N_HEADS, ATTN_SCALE * LOG2E, BF)):
        qt_ref[h * HEAD_DIM:(h + 1) * HEAD_DIM, :] = qh
    zkt = lax.dot_general(wkt_ref[...], hb, NT, preferred_element_type=F32)
    for h, kh in enumerate(_rope_rows(zkt, cos_t, sin_t, N_KV_HEADS, 1.0, F32)):
        kt_ref[h * HEAD_DIM:(h + 1) * HEAD_DIM, :] = kh
    zvt = lax.dot_general(wvt_ref[...], hb, NT, preferred_element_type=F32)
    vtf_ref[...] = zvt
    vt_ref[...] = zvt.astype(BF)
    if dsa:
        qit_ref, wt_ref, kit_ref, kiz_ref = refs[6:]
        zit = lax.dot_general(wit_ref[...], hb, NT, preferred_element_type=F32)
        for h, qh in enumerate(_rope_rows(zit, cos_t, sin_t, IDX_HEADS, 1.0, BF)):
            qit_ref[h * IDX_DIM:(h + 1) * IDX_DIM, :] = qh
        zr = jnp.dot(hb, w_ref[:, IN_ATTN + IDX_Q_DIM:], preferred_element_type=F32)
        kiz_ref[...] = jnp.where(lane < IDX_DIM, _rope_chunk(zr, cos_ref[...], sin_ref[...], lo_half), 0.0).astype(BF)
        zrt = lax.dot_general(wrt_ref[...], hb, NT, preferred_element_type=F32)
        kit_ref[...] = _rope_rows(zrt, cos_t, sin_t, 1, 1.0, F32)[0]
        wt_ref[...] = zrt[IDX_DIM:IDX_DIM + IDX_HEADS] * IDX_W_SCALE
    zk = jnp.dot(hb, w_ref[:, Q_DIM:Q_DIM + KV_DIM], preferred_element_type=F32)
    tm = zk.shape[0]
    for c in range(KV_DIM // LANES):
        sl = slice(c * LANES, (c + 1) * LANES)
        kr = _rope_chunk(zk[:, sl], cos_ref[...], sin_ref[...], lo_half)
        kmean_ref[:, :, sl] = jnp.mean(kr.reshape(tm // MOBA_BLOCK, MOBA_BLOCK, LANES), axis=1, keepdims=True)
        kz_ref[:, 2 * c * LANES:(2 * c + 1) * LANES] = jnp.where(lane < HEAD_DIM, kr, 0.0).astype(BF)
        kz_ref[:, (2 * c + 1) * LANES:(2 * c + 2) * LANES] = jnp.where(
            lane < HEAD_DIM, pltpu.roll(kr, HEAD_DIM, 1), 0.0).astype(BF)


def _project_t(x, g, scale, shift, cos_tab, sin_tab, cos_t, sin_t, w, rows_per_seq, dsa):
    rows = x.shape[0]
    tm = min(ROW_TILE, rows)
    assert rows_per_seq % tm == 0 and tm % MOBA_BLOCK == 0
    tab_tiles = cos_tab.shape[0] // tm
    tiles_per_seq = rows_per_seq // tm
    row_spec = lambda n: pl.BlockSpec((tm, n), lambda i: (i, 0))
    col_spec = lambda n: pl.BlockSpec((n, tm), lambda i: (0, i))
    seq_spec = lambda n: pl.BlockSpec((None, n, tm), lambda i: (i // tiles_per_seq, 0, i % tiles_per_seq))
    seq_shape = lambda n: jax.ShapeDtypeStruct((rows // rows_per_seq, n, rows_per_seq), F32)
    tab_spec = pl.BlockSpec((tm, LANES), lambda i: (i % tab_tiles, 0))
    tabt_spec = pl.BlockSpec((HEAD_DIM, tm), lambda i: (0, i % tab_tiles))
    whole = lambda a: pl.BlockSpec(a.shape, lambda i: (0, 0))
    weights = [w, w[:, :Q_DIM].T, w[:, Q_DIM:Q_DIM + KV_DIM].T, w[:, Q_DIM + KV_DIM:IN_ATTN].T]
    out_specs = [col_spec(Q_DIM), seq_spec(KV_DIM), seq_spec(KV_DIM), row_spec(2 * KV_DIM), col_spec(KV_DIM),
                 pl.BlockSpec((tm // MOBA_BLOCK, 1, KV_DIM), lambda i: (i, 0, 0))]
    out_shape = [jax.ShapeDtypeStruct((Q_DIM, rows), BF), seq_shape(KV_DIM), seq_shape(KV_DIM),
                 jax.ShapeDtypeStruct((rows, 2 * KV_DIM), BF), jax.ShapeDtypeStruct((KV_DIM, rows), BF),
                 jax.ShapeDtypeStruct((rows // MOBA_BLOCK, 1, KV_DIM), F32)]
    if dsa:
        weights += [w[:, IN_ATTN:IN_ATTN + IDX_Q_DIM].T, w[:, IN_ATTN + IDX_Q_DIM:].T]
        out_specs[4] = pl.BlockSpec((None, KV_DIM, tm), lambda i: (i, 0, 0))
        out_shape[4] = jax.ShapeDtypeStruct((rows // tm, KV_DIM, tm), BF)
        out_specs += [col_spec(IDX_Q_DIM), col_spec(IDX_HEADS), seq_spec(IDX_DIM), row_spec(LANES)]
        out_shape += [jax.ShapeDtypeStruct((IDX_Q_DIM, rows), BF), jax.ShapeDtypeStruct((IDX_HEADS, rows), F32),
                      seq_shape(IDX_DIM), jax.ShapeDtypeStruct((rows, LANES), BF)]
    return pl.pallas_call(
        functools.partial(_proj_t_kernel, dsa=dsa),
        grid=(rows // tm,),
        in_specs=[row_spec(D_MODEL),
                  pl.BlockSpec((1, D_MODEL), lambda i: (0, 0)),
                  _mod_spec(scale, tm, rows_per_seq), _mod_spec(shift, tm, rows_per_seq),
                  tab_spec, tab_spec, tabt_spec, tabt_spec] + [whole(a) for a in weights],
        out_specs=out_specs,
        out_shape=out_shape,
        compiler_params=_params("parallel"),
        name="project_t_dsa" if dsa else "project_t",
    )(x, g, scale, shift, cos_tab, sin_tab, cos_t, sin_t, *weights)


def _post_kernel(x_ref, o_ref, wo_ref, ga_ref, g_ref, sc_ref, sh_ref, gm_ref, w1_ref, w2_ref, y_ref,
                 x1_scr, h_scr, acc_scr):
    j = pl.program_id(1)

    @pl.when(j == 0)
    def _():
        x1 = x_ref[...] + ga_ref[...] * jnp.dot(o_ref[...], wo_ref[...], preferred_element_type=F32)
        x1_scr[...] = x1
        h_scr[...] = _norm_mod(x1, g_ref[...], sc_ref[...], sh_ref[...]).astype(BF)
        acc_scr[...] = jnp.zeros_like(acc_scr)

    a = jnp.maximum(jnp.dot(h_scr[...], w1_ref[...], preferred_element_type=F32), 0.0)
    acc_scr[...] += jnp.dot((a * a).astype(BF), w2_ref[...], preferred_element_type=F32)

    @pl.when(j == pl.num_programs(1) - 1)
    def _():
        y_ref[...] = x1_scr[...] + gm_ref[...] * acc_scr[...]


def _post(x, o, w_out, gate_a, g_mlp, scale_m, shift_m, gate_m, w1, w2, rows_per_seq):
    rows = x.shape[0]
    tm = min(POST_ROW_TILE, rows)
    tf = FF_TILE
    ms = lambda a: _mod_spec(a, tm, rows_per_seq)
    return pl.pallas_call(
        _post_kernel,
        grid=(rows // tm, D_FF // tf),
        in_specs=[pl.BlockSpec((tm, D_MODEL), lambda i, j: (i, 0)),
                  pl.BlockSpec((tm, Q_DIM), lambda i, j: (i, 0)),
                  pl.BlockSpec((Q_DIM, D_MODEL), lambda i, j: (0, 0)),
                  ms(gate_a),
                  pl.BlockSpec((1, D_MODEL), lambda i, j: (0, 0)),
                  ms(scale_m), ms(shift_m), ms(gate_m),
                  pl.BlockSpec((D_MODEL, tf), lambda i, j: (0, j)),
                  pl.BlockSpec((tf, D_MODEL), lambda i, j: (j, 0))],
        out_specs=pl.BlockSpec((tm, D_MODEL), lambda i, j: (i, 0)),
        out_shape=jax.ShapeDtypeStruct((rows, D_MODEL), F32),
        scratch_shapes=[pltpu.VMEM((tm, D_MODEL), F32), pltpu.VMEM((tm, D_MODEL), BF),
                        pltpu.VMEM((tm, D_MODEL), F32)],
        compiler_params=_params("parallel", "arbitrary"),
        name="post",
    )(x, o, w_out, gate_a, g_mlp, scale_m, shift_m, gate_m, w1, w2)


def _final_norm_kernel(x_ref, g_ref, y_ref):
    x = x_ref[...]
    y_ref[...] = x * lax.rsqrt(jnp.mean(x * x, axis=-1, keepdims=True) + NORM_EPS) * g_ref[...]


def _final_norm(x, g):
    rows = x.shape[0]
    tm = min(ROW_TILE, rows)
    return pl.pallas_call(
        _final_norm_kernel,
        grid=(rows // tm,),
        in_specs=[pl.BlockSpec((tm, D_MODEL), lambda i: (i, 0)), pl.BlockSpec((1, D_MODEL), lambda i: (0, 0))],
        out_specs=pl.BlockSpec((tm, D_MODEL), lambda i: (i, 0)),
        out_shape=jax.ShapeDtypeStruct((rows, D_MODEL), F32),
        compiler_params=_params("parallel"),
        name="final_norm",
    )(x, g)


def _top_mask(gs, ids, n_sel, axis):
    picked = None
    for _ in range(n_sel):
        mx = jnp.max(gs, axis=axis, keepdims=True)
        am = jnp.min(jnp.where(gs == mx, ids, 2 ** 30), axis=axis, keepdims=True)
        hit = (ids == am) & (mx > NEG_INF)
        picked = hit if picked is None else (picked | hit)
        gs = jnp.where(ids == am, NEG_INF, gs)
    return picked


MOBA_MASKED = -1e30

def _moba_prompt_kernel(qi_tab, kp_tab, last_tab, qt_ref, kz_own_ref, vt_own_ref, kz_pair_ref, vt_pair_ref, kmz_ref,
                        o_ref, qz_scr, m_scr, acc_scr, *, n_sel):
    p = pl.program_id(1)
    i = qi_tab[p]
    kp = kp_tab[p]
    blk = MOBA_BLOCK

    @pl.when(kp < 0)
    def _():
        blocks = lax.broadcasted_iota(I32, (LANES - HEAD_DIM, blk), 0)
        earlier = blocks < i
        for h in range(N_HEADS):
            qt = qt_ref[h * HEAD_DIM:(h + 1) * HEAD_DIM, :]
            gs = jnp.dot(kmz_ref[h // GROUP, HEAD_DIM:, :], qt.astype(F32), precision=lax.Precision.HIGHEST,
                         preferred_element_type=F32)
            picked = _top_mask(jnp.where(earlier, gs, NEG_INF), blocks, n_sel, 0)
            usable = (earlier & picked) | (blocks == i)
            bias = jnp.where(usable, 0.0, MOBA_MASKED)
            qz_scr[h // GROUP, :, (h % GROUP) * blk:(h % GROUP + 1) * blk] = jnp.concatenate(
                [qt, bias.astype(BF)], axis=0)
        m_scr[...] = jnp.full(m_scr.shape, NEG_INF, F32)
        acc_scr[...] = jnp.zeros_like(acc_scr)

    def step(kz_ref, vt_ref, n_keys, block_of_key, causal):
        lane = lax.broadcasted_iota(I32, (n_keys, LANES), 1)
        ones_rows = (lax.broadcasted_iota(I32, (ACC_ROWS - HEAD_DIM, n_keys), 0) == 0).astype(BF)
        sts = []
        for kv in range(N_KV_HEADS):
            kz = jnp.where(lane == HEAD_DIM + block_of_key, 1.0, kz_ref[:, kv * LANES:(kv + 1) * LANES]).astype(BF)
            sts.append(jnp.dot(kz, qz_scr[kv], preferred_element_type=F32))
        es, alphas = [], []
        for kv in range(N_KV_HEADS):
            st = sts[kv]
            if causal:
                key_i = lax.broadcasted_iota(I32, (n_keys, GROUP * blk), 0)
                qry_i = lax.broadcasted_iota(I32, (n_keys, GROUP * blk), 1) % blk
                st = jnp.where(key_i <= qry_i, st, NEG_INF)
            m_old = m_scr[kv]
            m_new = jnp.maximum(m_old, jnp.max(st, axis=0, keepdims=True))
            alphas.append(jnp.exp2(m_old - m_new))
            es.append(jnp.exp2(st - m_new).astype(BF))
            m_scr[kv] = m_new
        for kv in range(N_KV_HEADS):
            vaug = jnp.concatenate([vt_ref[kv * HEAD_DIM:(kv + 1) * HEAD_DIM, :], ones_rows], axis=0)
            acc_scr[kv] = alphas[kv] * acc_scr[kv] + jnp.dot(vaug, es[kv], preferred_element_type=F32)

    @pl.when(kp < 0)
    def _():
        step(kz_own_ref, vt_own_ref, blk, i, True)

    @pl.when(kp >= 0)
    def _():
        first = 2 * kp
        second = jnp.where(first + 1 < i, first + 1, i + 1)
        key_row = lax.broadcasted_iota(I32, (2 * blk, 1), 0)
        step(kz_pair_ref, vt_pair_ref, 2 * blk, jnp.where(key_row < blk, first, second), False)

    @pl.when(last_tab[p] == 1)
    def _():
        for c in range(N_HEADS // 2):
            halves = []
            for h in (2 * c, 2 * c + 1):
                a = acc_scr[h // GROUP, :, (h % GROUP) * blk:(h % GROUP + 1) * blk]
                halves.append(a[:HEAD_DIM] * (1.0 / a[HEAD_DIM:HEAD_DIM + 1]))
            o_ref[:, c * LANES:(c + 1) * LANES] = jnp.concatenate(halves, axis=0).T.astype(BF)


def _moba_prompt(qt, kz, vt, kmeans, batch):
    rows = qt.shape[1]
    seq = rows // batch
    blk = MOBA_BLOCK
    assert seq % blk == 0
    nblk = seq // blk
    assert HEAD_DIM + nblk + 1 <= LANES and nblk % 2 == 0
    n_sel = min(MOBA_TOPK, nblk)
    kmean = kmeans.reshape(batch, nblk, N_KV_HEADS, HEAD_DIM).transpose(0, 2, 1, 3)
    kmz = jnp.pad(kmean, ((0, 0), (0, 0), (HEAD_DIM, LANES - HEAD_DIM - nblk), (0, 0)))
    qi, kp, last = [], [], []
    for i in range(nblk):
        order = [-1] + list(range((i + 1) // 2))
        qi += [i] * len(order)
        kp += order
        last += [0] * (len(order) - 1) + [1]
    tabs = [jnp.asarray(np.array(t, np.int32)) for t in (qi, kp, last)]
    npair = nblk // 2
    own_rows = lambda b, p, qi, kp, la: (b * nblk + qi[p], 0)
    own_cols = lambda b, p, qi, kp, la: (0, b * nblk + qi[p])
    grid_spec = pltpu.PrefetchScalarGridSpec(
        num_scalar_prefetch=3,
        grid=(batch, len(qi)),
        in_specs=[pl.BlockSpec((Q_DIM, blk), own_cols),
                  pl.BlockSpec((blk, 2 * KV_DIM), own_rows),
                  pl.BlockSpec((KV_DIM, blk), own_cols),
                  pl.BlockSpec((2 * blk, 2 * KV_DIM), lambda b, p, qi, kp, la: (b * npair + jnp.maximum(kp[p], 0), 0)),
                  pl.BlockSpec((KV_DIM, 2 * blk), lambda b, p, qi, kp, la: (0, b * npair + jnp.maximum(kp[p], 0))),
                  pl.BlockSpec((None, N_KV_HEADS, LANES, HEAD_DIM), lambda b, p, qi, kp, la: (b, 0, 0, 0))],
        out_specs=pl.BlockSpec((blk, Q_DIM), own_rows),
        scratch_shapes=[pltpu.VMEM((N_KV_HEADS, LANES, GROUP * blk), BF), pltpu.VMEM((N_KV_HEADS, 1, GROUP * blk), F32),
                        pltpu.VMEM((N_KV_HEADS, ACC_ROWS, GROUP * blk), F32)])
    return pl.pallas_call(
        functools.partial(_moba_prompt_kernel, n_sel=n_sel),
        grid_spec=grid_spec,
        out_shape=jax.ShapeDtypeStruct((rows, Q_DIM), BF),
        compiler_params=_params("parallel", "arbitrary"),
        name="moba_prompt",
    )(*tabs, qt, kz, vt, kz, vt, kmz)


def _float_key(x):
    b = pltpu.bitcast(x, I32)
    return jnp.where(b < 0, b ^ 0x7FFFFFFF, b)


def _kth_largest_key(count_ge, k_top, shape, bits_per_step=1):
    assert 32 % bits_per_step == 0
    def body(it, cur):
        shift = 32 - bits_per_step * (it + 1)
        best = cur
        for v in range(1, 2 ** bits_per_step):
            cand = cur | jnp.left_shift(jnp.int32(v), shift)
            best = jnp.where(count_ge(cand ^ INT_MIN) >= k_top, cand, best)
        return best
    return lax.fori_loop(0, 32 // bits_per_step, body, jnp.zeros(shape, I32)) ^ INT_MIN


def _tie_cutoff(count_eq_below, need, n_bits, shape):
    def body(it, cur):
        cand = cur | jnp.left_shift(jnp.int32(1), n_bits - 1 - it)
        return jnp.where(count_eq_below(cand) < need, cand, cur)
    return lax.fori_loop(0, n_bits, body, jnp.zeros(shape, I32))


def _swap_halves(x):
    return jnp.concatenate([x[HEAD_DIM:], x[:HEAD_DIM]], axis=0)


def _dsa_prompt_kernel(qt_ref, qit_ref, wt_ref, kiz_ref, kz_ref, vt_ref, o_ref,
                       qp_scr, qip_scr, key_scr, m_scr, acc_scr, *, k_top, n_bits):
    i = pl.program_id(1)
    tq, ch = DSA_QUERY_TILE, DSA_KEY_CHUNK
    n_ch = ((i + 1) * tq + ch - 1) // ch
    qpos = i * tq + lax.broadcasted_iota(I32, (1, tq), 1)
    key_row = lax.broadcasted_iota(I32, (ch, tq), 0)
    ones_rows = (lax.broadcasted_iota(I32, (ACC_ROWS - HEAD_DIM, ch), 0) == 0).astype(BF)

    for c in range(IDX_HEADS // 2):
        pair = qit_ref[c * LANES:(c + 1) * LANES, :]
        qip_scr[0, :, c * tq:(c + 1) * tq] = pair
        qip_scr[1, :, c * tq:(c + 1) * tq] = _swap_halves(pair)
    for c in range(N_HEADS // 2):
        pair = qt_ref[c * LANES:(c + 1) * LANES, :]
        cols = slice((c % 2) * tq, (c % 2 + 1) * tq)
        qp_scr[2 * (c // 2), :, cols] = pair
        qp_scr[2 * (c // 2) + 1, :, cols] = _swap_halves(pair)

    def score_body(c, carry):
        kiz = kiz_ref[pl.ds(pl.multiple_of(c * ch, ch), ch), :]
        sc = jnp.zeros((ch, tq), F32)
        for second in range(2):
            d = jnp.dot(kiz, qip_scr[second], preferred_element_type=F32)
            for u in range(IDX_HEADS // 2):
                h = 2 * u + second
                sc = sc + wt_ref[h:h + 1, :] * jnp.maximum(d[:, u * tq:(u + 1) * tq], 0.0)
        sc = jnp.where(c * ch + key_row <= qpos, sc, NEG_INF)
        key_scr[c] = _float_key(sc)
        return carry
    lax.fori_loop(0, n_ch, score_body, 0)

    def count(pred):
        def body(c, acc):
            return acc + jnp.sum(pred(key_scr[c], c * ch + key_row).astype(I32), axis=0, keepdims=True)
        return lax.fori_loop(0, n_ch, body, jnp.zeros((1, tq), I32))

    thr = _kth_largest_key(lambda t: count(lambda key, idx: key >= t), k_top, (1, tq))
    n_gt = count(lambda key, idx: key > thr)
    n_ge = count(lambda key, idx: key >= thr)
    need = k_top - n_gt
    tie_cols = ((n_ge > k_top) & (thr != KEY_NEG_INF)).astype(I32)
    cutoff = lax.cond(
        jnp.max(tie_cols) > 0,
        lambda: _tie_cutoff(lambda c_: count(lambda key, idx: (key == thr) & (idx < c_)), need, n_bits, (1, tq)),
        lambda: jnp.full((1, tq), 2 ** 30, I32))

    m_scr[...] = jnp.full(m_scr.shape, NEG_INF, F32)
    acc_scr[...] = jnp.zeros_like(acc_scr)

    def attn_body(c, carry):
        key = key_scr[c]
        idx = c * ch + key_row
        keep = ((key > thr) | ((key == thr) & (idx <= cutoff))) & (idx <= qpos)
        bias = jnp.where(keep, 0.0, NEG_INF)
        bias = jnp.concatenate([bias, bias], axis=1)
        rows = pl.ds(pl.multiple_of(c * ch, ch), ch)
        sts = []
        for kv in range(N_KV_HEADS):
            kz = kz_ref[rows, kv * LANES:(kv + 1) * LANES]
            for second in range(2):
                sts.append(jnp.dot(kz, qp_scr[2 * kv + second], preferred_element_type=F32))
        for kv in range(N_KV_HEADS):
            vaug = jnp.concatenate([vt_ref[c, kv * HEAD_DIM:(kv + 1) * HEAD_DIM, :], ones_rows], axis=0)
            for second in range(2):
                g = 2 * kv + second
                st = sts[g] + bias
                m_old = m_scr[g]
                m_new = jnp.maximum(m_old, jnp.max(st, axis=0, keepdims=True))
                m_safe = jnp.where(m_new == NEG_INF, 0.0, m_new)
                alpha = jnp.exp2(m_old - m_safe)
                e = jnp.exp2(st - m_safe).astype(BF)
                acc_scr[g] = alpha * acc_scr[g] + jnp.dot(vaug, e, preferred_element_type=F32)
                m_scr[g] = m_new
        return carry
    lax.fori_loop(0, n_ch, attn_body, 0)

    for kv in range(N_KV_HEADS):
        outs = []
        for second in range(2):
            a = acc_scr[2 * kv + second]
            outs.append(a[:HEAD_DIM] * (1.0 / a[HEAD_DIM:HEAD_DIM + 1]))
        for u in range(2):
            both = jnp.concatenate([outs[0][:, u * tq:(u + 1) * tq], outs[1][:, u * tq:(u + 1) * tq]], axis=0)
            o_ref[:, (2 * kv + u) * LANES:(2 * kv + u + 1) * LANES] = both.T.astype(BF)


def _dsa_prompt(qt, qit, wt, kiz, kz, vt3, batch):
    rows = qt.shape[1]
    seq = rows // batch
    tq, ch = DSA_QUERY_TILE, DSA_KEY_CHUNK
    k_top = min(DSA_TOPK_MAX, seq // 4)
    assert seq % ch == 0 and ch >= k_top and ch % tq == 0 and tq % LANES == 0 and vt3.shape[2] == ch
    nq, n_chunks = seq // tq, seq // ch
    n_bits = max(1, (seq - 1).bit_length())
    tile = lambda n: pl.BlockSpec((n, tq), lambda b, i: (0, b * nq + i))
    return pl.pallas_call(
        functools.partial(_dsa_prompt_kernel, k_top=k_top, n_bits=n_bits),
        grid=(batch, nq),
        in_specs=[tile(Q_DIM), tile(IDX_Q_DIM), tile(IDX_HEADS),
                  pl.BlockSpec((seq, LANES), lambda b, i: (b, 0), pipeline_mode=pl.Buffered(1)),
                  pl.BlockSpec((seq, 2 * KV_DIM), lambda b, i: (b, 0), pipeline_mode=pl.Buffered(1)),
                  pl.BlockSpec((n_chunks, KV_DIM, ch), lambda b, i: (b, 0, 0), pipeline_mode=pl.Buffered(1))],
        out_specs=pl.BlockSpec((tq, Q_DIM), lambda b, i: (b * nq + i, 0)),
        out_shape=jax.ShapeDtypeStruct((rows, Q_DIM), BF),
        scratch_shapes=[pltpu.VMEM((2 * N_KV_HEADS, LANES, 2 * tq), BF),
                        pltpu.VMEM((2, LANES, (IDX_HEADS // 2) * tq), BF),
                        pltpu.VMEM((n_chunks, ch, tq), I32),
                        pltpu.VMEM((2 * N_KV_HEADS, 1, 2 * tq), F32),
                        pltpu.VMEM((2 * N_KV_HEADS, ACC_ROWS, 2 * tq), F32)],
        compiler_params=_params("parallel", "arbitrary"),
        name="dsa_prompt",
    )(qt, qit, wt, kiz, kz, vt3)


def _swa_prompt_kernel(qt_ref, kzp_ref, kzc_ref, vtp_ref, vtc_ref, sink_ref, o_ref):
    n = pl.program_id(1)
    w = WINDOW
    key_i = lax.broadcasted_iota(I32, (2 * w, w), 0)
    qry_i = lax.broadcasted_iota(I32, (2 * w, w), 1)
    keep = (key_i >= qry_i) & (key_i <= qry_i + w) & ((key_i >= w) | (n > 0))
    bias = jnp.where(keep, 0.0, NEG_INF)
    bias = jnp.concatenate([bias, bias], axis=1)
    ones_rows = (lax.broadcasted_iota(I32, (ACC_ROWS - HEAD_DIM, 2 * w), 0) == 0).astype(BF)
    sts = []
    for kv in range(N_KV_HEADS):
        lanes = slice(kv * LANES, (kv + 1) * LANES)
        kz = jnp.concatenate([kzp_ref[:, lanes], kzc_ref[:, lanes]], axis=0)
        pairs = [qt_ref[(2 * kv + u) * LANES:(2 * kv + u + 1) * LANES, :] for u in range(2)]
        for second in range(2):
            ops = pairs if second == 0 else [_swap_halves(p) for p in pairs]
            sts.append(jnp.dot(kz, jnp.concatenate(ops, axis=1), preferred_element_type=F32))
    for kv in range(N_KV_HEADS):
        feats = slice(kv * HEAD_DIM, (kv + 1) * HEAD_DIM)
        vaug = jnp.concatenate([jnp.concatenate([vtp_ref[feats, :], vtc_ref[feats, :]], axis=1), ones_rows], axis=0)
        outs = []
        for second in range(2):
            st = sts[2 * kv + second] + bias
            sink = sink_ref[2 * kv + second]
            m = jnp.maximum(jnp.max(st, axis=0, keepdims=True), sink)
            acc = jnp.dot(vaug, jnp.exp2(st - m).astype(BF), preferred_element_type=F32)
            outs.append(acc[:HEAD_DIM] * (1.0 / (acc[HEAD_DIM:HEAD_DIM + 1] + jnp.exp2(sink - m))))
        for u in range(2):
            both = jnp.concatenate([outs[0][:, u * w:(u + 1) * w], outs[1][:, u * w:(u + 1) * w]], axis=0)
            o_ref[:, (2 * kv + u) * LANES:(2 * kv + u + 1) * LANES] = both.T.astype(BF)


def _swa_prompt(qt, kz, vt, sinks, batch):
    rows = qt.shape[1]
    seq = rows // batch
    w = WINDOW
    assert seq % w == 0 and w == LANES
    nb = seq // w
    cur = lambda b, n: b * nb + n
    prev = lambda b, n: b * nb + jnp.maximum(n - 1, 0)
    sk = sinks.astype(F32).reshape(N_KV_HEADS, 2, 2)
    sink_cols = jnp.repeat(sk.transpose(0, 2, 1) * LOG2E, w, axis=2).reshape(2 * N_KV_HEADS, 1, 2 * w)
    return pl.pallas_call(
        _swa_prompt_kernel,
        grid=(batch, nb),
        in_specs=[pl.BlockSpec((Q_DIM, w), lambda b, n: (0, cur(b, n))),
                  pl.BlockSpec((w, 2 * KV_DIM), lambda b, n: (prev(b, n), 0)),
                  pl.BlockSpec((w, 2 * KV_DIM), lambda b, n: (cur(b, n), 0)),
                  pl.BlockSpec((KV_DIM, w), lambda b, n: (0, prev(b, n))),
                  pl.BlockSpec((KV_DIM, w), lambda b, n: (0, cur(b, n))),
                  pl.BlockSpec((2 * N_KV_HEADS, 1, 2 * w), lambda b, n: (0, 0, 0))],
        out_specs=pl.BlockSpec((w, Q_DIM), lambda b, n: (cur(b, n), 0)),
        out_shape=jax.ShapeDtypeStruct((rows, Q_DIM), BF),
        compiler_params=_params("parallel", "parallel"),
        name="swa_prompt",
    )(qt, kz, kz, vt, vt, sink_cols)


def _row_token(n_rows, n_tok):
    r = lax.broadcasted_iota(I32, (n_rows, 1), 0)
    return (r // GROUP) % n_tok


def _block_diag_queries(q, n_seq, n_tok):
    qr = q.reshape(n_seq, n_tok, N_KV_HEADS, GROUP, HEAD_DIM).transpose(0, 2, 1, 3, 4)
    qr = qr.reshape(n_seq, N_KV_HEADS, n_tok * GROUP, HEAD_DIM)
    eye = jnp.eye(N_KV_HEADS, dtype=q.dtype)
    qbd = qr[:, :, :, None, :] * eye[None, :, None, :, None]
    return qbd.reshape(n_seq, N_KV_HEADS * n_tok * GROUP, KV_DIM)


def _undiag_outputs(out, n_seq, n_tok):
    o = out.reshape(n_seq, N_KV_HEADS, n_tok, GROUP, N_KV_HEADS, HEAD_DIM)
    o = jnp.stack([o[:, kv, :, :, kv] for kv in range(N_KV_HEADS)], axis=1)
    return o.transpose(0, 2, 1, 3, 4).reshape(n_seq * n_tok, Q_DIM)


def _tokens_on_lanes(cache):
    if cache.ndim == 5:
        t = jnp.transpose(cache, (0, 1, 3, 4, 2))
        return t.reshape(t.shape[:2] + (t.shape[2] * t.shape[3], t.shape[4]))
    return jnp.transpose(cache, (0, 1, 3, 2))


def _page_specs(n_pages_per_step, slot, width, seq_of=lambda b: b):
    def spec(r):
        return pl.BlockSpec((None, None, width, PAGE_SIZE),
                            lambda b, s, pt: (slot, pt[seq_of(b), s * n_pages_per_step + r], 0, 0))
    return [spec(r) for r in range(n_pages_per_step)]


def _pad_rows(a, n_seq, n_tok):
    a = a.reshape(n_seq, n_tok, a.shape[-1])
    return jnp.pad(a, ((0, 0), (0, SUBLANES - n_tok), (0, 0)))


def _moba_sample_kernel(pt_ref, q_ref, kn_ref, vn_ref, *refs, n_tok, n_sel, pps):
    kt_pages = refs[:pps]
    vt_pages = refs[pps:2 * pps]
    o_ref = refs[2 * pps]
    m_scr, l_scr, g_scr, o_scr = refs[2 * pps + 1:]
    s = pl.program_id(1)
    n_steps = m_scr.shape[0]
    bps = pps // 2
    q = q_ref[...]
    n_row = q.shape[0]
    m_scr[s] = jnp.full(m_scr.shape[1:], NEG_INF, F32)
    l_scr[s] = jnp.zeros(l_scr.shape[1:], F32)
    g_scr[s] = jnp.full(g_scr.shape[1:], NEG_INF, F32)
    kt_all = jnp.concatenate([p[...] for p in kt_pages], axis=1).astype(BF)
    sc_all = jnp.dot(q, kt_all, preferred_element_type=F32)
    for r in range(bps):
        vt = jnp.concatenate([vt_pages[2 * r][...], vt_pages[2 * r + 1][...]], axis=1).astype(BF)
        sc = sc_all[:, r * MOBA_BLOCK:(r + 1) * MOBA_BLOCK]
        m = jnp.max(sc, axis=1, keepdims=True)
        e = jnp.exp(sc - m)
        m_scr[s, :, r:r + 1] = m
        l_scr[s, :, r:r + 1] = jnp.sum(e, axis=1, keepdims=True)
        g_scr[s, :, r:r + 1] = jnp.sum(sc, axis=1, keepdims=True)
        o_scr[s * bps + r] = lax.dot_general(e.astype(BF), vt, NT, preferred_element_type=F32)

    @pl.when(s == pl.num_programs(1) - 1)
    def _():
        lane = lax.broadcasted_iota(I32, (n_row, LANES), 1)
        ids = [t * bps + lane for t in range(n_steps)]
        gs = [g_scr[t] for t in range(n_steps)]
        picked = [None] * n_steps
        for _ in range(n_sel):
            mx = functools.reduce(jnp.maximum, [jnp.max(g, axis=1, keepdims=True) for g in gs])
            am = functools.reduce(jnp.minimum, [jnp.min(jnp.where(g == mx, i_, 2 ** 30), axis=1, keepdims=True)
                                                for g, i_ in zip(gs, ids)])
            for t in range(n_steps):
                hit = (ids[t] == am) & (mx > NEG_INF)
                picked[t] = hit if picked[t] is None else (picked[t] | hit)
                gs[t] = jnp.where(ids[t] == am, NEG_INF, gs[t])
        own_ok = lax.broadcasted_iota(I32, (n_row, SUBLANES), 1) <= _row_token(n_row, n_tok)
        s_own = jnp.where(own_ok, lax.dot_general(q, kn_ref[...], NT, preferred_element_type=F32), NEG_INF)
        m_tot = jnp.max(s_own, axis=1, keepdims=True)
        for t in range(n_steps):
            m_tot = jnp.maximum(m_tot, jnp.max(jnp.where(picked[t], m_scr[t], NEG_INF), axis=1, keepdims=True))
        e_own = jnp.exp(s_own - m_tot)
        den = jnp.sum(e_own, axis=1, keepdims=True)
        num = jnp.dot(e_own.astype(BF), vn_ref[...], preferred_element_type=F32)
        for t in range(n_steps):
            wgt = jnp.where(picked[t], jnp.exp(m_scr[t] - m_tot), 0.0)
            den = den + jnp.sum(wgt * l_scr[t], axis=1, keepdims=True)
            for r in range(bps):
                num = num + wgt[:, r:r + 1] * o_scr[t * bps + r]
        o_ref[...] = num * (1.0 / den)


def _paged_attention_call(kernel, name, slot, page_table, cache_kt, cache_vt, extra_inputs, extra_specs,
                          scratch_shapes, n_row):
    n_seq, n_pages = page_table.shape
    pps = PAGES_PER_STEP
    assert n_pages % pps == 0
    grid_spec = pltpu.PrefetchScalarGridSpec(
        num_scalar_prefetch=1,
        grid=(n_seq, n_pages // pps),
        in_specs=extra_specs + _page_specs(pps, slot, KV_DIM) + _page_specs(pps, slot, KV_DIM),
        out_specs=pl.BlockSpec((None, n_row, KV_DIM), lambda b, s, pt: (b, 0, 0)),
        scratch_shapes=scratch_shapes)
    return pl.pallas_call(
        kernel,
        grid_spec=grid_spec,
        out_shape=jax.ShapeDtypeStruct((n_seq, n_row, KV_DIM), F32),
        compiler_params=_params("parallel", "arbitrary"),
        name=name,
    )(page_table, *extra_inputs, *([cache_kt] * pps), *([cache_vt] * pps))


def _moba_sample(qbd, kn_pad, vn_pad, cache_kt, cache_vt, slot, page_table, n_tok):
    n_seq, n_pages = page_table.shape
    past = n_pages * PAGE_SIZE
    assert past % MOBA_BLOCK == 0 and MOBA_BLOCK == 2 * PAGE_SIZE and PAGES_PER_STEP % 2 == 0
    nblk = past // MOBA_BLOCK
    n_steps = n_pages // PAGES_PER_STEP
    n_row = qbd.shape[1]
    per_seq = lambda w: pl.BlockSpec((None, w[0], w[1]), lambda b, s, pt: (b, 0, 0))
    kernel = functools.partial(_moba_sample_kernel, n_tok=n_tok, n_sel=min(MOBA_TOPK, nblk), pps=PAGES_PER_STEP)
    stats = pltpu.VMEM((n_steps, n_row, LANES), F32)
    return _paged_attention_call(
        kernel, "moba_sample", slot, page_table, cache_kt, cache_vt,
        [qbd, kn_pad, vn_pad],
        [per_seq((n_row, KV_DIM)), per_seq((SUBLANES, KV_DIM)), per_seq((SUBLANES, KV_DIM))],
        [stats, stats, stats, pltpu.VMEM((nblk, n_row, KV_DIM), F32)],
        n_row)


def _dsa_index_kernel(pt_ref, qi_ref, w_ref, kin_ref, *refs, n_tok, k_top, ppi, n_bits):
    n_par = SEQS_PER_INDEX_STEP
    ki_pages = [refs[a * ppi:(a + 1) * ppi] for a in range(n_par)]
    mask_ref = refs[n_par * ppi]
    key_scr = refs[n_par * ppi + 1]
    s = pl.program_id(1)
    n_steps = pl.num_programs(1)
    n_pages = n_steps * ppi
    n_fill = SUBLANES - n_par * n_tok

    def scores(a, d):
        wd = (w_ref[a] * jnp.maximum(d, 0.0)).reshape(n_tok, IDX_HEADS, d.shape[1])
        return jnp.sum(wd, axis=1)

    def tile(parts):
        if n_fill:
            parts = parts + [jnp.full((n_fill, LANES), NEG_INF, F32)]
        return _float_key(jnp.concatenate(parts, axis=0))

    sc_all = []
    for a in range(n_par):
        kit = jnp.concatenate([p[...] for p in ki_pages[a]], axis=1).astype(BF)
        sc_all.append(scores(a, jnp.dot(qi_ref[a], kit, preferred_element_type=F32)))
    for r in range(ppi):
        key_scr[s * ppi + r] = tile([sc[:, r * PAGE_SIZE:(r + 1) * PAGE_SIZE] for sc in sc_all])

    @pl.when(s == n_steps - 1)
    def _():
        tok = lax.broadcasted_iota(I32, (n_tok, LANES), 0)
        lane_t = lax.broadcasted_iota(I32, (n_tok, LANES), 1)
        new = []
        for a in range(n_par):
            sc_new = scores(a, lax.dot_general(qi_ref[a], kin_ref[a], NT, preferred_element_type=F32))
            new.append(jnp.where(lane_t <= tok, sc_new, NEG_INF))
        key_scr[n_pages] = tile(new)
        tiles = (n_pages + 1, SUBLANES, LANES)
        idx = lax.broadcasted_iota(I32, tiles, 0) * PAGE_SIZE + lax.broadcasted_iota(I32, tiles, 2)

        def count(hit):
            return jnp.sum(jnp.sum(hit.astype(I32), axis=0), axis=1, keepdims=True)

        thr = _kth_largest_key(lambda t: count(key_scr[...] >= t), k_top, (SUBLANES, 1), bits_per_step=2)
        need = k_top - count(key_scr[...] > thr)
        tie_rows = ((count(key_scr[...] >= thr) > k_top) & (thr != KEY_NEG_INF)).astype(I32)
        cutoff = lax.cond(
            jnp.max(tie_rows) > 0,
            lambda: _tie_cutoff(lambda c_: count((key_scr[...] == thr) & (idx < c_)), need, n_bits, (SUBLANES, 1)),
            lambda: jnp.full((SUBLANES, 1), 2 ** 30, I32))
        keys = key_scr[...]
        keep = (keys > thr) | ((keys == thr) & (idx <= cutoff))
        mask_ref[...] = jnp.where(keep & (keys != KEY_NEG_INF), 1.0, 0.0).astype(BF)


def _dsa_index(qi_rows, w_rows, ki_new, cache_idx, islot, page_table, n_tok):
    n_seq, n_pages = page_table.shape
    ppi = min(INDEX_PAGES_PER_STEP, n_pages)
    n_par = SEQS_PER_INDEX_STEP
    total = n_pages * PAGE_SIZE + n_tok
    k_top = min(DSA_TOPK_MAX, total // 4)
    assert n_pages % ppi == 0 and n_par * n_tok <= SUBLANES and n_seq % n_par == 0 and n_pages * PAGE_SIZE >= k_top
    n_bits = max(1, ((n_pages + 1) * PAGE_SIZE - 1).bit_length())
    nr = n_tok * IDX_HEADS
    group = lambda n, m: pl.BlockSpec((n_par, n, m), lambda g, s, pt: (g, 0, 0))
    pages = []
    for a in range(n_par):
        pages += _page_specs(ppi, islot, IDX_DIM, seq_of=lambda g, a=a: g * n_par + a)
    grid_spec = pltpu.PrefetchScalarGridSpec(
        num_scalar_prefetch=1,
        grid=(n_seq // n_par, n_pages // ppi),
        in_specs=[group(nr, IDX_DIM), group(nr, 1), group(LANES, IDX_DIM)] + pages,
        out_specs=pl.BlockSpec((None, n_pages + 1, SUBLANES, LANES), lambda g, s, pt: (g, 0, 0, 0)),
        scratch_shapes=[pltpu.VMEM((n_pages + 1, SUBLANES, LANES), I32)])
    return pl.pallas_call(
        functools.partial(_dsa_index_kernel, n_tok=n_tok, k_top=k_top, ppi=ppi, n_bits=n_bits),
        grid_spec=grid_spec,
        out_shape=jax.ShapeDtypeStruct((n_seq // n_par, n_pages + 1, SUBLANES, LANES), BF),
        compiler_params=_params("parallel", "arbitrary"),
        name="dsa_index",
    )(page_table, qi_rows, w_rows, ki_new, *([cache_idx] * (n_par * ppi)))


def _dsa_sample_kernel(pt_ref, q_ref, kn_ref, vn_ref, mask_ref, mnew_ref, *refs, n_tok, pps):
    kt_pages = refs[:pps]
    vt_pages = refs[pps:2 * pps]
    o_ref = refs[2 * pps]
    m_scr, l_scr, acc_scr = refs[2 * pps + 1:]
    s = pl.program_id(1)
    q = q_ref[...]
    n_row = q.shape[0]
    first_row = (pl.program_id(0) % SEQS_PER_INDEX_STEP) * n_tok
    expand = (lax.broadcasted_iota(I32, (n_row, SUBLANES), 1) == _row_token(n_row, n_tok) + first_row).astype(BF)

    @pl.when(s == 0)
    def _():
        m_scr[...] = jnp.full(m_scr.shape, NEG_INF, F32)
        l_scr[...] = jnp.zeros_like(l_scr)
        acc_scr[...] = jnp.zeros_like(acc_scr)

    def update(sc, keep, pv):
        sc = jnp.where(keep > 0.5, sc, NEG_INF)
        m_old = m_scr[...]
        m_new = jnp.maximum(m_old, jnp.max(sc, axis=1, keepdims=True))
        m_safe = jnp.where(m_new == NEG_INF, 0.0, m_new)
        alpha = jnp.exp(m_old - m_safe)
        e = jnp.exp(sc - m_safe)
        l_scr[...] = alpha * l_scr[...] + jnp.sum(e, axis=1, keepdims=True)
        acc_scr[...] = alpha * acc_scr[...] + pv(e.astype(BF))
        m_scr[...] = m_new

    kt_all = jnp.concatenate([p[...] for p in kt_pages], axis=1).astype(BF)
    sc_all = jnp.dot(q, kt_all, preferred_element_type=F32)
    keep_all = jnp.dot(expand, jnp.concatenate([mask_ref[r] for r in range(pps)], axis=1),
                       preferred_element_type=F32)
    width = DSA_PAGES_PER_UPDATE * PAGE_SIZE
    for u in range(pps // DSA_PAGES_PER_UPDATE):
        rs = range(u * DSA_PAGES_PER_UPDATE, (u + 1) * DSA_PAGES_PER_UPDATE)
        vt = jnp.concatenate([vt_pages[r][...] for r in rs], axis=1).astype(BF)
        cols = slice(u * width, (u + 1) * width)
        update(sc_all[:, cols], keep_all[:, cols], lambda e, vt=vt: lax.dot_general(e, vt, NT, preferred_element_type=F32))

    @pl.when(s == pl.num_programs(1) - 1)
    def _():
        sc = lax.dot_general(q, kn_ref[...], NT, preferred_element_type=F32)
        keep = jnp.dot(expand, mnew_ref[...], preferred_element_type=F32)[:, :SUBLANES]
        update(sc, keep, lambda e: jnp.dot(e, vn_ref[...], preferred_element_type=F32))
        o_ref[...] = acc_scr[...] * (1.0 / l_scr[...])


def _dsa_sample(qbd, kn_pad, vn_pad, mask, cache_kt, cache_vt, slot, page_table, n_tok):
    n_seq, n_pages = page_table.shape
    n_row = qbd.shape[1]
    pps = PAGES_PER_STEP
    per_seq = lambda w: pl.BlockSpec((None, w[0], w[1]), lambda b, s, pt: (b, 0, 0))
    return _paged_attention_call(
        functools.partial(_dsa_sample_kernel, n_tok=n_tok, pps=pps), "dsa_sample", slot, page_table,
        cache_kt, cache_vt,
        [qbd, kn_pad, vn_pad, mask, mask],
        [per_seq((n_row, KV_DIM)), per_seq((SUBLANES, KV_DIM)), per_seq((SUBLANES, KV_DIM)),
         pl.BlockSpec((None, pps, SUBLANES, LANES), lambda b, s, pt: (b // SEQS_PER_INDEX_STEP, s, 0, 0)),
         pl.BlockSpec((None, None, SUBLANES, LANES), lambda b, s, pt: (b // SEQS_PER_INDEX_STEP, n_pages, 0, 0))],
        [pltpu.VMEM((n_row, 1), F32), pltpu.VMEM((n_row, 1), F32), pltpu.VMEM((n_row, KV_DIM), F32)],
        n_row)


def _swa_sample_kernel(q_ref, kt_ref, vt_ref, kn_ref, vn_ref, sink_ref, o_ref, *, n_tok):
    q = q_ref[...]
    n_row = q.shape[0]
    tok_r = _row_token(n_row, n_tok)
    wk = kt_ref.shape[1]
    s_buf = jnp.dot(q, kt_ref[...].astype(BF), preferred_element_type=F32)
    s_buf = jnp.where(lax.broadcasted_iota(I32, (n_row, wk), 1) >= tok_r, s_buf, NEG_INF)
    s_new = lax.dot_general(q, kn_ref[...], NT, preferred_element_type=F32)
    s_new = jnp.where(lax.broadcasted_iota(I32, (n_row, SUBLANES), 1) <= tok_r, s_new, NEG_INF)
    sink = sink_ref[...]
    m = jnp.maximum(jnp.maximum(jnp.max(s_buf, axis=1, keepdims=True), jnp.max(s_new, axis=1, keepdims=True)), sink)
    e_buf = jnp.exp(s_buf - m)
    e_new = jnp.exp(s_new - m)
    den = jnp.sum(e_buf, axis=1, keepdims=True) + jnp.sum(e_new, axis=1, keepdims=True) + jnp.exp(sink - m)
    num = (lax.dot_general(e_buf.astype(BF), vt_ref[...].astype(BF), NT, preferred_element_type=F32)
           + jnp.dot(e_new.astype(BF), vn_ref[...], preferred_element_type=F32))
    o_ref[...] = num * (1.0 / den)


def _swa_sample(qbd, kn_pad, vn_pad, buf_kt, buf_vt, sinks, n_tok):
    n_seq, _, wk = buf_kt.shape
    assert wk == WINDOW
    n_row = qbd.shape[1]
    sink_rows = jnp.repeat(sinks.reshape(N_KV_HEADS, 1, GROUP), n_tok, axis=1).reshape(n_row, 1)
    per_seq = lambda a, c: pl.BlockSpec((None, a, c), lambda b: (b, 0, 0))
    return pl.pallas_call(
        functools.partial(_swa_sample_kernel, n_tok=n_tok),
        grid=(n_seq,),
        in_specs=[per_seq(n_row, KV_DIM), per_seq(KV_DIM, wk), per_seq(KV_DIM, wk),
                  per_seq(SUBLANES, KV_DIM), per_seq(SUBLANES, KV_DIM),
                  pl.BlockSpec((n_row, 1), lambda b: (0, 0))],
        out_specs=per_seq(n_row, KV_DIM),
        out_shape=jax.ShapeDtypeStruct((n_seq, n_row, KV_DIM), F32),
        compiler_params=_params("parallel"),
        name="swa_sample",
    )(qbd, buf_kt, buf_vt, kn_pad, vn_pad, sink_rows)


def _rope_tables(pos):
    half = HEAD_DIM // 2
    inv = ROPE_THETA ** (-jnp.arange(half, dtype=F32) / half)
    ang = pos.astype(F32)[:, None] * inv[None, :]
    cos, sin = jnp.cos(ang), jnp.sin(ang)
    reps = LANES // HEAD_DIM
    return jnp.tile(jnp.concatenate([cos, cos], axis=1), (1, reps)), jnp.tile(jnp.concatenate([-sin, sin], axis=1), (1, reps))


def kernel(x_prompt, x_sample, cache_k, cache_v, cache_idx_k, state_swa_k, state_swa_v, page_table, c_prompt, c_sample, g_attn, g_mlp, w_mod, b_mod, w_in_moba, w_in_dsa, w_in_swa, swa_sinks, w_out, w_ff1, w_ff2, g_final):
    batch, seq, _ = x_prompt.shape
    n_seq, n_tok, _ = x_sample.shape
    depth = g_attn.shape[0]
    n_pages = page_table.shape[1]
    past = n_pages * PAGE_SIZE
    rows_p, rows_s = batch * seq, n_seq * n_tok

    n_cond = batch + n_seq
    pad_cond = -n_cond % SUBLANES
    c_all = jnp.pad(jnp.concatenate([c_prompt, c_sample], axis=0), ((0, pad_cond), (0, 0)))
    mod = _adaln(c_all, w_mod, b_mod)

    def mods(layer):
        parts = [mod[layer, :, r * D_MODEL:(r + 1) * D_MODEL] for r in range(6)]
        mp = [p[:batch].reshape(batch, 1, D_MODEL) for p in parts]
        ms = [jnp.repeat(p[batch:n_cond], n_tok, axis=0) for p in parts]
        return mp, ms

    cos_p, sin_p = _rope_tables(jnp.arange(seq, dtype=jnp.int32))
    cos_s, sin_s = _rope_tables(jnp.tile(past + jnp.arange(n_tok, dtype=jnp.int32), n_seq))
    cos_pt, sin_pt = cos_p[:, :HEAD_DIM].T, sin_p[:, :HEAD_DIM].T

    cache_kt, cache_vt, cache_it = _tokens_on_lanes(cache_k), _tokens_on_lanes(cache_v), _tokens_on_lanes(cache_idx_k)
    swa_kt, swa_vt = _tokens_on_lanes(state_swa_k), _tokens_on_lanes(state_swa_v)

    xp = x_prompt.reshape(rows_p, D_MODEL)
    xs = x_sample.reshape(rows_s, D_MODEL)
    kp_l, vp_l, ks_l, vs_l, ip_l, is_l = [], [], [], [], [], []
    skp_l, svp_l, sks_l, svs_l = [], [], [], []
    for i, (mixer, j, slot) in enumerate(_layer_plan(depth)):
        (sh_ap, sc_ap, gt_ap, sh_mp, sc_mp, gt_mp), (sh_as, sc_as, gt_as, sh_ms, sc_ms, gt_ms) = mods(i)
        dsa = mixer == 1
        if mixer == 0:
            w_in = w_in_moba[j]
        elif dsa:
            w_in = jnp.pad(w_in_dsa[j], ((0, 0), (0, -IN_DSA % LANES)))
        else:
            w_in = w_in_swa[j]
        w_in = w_in.astype(BF)
        g_a = g_attn[i].reshape(1, D_MODEL)
        outs_p = _project_t(xp, g_a, sc_ap, sh_ap, cos_p, sin_p, cos_pt, sin_pt, w_in, seq, dsa)
        qt_p, kt_p, vtf_p, kz_p, vt_p, kmeans_p = outs_p[:6]
        outs_s = _project(xs, g_a, sc_as, sh_as, cos_s, sin_s, w_in, n_tok, dsa)
        q_s, k_s, v_s, kb_s, vb_s = outs_s[:5]
        qbd = _block_diag_queries(q_s, n_seq, n_tok)
        kn_pad = _pad_rows(kb_s, n_seq, n_tok)
        vn_pad = _pad_rows(vb_s, n_seq, n_tok)
        k_p5 = kt_p.reshape(batch, N_KV_HEADS, HEAD_DIM, seq).transpose(0, 3, 1, 2)
        v_p5 = vtf_p.reshape(batch, N_KV_HEADS, HEAD_DIM, seq).transpose(0, 3, 1, 2)
        k_s5 = k_s.reshape(n_seq, n_tok, N_KV_HEADS, HEAD_DIM)
        v_s5 = v_s.reshape(n_seq, n_tok, N_KV_HEADS, HEAD_DIM)
        if mixer == 0:
            o_p = _moba_prompt(qt_p, kz_p, vt_p, kmeans_p, batch)
            o_t = _moba_sample(qbd, kn_pad, vn_pad, cache_kt, cache_vt, slot, page_table, n_tok)
        elif dsa:
            qit_p, wt_p, kit_p, kiz_p = outs_p[6:]
            qi_s, rest_s, restb_s = outs_s[5:]
            o_p = _dsa_prompt(qt_p, qit_p, wt_p, kiz_p, kz_p, vt_p, batch)
            qi_rows = qi_s.reshape(n_seq, n_tok * IDX_HEADS, IDX_DIM)
            w_rows = rest_s[:, IDX_DIM:IDX_DIM + IDX_HEADS].reshape(n_seq, n_tok * IDX_HEADS, 1)
            ki_new = jnp.pad(restb_s[:, :IDX_DIM].reshape(n_seq, n_tok, IDX_DIM),
                             ((0, 0), (0, LANES - n_tok), (0, 0)))
            mask = _dsa_index(qi_rows, w_rows, ki_new, cache_it, j, page_table, n_tok)
            o_t = _dsa_sample(qbd, kn_pad, vn_pad, mask, cache_kt, cache_vt, slot, page_table, n_tok)
            ip_l.append(kit_p.transpose(0, 2, 1))
            is_l.append(rest_s[:, :IDX_DIM].reshape(n_seq, n_tok, IDX_DIM))
        else:
            o_p = _swa_prompt(qt_p, kz_p, vt_p, swa_sinks[j], batch)
            o_t = _swa_sample(qbd, kn_pad, vn_pad, swa_kt[j], swa_vt[j], swa_sinks[j], n_tok)
            keep_p = min(WINDOW, seq)
            skp_l.append(k_p5[:, seq - keep_p:])
            svp_l.append(v_p5[:, seq - keep_p:])
            wk = state_swa_k.shape[2]
            sks_l.append(jnp.concatenate([state_swa_k[j], k_s5], axis=1)[:, -wk:])
            svs_l.append(jnp.concatenate([state_swa_v[j], v_s5], axis=1)[:, -wk:])
        if mixer < 2:
            kp_l.append(k_p5)
            vp_l.append(v_p5)
            ks_l.append(k_s5)
            vs_l.append(v_s5)
        o_s = _undiag_outputs(o_t, n_seq, n_tok).astype(BF)
        wo, w1, w2 = w_out[i].astype(BF), w_ff1[i].astype(BF), w_ff2[i].astype(BF)
        g_m = g_mlp[i].reshape(1, D_MODEL)
        xp = _post(xp, o_p, wo, gt_ap, g_m, sc_mp, sh_mp, gt_mp, w1, w2, seq)
        xs = _post(xs, o_s, wo, gt_as, g_m, sc_ms, sh_ms, gt_ms, w1, w2, n_tok)
    g_f = g_final.reshape(1, D_MODEL)
    y_prompt = _final_norm(xp, g_f).reshape(batch, seq, D_MODEL)
    y_sample = _final_norm(xs, g_f).reshape(n_seq, n_tok, D_MODEL)
    return (y_prompt, y_sample, jnp.stack(kp_l), jnp.stack(vp_l), jnp.stack(ks_l), jnp.stack(vs_l),
            jnp.stack(ip_l), jnp.stack(is_l), jnp.stack(skp_l), jnp.stack(svp_l), jnp.stack(sks_l), jnp.stack(svs_l))
```

```python
import functools

import numpy as np
import jax
import jax.numpy as jnp
from jax import lax
from jax.experimental import pallas as pl
from jax.experimental.pallas import tpu as pltpu

D_MODEL = 1024
N_HEADS = 16
HEAD_DIM = D_MODEL // N_HEADS
N_KV_HEADS = 4
GROUP = N_HEADS // N_KV_HEADS
Q_DIM = N_HEADS * HEAD_DIM
KV_DIM = N_KV_HEADS * HEAD_DIM
D_FF = 4 * D_MODEL
ROPE_THETA = 10000.0
NORM_EPS = 1e-6
N_MIXERS = 3
PAGE_SIZE = 128
MOBA_BLOCK = 256
MOBA_TOPK = 3
IDX_HEADS = 8
IDX_DIM = 64
DSA_TOPK_MAX = 256
WINDOW = 128
IN_ATTN = Q_DIM + 2 * KV_DIM
IDX_Q_DIM = IDX_HEADS * IDX_DIM
IN_DSA = IN_ATTN + IDX_Q_DIM + IDX_DIM + IDX_HEADS
ATTN_SCALE = HEAD_DIM ** -0.5
LOG2E = 1.4426950408889634
IDX_W_SCALE = IDX_Q_DIM ** -0.5

LANES = 128
SUBLANES = 8
BF16_SUBLANES = 16
ACC_ROWS = HEAD_DIM + BF16_SUBLANES
VMEM_LIMIT = 56 << 20
ROW_TILE = 512
POST_ROW_TILE = 1024
FF_TILE = 1024
DSA_KEY_CHUNK = 512
DSA_QUERY_TILE = 512
PAGES_PER_STEP = 64
INDEX_PAGES_PER_STEP = 64
SEQS_PER_INDEX_STEP = 2
DSA_PAGES_PER_UPDATE = 4

BF = jnp.bfloat16
F32 = jnp.float32
I32 = jnp.int32
NEG_INF = float("-inf")
INT_MIN = -2 ** 31
KEY_NEG_INF = INT_MIN + 0x7FFFFF
NT = (((1,), (1,)), ((), ()))


def _params(*semantics):
    return pltpu.CompilerParams(dimension_semantics=semantics, vmem_limit_bytes=VMEM_LIMIT)


def _layer_plan(depth):
    plan, counts, n_paged = [], [0] * N_MIXERS, 0
    for i in range(depth):
        m = i % N_MIXERS
        slot = -1
        if m < 2:
            slot = n_paged
            n_paged += 1
        plan.append((m, counts[m], slot))
        counts[m] += 1
    return plan


def _adaln_kernel(c_ref, w_ref, b_ref, o_ref):
    c = c_ref[...]
    a = (c * (1.0 / (1.0 + jnp.exp(-c)))).astype(BF)
    o_ref[...] = jnp.dot(a, w_ref[...].astype(BF), preferred_element_type=F32) + b_ref[...]


def _adaln(c_all, w_mod, b_mod):
    depth, _, n_out = w_mod.shape
    nc = c_all.shape[0]
    tn = 1536
    return pl.pallas_call(
        _adaln_kernel,
        grid=(depth, n_out // tn),
        in_specs=[pl.BlockSpec((nc, D_MODEL), lambda l, j: (0, 0)),
                  pl.BlockSpec((None, D_MODEL, tn), lambda l, j: (l, 0, j)),
                  pl.BlockSpec((None, 1, tn), lambda l, j: (l, 0, j))],
        out_specs=pl.BlockSpec((None, nc, tn), lambda l, j: (l, 0, j)),
        out_shape=jax.ShapeDtypeStruct((depth, nc, n_out), F32),
        compiler_params=_params("parallel", "parallel"),
        name="adaln",
    )(c_all, w_mod, b_mod.reshape(depth, 1, n_out))


def _norm_mod(x, g, scale, shift):
    y = x * lax.rsqrt(jnp.mean(x * x, axis=-1, keepdims=True) + NORM_EPS)
    return (y * g) * (1.0 + scale) + shift


def _rope_chunk(z, cos, sin_signed, lo_half):
    partner = jnp.where(lo_half, pltpu.roll(z, LANES - HEAD_DIM // 2, 1), pltpu.roll(z, HEAD_DIM // 2, 1))
    return z * cos + partner * sin_signed


def _proj_kernel(x_ref, g_ref, sc_ref, sh_ref, cos_ref, sin_ref, w_ref, *out_refs, dsa):
    hb = _norm_mod(x_ref[...], g_ref[...], sc_ref[...], sh_ref[...]).astype(BF)
    cos = cos_ref[...]
    sin_s = sin_ref[...]
    lane = lax.broadcasted_iota(I32, (1, LANES), 1)
    lo_half = (lane % HEAD_DIM) < (HEAD_DIM // 2)
    q_ref, k_ref, v_ref, kb_ref, vb_ref = out_refs[:5]

    zq = jnp.dot(hb, w_ref[:, :Q_DIM], preferred_element_type=F32)
    for c in range(Q_DIM // LANES):
        sl = slice(c * LANES, (c + 1) * LANES)
        q_ref[:, sl] = (_rope_chunk(zq[:, sl], cos, sin_s, lo_half) * ATTN_SCALE).astype(BF)
    zk = jnp.dot(hb, w_ref[:, Q_DIM:Q_DIM + KV_DIM], preferred_element_type=F32)
    for c in range(KV_DIM // LANES):
        sl = slice(c * LANES, (c + 1) * LANES)
        kr = _rope_chunk(zk[:, sl], cos, sin_s, lo_half)
        k_ref[:, sl] = kr
        kb_ref[:, sl] = kr.astype(BF)
    zv = jnp.dot(hb, w_ref[:, Q_DIM + KV_DIM:IN_ATTN], preferred_element_type=F32)
    v_ref[...] = zv
    vb_ref[...] = zv.astype(BF)
    if dsa:
        qi_ref, rest_ref, restb_ref = out_refs[5:]
        zi = jnp.dot(hb, w_ref[:, IN_ATTN:IN_ATTN + IDX_Q_DIM], preferred_element_type=F32)
        for c in range(IDX_Q_DIM // LANES):
            sl = slice(c * LANES, (c + 1) * LANES)
            qi_ref[:, sl] = _rope_chunk(zi[:, sl], cos, sin_s, lo_half).astype(BF)
        zr = jnp.dot(hb, w_ref[:, IN_ATTN + IDX_Q_DIM:], preferred_element_type=F32)
        roped = _rope_chunk(zr, cos, sin_s, lo_half)
        rest = jnp.where(lane < IDX_DIM, roped, zr * IDX_W_SCALE)
        rest_ref[...] = rest
        restb_ref[...] = rest.astype(BF)


def _mod_spec(arr, tm, rows_per_seq):
    if arr.ndim == 3:
        return pl.BlockSpec((None, 1, D_MODEL), lambda i, *_: ((i * tm) // rows_per_seq, 0, 0))
    return pl.BlockSpec((tm, D_MODEL), lambda i, *_: (i, 0))


def _project(x, g, scale, shift, cos_tab, sin_tab, w, rows_per_seq, dsa):
    rows = x.shape[0]
    tm = min(ROW_TILE, rows)
    n_in = w.shape[1]
    tab_tiles = cos_tab.shape[0] // tm
    row_spec = lambda n: pl.BlockSpec((tm, n), lambda i: (i, 0))
    tab_spec = pl.BlockSpec((tm, LANES), lambda i: (i % tab_tiles, 0))
    out_shape = [jax.ShapeDtypeStruct((rows, Q_DIM), BF),
                 jax.ShapeDtypeStruct((rows, KV_DIM), F32), jax.ShapeDtypeStruct((rows, KV_DIM), F32),
                 jax.ShapeDtypeStruct((rows, KV_DIM), BF), jax.ShapeDtypeStruct((rows, KV_DIM), BF)]
    out_specs = [row_spec(Q_DIM), row_spec(KV_DIM), row_spec(KV_DIM), row_spec(KV_DIM), row_spec(KV_DIM)]
    if dsa:
        out_shape += [jax.ShapeDtypeStruct((rows, IDX_Q_DIM), BF),
                      jax.ShapeDtypeStruct((rows, LANES), F32), jax.ShapeDtypeStruct((rows, LANES), BF)]
        out_specs += [row_spec(IDX_Q_DIM), row_spec(LANES), row_spec(LANES)]
    return pl.pallas_call(
        functools.partial(_proj_kernel, dsa=dsa),
        grid=(rows // tm,),
        in_specs=[row_spec(D_MODEL),
                  pl.BlockSpec((1, D_MODEL), lambda i: (0, 0)),
                  _mod_spec(scale, tm, rows_per_seq), _mod_spec(shift, tm, rows_per_seq),
                  tab_spec, tab_spec,
                  pl.BlockSpec((D_MODEL, n_in), lambda i: (0, 0))],
        out_specs=out_specs,
        out_shape=out_shape,
        compiler_params=_params("parallel"),
        name="project_dsa" if dsa else "project",
    )(x, g, scale, shift, cos_tab, sin_tab, w)


def _rope_rows(zt, cos_t, sin_t, n_heads, scale, dtype):
    half = HEAD_DIM // 2
    out = []
    for h in range(n_heads):
        x = zt[h * HEAD_DIM:(h + 1) * HEAD_DIM]
        partner = jnp.concatenate([x[half:], x[:half]], axis=0)
        out.append(((x * cos_t + partner * sin_t) * scale).astype(dtype))
    return out


def _proj_t_kernel(x_ref, g_ref, sc_ref, sh_ref, cos_ref, sin_ref, cost_ref, sint_ref, w_ref, wqt_ref, wkt_ref, wvt_ref,
                   *refs, dsa):
    if dsa:
        wit_ref, wrt_ref = refs[:2]
        refs = refs[2:]
    qt_ref, kt_ref, vtf_ref, kz_ref, vt_ref, kmean_ref = refs[:6]
    hb = _norm_mod(x_ref[...], g_ref[...], sc_ref[...], sh_ref[...]).astype(BF)
    lane = lax.broadcasted_iota(I32, (1, LANES), 1)
    lo_half = (lane % HEAD_DIM) < (HEAD_DIM // 2)

    zqt = lax.dot_general(wqt_ref[...], hb, NT, preferred_element_type=F32)
    cos_t, sin_t = cost_ref[...], sint_ref[...]
    for h, qh in enumerate(_rope_rows(zqt, cos_t, sin_t, N_HEADS, ATTN_SCALE * LOG2E, BF)):
        qt_ref[h * HEAD_DIM:(h + 1) * HEAD_DIM, :] = qh
    zkt = lax.dot_general(wkt_ref[...], hb, NT, preferred_element_type=F32)
    for h, kh in enumerate(_rope_rows(zkt, cos_t, sin_t, N_KV_HEADS, 1.0, F32)):
        kt_ref[h * HEAD_DIM:(h + 1) * HEAD_DIM, :] = kh
    zvt = lax.dot_general(wvt_ref[...], hb, NT, preferred_element_type=F32)
    vtf_ref[...] = zvt
    vt_ref[...] = zvt.astype(BF)
    if dsa:
        qit_ref, wt_ref, kit_ref, kiz_ref = refs[6:]
        zit = lax.dot_general(wit_ref[...], hb, NT, preferred_element_type=F32)
        for h, qh in enumerate(_rope_rows(zit, cos_t, sin_t, IDX_HEADS, 1.0, BF)):
            qit_ref[h * IDX_DIM:(h + 1) * IDX_DIM, :] = qh
        zr = jnp.dot(hb, w_ref[:, IN_ATTN + IDX_Q_DIM:], preferred_element_type=F32)
        kiz_ref[...] = jnp.where(lane < IDX_DIM, _rope_chunk(zr, cos_ref[...], sin_ref[...], lo_half), 0.0).astype(BF)
        zrt = lax.dot_general(wrt_ref[...], hb, NT, preferred_element_type=F32)
        kit_ref[...] = _rope_rows(zrt, cos_t, sin_t, 1, 1.0, F32)[0]
        wt_ref[...] = zrt[IDX_DIM:IDX_DIM + IDX_HEADS] * IDX_W_SCALE
    zk = jnp.dot(hb, w_ref[:, Q_DIM:Q_DIM + KV_DIM], preferred_element_type=F32)
    tm = zk.shape[0]
    for c in range(KV_DIM // LANES):
        sl = slice(c * LANES, (c + 1) * LANES)
        kr = _rope_chunk(zk[:, sl], cos_ref[...], sin_ref[...], lo_half)
        kmean_ref[:, :, sl] = jnp.mean(kr.reshape(tm // MOBA_BLOCK, MOBA_BLOCK, LANES), axis=1, keepdims=True)
        kz_ref[:, 2 * c * LANES:(2 * c + 1) * LANES] = jnp.where(lane < HEAD_DIM, kr, 0.0).astype(BF)
        kz_ref[:, (2 * c + 1) * LANES:(2 * c + 2) * LANES] = jnp.where(
            lane < HEAD_DIM, pltpu.roll(kr, HEAD_DIM, 1), 0.0).astype(BF)


def _project_t(x, g, scale, shift, cos_tab, sin_tab, cos_t, sin_t, w, rows_per_seq, dsa):
    rows = x.shape[0]
    tm = min(ROW_TILE, rows)
    assert rows_per_seq % tm == 0 and tm % MOBA_BLOCK == 0
    tab_tiles = cos_tab.shape[0] // tm
    tiles_per_seq = rows_per_seq // tm
    row_spec = lambda n: pl.BlockSpec((tm, n), lambda i: (i, 0))
    col_spec = lambda n: pl.BlockSpec((n, tm), lambda i: (0, i))
    seq_spec = lambda n: pl.BlockSpec((None, n, tm), lambda i: (i // tiles_per_seq, 0, i % tiles_per_seq))
    seq_shape = lambda n: jax.ShapeDtypeStruct((rows // rows_per_seq, n, rows_per_seq), F32)
    tab_spec = pl.BlockSpec((tm, LANES), lambda i: (i % tab_tiles, 0))
    tabt_spec = pl.BlockSpec((HEAD_DIM, tm), lambda i: (0, i % tab_tiles))
    whole = lambda a: pl.BlockSpec(a.shape, lambda i: (0, 0))
    weights = [w, w[:, :Q_DIM].T, w[:, Q_DIM:Q_DIM + KV_DIM].T, w[:, Q_DIM + KV_DIM:IN_ATTN].T]
    out_specs = [col_spec(Q_DIM), seq_spec(KV_DIM), seq_spec(KV_DIM), row_spec(2 * KV_DIM), col_spec(KV_DIM),
                 pl.BlockSpec((tm // MOBA_BLOCK, 1, KV_DIM), lambda i: (i, 0, 0))]
    out_shape = [jax.ShapeDtypeStruct((Q_DIM, rows), BF), seq_shape(KV_DIM), seq_shape(KV_DIM),
                 jax.ShapeDtypeStruct((rows, 2 * KV_DIM), BF), jax.ShapeDtypeStruct((KV_DIM, rows), BF),
                 jax.ShapeDtypeStruct((rows // MOBA_BLOCK, 1, KV_DIM), F32)]
    if dsa:
        weights += [w[:, IN_ATTN:IN_ATTN + IDX_Q_DIM].T, w[:, IN_ATTN + IDX_Q_DIM:].T]
        out_specs[4] = pl.BlockSpec((None, KV_DIM, tm), lambda i: (i, 0, 0))
        out_shape[4] = jax.ShapeDtypeStruct((rows // tm, KV_DIM, tm), BF)
        out_specs += [col_spec(IDX_Q_DIM), col_spec(IDX_HEADS), seq_spec(IDX_DIM), row_spec(LANES)]
        out_shape += [jax.ShapeDtypeStruct((IDX_Q_DIM, rows), BF), jax.ShapeDtypeStruct((IDX_HEADS, rows), F32),
                      seq_shape(IDX_DIM), jax.ShapeDtypeStruct((rows, LANES), BF)]
    return pl.pallas_call(
        functools.partial(_proj_t_kernel, dsa=dsa),
        grid=(rows // tm,),
        in_specs=[row_spec(D_MODEL),
                  pl.BlockSpec((1, D_MODEL), lambda i: (0, 0)),
                  _mod_spec(scale, tm, rows_per_seq), _mod_spec(shift, tm, rows_per_seq),
                  tab_spec, tab_spec, tabt_spec, tabt_spec] + [whole(a) for a in weights],
        out_specs=out_specs,
        out_shape=out_shape,
        compiler_params=_params("parallel"),
        name="project_t_dsa" if dsa else "project_t",
    )(x, g, scale, shift, cos_tab, sin_tab, cos_t, sin_t, *weights)


def _post_kernel(x_ref, o_ref, wo_ref, ga_ref, g_ref, sc_ref, sh_ref, gm_ref, w1_ref, w2_ref, y_ref,
                 x1_scr, h_scr, acc_scr):
    j = pl.program_id(1)

    @pl.when(j == 0)
    def _():
        x1 = x_ref[...] + ga_ref[...] * jnp.dot(o_ref[...], wo_ref[...], preferred_element_type=F32)
        x1_scr[...] = x1
        h_scr[...] = _norm_mod(x1, g_ref[...], sc_ref[...], sh_ref[...]).astype(BF)
        acc_scr[...] = jnp.zeros_like(acc_scr)

    a = jnp.maximum(jnp.dot(h_scr[...], w1_ref[...], preferred_element_type=F32), 0.0)
    acc_scr[...] += jnp.dot((a * a).astype(BF), w2_ref[...], preferred_element_type=F32)

    @pl.when(j == pl.num_programs(1) - 1)
    def _():
        y_ref[...] = x1_scr[...] + gm_ref[...] * acc_scr[...]


def _post(x, o, w_out, gate_a, g_mlp, scale_m, shift_m, gate_m, w1, w2, rows_per_seq):
    rows = x.shape[0]
    tm = min(POST_ROW_TILE, rows)
    tf = FF_TILE
    ms = lambda a: _mod_spec(a, tm, rows_per_seq)
    return pl.pallas_call(
        _post_kernel,
        grid=(rows // tm, D_FF // tf),
        in_specs=[pl.BlockSpec((tm, D_MODEL), lambda i, j: (i, 0)),
                  pl.BlockSpec((tm, Q_DIM), lambda i, j: (i, 0)),
                  pl.BlockSpec((Q_DIM, D_MODEL), lambda i, j: (0, 0)),
                  ms(gate_a),
                  pl.BlockSpec((1, D_MODEL), lambda i, j: (0, 0)),
                  ms(scale_m), ms(shift_m), ms(gate_m),
                  pl.BlockSpec((D_MODEL, tf), lambda i, j: (0, j)),
                  pl.BlockSpec((tf, D_MODEL), lambda i, j: (j, 0))],
        out_specs=pl.BlockSpec((tm, D_MODEL), lambda i, j: (i, 0)),
        out_shape=jax.ShapeDtypeStruct((rows, D_MODEL), F32),
        scratch_shapes=[pltpu.VMEM((tm, D_MODEL), F32), pltpu.VMEM((tm, D_MODEL), BF),
                        pltpu.VMEM((tm, D_MODEL), F32)],
        compiler_params=_params("parallel", "arbitrary"),
        name="post",
    )(x, o, w_out, gate_a, g_mlp, scale_m, shift_m, gate_m, w1, w2)


def _final_norm_kernel(x_ref, g_ref, y_ref):
    x = x_ref[...]
    y_ref[...] = x * lax.rsqrt(jnp.mean(x * x, axis=-1, keepdims=True) + NORM_EPS) * g_ref[...]


def _final_norm(x, g):
    rows = x.shape[0]
    tm = min(ROW_TILE, rows)
    return pl.pallas_call(
        _final_norm_kernel,
        grid=(rows // tm,),
        in_specs=[pl.BlockSpec((tm, D_MODEL), lambda i: (i, 0)), pl.BlockSpec((1, D_MODEL), lambda i: (0, 0))],
        out_specs=pl.BlockSpec((tm, D_MODEL), lambda i: (i, 0)),
        out_shape=jax.ShapeDtypeStruct((rows, D_MODEL), F32),
        compiler_params=_params("parallel"),
        name="final_norm",
    )(x, g)


def _top_mask(gs, ids, n_sel, axis):
    picked = None
    for _ in range(n_sel):
        mx = jnp.max(gs, axis=axis, keepdims=True)
        am = jnp.min(jnp.where(gs == mx, ids, 2 ** 30), axis=axis, keepdims=True)
        hit = (ids == am) & (mx > NEG_INF)
        picked = hit if picked is None else (picked | hit)
        gs = jnp.where(ids == am, NEG_INF, gs)
    return picked


MOBA_MASKED = -1e30
MOBA_BLOCKS_PER_STEP = 4

def _moba_prompt_kernel(qi_tab, kp_tab, last_tab, qt_ref, kz_own_ref, vt_own_ref, kz_pair_ref, vt_pair_ref, kmz_ref,
                        o_ref, qz_scr, m_scr, acc_scr, *, n_sel):
    p = pl.program_id(1)
    i = qi_tab[p]
    kp = kp_tab[p]
    blk = MOBA_BLOCK

    @pl.when(kp < 0)
    def _():
        blocks = lax.broadcasted_iota(I32, (LANES - HEAD_DIM, blk), 0)
        earlier = blocks < i
        for h in range(N_HEADS):
            qt = qt_ref[h * HEAD_DIM:(h + 1) * HEAD_DIM, :]
            gs = jnp.dot(kmz_ref[h // GROUP, HEAD_DIM:, :], qt.astype(F32), precision=lax.Precision.HIGHEST,
                         preferred_element_type=F32)
            picked = _top_mask(jnp.where(earlier, gs, NEG_INF), blocks, n_sel, 0)
            usable = (earlier & picked) | (blocks == i)
            bias = jnp.where(usable, 0.0, MOBA_MASKED)
            qz_scr[h // GROUP, :, (h % GROUP) * blk:(h % GROUP + 1) * blk] = jnp.concatenate(
                [qt, bias.astype(BF)], axis=0)
        m_scr[...] = jnp.full(m_scr.shape, NEG_INF, F32)
        acc_scr[...] = jnp.zeros_like(acc_scr)

    def step(kz_ref, vt_ref, n_keys, block_of_key, causal):
        lane = lax.broadcasted_iota(I32, (n_keys, LANES), 1)
        ones_rows = (lax.broadcasted_iota(I32, (ACC_ROWS - HEAD_DIM, n_keys), 0) == 0).astype(BF)
        sts = []
        for kv in range(N_KV_HEADS):
            kz = jnp.where(lane == HEAD_DIM + block_of_key, 1.0, kz_ref[:, kv * LANES:(kv + 1) * LANES]).astype(BF)
            sts.append(jnp.dot(kz, qz_scr[kv], preferred_element_type=F32))
        es, alphas = [], []
        for kv in range(N_KV_HEADS):
            st = sts[kv]
            if causal:
                key_i = lax.broadcasted_iota(I32, (n_keys, GROUP * blk), 0)
                qry_i = lax.broadcasted_iota(I32, (n_keys, GROUP * blk), 1) % blk
                st = jnp.where(key_i <= qry_i, st, NEG_INF)
            m_old = m_scr[kv]
            m_new = jnp.maximum(m_old, jnp.max(st, axis=0, keepdims=True))
            alphas.append(jnp.exp2(m_old - m_new))
            es.append(jnp.exp2(st - m_new).astype(BF))
            m_scr[kv] = m_new
        for kv in range(N_KV_HEADS):
            vaug = jnp.concatenate([vt_ref[kv * HEAD_DIM:(kv + 1) * HEAD_DIM, :], ones_rows], axis=0)
            acc_scr[kv] = alphas[kv] * acc_scr[kv] + jnp.dot(vaug, es[kv], preferred_element_type=F32)

    @pl.when(kp < 0)
    def _():
        step(kz_own_ref, vt_own_ref, blk, i, True)

    @pl.when(kp >= 0)
    def _():
        n_keys = MOBA_BLOCKS_PER_STEP * blk
        key_row = lax.broadcasted_iota(I32, (n_keys, 1), 0)
        block = MOBA_BLOCKS_PER_STEP * kp + jnp.right_shift(key_row, blk.bit_length() - 1)
        step(kz_pair_ref, vt_pair_ref, n_keys, jnp.where(block < i, block, i + 1), False)

    @pl.when(last_tab[p] == 1)
    def _():
        for c in range(N_HEADS // 2):
            halves = []
            for h in (2 * c, 2 * c + 1):
                a = acc_scr[h // GROUP, :, (h % GROUP) * blk:(h % GROUP + 1) * blk]
                halves.append(a[:HEAD_DIM] * (1.0 / a[HEAD_DIM:HEAD_DIM + 1]))
            o_ref[:, c * LANES:(c + 1) * LANES] = jnp.concatenate(halves, axis=0).T.astype(BF)


def _moba_prompt(qt, kz, vt, kmeans, batch):
    rows = qt.shape[1]
    seq = rows // batch
    blk = MOBA_BLOCK
    assert seq % blk == 0
    nblk = seq // blk
    per_step = MOBA_BLOCKS_PER_STEP
    assert HEAD_DIM + nblk + 1 <= LANES and nblk % per_step == 0 and blk & (blk - 1) == 0
    n_sel = min(MOBA_TOPK, nblk)
    kmean = kmeans.reshape(batch, nblk, N_KV_HEADS, HEAD_DIM).transpose(0, 2, 1, 3)
    kmz = jnp.pad(kmean, ((0, 0), (0, 0), (HEAD_DIM, LANES - HEAD_DIM - nblk), (0, 0)))
    qi, kp, last = [], [], []
    for i in range(nblk):
        order = [-1] + list(range((i + per_step - 1) // per_step))
        qi += [i] * len(order)
        kp += order
        last += [0] * (len(order) - 1) + [1]
    tabs = [jnp.asarray(np.array(t, np.int32)) for t in (qi, kp, last)]
    npair = nblk // per_step
    own_rows = lambda b, p, qi, kp, la: (b * nblk + qi[p], 0)
    own_cols = lambda b, p, qi, kp, la: (0, b * nblk + qi[p])
    grid_spec = pltpu.PrefetchScalarGridSpec(
        num_scalar_prefetch=3,
        grid=(batch, len(qi)),
        in_specs=[pl.BlockSpec((Q_DIM, blk), own_cols),
                  pl.BlockSpec((blk, 2 * KV_DIM), own_rows),
                  pl.BlockSpec((KV_DIM, blk), own_cols),
                  pl.BlockSpec((per_step * blk, 2 * KV_DIM),
                               lambda b, p, qi, kp, la: (b * npair + jnp.maximum(kp[p], 0), 0)),
                  pl.BlockSpec((KV_DIM, per_step * blk),
                               lambda b, p, qi, kp, la: (0, b * npair + jnp.maximum(kp[p], 0))),
                  pl.BlockSpec((None, N_KV_HEADS, LANES, HEAD_DIM), lambda b, p, qi, kp, la: (b, 0, 0, 0))],
        out_specs=pl.BlockSpec((blk, Q_DIM), own_rows),
        scratch_shapes=[pltpu.VMEM((N_KV_HEADS, LANES, GROUP * blk), BF), pltpu.VMEM((N_KV_HEADS, 1, GROUP * blk), F32),
                        pltpu.VMEM((N_KV_HEADS, ACC_ROWS, GROUP * blk), F32)])
    return pl.pallas_call(
        functools.partial(_moba_prompt_kernel, n_sel=n_sel),
        grid_spec=grid_spec,
        out_shape=jax.ShapeDtypeStruct((rows, Q_DIM), BF),
        compiler_params=_params("parallel", "arbitrary"),
        name="moba_prompt",
    )(*tabs, qt, kz, vt, kz, vt, kmz)


def _float_key(x):
    b = pltpu.bitcast(x, I32)
    return jnp.where(b < 0, b ^ 0x7FFFFFFF, b)


def _kth_largest_key(count_ge, k_top, shape, bits_per_step=1):
    assert 32 % bits_per_step == 0
    def body(it, cur):
        shift = 32 - bits_per_step * (it + 1)
        best = cur
        for v in range(1, 2 ** bits_per_step):
            cand = cur | jnp.left_shift(jnp.int32(v), shift)
            best = jnp.where(count_ge(cand ^ INT_MIN) >= k_top, cand, best)
        return best
    return lax.fori_loop(0, 32 // bits_per_step, body, jnp.zeros(shape, I32)) ^ INT_MIN


def _tie_cutoff(count_eq_below, need, n_bits, shape):
    def body(it, cur):
        cand = cur | jnp.left_shift(jnp.int32(1), n_bits - 1 - it)
        return jnp.where(count_eq_below(cand) < need, cand, cur)
    return lax.fori_loop(0, n_bits, body, jnp.zeros(shape, I32))


def _swap_halves(x):
    return jnp.concatenate([x[HEAD_DIM:], x[:HEAD_DIM]], axis=0)


def _dsa_prompt_kernel(qt_ref, qit_ref, wt_ref, kiz_ref, kz_ref, vt_ref, o_ref,
                       qp_scr, qip_scr, key_scr, m_scr, acc_scr, *, k_top, n_bits):
    i = pl.program_id(1)
    tq, ch = DSA_QUERY_TILE, DSA_KEY_CHUNK
    n_ch = ((i + 1) * tq + ch - 1) // ch
    qpos = i * tq + lax.broadcasted_iota(I32, (1, tq), 1)
    key_row = lax.broadcasted_iota(I32, (ch, tq), 0)
    ones_rows = (lax.broadcasted_iota(I32, (ACC_ROWS - HEAD_DIM, ch), 0) == 0).astype(BF)

    for c in range(IDX_HEADS // 2):
        pair = qit_ref[c * LANES:(c + 1) * LANES, :]
        qip_scr[0, :, c * tq:(c + 1) * tq] = pair
        qip_scr[1, :, c * tq:(c + 1) * tq] = _swap_halves(pair)
    for c in range(N_HEADS // 2):
        pair = qt_ref[c * LANES:(c + 1) * LANES, :]
        cols = slice((c % 2) * tq, (c % 2 + 1) * tq)
        qp_scr[2 * (c // 2), :, cols] = pair
        qp_scr[2 * (c // 2) + 1, :, cols] = _swap_halves(pair)

    def score_body(c, carry):
        kiz = kiz_ref[pl.ds(pl.multiple_of(c * ch, ch), ch), :]
        sc = jnp.zeros((ch, tq), F32)
        for second in range(2):
            d = jnp.dot(kiz, qip_scr[second], preferred_element_type=F32)
            for u in range(IDX_HEADS // 2):
                h = 2 * u + second
                sc = sc + wt_ref[h:h + 1, :] * jnp.maximum(d[:, u * tq:(u + 1) * tq], 0.0)
        sc = jnp.where(c * ch + key_row <= qpos, sc, NEG_INF)
        key_scr[c] = _float_key(sc)
        return carry
    lax.fori_loop(0, n_ch, score_body, 0)

    def count(pred):
        def body(c, acc):
            return acc + jnp.sum(pred(key_scr[c], c * ch + key_row).astype(I32), axis=0, keepdims=True)
        return lax.fori_loop(0, n_ch, body, jnp.zeros((1, tq), I32))

    thr = _kth_largest_key(lambda t: count(lambda key, idx: key >= t), k_top, (1, tq))
    n_gt = count(lambda key, idx: key > thr)
    n_ge = count(lambda key, idx: key >= thr)
    need = k_top - n_gt
    tie_cols = ((n_ge > k_top) & (thr != KEY_NEG_INF)).astype(I32)
    cutoff = lax.cond(
        jnp.max(tie_cols) > 0,
        lambda: _tie_cutoff(lambda c_: count(lambda key, idx: (key == thr) & (idx < c_)), need, n_bits, (1, tq)),
        lambda: jnp.full((1, tq), 2 ** 30, I32))

    m_scr[...] = jnp.full(m_scr.shape, NEG_INF, F32)
    acc_scr[...] = jnp.zeros_like(acc_scr)

    def attn_body(c, carry):
        key = key_scr[c]
        idx = c * ch + key_row
        keep = ((key > thr) | ((key == thr) & (idx <= cutoff))) & (idx <= qpos)
        bias = jnp.where(keep, 0.0, NEG_INF)
        bias = jnp.concatenate([bias, bias], axis=1)
        rows = pl.ds(pl.multiple_of(c * ch, ch), ch)
        sts = []
        for kv in range(N_KV_HEADS):
            kz = kz_ref[rows, kv * LANES:(kv + 1) * LANES]
            for second in range(2):
                sts.append(jnp.dot(kz, qp_scr[2 * kv + second], preferred_element_type=F32))
        for kv in range(N_KV_HEADS):
            vaug = jnp.concatenate([vt_ref[c, kv * HEAD_DIM:(kv + 1) * HEAD_DIM, :], ones_rows], axis=0)
            for second in range(2):
                g = 2 * kv + second
                st = sts[g] + bias
                m_old = m_scr[g]
                m_new = jnp.maximum(m_old, jnp.max(st, axis=0, keepdims=True))
                m_safe = jnp.where(m_new == NEG_INF, 0.0, m_new)
                alpha = jnp.exp2(m_old - m_safe)
                e = jnp.exp2(st - m_safe).astype(BF)
                acc_scr[g] = alpha * acc_scr[g] + jnp.dot(vaug, e, preferred_element_type=F32)
                m_scr[g] = m_new
        return carry
    lax.fori_loop(0, n_ch, attn_body, 0)

    for kv in range(N_KV_HEADS):
        outs = []
        for second in range(2):
            a = acc_scr[2 * kv + second]
            outs.append(a[:HEAD_DIM] * (1.0 / a[HEAD_DIM:HEAD_DIM + 1]))
        for u in range(2):
            both = jnp.concatenate([outs[0][:, u * tq:(u + 1) * tq], outs[1][:, u * tq:(u + 1) * tq]], axis=0)
            o_ref[:, (2 * kv + u) * LANES:(2 * kv + u + 1) * LANES] = both.T.astype(BF)


def _dsa_prompt(qt, qit, wt, kiz, kz, vt3, batch):
    rows = qt.shape[1]
    seq = rows // batch
    tq, ch = DSA_QUERY_TILE, DSA_KEY_CHUNK
    k_top = min(DSA_TOPK_MAX, seq // 4)
    assert seq % ch == 0 and ch >= k_top and ch % tq == 0 and tq % LANES == 0 and vt3.shape[2] == ch
    nq, n_chunks = seq // tq, seq // ch
    n_bits = max(1, (seq - 1).bit_length())
    tile = lambda n: pl.BlockSpec((n, tq), lambda b, i: (0, b * nq + i))
    return pl.pallas_call(
        functools.partial(_dsa_prompt_kernel, k_top=k_top, n_bits=n_bits),
        grid=(batch, nq),
        in_specs=[tile(Q_DIM), tile(IDX_Q_DIM), tile(IDX_HEADS),
                  pl.BlockSpec((seq, LANES), lambda b, i: (b, 0), pipeline_mode=pl.Buffered(1)),
                  pl.BlockSpec((seq, 2 * KV_DIM), lambda b, i: (b, 0), pipeline_mode=pl.Buffered(1)),
                  pl.BlockSpec((n_chunks, KV_DIM, ch), lambda b, i: (b, 0, 0), pipeline_mode=pl.Buffered(1))],
        out_specs=pl.BlockSpec((tq, Q_DIM), lambda b, i: (b * nq + i, 0)),
        out_shape=jax.ShapeDtypeStruct((rows, Q_DIM), BF),
        scratch_shapes=[pltpu.VMEM((2 * N_KV_HEADS, LANES, 2 * tq), BF),
                        pltpu.VMEM((2, LANES, (IDX_HEADS // 2) * tq), BF),
                        pltpu.VMEM((n_chunks, ch, tq), I32),
                        pltpu.VMEM((2 * N_KV_HEADS, 1, 2 * tq), F32),
                        pltpu.VMEM((2 * N_KV_HEADS, ACC_ROWS, 2 * tq), F32)],
        compiler_params=_params("parallel", "arbitrary"),
        name="dsa_prompt",
    )(qt, qit, wt, kiz, kz, vt3)


def _swa_prompt_kernel(qt_ref, kzp_ref, kzc_ref, vtp_ref, vtc_ref, sink_ref, o_ref):
    n = pl.program_id(1)
    w = WINDOW
    key_i = lax.broadcasted_iota(I32, (2 * w, w), 0)
    qry_i = lax.broadcasted_iota(I32, (2 * w, w), 1)
    keep = (key_i >= qry_i) & (key_i <= qry_i + w) & ((key_i >= w) | (n > 0))
    bias = jnp.where(keep, 0.0, NEG_INF)
    bias = jnp.concatenate([bias, bias], axis=1)
    ones_rows = (lax.broadcasted_iota(I32, (ACC_ROWS - HEAD_DIM, 2 * w), 0) == 0).astype(BF)
    sts = []
    for kv in range(N_KV_HEADS):
        lanes = slice(kv * LANES, (kv + 1) * LANES)
        kz = jnp.concatenate([kzp_ref[:, lanes], kzc_ref[:, lanes]], axis=0)
        pairs = [qt_ref[(2 * kv + u) * LANES:(2 * kv + u + 1) * LANES, :] for u in range(2)]
        for second in range(2):
            ops = pairs if second == 0 else [_swap_halves(p) for p in pairs]
            sts.append(jnp.dot(kz, jnp.concatenate(ops, axis=1), preferred_element_type=F32))
    for kv in range(N_KV_HEADS):
        feats = slice(kv * HEAD_DIM, (kv + 1) * HEAD_DIM)
        vaug = jnp.concatenate([jnp.concatenate([vtp_ref[feats, :], vtc_ref[feats, :]], axis=1), ones_rows], axis=0)
        outs = []
        for second in range(2):
            st = sts[2 * kv + second] + bias
            sink = sink_ref[2 * kv + second]
            m = jnp.maximum(jnp.max(st, axis=0, keepdims=True), sink)
            acc = jnp.dot(vaug, jnp.exp2(st - m).astype(BF), preferred_element_type=F32)
            outs.append(acc[:HEAD_DIM] * (1.0 / (acc[HEAD_DIM:HEAD_DIM + 1] + jnp.exp2(sink - m))))
        for u in range(2):
            both = jnp.concatenate([outs[0][:, u * w:(u + 1) * w], outs[1][:, u * w:(u + 1) * w]], axis=0)
            o_ref[:, (2 * kv + u) * LANES:(2 * kv + u + 1) * LANES] = both.T.astype(BF)


def _swa_prompt(qt, kz, vt, sinks, batch):
    rows = qt.shape[1]
    seq = rows // batch
    w = WINDOW
    assert seq % w == 0 and w == LANES
    nb = seq // w
    cur = lambda b, n: b * nb + n
    prev = lambda b, n: b * nb + jnp.maximum(n - 1, 0)
    sk = sinks.astype(F32).reshape(N_KV_HEADS, 2, 2)
    sink_cols = jnp.repeat(sk.transpose(0, 2, 1) * LOG2E, w, axis=2).reshape(2 * N_KV_HEADS, 1, 2 * w)
    return pl.pallas_call(
        _swa_prompt_kernel,
        grid=(batch, nb),
        in_specs=[pl.BlockSpec((Q_DIM, w), lambda b, n: (0, cur(b, n))),
                  pl.BlockSpec((w, 2 * KV_DIM), lambda b, n: (prev(b, n), 0)),
                  pl.BlockSpec((w, 2 * KV_DIM), lambda b, n: (cur(b, n), 0)),
                  pl.BlockSpec((KV_DIM, w), lambda b, n: (0, prev(b, n))),
                  pl.BlockSpec((KV_DIM, w), lambda b, n: (0, cur(b, n))),
                  pl.BlockSpec((2 * N_KV_HEADS, 1, 2 * w), lambda b, n: (0, 0, 0))],
        out_specs=pl.BlockSpec((w, Q_DIM), lambda b, n: (cur(b, n), 0)),
        out_shape=jax.ShapeDtypeStruct((rows, Q_DIM), BF),
        compiler_params=_params("parallel", "parallel"),
        name="swa_prompt",
    )(qt, kz, kz, vt, vt, sink_cols)


def _row_token(n_rows, n_tok):
    r = lax.broadcasted_iota(I32, (n_rows, 1), 0)
    return (r // GROUP) % n_tok


def _block_diag_queries(q, n_seq, n_tok):
    qr = q.reshape(n_seq, n_tok, N_KV_HEADS, GROUP, HEAD_DIM).transpose(0, 2, 1, 3, 4)
    qr = qr.reshape(n_seq, N_KV_HEADS, n_tok * GROUP, HEAD_DIM)
    eye = jnp.eye(N_KV_HEADS, dtype=q.dtype)
    qbd = qr[:, :, :, None, :] * eye[None, :, None, :, None]
    return qbd.reshape(n_seq, N_KV_HEADS * n_tok * GROUP, KV_DIM)


def _undiag_outputs(out, n_seq, n_tok):
    o = out.reshape(n_seq, N_KV_HEADS, n_tok, GROUP, N_KV_HEADS, HEAD_DIM)
    o = jnp.stack([o[:, kv, :, :, kv] for kv in range(N_KV_HEADS)], axis=1)
    return o.transpose(0, 2, 1, 3, 4).reshape(n_seq * n_tok, Q_DIM)


def _tokens_on_lanes(cache):
    if cache.ndim == 5:
        t = jnp.transpose(cache, (0, 1, 3, 4, 2))
        return t.reshape(t.shape[:2] + (t.shape[2] * t.shape[3], t.shape[4]))
    return jnp.transpose(cache, (0, 1, 3, 2))


def _page_specs(n_pages_per_step, slot, width, seq_of=lambda b: b):
    def spec(r):
        return pl.BlockSpec((None, None, width, PAGE_SIZE),
                            lambda b, s, pt: (slot, pt[seq_of(b), s * n_pages_per_step + r], 0, 0))
    return [spec(r) for r in range(n_pages_per_step)]


def _pad_rows(a, n_seq, n_tok):
    a = a.reshape(n_seq, n_tok, a.shape[-1])
    return jnp.pad(a, ((0, 0), (0, SUBLANES - n_tok), (0, 0)))


def _moba_sample_kernel(pt_ref, q_ref, kn_ref, vn_ref, *refs, n_tok, n_sel, pps):
    kt_pages = refs[:pps]
    vt_pages = refs[pps:2 * pps]
    o_ref = refs[2 * pps]
    m_scr, l_scr, g_scr, o_scr = refs[2 * pps + 1:]
    s = pl.program_id(1)
    n_steps = m_scr.shape[0]
    bps = pps // 2
    q = q_ref[...]
    n_row = q.shape[0]
    m_scr[s] = jnp.full(m_scr.shape[1:], NEG_INF, F32)
    l_scr[s] = jnp.zeros(l_scr.shape[1:], F32)
    g_scr[s] = jnp.full(g_scr.shape[1:], NEG_INF, F32)
    kt_all = jnp.concatenate([p[...] for p in kt_pages], axis=1).astype(BF)
    sc_all = jnp.dot(q, kt_all, preferred_element_type=F32)
    for r in range(bps):
        vt = jnp.concatenate([vt_pages[2 * r][...], vt_pages[2 * r + 1][...]], axis=1).astype(BF)
        sc = sc_all[:, r * MOBA_BLOCK:(r + 1) * MOBA_BLOCK]
        m = jnp.max(sc, axis=1, keepdims=True)
        e = jnp.exp(sc - m)
        m_scr[s, :, r:r + 1] = m
        l_scr[s, :, r:r + 1] = jnp.sum(e, axis=1, keepdims=True)
        g_scr[s, :, r:r + 1] = jnp.sum(sc, axis=1, keepdims=True)
        o_scr[s * bps + r] = lax.dot_general(e.astype(BF), vt, NT, preferred_element_type=F32)

    @pl.when(s == pl.num_programs(1) - 1)
    def _():
        lane = lax.broadcasted_iota(I32, (n_row, LANES), 1)
        ids = [t * bps + lane for t in range(n_steps)]
        gs = [g_scr[t] for t in range(n_steps)]
        picked = [None] * n_steps
        for _ in range(n_sel):
            mx = functools.reduce(jnp.maximum, [jnp.max(g, axis=1, keepdims=True) for g in gs])
            am = functools.reduce(jnp.minimum, [jnp.min(jnp.where(g == mx, i_, 2 ** 30), axis=1, keepdims=True)
                                                for g, i_ in zip(gs, ids)])
            for t in range(n_steps):
                hit = (ids[t] == am) & (mx > NEG_INF)
                picked[t] = hit if picked[t] is None else (picked[t] | hit)
                gs[t] = jnp.where(ids[t] == am, NEG_INF, gs[t])
        own_ok = lax.broadcasted_iota(I32, (n_row, SUBLANES), 1) <= _row_token(n_row, n_tok)
        s_own = jnp.where(own_ok, lax.dot_general(q, kn_ref[...], NT, preferred_element_type=F32), NEG_INF)
        m_tot = jnp.max(s_own, axis=1, keepdims=True)
        for t in range(n_steps):
            m_tot = jnp.maximum(m_tot, jnp.max(jnp.where(picked[t], m_scr[t], NEG_INF), axis=1, keepdims=True))
        e_own = jnp.exp(s_own - m_tot)
        den = jnp.sum(e_own, axis=1, keepdims=True)
        num = jnp.dot(e_own.astype(BF), vn_ref[...], preferred_element_type=F32)
        for t in range(n_steps):
            wgt = jnp.where(picked[t], jnp.exp(m_scr[t] - m_tot), 0.0)
            den = den + jnp.sum(wgt * l_scr[t], axis=1, keepdims=True)
            for r in range(bps):
                num = num + wgt[:, r:r + 1] * o_scr[t * bps + r]
        o_ref[...] = num * (1.0 / den)


def _paged_attention_call(kernel, name, slot, page_table, cache_kt, cache_vt, extra_inputs, extra_specs,
                          scratch_shapes, n_row):
    n_seq, n_pages = page_table.shape
    pps = PAGES_PER_STEP
    assert n_pages % pps == 0
    grid_spec = pltpu.PrefetchScalarGridSpec(
        num_scalar_prefetch=1,
        grid=(n_seq, n_pages // pps),
        in_specs=extra_specs + _page_specs(pps, slot, KV_DIM) + _page_specs(pps, slot, KV_DIM),
        out_specs=pl.BlockSpec((None, n_row, KV_DIM), lambda b, s, pt: (b, 0, 0)),
        scratch_shapes=scratch_shapes)
    return pl.pallas_call(
        kernel,
        grid_spec=grid_spec,
        out_shape=jax.ShapeDtypeStruct((n_seq, n_row, KV_DIM), F32),
        compiler_params=_params("parallel", "arbitrary"),
        name=name,
    )(page_table, *extra_inputs, *([cache_kt] * pps), *([cache_vt] * pps))


def _moba_sample(qbd, kn_pad, vn_pad, cache_kt, cache_vt, slot, page_table, n_tok):
    n_seq, n_pages = page_table.shape
    past = n_pages * PAGE_SIZE
    assert past % MOBA_BLOCK == 0 and MOBA_BLOCK == 2 * PAGE_SIZE and PAGES_PER_STEP % 2 == 0
    nblk = past // MOBA_BLOCK
    n_steps = n_pages // PAGES_PER_STEP
    n_row = qbd.shape[1]
    per_seq = lambda w: pl.BlockSpec((None, w[0], w[1]), lambda b, s, pt: (b, 0, 0))
    kernel = functools.partial(_moba_sample_kernel, n_tok=n_tok, n_sel=min(MOBA_TOPK, nblk), pps=PAGES_PER_STEP)
    stats = pltpu.VMEM((n_steps, n_row, LANES), F32)
    return _paged_attention_call(
        kernel, "moba_sample", slot, page_table, cache_kt, cache_vt,
        [qbd, kn_pad, vn_pad],
        [per_seq((n_row, KV_DIM)), per_seq((SUBLANES, KV_DIM)), per_seq((SUBLANES, KV_DIM))],
        [stats, stats, stats, pltpu.VMEM((nblk, n_row, KV_DIM), F32)],
        n_row)


def _dsa_index_kernel(pt_ref, qi_ref, w_ref, kin_ref, *refs, n_tok, k_top, ppi, n_bits):
    n_par = SEQS_PER_INDEX_STEP
    ki_pages = [refs[a * ppi:(a + 1) * ppi] for a in range(n_par)]
    mask_ref = refs[n_par * ppi]
    key_scr = refs[n_par * ppi + 1]
    s = pl.program_id(1)
    n_steps = pl.num_programs(1)
    n_pages = n_steps * ppi
    n_fill = SUBLANES - n_par * n_tok

    def scores(a, d):
        wd = (w_ref[a] * jnp.maximum(d, 0.0)).reshape(n_tok, IDX_HEADS, d.shape[1])
        return jnp.sum(wd, axis=1)

    def tile(parts):
        if n_fill:
            parts = parts + [jnp.full((n_fill, LANES), NEG_INF, F32)]
        return _float_key(jnp.concatenate(parts, axis=0))

    sc_all = []
    for a in range(n_par):
        kit = jnp.concatenate([p[...] for p in ki_pages[a]], axis=1).astype(BF)
        sc_all.append(scores(a, jnp.dot(qi_ref[a], kit, preferred_element_type=F32)))
    for r in range(ppi):
        key_scr[s * ppi + r] = tile([sc[:, r * PAGE_SIZE:(r + 1) * PAGE_SIZE] for sc in sc_all])

    @pl.when(s == n_steps - 1)
    def _():
        tok = lax.broadcasted_iota(I32, (n_tok, LANES), 0)
        lane_t = lax.broadcasted_iota(I32, (n_tok, LANES), 1)
        new = []
        for a in range(n_par):
            sc_new = scores(a, lax.dot_general(qi_ref[a], kin_ref[a], NT, preferred_element_type=F32))
            new.append(jnp.where(lane_t <= tok, sc_new, NEG_INF))
        key_scr[n_pages] = tile(new)
        tiles = (n_pages + 1, SUBLANES, LANES)
        idx = lax.broadcasted_iota(I32, tiles, 0) * PAGE_SIZE + lax.broadcasted_iota(I32, tiles, 2)

        def count(hit):
            return jnp.sum(jnp.sum(hit.astype(I32), axis=0), axis=1, keepdims=True)

        thr = _kth_largest_key(lambda t: count(key_scr[...] >= t), k_top, (SUBLANES, 1), bits_per_step=2)
        need = k_top - count(key_scr[...] > thr)
        tie_rows = ((count(key_scr[...] >= thr) > k_top) & (thr != KEY_NEG_INF)).astype(I32)
        cutoff = lax.cond(
            jnp.max(tie_rows) > 0,
            lambda: _tie_cutoff(lambda c_: count((key_scr[...] == thr) & (idx < c_)), need, n_bits, (SUBLANES, 1)),
            lambda: jnp.full((SUBLANES, 1), 2 ** 30, I32))
        keys = key_scr[...]
        keep = (keys > thr) | ((keys == thr) & (idx <= cutoff))
        mask_ref[...] = jnp.where(keep & (keys != KEY_NEG_INF), 1.0, 0.0).astype(BF)


def _dsa_index(qi_rows, w_rows, ki_new, cache_idx, islot, page_table, n_tok):
    n_seq, n_pages = page_table.shape
    ppi = min(INDEX_PAGES_PER_STEP, n_pages)
    n_par = SEQS_PER_INDEX_STEP
    total = n_pages * PAGE_SIZE + n_tok
    k_top = min(DSA_TOPK_MAX, total // 4)
    assert n_pages % ppi == 0 and n_par * n_tok <= SUBLANES and n_seq % n_par == 0 and n_pages * PAGE_SIZE >= k_top
    n_bits = max(1, ((n_pages + 1) * PAGE_SIZE - 1).bit_length())
    nr = n_tok * IDX_HEADS
    group = lambda n, m: pl.BlockSpec((n_par, n, m), lambda g, s, pt: (g, 0, 0))
    pages = []
    for a in range(n_par):
        pages += _page_specs(ppi, islot, IDX_DIM, seq_of=lambda g, a=a: g * n_par + a)
    grid_spec = pltpu.PrefetchScalarGridSpec(
        num_scalar_prefetch=1,
        grid=(n_seq // n_par, n_pages // ppi),
        in_specs=[group(nr, IDX_DIM), group(nr, 1), group(LANES, IDX_DIM)] + pages,
        out_specs=pl.BlockSpec((None, n_pages + 1, SUBLANES, LANES), lambda g, s, pt: (g, 0, 0, 0)),
        scratch_shapes=[pltpu.VMEM((n_pages + 1, SUBLANES, LANES), I32)])
    return pl.pallas_call(
        functools.partial(_dsa_index_kernel, n_tok=n_tok, k_top=k_top, ppi=ppi, n_bits=n_bits),
        grid_spec=grid_spec,
        out_shape=jax.ShapeDtypeStruct((n_seq // n_par, n_pages + 1, SUBLANES, LANES), BF),
        compiler_params=_params("parallel", "arbitrary"),
        name="dsa_index",
    )(page_table, qi_rows, w_rows, ki_new, *([cache_idx] * (n_par * ppi)))


def _dsa_sample_kernel(pt_ref, q_ref, kn_ref, vn_ref, mask_ref, mnew_ref, *refs, n_tok, pps):
    kt_pages = refs[:pps]
    vt_pages = refs[pps:2 * pps]
    o_ref = refs[2 * pps]
    m_scr, l_scr, acc_scr = refs[2 * pps + 1:]
    s = pl.program_id(1)
    q = q_ref[...]
    n_row = q.shape[0]
    first_row = (pl.program_id(0) % SEQS_PER_INDEX_STEP) * n_tok
    expand = (lax.broadcasted_iota(I32, (n_row, SUBLANES), 1) == _row_token(n_row, n_tok) + first_row).astype(BF)

    @pl.when(s == 0)
    def _():
        m_scr[...] = jnp.full(m_scr.shape, NEG_INF, F32)
        l_scr[...] = jnp.zeros_like(l_scr)
        acc_scr[...] = jnp.zeros_like(acc_scr)

    def update(sc, keep, pv):
        sc = jnp.where(keep > 0.5, sc, NEG_INF)
        m_old = m_scr[...]
        m_new = jnp.maximum(m_old, jnp.max(sc, axis=1, keepdims=True))
        m_safe = jnp.where(m_new == NEG_INF, 0.0, m_new)
        alpha = jnp.exp(m_old - m_safe)
        e = jnp.exp(sc - m_safe)
        l_scr[...] = alpha * l_scr[...] + jnp.sum(e, axis=1, keepdims=True)
        acc_scr[...] = alpha * acc_scr[...] + pv(e.astype(BF))
        m_scr[...] = m_new

    kt_all = jnp.concatenate([p[...] for p in kt_pages], axis=1).astype(BF)
    sc_all = jnp.dot(q, kt_all, preferred_element_type=F32)
    keep_all = jnp.dot(expand, jnp.concatenate([mask_ref[r] for r in range(pps)], axis=1),
                       preferred_element_type=F32)
    width = DSA_PAGES_PER_UPDATE * PAGE_SIZE
    for u in range(pps // DSA_PAGES_PER_UPDATE):
        rs = range(u * DSA_PAGES_PER_UPDATE, (u + 1) * DSA_PAGES_PER_UPDATE)
        vt = jnp.concatenate([vt_pages[r][...] for r in rs], axis=1).astype(BF)
        cols = slice(u * width, (u + 1) * width)
        update(sc_all[:, cols], keep_all[:, cols], lambda e, vt=vt: lax.dot_general(e, vt, NT, preferred_element_type=F32))

    @pl.when(s == pl.num_programs(1) - 1)
    def _():
        sc = lax.dot_general(q, kn_ref[...], NT, preferred_element_type=F32)
        keep = jnp.dot(expand, mnew_ref[...], preferred_element_type=F32)[:, :SUBLANES]
        update(sc, keep, lambda e: jnp.dot(e, vn_ref[...], preferred_element_type=F32))
        o_ref[...] = acc_scr[...] * (1.0 / l_scr[...])


def _dsa_sample(qbd, kn_pad, vn_pad, mask, cache_kt, cache_vt, slot, page_table, n_tok):
    n_seq, n_pages = page_table.shape
    n_row = qbd.shape[1]
    pps = PAGES_PER_STEP
    per_seq = lambda w: pl.BlockSpec((None, w[0], w[1]), lambda b, s, pt: (b, 0, 0))
    return _paged_attention_call(
        functools.partial(_dsa_sample_kernel, n_tok=n_tok, pps=pps), "dsa_sample", slot, page_table,
        cache_kt, cache_vt,
        [qbd, kn_pad, vn_pad, mask, mask],
        [per_seq((n_row, KV_DIM)), per_seq((SUBLANES, KV_DIM)), per_seq((SUBLANES, KV_DIM)),
         pl.BlockSpec((None, pps, SUBLANES, LANES), lambda b, s, pt: (b // SEQS_PER_INDEX_STEP, s, 0, 0)),
         pl.BlockSpec((None, None, SUBLANES, LANES), lambda b, s, pt: (b // SEQS_PER_INDEX_STEP, n_pages, 0, 0))],
        [pltpu.VMEM((n_row, 1), F32), pltpu.VMEM((n_row, 1), F32), pltpu.VMEM((n_row, KV_DIM), F32)],
        n_row)


def _swa_sample_kernel(q_ref, kt_ref, vt_ref, kn_ref, vn_ref, sink_ref, o_ref, *, n_tok):
    q = q_ref[...]
    n_row = q.shape[0]
    tok_r = _row_token(n_row, n_tok)
    wk = kt_ref.shape[1]
    s_buf = jnp.dot(q, kt_ref[...].astype(BF), preferred_element_type=F32)
    s_buf = jnp.where(lax.broadcasted_iota(I32, (n_row, wk), 1) >= tok_r, s_buf, NEG_INF)
    s_new = lax.dot_general(q, kn_ref[...], NT, preferred_element_type=F32)
    s_new = jnp.where(lax.broadcasted_iota(I32, (n_row, SUBLANES), 1) <= tok_r, s_new, NEG_INF)
    sink = sink_ref[...]
    m = jnp.maximum(jnp.maximum(jnp.max(s_buf, axis=1, keepdims=True), jnp.max(s_new, axis=1, keepdims=True)), sink)
    e_buf = jnp.exp(s_buf - m)
    e_new = jnp.exp(s_new - m)
    den = jnp.sum(e_buf, axis=1, keepdims=True) + jnp.sum(e_new, axis=1, keepdims=True) + jnp.exp(sink - m)
    num = (lax.dot_general(e_buf.astype(BF), vt_ref[...].astype(BF), NT, preferred_element_type=F32)
           + jnp.dot(e_new.astype(BF), vn_ref[...], preferred_element_type=F32))
    o_ref[...] = num * (1.0 / den)


def _swa_sample(qbd, kn_pad, vn_pad, buf_kt, buf_vt, sinks, n_tok):
    n_seq, _, wk = buf_kt.shape
    assert wk == WINDOW
    n_row = qbd.shape[1]
    sink_rows = jnp.repeat(sinks.reshape(N_KV_HEADS, 1, GROUP), n_tok, axis=1).reshape(n_row, 1)
    per_seq = lambda a, c: pl.BlockSpec((None, a, c), lambda b: (b, 0, 0))
    return pl.pallas_call(
        functools.partial(_swa_sample_kernel, n_tok=n_tok),
        grid=(n_seq,),
        in_specs=[per_seq(n_row, KV_DIM), per_seq(KV_DIM, wk), per_seq(KV_DIM, wk),
                  per_seq(SUBLANES, KV_DIM), per_seq(SUBLANES, KV_DIM),
                  pl.BlockSpec((n_row, 1), lambda b: (0, 0))],
        out_specs=per_seq(n_row, KV_DIM),
        out_shape=jax.ShapeDtypeStruct((n_seq, n_row, KV_DIM), F32),
        compiler_params=_params("parallel"),
        name="swa_sample",
    )(qbd, buf_kt, buf_vt, kn_pad, vn_pad, sink_rows)


def _rope_tables(pos):
    half = HEAD_DIM // 2
    inv = ROPE_THETA ** (-jnp.arange(half, dtype=F32) / half)
    ang = pos.astype(F32)[:, None] * inv[None, :]
    cos, sin = jnp.cos(ang), jnp.sin(ang)
    reps = LANES // HEAD_DIM
    return jnp.tile(jnp.concatenate([cos, cos], axis=1), (1, reps)), jnp.tile(jnp.concatenate([-sin, sin], axis=1), (1, reps))


def kernel(x_prompt, x_sample, cache_k, cache_v, cache_idx_k, state_swa_k, state_swa_v, page_table, c_prompt, c_sample, g_attn, g_mlp, w_mod, b_mod, w_in_moba, w_in_dsa, w_in_swa, swa_sinks, w_out, w_ff1, w_ff2, g_final):
    batch, seq, _ = x_prompt.shape
    n_seq, n_tok, _ = x_sample.shape
    depth = g_attn.shape[0]
    n_pages = page_table.shape[1]
    past = n_pages * PAGE_SIZE
    rows_p, rows_s = batch * seq, n_seq * n_tok

    n_cond = batch + n_seq
    pad_cond = -n_cond % SUBLANES
    c_all = jnp.pad(jnp.concatenate([c_prompt, c_sample], axis=0), ((0, pad_cond), (0, 0)))
    mod = _adaln(c_all, w_mod, b_mod)

    def mods(layer):
        parts = [mod[layer, :, r * D_MODEL:(r + 1) * D_MODEL] for r in range(6)]
        mp = [p[:batch].reshape(batch, 1, D_MODEL) for p in parts]
        ms = [jnp.repeat(p[batch:n_cond], n_tok, axis=0) for p in parts]
        return mp, ms

    cos_p, sin_p = _rope_tables(jnp.arange(seq, dtype=jnp.int32))
    cos_s, sin_s = _rope_tables(jnp.tile(past + jnp.arange(n_tok, dtype=jnp.int32), n_seq))
    cos_pt, sin_pt = cos_p[:, :HEAD_DIM].T, sin_p[:, :HEAD_DIM].T

    cache_kt, cache_vt, cache_it = _tokens_on_lanes(cache_k), _tokens_on_lanes(cache_v), _tokens_on_lanes(cache_idx_k)
    swa_kt, swa_vt = _tokens_on_lanes(state_swa_k), _tokens_on_lanes(state_swa_v)

    xp = x_prompt.reshape(rows_p, D_MODEL)
    xs = x_sample.reshape(rows_s, D_MODEL)
    kp_l, vp_l, ks_l, vs_l, ip_l, is_l = [], [], [], [], [], []
    skp_l, svp_l, sks_l, svs_l = [], [], [], []
    for i, (mixer, j, slot) in enumerate(_layer_plan(depth)):
        (sh_ap, sc_ap, gt_ap, sh_mp, sc_mp, gt_mp), (sh_as, sc_as, gt_as, sh_ms, sc_ms, gt_ms) = mods(i)
        dsa = mixer == 1
        if mixer == 0:
            w_in = w_in_moba[j]
        elif dsa:
            w_in = jnp.pad(w_in_dsa[j], ((0, 0), (0, -IN_DSA % LANES)))
        else:
            w_in = w_in_swa[j]
        w_in = w_in.astype(BF)
        g_a = g_attn[i].reshape(1, D_MODEL)
        outs_p = _project_t(xp, g_a, sc_ap, sh_ap, cos_p, sin_p, cos_pt, sin_pt, w_in, seq, dsa)
        qt_p, kt_p, vtf_p, kz_p, vt_p, kmeans_p = outs_p[:6]
        outs_s = _project(xs, g_a, sc_as, sh_as, cos_s, sin_s, w_in, n_tok, dsa)
        q_s, k_s, v_s, kb_s, vb_s = outs_s[:5]
        qbd = _block_diag_queries(q_s, n_seq, n_tok)
        kn_pad = _pad_rows(kb_s, n_seq, n_tok)
        vn_pad = _pad_rows(vb_s, n_seq, n_tok)
        k_p5 = kt_p.reshape(batch, N_KV_HEADS, HEAD_DIM, seq).transpose(0, 3, 1, 2)
        v_p5 = vtf_p.reshape(batch, N_KV_HEADS, HEAD_DIM, seq).transpose(0, 3, 1, 2)
        k_s5 = k_s.reshape(n_seq, n_tok, N_KV_HEADS, HEAD_DIM)
        v_s5 = v_s.reshape(n_seq, n_tok, N_KV_HEADS, HEAD_DIM)
        if mixer == 0:
            o_p = _moba_prompt(qt_p, kz_p, vt_p, kmeans_p, batch)
            o_t = _moba_sample(qbd, kn_pad, vn_pad, cache_kt, cache_vt, slot, page_table, n_tok)
        elif dsa:
            qit_p, wt_p, kit_p, kiz_p = outs_p[6:]
            qi_s, rest_s, restb_s = outs_s[5:]
            o_p = _dsa_prompt(qt_p, qit_p, wt_p, kiz_p, kz_p, vt_p, batch)
            qi_rows = qi_s.reshape(n_seq, n_tok * IDX_HEADS, IDX_DIM)
            w_rows = rest_s[:, IDX_DIM:IDX_DIM + IDX_HEADS].reshape(n_seq, n_tok * IDX_HEADS, 1)
            ki_new = jnp.pad(restb_s[:, :IDX_DIM].reshape(n_seq, n_tok, IDX_DIM),
                             ((0, 0), (0, LANES - n_tok), (0, 0)))
            mask = _dsa_index(qi_rows, w_rows, ki_new, cache_it, j, page_table, n_tok)
            o_t = _dsa_sample(qbd, kn_pad, vn_pad, mask, cache_kt, cache_vt, slot, page_table, n_tok)
            ip_l.append(kit_p.transpose(0, 2, 1))
            is_l.append(rest_s[:, :IDX_DIM].reshape(n_seq, n_tok, IDX_DIM))
        else:
            o_p = _swa_prompt(qt_p, kz_p, vt_p, swa_sinks[j], batch)
            o_t = _swa_sample(qbd, kn_pad, vn_pad, swa_kt[j], swa_vt[j], swa_sinks[j], n_tok)
            keep_p = min(WINDOW, seq)
            skp_l.append(k_p5[:, seq - keep_p:])
            svp_l.append(v_p5[:, seq - keep_p:])
            wk = state_swa_k.shape[2]
            sks_l.append(jnp.concatenate([state_swa_k[j], k_s5], axis=1)[:, -wk:])
            svs_l.append(jnp.concatenate([state_swa_v[j], v_s5], axis=1)[:, -wk:])
        if mixer < 2:
            kp_l.append(k_p5)
            vp_l.append(v_p5)
            ks_l.append(k_s5)
            vs_l.append(v_s5)
        o_s = _undiag_outputs(o_t, n_seq, n_tok).astype(BF)
        wo, w1, w2 = w_out[i].astype(BF), w_ff1[i].astype(BF), w_ff2[i].astype(BF)
        g_m = g_mlp[i].reshape(1, D_MODEL)
        xp = _post(xp, o_p, wo, gt_ap, g_m, sc_mp, sh_mp, gt_mp, w1, w2, seq)
        xs = _post(xs, o_s, wo, gt_as, g_m, sc_ms, sh_ms, gt_ms, w1, w2, n_tok)
    g_f = g_final.reshape(1, D_MODEL)
    y_prompt = _final_norm(xp, g_f).reshape(batch, seq, D_MODEL)
    y_sample = _final_norm(xs, g_f).reshape(n_seq, n_tok, D_MODEL)
    return (y_prompt, y_sample, jnp.stack(kp_l), jnp.stack(vp_l), jnp.stack(ks_l), jnp.stack(vs_l),
            jnp.stack(ip_l), jnp.stack(is_l), jnp.stack(skp_l), jnp.stack(svp_l), jnp.stack(sks_l), jnp.stack(svs_l))
```

```python
import functools

import numpy as np
import jax
import jax.numpy as jnp
from jax import lax
from jax.experimental import pallas as pl
from jax.experimental.pallas import tpu as pltpu

D_MODEL = 1024
N_HEADS = 16
HEAD_DIM = D_MODEL // N_HEADS
N_KV_HEADS = 4
GROUP = N_HEADS // N_KV_HEADS
Q_DIM = N_HEADS * HEAD_DIM
KV_DIM = N_KV_HEADS * HEAD_DIM
D_FF = 4 * D_MODEL
ROPE_THETA = 10000.0
NORM_EPS = 1e-6
N_MIXERS = 3
PAGE_SIZE = 128
MOBA_BLOCK = 256
MOBA_TOPK = 3
IDX_HEADS = 8
IDX_DIM = 64
DSA_TOPK_MAX = 256
WINDOW = 128
IN_ATTN = Q_DIM + 2 * KV_DIM
IDX_Q_DIM = IDX_HEADS * IDX_DIM
IN_DSA = IN_ATTN + IDX_Q_DIM + IDX_DIM + IDX_HEADS
ATTN_SCALE = HEAD_DIM ** -0.5
LOG2E = 1.4426950408889634
IDX_W_SCALE = IDX_Q_DIM ** -0.5

LANES = 128
SUBLANES = 8
BF16_SUBLANES = 16
ACC_ROWS = HEAD_DIM + BF16_SUBLANES
VMEM_LIMIT = 56 << 20
ROW_TILE = 512
POST_ROW_TILE = 1024
FF_TILE = 1024
DSA_KEY_CHUNK = 512
DSA_QUERY_TILE = 512
PAGES_PER_STEP = 64
INDEX_PAGES_PER_STEP = 64
SEQS_PER_INDEX_STEP = 2
DSA_PAGES_PER_UPDATE = 8

BF = jnp.bfloat16
F32 = jnp.float32
I32 = jnp.int32
NEG_INF = float("-inf")
INT_MIN = -2 ** 31
KEY_NEG_INF = INT_MIN + 0x7FFFFF
NT = (((1,), (1,)), ((), ()))


def _params(*semantics):
    return pltpu.CompilerParams(dimension_semantics=semantics, vmem_limit_bytes=VMEM_LIMIT)


def _layer_plan(depth):
    plan, counts, n_paged = [], [0] * N_MIXERS, 0
    for i in range(depth):
        m = i % N_MIXERS
        slot = -1
        if m < 2:
            slot = n_paged
            n_paged += 1
        plan.append((m, counts[m], slot))
        counts[m] += 1
    return plan


def _adaln_kernel(c_ref, w_ref, b_ref, o_ref):
    c = c_ref[...]
    a = (c * (1.0 / (1.0 + jnp.exp(-c)))).astype(BF)
    o_ref[...] = jnp.dot(a, w_ref[...].astype(BF), preferred_element_type=F32) + b_ref[...]


def _adaln(c_all, w_mod, b_mod):
    depth, _, n_out = w_mod.shape
    nc = c_all.shape[0]
    tn = 1536
    return pl.pallas_call(
        _adaln_kernel,
        grid=(depth, n_out // tn),
        in_specs=[pl.BlockSpec((nc, D_MODEL), lambda l, j: (0, 0)),
                  pl.BlockSpec((None, D_MODEL, tn), lambda l, j: (l, 0, j)),
                  pl.BlockSpec((None, 1, tn), lambda l, j: (l, 0, j))],
        out_specs=pl.BlockSpec((None, nc, tn), lambda l, j: (l, 0, j)),
        out_shape=jax.ShapeDtypeStruct((depth, nc, n_out), F32),
        compiler_params=_params("parallel", "parallel"),
        name="adaln",
    )(c_all, w_mod, b_mod.reshape(depth, 1, n_out))


def _norm_mod(x, g, scale, shift):
    y = x * lax.rsqrt(jnp.mean(x * x, axis=-1, keepdims=True) + NORM_EPS)
    return (y * g) * (1.0 + scale) + shift


def _rope_chunk(z, cos, sin_signed, lo_half):
    partner = jnp.where(lo_half, pltpu.roll(z, LANES - HEAD_DIM // 2, 1), pltpu.roll(z, HEAD_DIM // 2, 1))
    return z * cos + partner * sin_signed


def _proj_kernel(x_ref, g_ref, sc_ref, sh_ref, cos_ref, sin_ref, w_ref, *out_refs, dsa):
    hb = _norm_mod(x_ref[...], g_ref[...], sc_ref[...], sh_ref[...]).astype(BF)
    cos = cos_ref[...]
    sin_s = sin_ref[...]
    lane = lax.broadcasted_iota(I32, (1, LANES), 1)
    lo_half = (lane % HEAD_DIM) < (HEAD_DIM // 2)
    q_ref, k_ref, v_ref, kb_ref, vb_ref = out_refs[:5]

    zq = jnp.dot(hb, w_ref[:, :Q_DIM], preferred_element_type=F32)
    for c in range(Q_DIM // LANES):
        sl = slice(c * LANES, (c + 1) * LANES)
        q_ref[:, sl] = (_rope_chunk(zq[:, sl], cos, sin_s, lo_half) * ATTN_SCALE).astype(BF)
    zk = jnp.dot(hb, w_ref[:, Q_DIM:Q_DIM + KV_DIM], preferred_element_type=F32)
    for c in range(KV_DIM // LANES):
        sl = slice(c * LANES, (c + 1) * LANES)
        kr = _rope_chunk(zk[:, sl], cos, sin_s, lo_half)
        k_ref[:, sl] = kr
        kb_ref[:, sl] = kr.astype(BF)
    zv = jnp.dot(hb, w_ref[:, Q_DIM + KV_DIM:IN_ATTN], preferred_element_type=F32)
    v_ref[...] = zv
    vb_ref[...] = zv.astype(BF)
    if dsa:
        qi_ref, rest_ref, restb_ref = out_refs[5:]
        zi = jnp.dot(hb, w_ref[:, IN_ATTN:IN_ATTN + IDX_Q_DIM], preferred_element_type=F32)
        for c in range(IDX_Q_DIM // LANES):
            sl = slice(c * LANES, (c + 1) * LANES)
            qi_ref[:, sl] = _rope_chunk(zi[:, sl], cos, sin_s, lo_half).astype(BF)
        zr = jnp.dot(hb, w_ref[:, IN_ATTN + IDX_Q_DIM:], preferred_element_type=F32)
        roped = _rope_chunk(zr, cos, sin_s, lo_half)
        rest = jnp.where(lane < IDX_DIM, roped, zr * IDX_W_SCALE)
        rest_ref[...] = rest
        restb_ref[...] = rest.astype(BF)


def _mod_spec(arr, tm, rows_per_seq):
    if arr.ndim == 3:
        return pl.BlockSpec((None, 1, D_MODEL), lambda i, *_: ((i * tm) // rows_per_seq, 0, 0))
    return pl.BlockSpec((tm, D_MODEL), lambda i, *_: (i, 0))


def _project(x, g, scale, shift, cos_tab, sin_tab, w, rows_per_seq, dsa):
    rows = x.shape[0]
    tm = min(ROW_TILE, rows)
    n_in = w.shape[1]
    tab_tiles = cos_tab.shape[0] // tm
    row_spec = lambda n: pl.BlockSpec((tm, n), lambda i: (i, 0))
    tab_spec = pl.BlockSpec((tm, LANES), lambda i: (i % tab_tiles, 0))
    out_shape = [jax.ShapeDtypeStruct((rows, Q_DIM), BF),
                 jax.ShapeDtypeStruct((rows, KV_DIM), F32), jax.ShapeDtypeStruct((rows, KV_DIM), F32),
                 jax.ShapeDtypeStruct((rows, KV_DIM), BF), jax.ShapeDtypeStruct((rows, KV_DIM), BF)]
    out_specs = [row_spec(Q_DIM), row_spec(KV_DIM), row_spec(KV_DIM), row_spec(KV_DIM), row_spec(KV_DIM)]
    if dsa:
        out_shape += [jax.ShapeDtypeStruct((rows, IDX_Q_DIM), BF),
                      jax.ShapeDtypeStruct((rows, LANES), F32), jax.ShapeDtypeStruct((rows, LANES), BF)]
        out_specs += [row_spec(IDX_Q_DIM), row_spec(LANES), row_spec(LANES)]
    return pl.pallas_call(
        functools.partial(_proj_kernel, dsa=dsa),
        grid=(rows // tm,),
        in_specs=[row_spec(D_MODEL),
                  pl.BlockSpec((1, D_MODEL), lambda i: (0, 0)),
                  _mod_spec(scale, tm, rows_per_seq), _mod_spec(shift, tm, rows_per_seq),
                  tab_spec, tab_spec,
                  pl.BlockSpec((D_MODEL, n_in), lambda i: (0, 0))],
        out_specs=out_specs,
        out_shape=out_shape,
        compiler_params=_params("parallel"),
        name="project_dsa" if dsa else "project",
    )(x, g, scale, shift, cos_tab, sin_tab, w)


def _rope_rows(zt, cos_t, sin_t, n_heads, scale, dtype):
    half = HEAD_DIM // 2
    out = []
    for h in range(n_heads):
        x = zt[h * HEAD_DIM:(h + 1) * HEAD_DIM]
        partner = jnp.concatenate([x[half:], x[:half]], axis=0)
        out.append(((x * cos_t + partner * sin_t) * scale).astype(dtype))
    return out


def _proj_t_kernel(x_ref, g_ref, sc_ref, sh_ref, cos_ref, sin_ref, cost_ref, sint_ref, w_ref, wqt_ref, wkt_ref, wvt_ref,
                   *refs, dsa):
    if dsa:
        wit_ref, wrt_ref = refs[:2]
        refs = refs[2:]
    qt_ref, kt_ref, vtf_ref, kz_ref, vt_ref, kmean_ref = refs[:6]
    hb = _norm_mod(x_ref[...], g_ref[...], sc_ref[...], sh_ref[...]).astype(BF)
    lane = lax.broadcasted_iota(I32, (1, LANES), 1)
    lo_half = (lane % HEAD_DIM) < (HEAD_DIM // 2)

    zqt = lax.dot_general(wqt_ref[...], hb, NT, preferred_element_type=F32)
    cos_t, sin_t = cost_ref[...], sint_ref[...]
    for h, qh in enumerate(_rope_rows(zqt, cos_t, sin_t, N_HEADS, ATTN_SCALE * LOG2E, BF)):
        qt_ref[h * HEAD_DIM:(h + 1) * HEAD_DIM, :] = qh
    zkt = lax.dot_general(wkt_ref[...], hb, NT, preferred_element_type=F32)
    for h, kh in enumerate(_rope_rows(zkt, cos_t, sin_t, N_KV_HEADS, 1.0, F32)):
        kt_ref[h * HEAD_DIM:(h + 1) * HEAD_DIM, :] = kh
    zvt = lax.dot_general(wvt_ref[...], hb, NT, preferred_element_type=F32)
    vtf_ref[...] = zvt
    vt_ref[...] = zvt.astype(BF)
    if dsa:
        qit_ref, wt_ref, kit_ref, kiz_ref = refs[6:]
        zit = lax.dot_general(wit_ref[...], hb, NT, preferred_element_type=F32)
        for h, qh in enumerate(_rope_rows(zit, cos_t, sin_t, IDX_HEADS, 1.0, BF)):
            qit_ref[h * IDX_DIM:(h + 1) * IDX_DIM, :] = qh
        zr = jnp.dot(hb, w_ref[:, IN_ATTN + IDX_Q_DIM:], preferred_element_type=F32)
        kiz_ref[...] = jnp.where(lane < IDX_DIM, _rope_chunk(zr, cos_ref[...], sin_ref[...], lo_half), 0.0).astype(BF)
        zrt = lax.dot_general(wrt_ref[...], hb, NT, preferred_element_type=F32)
        kit_ref[...] = _rope_rows(zrt, cos_t, sin_t, 1, 1.0, F32)[0]
        wt_ref[...] = zrt[IDX_DIM:IDX_DIM + IDX_HEADS] * IDX_W_SCALE
    zk = jnp.dot(hb, w_ref[:, Q_DIM:Q_DIM + KV_DIM], preferred_element_type=F32)
    tm = zk.shape[0]
    for c in range(KV_DIM // LANES):
        sl = slice(c * LANES, (c + 1) * LANES)
        kr = _rope_chunk(zk[:, sl], cos_ref[...], sin_ref[...], lo_half)
        kmean_ref[:, :, sl] = jnp.mean(kr.reshape(tm // MOBA_BLOCK, MOBA_BLOCK, LANES), axis=1, keepdims=True)
        kz_ref[:, 2 * c * LANES:(2 * c + 1) * LANES] = jnp.where(lane < HEAD_DIM, kr, 0.0).astype(BF)
        kz_ref[:, (2 * c + 1) * LANES:(2 * c + 2) * LANES] = jnp.where(
            lane < HEAD_DIM, pltpu.roll(kr, HEAD_DIM, 1), 0.0).astype(BF)


def _project_t(x, g, scale, shift, cos_tab, sin_tab, cos_t, sin_t, w, rows_per_seq, dsa):
    rows = x.shape[0]
    tm = min(ROW_TILE, rows)
    assert rows_per_seq % tm == 0 and tm % MOBA_BLOCK == 0
    tab_tiles = cos_tab.shape[0] // tm
    tiles_per_seq = rows_per_seq // tm
    row_spec = lambda n: pl.BlockSpec((tm, n), lambda i: (i, 0))
    col_spec = lambda n: pl.BlockSpec((n, tm), lambda i: (0, i))
    seq_spec = lambda n: pl.BlockSpec((None, n, tm), lambda i: (i // tiles_per_seq, 0, i % tiles_per_seq))
    seq_shape = lambda n: jax.ShapeDtypeStruct((rows // rows_per_seq, n, rows_per_seq), F32)
    tab_spec = pl.BlockSpec((tm, LANES), lambda i: (i % tab_tiles, 0))
    tabt_spec = pl.BlockSpec((HEAD_DIM, tm), lambda i: (0, i % tab_tiles))
    whole = lambda a: pl.BlockSpec(a.shape, lambda i: (0, 0))
    weights = [w, w[:, :Q_DIM].T, w[:, Q_DIM:Q_DIM + KV_DIM].T, w[:, Q_DIM + KV_DIM:IN_ATTN].T]
    out_specs = [col_spec(Q_DIM), seq_spec(KV_DIM), seq_spec(KV_DIM), row_spec(2 * KV_DIM), col_spec(KV_DIM),
                 pl.BlockSpec((tm // MOBA_BLOCK, 1, KV_DIM), lambda i: (i, 0, 0))]
    out_shape = [jax.ShapeDtypeStruct((Q_DIM, rows), BF), seq_shape(KV_DIM), seq_shape(KV_DIM),
                 jax.ShapeDtypeStruct((rows, 2 * KV_DIM), BF), jax.ShapeDtypeStruct((KV_DIM, rows), BF),
                 jax.ShapeDtypeStruct((rows // MOBA_BLOCK, 1, KV_DIM), F32)]
    if dsa:
        weights += [w[:, IN_ATTN:IN_ATTN + IDX_Q_DIM].T, w[:, IN_ATTN + IDX_Q_DIM:].T]
        out_specs[4] = pl.BlockSpec((None, KV_DIM, tm), lambda i: (i, 0, 0))
        out_shape[4] = jax.ShapeDtypeStruct((rows // tm, KV_DIM, tm), BF)
        out_specs += [col_spec(IDX_Q_DIM), col_spec(IDX_HEADS), seq_spec(IDX_DIM), row_spec(LANES)]
        out_shape += [jax.ShapeDtypeStruct((IDX_Q_DIM, rows), BF), jax.ShapeDtypeStruct((IDX_HEADS, rows), F32),
                      seq_shape(IDX_DIM), jax.ShapeDtypeStruct((rows, LANES), BF)]
    return pl.pallas_call(
        functools.partial(_proj_t_kernel, dsa=dsa),
        grid=(rows // tm,),
        in_specs=[row_spec(D_MODEL),
                  pl.BlockSpec((1, D_MODEL), lambda i: (0, 0)),
                  _mod_spec(scale, tm, rows_per_seq), _mod_spec(shift, tm, rows_per_seq),
                  tab_spec, tab_spec, tabt_spec, tabt_spec] + [whole(a) for a in weights],
        out_specs=out_specs,
        out_shape=out_shape,
        compiler_params=_params("parallel"),
        name="project_t_dsa" if dsa else "project_t",
    )(x, g, scale, shift, cos_tab, sin_tab, cos_t, sin_t, *weights)


def _post_kernel(x_ref, o_ref, wo_ref, ga_ref, g_ref, sc_ref, sh_ref, gm_ref, w1_ref, w2_ref, y_ref,
                 x1_scr, h_scr, acc_scr):
    j = pl.program_id(1)

    @pl.when(j == 0)
    def _():
        x1 = x_ref[...] + ga_ref[...] * jnp.dot(o_ref[...], wo_ref[...], preferred_element_type=F32)
        x1_scr[...] = x1
        h_scr[...] = _norm_mod(x1, g_ref[...], sc_ref[...], sh_ref[...]).astype(BF)
        acc_scr[...] = jnp.zeros_like(acc_scr)

    a = jnp.maximum(jnp.dot(h_scr[...], w1_ref[...], preferred_element_type=F32), 0.0)
    acc_scr[...] += jnp.dot((a * a).astype(BF), w2_ref[...], preferred_element_type=F32)

    @pl.when(j == pl.num_programs(1) - 1)
    def _():
        y_ref[...] = x1_scr[...] + gm_ref[...] * acc_scr[...]


def _post(x, o, w_out, gate_a, g_mlp, scale_m, shift_m, gate_m, w1, w2, rows_per_seq):
    rows = x.shape[0]
    tm = min(POST_ROW_TILE, rows)
    tf = FF_TILE
    ms = lambda a: _mod_spec(a, tm, rows_per_seq)
    return pl.pallas_call(
        _post_kernel,
        grid=(rows // tm, D_FF // tf),
        in_specs=[pl.BlockSpec((tm, D_MODEL), lambda i, j: (i, 0)),
                  pl.BlockSpec((tm, Q_DIM), lambda i, j: (i, 0)),
                  pl.BlockSpec((Q_DIM, D_MODEL), lambda i, j: (0, 0)),
                  ms(gate_a),
                  pl.BlockSpec((1, D_MODEL), lambda i, j: (0, 0)),
                  ms(scale_m), ms(shift_m), ms(gate_m),
                  pl.BlockSpec((D_MODEL, tf), lambda i, j: (0, j)),
                  pl.BlockSpec((tf, D_MODEL), lambda i, j: (j, 0))],
        out_specs=pl.BlockSpec((tm, D_MODEL), lambda i, j: (i, 0)),
        out_shape=jax.ShapeDtypeStruct((rows, D_MODEL), F32),
        scratch_shapes=[pltpu.VMEM((tm, D_MODEL), F32), pltpu.VMEM((tm, D_MODEL), BF),
                        pltpu.VMEM((tm, D_MODEL), F32)],
        compiler_params=_params("parallel", "arbitrary"),
        name="post",
    )(x, o, w_out, gate_a, g_mlp, scale_m, shift_m, gate_m, w1, w2)


def _final_norm_kernel(x_ref, g_ref, y_ref):
    x = x_ref[...]
    y_ref[...] = x * lax.rsqrt(jnp.mean(x * x, axis=-1, keepdims=True) + NORM_EPS) * g_ref[...]


def _final_norm(x, g):
    rows = x.shape[0]
    tm = min(ROW_TILE, rows)
    return pl.pallas_call(
        _final_norm_kernel,
        grid=(rows // tm,),
        in_specs=[pl.BlockSpec((tm, D_MODEL), lambda i: (i, 0)), pl.BlockSpec((1, D_MODEL), lambda i: (0, 0))],
        out_specs=pl.BlockSpec((tm, D_MODEL), lambda i: (i, 0)),
        out_shape=jax.ShapeDtypeStruct((rows, D_MODEL), F32),
        compiler_params=_params("parallel"),
        name="final_norm",
    )(x, g)


def _top_mask(gs, ids, n_sel, axis):
    picked = None
    for _ in range(n_sel):
        mx = jnp.max(gs, axis=axis, keepdims=True)
        am = jnp.min(jnp.where(gs == mx, ids, 2 ** 30), axis=axis, keepdims=True)
        hit = (ids == am) & (mx > NEG_INF)
        picked = hit if picked is None else (picked | hit)
        gs = jnp.where(ids == am, NEG_INF, gs)
    return picked


MOBA_MASKED = -1e30
MOBA_BLOCKS_PER_STEP = 4

def _moba_prompt_kernel(qi_tab, kp_tab, last_tab, qt_ref, kz_own_ref, vt_own_ref, kz_pair_ref, vt_pair_ref, kmz_ref,
                        o_ref, qz_scr, m_scr, acc_scr, *, n_sel):
    p = pl.program_id(1)
    i = qi_tab[p]
    kp = kp_tab[p]
    blk = MOBA_BLOCK

    @pl.when(kp < 0)
    def _():
        blocks = lax.broadcasted_iota(I32, (LANES - HEAD_DIM, blk), 0)
        earlier = blocks < i
        for h in range(N_HEADS):
            qt = qt_ref[h * HEAD_DIM:(h + 1) * HEAD_DIM, :]
            gs = jnp.dot(kmz_ref[h // GROUP, HEAD_DIM:, :], qt.astype(F32), precision=lax.Precision.HIGHEST,
                         preferred_element_type=F32)
            picked = _top_mask(jnp.where(earlier, gs, NEG_INF), blocks, n_sel, 0)
            usable = (earlier & picked) | (blocks == i)
            bias = jnp.where(usable, 0.0, MOBA_MASKED)
            qz_scr[h // GROUP, :, (h % GROUP) * blk:(h % GROUP + 1) * blk] = jnp.concatenate(
                [qt, bias.astype(BF)], axis=0)
        m_scr[...] = jnp.full(m_scr.shape, NEG_INF, F32)
        acc_scr[...] = jnp.zeros_like(acc_scr)

    def step(kz_ref, vt_ref, n_keys, block_of_key, causal):
        lane = lax.broadcasted_iota(I32, (n_keys, LANES), 1)
        ones_rows = (lax.broadcasted_iota(I32, (ACC_ROWS - HEAD_DIM, n_keys), 0) == 0).astype(BF)
        sts = []
        for kv in range(N_KV_HEADS):
            kz = jnp.where(lane == HEAD_DIM + block_of_key, 1.0, kz_ref[:, kv * LANES:(kv + 1) * LANES]).astype(BF)
            sts.append(jnp.dot(kz, qz_scr[kv], preferred_element_type=F32))
        es, alphas = [], []
        for kv in range(N_KV_HEADS):
            st = sts[kv]
            if causal:
                key_i = lax.broadcasted_iota(I32, (n_keys, GROUP * blk), 0)
                qry_i = lax.broadcasted_iota(I32, (n_keys, GROUP * blk), 1) % blk
                st = jnp.where(key_i <= qry_i, st, NEG_INF)
            m_old = m_scr[kv]
            m_new = jnp.maximum(m_old, jnp.max(st, axis=0, keepdims=True))
            alphas.append(jnp.exp2(m_old - m_new))
            es.append(jnp.exp2(st - m_new).astype(BF))
            m_scr[kv] = m_new
        for kv in range(N_KV_HEADS):
            vaug = jnp.concatenate([vt_ref[kv * HEAD_DIM:(kv + 1) * HEAD_DIM, :], ones_rows], axis=0)
            acc_scr[kv] = alphas[kv] * acc_scr[kv] + jnp.dot(vaug, es[kv], preferred_element_type=F32)

    @pl.when(kp < 0)
    def _():
        step(kz_own_ref, vt_own_ref, blk, i, True)

    @pl.when(kp >= 0)
    def _():
        n_keys = MOBA_BLOCKS_PER_STEP * blk
        key_row = lax.broadcasted_iota(I32, (n_keys, 1), 0)
        block = MOBA_BLOCKS_PER_STEP * kp + jnp.right_shift(key_row, blk.bit_length() - 1)
        step(kz_pair_ref, vt_pair_ref, n_keys, jnp.where(block < i, block, i + 1), False)

    @pl.when(last_tab[p] == 1)
    def _():
        for c in range(N_HEADS // 2):
            halves = []
            for h in (2 * c, 2 * c + 1):
                a = acc_scr[h // GROUP, :, (h % GROUP) * blk:(h % GROUP + 1) * blk]
                halves.append(a[:HEAD_DIM] * (1.0 / a[HEAD_DIM:HEAD_DIM + 1]))
            o_ref[:, c * LANES:(c + 1) * LANES] = jnp.concatenate(halves, axis=0).T.astype(BF)


def _moba_prompt(qt, kz, vt, kmeans, batch):
    rows = qt.shape[1]
    seq = rows // batch
    blk = MOBA_BLOCK
    assert seq % blk == 0
    nblk = seq // blk
    per_step = MOBA_BLOCKS_PER_STEP
    assert HEAD_DIM + nblk + 1 <= LANES and nblk % per_step == 0 and blk & (blk - 1) == 0
    n_sel = min(MOBA_TOPK, nblk)
    kmean = kmeans.reshape(batch, nblk, N_KV_HEADS, HEAD_DIM).transpose(0, 2, 1, 3)
    kmz = jnp.pad(kmean, ((0, 0), (0, 0), (HEAD_DIM, LANES - HEAD_DIM - nblk), (0, 0)))
    qi, kp, last = [], [], []
    for i in range(nblk):
        order = [-1] + list(range((i + per_step - 1) // per_step))
        qi += [i] * len(order)
        kp += order
        last += [0] * (len(order) - 1) + [1]
    tabs = [jnp.asarray(np.array(t, np.int32)) for t in (qi, kp, last)]
    npair = nblk // per_step
    own_rows = lambda b, p, qi, kp, la: (b * nblk + qi[p], 0)
    own_cols = lambda b, p, qi, kp, la: (0, b * nblk + qi[p])
    grid_spec = pltpu.PrefetchScalarGridSpec(
        num_scalar_prefetch=3,
        grid=(batch, len(qi)),
        in_specs=[pl.BlockSpec((Q_DIM, blk), own_cols),
                  pl.BlockSpec((blk, 2 * KV_DIM), own_rows),
                  pl.BlockSpec((KV_DIM, blk), own_cols),
                  pl.BlockSpec((per_step * blk, 2 * KV_DIM),
                               lambda b, p, qi, kp, la: (b * npair + jnp.maximum(kp[p], 0), 0)),
                  pl.BlockSpec((KV_DIM, per_step * blk),
                               lambda b, p, qi, kp, la: (0, b * npair + jnp.maximum(kp[p], 0))),
                  pl.BlockSpec((None, N_KV_HEADS, LANES, HEAD_DIM), lambda b, p, qi, kp, la: (b, 0, 0, 0))],
        out_specs=pl.BlockSpec((blk, Q_DIM), own_rows),
        scratch_shapes=[pltpu.VMEM((N_KV_HEADS, LANES, GROUP * blk), BF), pltpu.VMEM((N_KV_HEADS, 1, GROUP * blk), F32),
                        pltpu.VMEM((N_KV_HEADS, ACC_ROWS, GROUP * blk), F32)])
    return pl.pallas_call(
        functools.partial(_moba_prompt_kernel, n_sel=n_sel),
        grid_spec=grid_spec,
        out_shape=jax.ShapeDtypeStruct((rows, Q_DIM), BF),
        compiler_params=_params("parallel", "arbitrary"),
        name="moba_prompt",
    )(*tabs, qt, kz, vt, kz, vt, kmz)


def _float_key(x):
    b = pltpu.bitcast(x, I32)
    return jnp.where(b < 0, b ^ 0x7FFFFFFF, b)


def _kth_largest_key(count_ge, k_top, shape, bits_per_step=1):
    assert 32 % bits_per_step == 0
    def body(it, cur):
        shift = 32 - bits_per_step * (it + 1)
        best = cur
        for v in range(1, 2 ** bits_per_step):
            cand = cur | jnp.left_shift(jnp.int32(v), shift)
            best = jnp.where(count_ge(cand ^ INT_MIN) >= k_top, cand, best)
        return best
    return lax.fori_loop(0, 32 // bits_per_step, body, jnp.zeros(shape, I32)) ^ INT_MIN


def _tie_cutoff(count_eq_below, need, n_bits, shape):
    def body(it, cur):
        cand = cur | jnp.left_shift(jnp.int32(1), n_bits - 1 - it)
        return jnp.where(count_eq_below(cand) < need, cand, cur)
    return lax.fori_loop(0, n_bits, body, jnp.zeros(shape, I32))


def _swap_halves(x):
    return jnp.concatenate([x[HEAD_DIM:], x[:HEAD_DIM]], axis=0)


def _dsa_prompt_kernel(qt_ref, qit_ref, wt_ref, kiz_ref, kz_ref, vt_ref, o_ref,
                       qp_scr, qip_scr, key_scr, m_scr, acc_scr, *, k_top, n_bits):
    i = pl.program_id(1)
    tq, ch = DSA_QUERY_TILE, DSA_KEY_CHUNK
    n_ch = ((i + 1) * tq + ch - 1) // ch
    qpos = i * tq + lax.broadcasted_iota(I32, (1, tq), 1)
    key_row = lax.broadcasted_iota(I32, (ch, tq), 0)
    ones_rows = (lax.broadcasted_iota(I32, (ACC_ROWS - HEAD_DIM, ch), 0) == 0).astype(BF)

    for c in range(IDX_HEADS // 2):
        pair = qit_ref[c * LANES:(c + 1) * LANES, :]
        qip_scr[0, :, c * tq:(c + 1) * tq] = pair
        qip_scr[1, :, c * tq:(c + 1) * tq] = _swap_halves(pair)
    for c in range(N_HEADS // 2):
        pair = qt_ref[c * LANES:(c + 1) * LANES, :]
        cols = slice((c % 2) * tq, (c % 2 + 1) * tq)
        qp_scr[2 * (c // 2), :, cols] = pair
        qp_scr[2 * (c // 2) + 1, :, cols] = _swap_halves(pair)

    def score_body(c, carry):
        kiz = kiz_ref[pl.ds(pl.multiple_of(c * ch, ch), ch), :]
        sc = jnp.zeros((ch, tq), F32)
        for second in range(2):
            d = jnp.dot(kiz, qip_scr[second], preferred_element_type=F32)
            for u in range(IDX_HEADS // 2):
                h = 2 * u + second
                sc = sc + wt_ref[h:h + 1, :] * jnp.maximum(d[:, u * tq:(u + 1) * tq], 0.0)
        sc = jnp.where(c * ch + key_row <= qpos, sc, NEG_INF)
        key_scr[c] = _float_key(sc)
        return carry
    lax.fori_loop(0, n_ch, score_body, 0)

    def count(pred):
        def body(c, acc):
            return acc + jnp.sum(pred(key_scr[c], c * ch + key_row).astype(I32), axis=0, keepdims=True)
        return lax.fori_loop(0, n_ch, body, jnp.zeros((1, tq), I32))

    thr = _kth_largest_key(lambda t: count(lambda key, idx: key >= t), k_top, (1, tq))
    n_gt = count(lambda key, idx: key > thr)
    n_ge = count(lambda key, idx: key >= thr)
    need = k_top - n_gt
    tie_cols = ((n_ge > k_top) & (thr != KEY_NEG_INF)).astype(I32)
    cutoff = lax.cond(
        jnp.max(tie_cols) > 0,
        lambda: _tie_cutoff(lambda c_: count(lambda key, idx: (key == thr) & (idx < c_)), need, n_bits, (1, tq)),
        lambda: jnp.full((1, tq), 2 ** 30, I32))

    m_scr[...] = jnp.full(m_scr.shape, NEG_INF, F32)
    acc_scr[...] = jnp.zeros_like(acc_scr)

    def attn_body(c, carry):
        key = key_scr[c]
        idx = c * ch + key_row
        keep = ((key > thr) | ((key == thr) & (idx <= cutoff))) & (idx <= qpos)
        bias = jnp.where(keep, 0.0, NEG_INF)
        bias = jnp.concatenate([bias, bias], axis=1)
        rows = pl.ds(pl.multiple_of(c * ch, ch), ch)
        sts = []
        for kv in range(N_KV_HEADS):
            kz = kz_ref[rows, kv * LANES:(kv + 1) * LANES]
            for second in range(2):
                sts.append(jnp.dot(kz, qp_scr[2 * kv + second], preferred_element_type=F32))
        for kv in range(N_KV_HEADS):
            vaug = jnp.concatenate([vt_ref[c, kv * HEAD_DIM:(kv + 1) * HEAD_DIM, :], ones_rows], axis=0)
            for second in range(2):
                g = 2 * kv + second
                st = sts[g] + bias
                m_old = m_scr[g]
                m_new = jnp.maximum(m_old, jnp.max(st, axis=0, keepdims=True))
                m_safe = jnp.where(m_new == NEG_INF, 0.0, m_new)
                alpha = jnp.exp2(m_old - m_safe)
                e = jnp.exp2(st - m_safe).astype(BF)
                acc_scr[g] = alpha * acc_scr[g] + jnp.dot(vaug, e, preferred_element_type=F32)
                m_scr[g] = m_new
        return carry
    lax.fori_loop(0, n_ch, attn_body, 0)

    for kv in range(N_KV_HEADS):
        outs = []
        for second in range(2):
            a = acc_scr[2 * kv + second]
            outs.append(a[:HEAD_DIM] * (1.0 / a[HEAD_DIM:HEAD_DIM + 1]))
        for u in range(2):
            both = jnp.concatenate([outs[0][:, u * tq:(u + 1) * tq], outs[1][:, u * tq:(u + 1) * tq]], axis=0)
            o_ref[:, (2 * kv + u) * LANES:(2 * kv + u + 1) * LANES] = both.T.astype(BF)


def _dsa_prompt(qt, qit, wt, kiz, kz, vt3, batch):
    rows = qt.shape[1]
    seq = rows // batch
    tq, ch = DSA_QUERY_TILE, DSA_KEY_CHUNK
    k_top = min(DSA_TOPK_MAX, seq // 4)
    assert seq % ch == 0 and ch >= k_top and ch % tq == 0 and tq % LANES == 0 and vt3.shape[2] == ch
    nq, n_chunks = seq // tq, seq // ch
    n_bits = max(1, (seq - 1).bit_length())
    tile = lambda n: pl.BlockSpec((n, tq), lambda b, i: (0, b * nq + i))
    return pl.pallas_call(
        functools.partial(_dsa_prompt_kernel, k_top=k_top, n_bits=n_bits),
        grid=(batch, nq),
        in_specs=[tile(Q_DIM), tile(IDX_Q_DIM), tile(IDX_HEADS),
                  pl.BlockSpec((seq, LANES), lambda b, i: (b, 0), pipeline_mode=pl.Buffered(1)),
                  pl.BlockSpec((seq, 2 * KV_DIM), lambda b, i: (b, 0), pipeline_mode=pl.Buffered(1)),
                  pl.BlockSpec((n_chunks, KV_DIM, ch), lambda b, i: (b, 0, 0), pipeline_mode=pl.Buffered(1))],
        out_specs=pl.BlockSpec((tq, Q_DIM), lambda b, i: (b * nq + i, 0)),
        out_shape=jax.ShapeDtypeStruct((rows, Q_DIM), BF),
        scratch_shapes=[pltpu.VMEM((2 * N_KV_HEADS, LANES, 2 * tq), BF),
                        pltpu.VMEM((2, LANES, (IDX_HEADS // 2) * tq), BF),
                        pltpu.VMEM((n_chunks, ch, tq), I32),
                        pltpu.VMEM((2 * N_KV_HEADS, 1, 2 * tq), F32),
                        pltpu.VMEM((2 * N_KV_HEADS, ACC_ROWS, 2 * tq), F32)],
        compiler_params=_params("parallel", "arbitrary"),
        name="dsa_prompt",
    )(qt, qit, wt, kiz, kz, vt3)


def _swa_prompt_kernel(qt_ref, kzp_ref, kzc_ref, vtp_ref, vtc_ref, sink_ref, o_ref):
    n = pl.program_id(1)
    w = WINDOW
    key_i = lax.broadcasted_iota(I32, (2 * w, w), 0)
    qry_i = lax.broadcasted_iota(I32, (2 * w, w), 1)
    keep = (key_i >= qry_i) & (key_i <= qry_i + w) & ((key_i >= w) | (n > 0))
    bias = jnp.where(keep, 0.0, NEG_INF)
    bias = jnp.concatenate([bias, bias], axis=1)
    ones_rows = (lax.broadcasted_iota(I32, (ACC_ROWS - HEAD_DIM, 2 * w), 0) == 0).astype(BF)
    sts = []
    for kv in range(N_KV_HEADS):
        lanes = slice(kv * LANES, (kv + 1) * LANES)
        kz = jnp.concatenate([kzp_ref[:, lanes], kzc_ref[:, lanes]], axis=0)
        pairs = [qt_ref[(2 * kv + u) * LANES:(2 * kv + u + 1) * LANES, :] for u in range(2)]
        for second in range(2):
            ops = pairs if second == 0 else [_swap_halves(p) for p in pairs]
            sts.append(jnp.dot(kz, jnp.concatenate(ops, axis=1), preferred_element_type=F32))
    for kv in range(N_KV_HEADS):
        feats = slice(kv * HEAD_DIM, (kv + 1) * HEAD_DIM)
        vaug = jnp.concatenate([jnp.concatenate([vtp_ref[feats, :], vtc_ref[feats, :]], axis=1), ones_rows], axis=0)
        outs = []
        for second in range(2):
            st = sts[2 * kv + second] + bias
            sink = sink_ref[2 * kv + second]
            m = jnp.maximum(jnp.max(st, axis=0, keepdims=True), sink)
            acc = jnp.dot(vaug, jnp.exp2(st - m).astype(BF), preferred_element_type=F32)
            outs.append(acc[:HEAD_DIM] * (1.0 / (acc[HEAD_DIM:HEAD_DIM + 1] + jnp.exp2(sink - m))))
        for u in range(2):
            both = jnp.concatenate([outs[0][:, u * w:(u + 1) * w], outs[1][:, u * w:(u + 1) * w]], axis=0)
            o_ref[:, (2 * kv + u) * LANES:(2 * kv + u + 1) * LANES] = both.T.astype(BF)


def _swa_prompt(qt, kz, vt, sinks, batch):
    rows = qt.shape[1]
    seq = rows // batch
    w = WINDOW
    assert seq % w == 0 and w == LANES
    nb = seq // w
    cur = lambda b, n: b * nb + n
    prev = lambda b, n: b * nb + jnp.maximum(n - 1, 0)
    sk = sinks.astype(F32).reshape(N_KV_HEADS, 2, 2)
    sink_cols = jnp.repeat(sk.transpose(0, 2, 1) * LOG2E, w, axis=2).reshape(2 * N_KV_HEADS, 1, 2 * w)
    return pl.pallas_call(
        _swa_prompt_kernel,
        grid=(batch, nb),
        in_specs=[pl.BlockSpec((Q_DIM, w), lambda b, n: (0, cur(b, n))),
                  pl.BlockSpec((w, 2 * KV_DIM), lambda b, n: (prev(b, n), 0)),
                  pl.BlockSpec((w, 2 * KV_DIM), lambda b, n: (cur(b, n), 0)),
                  pl.BlockSpec((KV_DIM, w), lambda b, n: (0, prev(b, n))),
                  pl.BlockSpec((KV_DIM, w), lambda b, n: (0, cur(b, n))),
                  pl.BlockSpec((2 * N_KV_HEADS, 1, 2 * w), lambda b, n: (0, 0, 0))],
        out_specs=pl.BlockSpec((w, Q_DIM), lambda b, n: (cur(b, n), 0)),
        out_shape=jax.ShapeDtypeStruct((rows, Q_DIM), BF),
        compiler_params=_params("parallel", "parallel"),
        name="swa_prompt",
    )(qt, kz, kz, vt, vt, sink_cols)


def _row_token(n_rows, n_tok):
    r = lax.broadcasted_iota(I32, (n_rows, 1), 0)
    return (r // GROUP) % n_tok


def _block_diag_queries(q, n_seq, n_tok):
    qr = q.reshape(n_seq, n_tok, N_KV_HEADS, GROUP, HEAD_DIM).transpose(0, 2, 1, 3, 4)
    qr = qr.reshape(n_seq, N_KV_HEADS, n_tok * GROUP, HEAD_DIM)
    eye = jnp.eye(N_KV_HEADS, dtype=q.dtype)
    qbd = qr[:, :, :, None, :] * eye[None, :, None, :, None]
    return qbd.reshape(n_seq, N_KV_HEADS * n_tok * GROUP, KV_DIM)


def _undiag_outputs(out, n_seq, n_tok):
    o = out.reshape(n_seq, N_KV_HEADS, n_tok, GROUP, N_KV_HEADS, HEAD_DIM)
    o = jnp.stack([o[:, kv, :, :, kv] for kv in range(N_KV_HEADS)], axis=1)
    return o.transpose(0, 2, 1, 3, 4).reshape(n_seq * n_tok, Q_DIM)


def _tokens_on_lanes(cache):
    if cache.ndim == 5:
        t = jnp.transpose(cache, (0, 1, 3, 4, 2))
        return t.reshape(t.shape[:2] + (t.shape[2] * t.shape[3], t.shape[4]))
    return jnp.transpose(cache, (0, 1, 3, 2))


def _page_specs(n_pages_per_step, slot, width, seq_of=lambda b: b):
    def spec(r):
        return pl.BlockSpec((None, None, width, PAGE_SIZE),
                            lambda b, s, pt: (slot, pt[seq_of(b), s * n_pages_per_step + r], 0, 0))
    return [spec(r) for r in range(n_pages_per_step)]


def _pad_rows(a, n_seq, n_tok):
    a = a.reshape(n_seq, n_tok, a.shape[-1])
    return jnp.pad(a, ((0, 0), (0, SUBLANES - n_tok), (0, 0)))


def _moba_sample_kernel(pt_ref, q_ref, kn_ref, vn_ref, *refs, n_tok, n_sel, pps):
    kt_pages = refs[:pps]
    vt_pages = refs[pps:2 * pps]
    o_ref = refs[2 * pps]
    m_scr, l_scr, g_scr, o_scr = refs[2 * pps + 1:]
    s = pl.program_id(1)
    n_steps = m_scr.shape[0]
    bps = pps // 2
    q = q_ref[...]
    n_row = q.shape[0]
    m_scr[s] = jnp.full(m_scr.shape[1:], NEG_INF, F32)
    l_scr[s] = jnp.zeros(l_scr.shape[1:], F32)
    g_scr[s] = jnp.full(g_scr.shape[1:], NEG_INF, F32)
    kt_all = jnp.concatenate([p[...] for p in kt_pages], axis=1).astype(BF)
    sc_all = jnp.dot(q, kt_all, preferred_element_type=F32)
    for r in range(bps):
        vt = jnp.concatenate([vt_pages[2 * r][...], vt_pages[2 * r + 1][...]], axis=1).astype(BF)
        sc = sc_all[:, r * MOBA_BLOCK:(r + 1) * MOBA_BLOCK]
        m = jnp.max(sc, axis=1, keepdims=True)
        e = jnp.exp(sc - m)
        m_scr[s, :, r:r + 1] = m
        l_scr[s, :, r:r + 1] = jnp.sum(e, axis=1, keepdims=True)
        g_scr[s, :, r:r + 1] = jnp.sum(sc, axis=1, keepdims=True)
        o_scr[s * bps + r] = lax.dot_general(e.astype(BF), vt, NT, preferred_element_type=F32)

    @pl.when(s == pl.num_programs(1) - 1)
    def _():
        lane = lax.broadcasted_iota(I32, (n_row, LANES), 1)
        ids = [t * bps + lane for t in range(n_steps)]
        gs = [g_scr[t] for t in range(n_steps)]
        picked = [None] * n_steps
        for _ in range(n_sel):
            mx = functools.reduce(jnp.maximum, [jnp.max(g, axis=1, keepdims=True) for g in gs])
            am = functools.reduce(jnp.minimum, [jnp.min(jnp.where(g == mx, i_, 2 ** 30), axis=1, keepdims=True)
                                                for g, i_ in zip(gs, ids)])
            for t in range(n_steps):
                hit = (ids[t] == am) & (mx > NEG_INF)
                picked[t] = hit if picked[t] is None else (picked[t] | hit)
                gs[t] = jnp.where(ids[t] == am, NEG_INF, gs[t])
        own_ok = lax.broadcasted_iota(I32, (n_row, SUBLANES), 1) <= _row_token(n_row, n_tok)
        s_own = jnp.where(own_ok, lax.dot_general(q, kn_ref[...], NT, preferred_element_type=F32), NEG_INF)
        m_tot = jnp.max(s_own, axis=1, keepdims=True)
        for t in range(n_steps):
            m_tot = jnp.maximum(m_tot, jnp.max(jnp.where(picked[t], m_scr[t], NEG_INF), axis=1, keepdims=True))
        e_own = jnp.exp(s_own - m_tot)
        den = jnp.sum(e_own, axis=1, keepdims=True)
        num = jnp.dot(e_own.astype(BF), vn_ref[...], preferred_element_type=F32)
        for t in range(n_steps):
            wgt = jnp.where(picked[t], jnp.exp(m_scr[t] - m_tot), 0.0)
            den = den + jnp.sum(wgt * l_scr[t], axis=1, keepdims=True)
            for r in range(bps):
                num = num + wgt[:, r:r + 1] * o_scr[t * bps + r]
        o_ref[...] = num * (1.0 / den)


def _paged_attention_call(kernel, name, slot, page_table, cache_kt, cache_vt, extra_inputs, extra_specs,
                          scratch_shapes, n_row):
    n_seq, n_pages = page_table.shape
    pps = PAGES_PER_STEP
    assert n_pages % pps == 0
    grid_spec = pltpu.PrefetchScalarGridSpec(
        num_scalar_prefetch=1,
        grid=(n_seq, n_pages // pps),
        in_specs=extra_specs + _page_specs(pps, slot, KV_DIM) + _page_specs(pps, slot, KV_DIM),
        out_specs=pl.BlockSpec((None, n_row, KV_DIM), lambda b, s, pt: (b, 0, 0)),
        scratch_shapes=scratch_shapes)
    return pl.pallas_call(
        kernel,
        grid_spec=grid_spec,
        out_shape=jax.ShapeDtypeStruct((n_seq, n_row, KV_DIM), F32),
        compiler_params=_params("parallel", "arbitrary"),
        name=name,
    )(page_table, *extra_inputs, *([cache_kt] * pps), *([cache_vt] * pps))


def _moba_sample(qbd, kn_pad, vn_pad, cache_kt, cache_vt, slot, page_table, n_tok):
    n_seq, n_pages = page_table.shape
    past = n_pages * PAGE_SIZE
    assert past % MOBA_BLOCK == 0 and MOBA_BLOCK == 2 * PAGE_SIZE and PAGES_PER_STEP % 2 == 0
    nblk = past // MOBA_BLOCK
    n_steps = n_pages // PAGES_PER_STEP
    n_row = qbd.shape[1]
    per_seq = lambda w: pl.BlockSpec((None, w[0], w[1]), lambda b, s, pt: (b, 0, 0))
    kernel = functools.partial(_moba_sample_kernel, n_tok=n_tok, n_sel=min(MOBA_TOPK, nblk), pps=PAGES_PER_STEP)
    stats = pltpu.VMEM((n_steps, n_row, LANES), F32)
    return _paged_attention_call(
        kernel, "moba_sample", slot, page_table, cache_kt, cache_vt,
        [qbd, kn_pad, vn_pad],
        [per_seq((n_row, KV_DIM)), per_seq((SUBLANES, KV_DIM)), per_seq((SUBLANES, KV_DIM))],
        [stats, stats, stats, pltpu.VMEM((nblk, n_row, KV_DIM), F32)],
        n_row)


def _dsa_index_kernel(pt_ref, qi_ref, w_ref, kin_ref, *refs, n_tok, k_top, ppi, n_bits):
    n_par = SEQS_PER_INDEX_STEP
    ki_pages = [refs[a * ppi:(a + 1) * ppi] for a in range(n_par)]
    mask_ref = refs[n_par * ppi]
    key_scr = refs[n_par * ppi + 1]
    s = pl.program_id(1)
    n_steps = pl.num_programs(1)
    n_pages = n_steps * ppi
    n_fill = SUBLANES - n_par * n_tok

    def scores(a, d):
        wd = (w_ref[a] * jnp.maximum(d, 0.0)).reshape(n_tok, IDX_HEADS, d.shape[1])
        return jnp.sum(wd, axis=1)

    def tile(parts):
        if n_fill:
            parts = parts + [jnp.full((n_fill, LANES), NEG_INF, F32)]
        return _float_key(jnp.concatenate(parts, axis=0))

    sc_all = []
    for a in range(n_par):
        kit = jnp.concatenate([p[...] for p in ki_pages[a]], axis=1).astype(BF)
        sc_all.append(scores(a, jnp.dot(qi_ref[a], kit, preferred_element_type=F32)))
    for r in range(ppi):
        key_scr[s * ppi + r] = tile([sc[:, r * PAGE_SIZE:(r + 1) * PAGE_SIZE] for sc in sc_all])

    @pl.when(s == n_steps - 1)
    def _():
        tok = lax.broadcasted_iota(I32, (n_tok, LANES), 0)
        lane_t = lax.broadcasted_iota(I32, (n_tok, LANES), 1)
        new = []
        for a in range(n_par):
            sc_new = scores(a, lax.dot_general(qi_ref[a], kin_ref[a], NT, preferred_element_type=F32))
            new.append(jnp.where(lane_t <= tok, sc_new, NEG_INF))
        key_scr[n_pages] = tile(new)
        tiles = (n_pages + 1, SUBLANES, LANES)
        idx = lax.broadcasted_iota(I32, tiles, 0) * PAGE_SIZE + lax.broadcasted_iota(I32, tiles, 2)

        def count(hit):
            return jnp.sum(jnp.sum(hit.astype(I32), axis=0), axis=1, keepdims=True)

        thr = _kth_largest_key(lambda t: count(key_scr[...] >= t), k_top, (SUBLANES, 1), bits_per_step=2)
        need = k_top - count(key_scr[...] > thr)
        tie_rows = ((count(key_scr[...] >= thr) > k_top) & (thr != KEY_NEG_INF)).astype(I32)
        cutoff = lax.cond(
            jnp.max(tie_rows) > 0,
            lambda: _tie_cutoff(lambda c_: count((key_scr[...] == thr) & (idx < c_)), need, n_bits, (SUBLANES, 1)),
            lambda: jnp.full((SUBLANES, 1), 2 ** 30, I32))
        keys = key_scr[...]
        keep = (keys > thr) | ((keys == thr) & (idx <= cutoff))
        mask_ref[...] = jnp.where(keep & (keys != KEY_NEG_INF), 1.0, 0.0).astype(BF)


def _dsa_index(qi_rows, w_rows, ki_new, cache_idx, islot, page_table, n_tok):
    n_seq, n_pages = page_table.shape
    ppi = min(INDEX_PAGES_PER_STEP, n_pages)
    n_par = SEQS_PER_INDEX_STEP
    total = n_pages * PAGE_SIZE + n_tok
    k_top = min(DSA_TOPK_MAX, total // 4)
    assert n_pages % ppi == 0 and n_par * n_tok <= SUBLANES and n_seq % n_par == 0 and n_pages * PAGE_SIZE >= k_top
    n_bits = max(1, ((n_pages + 1) * PAGE_SIZE - 1).bit_length())
    nr = n_tok * IDX_HEADS
    group = lambda n, m: pl.BlockSpec((n_par, n, m), lambda g, s, pt: (g, 0, 0))
    pages = []
    for a in range(n_par):
        pages += _page_specs(ppi, islot, IDX_DIM, seq_of=lambda g, a=a: g * n_par + a)
    grid_spec = pltpu.PrefetchScalarGridSpec(
        num_scalar_prefetch=1,
        grid=(n_seq // n_par, n_pages // ppi),
        in_specs=[group(nr, IDX_DIM), group(nr, 1), group(LANES, IDX_DIM)] + pages,
        out_specs=pl.BlockSpec((None, n_pages + 1, SUBLANES, LANES), lambda g, s, pt: (g, 0, 0, 0)),
        scratch_shapes=[pltpu.VMEM((n_pages + 1, SUBLANES, LANES), I32)])
    return pl.pallas_call(
        functools.partial(_dsa_index_kernel, n_tok=n_tok, k_top=k_top, ppi=ppi, n_bits=n_bits),
        grid_spec=grid_spec,
        out_shape=jax.ShapeDtypeStruct((n_seq // n_par, n_pages + 1, SUBLANES, LANES), BF),
        compiler_params=_params("parallel", "arbitrary"),
        name="dsa_index",
    )(page_table, qi_rows, w_rows, ki_new, *([cache_idx] * (n_par * ppi)))


def _dsa_sample_kernel(pt_ref, q_ref, kn_ref, vn_ref, mask_ref, mnew_ref, *refs, n_tok, pps):
    kt_pages = refs[:pps]
    vt_pages = refs[pps:2 * pps]
    o_ref = refs[2 * pps]
    m_scr, l_scr, acc_scr = refs[2 * pps + 1:]
    s = pl.program_id(1)
    q = q_ref[...]
    n_row = q.shape[0]
    first_row = (pl.program_id(0) % SEQS_PER_INDEX_STEP) * n_tok
    expand = (lax.broadcasted_iota(I32, (n_row, SUBLANES), 1) == _row_token(n_row, n_tok) + first_row).astype(BF)

    @pl.when(s == 0)
    def _():
        m_scr[...] = jnp.full(m_scr.shape, NEG_INF, F32)
        l_scr[...] = jnp.zeros_like(l_scr)
        acc_scr[...] = jnp.zeros_like(acc_scr)

    def update(sc, keep, pv):
        sc = jnp.where(keep > 0.5, sc, NEG_INF)
        m_old = m_scr[...]
        m_new = jnp.maximum(m_old, jnp.max(sc, axis=1, keepdims=True))
        m_safe = jnp.where(m_new == NEG_INF, 0.0, m_new)
        alpha = jnp.exp(m_old - m_safe)
        e = jnp.exp(sc - m_safe)
        l_scr[...] = alpha * l_scr[...] + jnp.sum(e, axis=1, keepdims=True)
        acc_scr[...] = alpha * acc_scr[...] + pv(e.astype(BF))
        m_scr[...] = m_new

    kt_all = jnp.concatenate([p[...] for p in kt_pages], axis=1).astype(BF)
    sc_all = jnp.dot(q, kt_all, preferred_element_type=F32)
    keep_all = jnp.dot(expand, jnp.concatenate([mask_ref[r] for r in range(pps)], axis=1),
                       preferred_element_type=F32)
    per_update = min(DSA_PAGES_PER_UPDATE, pps)
    width = per_update * PAGE_SIZE
    for u in range(pps // per_update):
        rs = range(u * per_update, (u + 1) * per_update)
        vt = jnp.concatenate([vt_pages[r][...] for r in rs], axis=1).astype(BF)
        cols = slice(u * width, (u + 1) * width)
        update(sc_all[:, cols], keep_all[:, cols], lambda e, vt=vt: lax.dot_general(e, vt, NT, preferred_element_type=F32))

    @pl.when(s == pl.num_programs(1) - 1)
    def _():
        sc = lax.dot_general(q, kn_ref[...], NT, preferred_element_type=F32)
        keep = jnp.dot(expand, mnew_ref[...], preferred_element_type=F32)[:, :SUBLANES]
        update(sc, keep, lambda e: jnp.dot(e, vn_ref[...], preferred_element_type=F32))
        o_ref[...] = acc_scr[...] * (1.0 / l_scr[...])


def _dsa_sample(qbd, kn_pad, vn_pad, mask, cache_kt, cache_vt, slot, page_table, n_tok):
    n_seq, n_pages = page_table.shape
    n_row = qbd.shape[1]
    pps = PAGES_PER_STEP
    per_seq = lambda w: pl.BlockSpec((None, w[0], w[1]), lambda b, s, pt: (b, 0, 0))
    return _paged_attention_call(
        functools.partial(_dsa_sample_kernel, n_tok=n_tok, pps=pps), "dsa_sample", slot, page_table,
        cache_kt, cache_vt,
        [qbd, kn_pad, vn_pad, mask, mask],
        [per_seq((n_row, KV_DIM)), per_seq((SUBLANES, KV_DIM)), per_seq((SUBLANES, KV_DIM)),
         pl.BlockSpec((None, pps, SUBLANES, LANES), lambda b, s, pt: (b // SEQS_PER_INDEX_STEP, s, 0, 0)),
         pl.BlockSpec((None, None, SUBLANES, LANES), lambda b, s, pt: (b // SEQS_PER_INDEX_STEP, n_pages, 0, 0))],
        [pltpu.VMEM((n_row, 1), F32), pltpu.VMEM((n_row, 1), F32), pltpu.VMEM((n_row, KV_DIM), F32)],
        n_row)


def _swa_sample_kernel(q_ref, kt_ref, vt_ref, kn_ref, vn_ref, sink_ref, o_ref, *, n_tok):
    q = q_ref[...]
    n_row = q.shape[0]
    tok_r = _row_token(n_row, n_tok)
    wk = kt_ref.shape[1]
    s_buf = jnp.dot(q, kt_ref[...].astype(BF), preferred_element_type=F32)
    s_buf = jnp.where(lax.broadcasted_iota(I32, (n_row, wk), 1) >= tok_r, s_buf, NEG_INF)
    s_new = lax.dot_general(q, kn_ref[...], NT, preferred_element_type=F32)
    s_new = jnp.where(lax.broadcasted_iota(I32, (n_row, SUBLANES), 1) <= tok_r, s_new, NEG_INF)
    sink = sink_ref[...]
    m = jnp.maximum(jnp.maximum(jnp.max(s_buf, axis=1, keepdims=True), jnp.max(s_new, axis=1, keepdims=True)), sink)
    e_buf = jnp.exp(s_buf - m)
    e_new = jnp.exp(s_new - m)
    den = jnp.sum(e_buf, axis=1, keepdims=True) + jnp.sum(e_new, axis=1, keepdims=True) + jnp.exp(sink - m)
    num = (lax.dot_general(e_buf.astype(BF), vt_ref[...].astype(BF), NT, preferred_element_type=F32)
           + jnp.dot(e_new.astype(BF), vn_ref[...], preferred_element_type=F32))
    o_ref[...] = num * (1.0 / den)


def _swa_sample(qbd, kn_pad, vn_pad, buf_kt, buf_vt, sinks, n_tok):
    n_seq, _, wk = buf_kt.shape
    assert wk == WINDOW
    n_row = qbd.shape[1]
    sink_rows = jnp.repeat(sinks.reshape(N_KV_HEADS, 1, GROUP), n_tok, axis=1).reshape(n_row, 1)
    per_seq = lambda a, c: pl.BlockSpec((None, a, c), lambda b: (b, 0, 0))
    return pl.pallas_call(
        functools.partial(_swa_sample_kernel, n_tok=n_tok),
        grid=(n_seq,),
        in_specs=[per_seq(n_row, KV_DIM), per_seq(KV_DIM, wk), per_seq(KV_DIM, wk),
                  per_seq(SUBLANES, KV_DIM), per_seq(SUBLANES, KV_DIM),
                  pl.BlockSpec((n_row, 1), lambda b: (0, 0))],
        out_specs=per_seq(n_row, KV_DIM),
        out_shape=jax.ShapeDtypeStruct((n_seq, n_row, KV_DIM), F32),
        compiler_params=_params("parallel"),
        name="swa_sample",
    )(qbd, buf_kt, buf_vt, kn_pad, vn_pad, sink_rows)


def _rope_tables(pos):
    half = HEAD_DIM // 2
    inv = ROPE_THETA ** (-jnp.arange(half, dtype=F32) / half)
    ang = pos.astype(F32)[:, None] * inv[None, :]
    cos, sin = jnp.cos(ang), jnp.sin(ang)
    reps = LANES // HEAD_DIM
    return jnp.tile(jnp.concatenate([cos, cos], axis=1), (1, reps)), jnp.tile(jnp.concatenate([-sin, sin], axis=1), (1, reps))


def kernel(x_prompt, x_sample, cache_k, cache_v, cache_idx_k, state_swa_k, state_swa_v, page_table, c_prompt, c_sample, g_attn, g_mlp, w_mod, b_mod, w_in_moba, w_in_dsa, w_in_swa, swa_sinks, w_out, w_ff1, w_ff2, g_final):
    batch, seq, _ = x_prompt.shape
    n_seq, n_tok, _ = x_sample.shape
    depth = g_attn.shape[0]
    n_pages = page_table.shape[1]
    past = n_pages * PAGE_SIZE
    rows_p, rows_s = batch * seq, n_seq * n_tok

    n_cond = batch + n_seq
    pad_cond = -n_cond % SUBLANES
    c_all = jnp.pad(jnp.concatenate([c_prompt, c_sample], axis=0), ((0, pad_cond), (0, 0)))
    mod = _adaln(c_all, w_mod, b_mod)

    def mods(layer):
        parts = [mod[layer, :, r * D_MODEL:(r + 1) * D_MODEL] for r in range(6)]
        mp = [p[:batch].reshape(batch, 1, D_MODEL) for p in parts]
        ms = [jnp.repeat(p[batch:n_cond], n_tok, axis=0) for p in parts]
        return mp, ms

    cos_p, sin_p = _rope_tables(jnp.arange(seq, dtype=jnp.int32))
    cos_s, sin_s = _rope_tables(jnp.tile(past + jnp.arange(n_tok, dtype=jnp.int32), n_seq))
    cos_pt, sin_pt = cos_p[:, :HEAD_DIM].T, sin_p[:, :HEAD_DIM].T

    cache_kt, cache_vt, cache_it = _tokens_on_lanes(cache_k), _tokens_on_lanes(cache_v), _tokens_on_lanes(cache_idx_k)
    swa_kt, swa_vt = _tokens_on_lanes(state_swa_k), _tokens_on_lanes(state_swa_v)

    xp = x_prompt.reshape(rows_p, D_MODEL)
    xs = x_sample.reshape(rows_s, D_MODEL)
    kp_l, vp_l, ks_l, vs_l, ip_l, is_l = [], [], [], [], [], []
    skp_l, svp_l, sks_l, svs_l = [], [], [], []
    for i, (mixer, j, slot) in enumerate(_layer_plan(depth)):
        (sh_ap, sc_ap, gt_ap, sh_mp, sc_mp, gt_mp), (sh_as, sc_as, gt_as, sh_ms, sc_ms, gt_ms) = mods(i)
        dsa = mixer == 1
        if mixer == 0:
            w_in = w_in_moba[j]
        elif dsa:
            w_in = jnp.pad(w_in_dsa[j], ((0, 0), (0, -IN_DSA % LANES)))
        else:
            w_in = w_in_swa[j]
        w_in = w_in.astype(BF)
        g_a = g_attn[i].reshape(1, D_MODEL)
        outs_p = _project_t(xp, g_a, sc_ap, sh_ap, cos_p, sin_p, cos_pt, sin_pt, w_in, seq, dsa)
        qt_p, kt_p, vtf_p, kz_p, vt_p, kmeans_p = outs_p[:6]
        outs_s = _project(xs, g_a, sc_as, sh_as, cos_s, sin_s, w_in, n_tok, dsa)
        q_s, k_s, v_s, kb_s, vb_s = outs_s[:5]
        qbd = _block_diag_queries(q_s, n_seq, n_tok)
        kn_pad = _pad_rows(kb_s, n_seq, n_tok)
        vn_pad = _pad_rows(vb_s, n_seq, n_tok)
        k_p5 = kt_p.reshape(batch, N_KV_HEADS, HEAD_DIM, seq).transpose(0, 3, 1, 2)
        v_p5 = vtf_p.reshape(batch, N_KV_HEADS, HEAD_DIM, seq).transpose(0, 3, 1, 2)
        k_s5 = k_s.reshape(n_seq, n_tok, N_KV_HEADS, HEAD_DIM)
        v_s5 = v_s.reshape(n_seq, n_tok, N_KV_HEADS, HEAD_DIM)
        if mixer == 0:
            o_p = _moba_prompt(qt_p, kz_p, vt_p, kmeans_p, batch)
            o_t = _moba_sample(qbd, kn_pad, vn_pad, cache_kt, cache_vt, slot, page_table, n_tok)
        elif dsa:
            qit_p, wt_p, kit_p, kiz_p = outs_p[6:]
            qi_s, rest_s, restb_s = outs_s[5:]
            o_p = _dsa_prompt(qt_p, qit_p, wt_p, kiz_p, kz_p, vt_p, batch)
            qi_rows = qi_s.reshape(n_seq, n_tok * IDX_HEADS, IDX_DIM)
            w_rows = rest_s[:, IDX_DIM:IDX_DIM + IDX_HEADS].reshape(n_seq, n_tok * IDX_HEADS, 1)
            ki_new = jnp.pad(restb_s[:, :IDX_DIM].reshape(n_seq, n_tok, IDX_DIM),
                             ((0, 0), (0, LANES - n_tok), (0, 0)))
            mask = _dsa_index(qi_rows, w_rows, ki_new, cache_it, j, page_table, n_tok)
            o_t = _dsa_sample(qbd, kn_pad, vn_pad, mask, cache_kt, cache_vt, slot, page_table, n_tok)
            ip_l.append(kit_p.transpose(0, 2, 1))
            is_l.append(rest_s[:, :IDX_DIM].reshape(n_seq, n_tok, IDX_DIM))
        else:
            o_p = _swa_prompt(qt_p, kz_p, vt_p, swa_sinks[j], batch)
            o_t = _swa_sample(qbd, kn_pad, vn_pad, swa_kt[j], swa_vt[j], swa_sinks[j], n_tok)
            keep_p = min(WINDOW, seq)
            skp_l.append(k_p5[:, seq - keep_p:])
            svp_l.append(v_p5[:, seq - keep_p:])
            wk = state_swa_k.shape[2]
            sks_l.append(jnp.concatenate([state_swa_k[j], k_s5], axis=1)[:, -wk:])
            svs_l.append(jnp.concatenate([state_swa_v[j], v_s5], axis=1)[:, -wk:])
        if mixer < 2:
            kp_l.append(k_p5)
            vp_l.append(v_p5)
            ks_l.append(k_s5)
            vs_l.append(v_s5)
        o_s = _undiag_outputs(o_t, n_seq, n_tok).astype(BF)
        wo, w1, w2 = w_out[i].astype(BF), w_ff1[i].astype(BF), w_ff2[i].astype(BF)
        g_m = g_mlp[i].reshape(1, D_MODEL)
        xp = _post(xp, o_p, wo, gt_ap, g_m, sc_mp, sh_mp, gt_mp, w1, w2, seq)
        xs = _post(xs, o_s, wo, gt_as, g_m, sc_ms, sh_ms, gt_ms, w1, w2, n_tok)
    g_f = g_final.reshape(1, D_MODEL)
    y_prompt = _final_norm(xp, g_f).reshape(batch, seq, D_MODEL)
    y_sample = _final_norm(xs, g_f).reshape(n_seq, n_tok, D_MODEL)
    return (y_prompt, y_sample, jnp.stack(kp_l), jnp.stack(vp_l), jnp.stack(ks_l), jnp.stack(vs_l),
            jnp.stack(ip_l), jnp.stack(is_l), jnp.stack(skp_l), jnp.stack(svp_l), jnp.stack(sks_l), jnp.stack(svs_l))
```
